```python
import jax
import jax.numpy as jnp
from jax import lax
import numpy as np

D_MODEL = 1024
BATCH = 8
SEQ = 4096
DEPTH = 2
DEC_BATCH = 32
DEC_SEQ = 16
PAST_LEN = 4096

CHUNK = 64
Q_BLOCK = 128
N_BRANCH = 4
BRANCH_W = 512
DSA_HEADS = 8
DSA_HEAD_DIM = 64
IDX_HEADS = 4
IDX_DIM = 64
DSA_TOPK_MAX = 256
POOL_WINDOWS = (2, 4, 8, 16)
POOL_GROUP = BRANCH_W // 4
POOL_HIST = 15
CONV_WIDTH = 3
MLA_HEADS = 8
MLA_Q_LORA = 256
MLA_KV_LORA = 256
MLA_NOPE = 64
MLA_ROPE = 32
MLA_V = 64
ROPE_BASE = 10000.0
FFN_DIM = 2816
N_EXPERTS = 8
MOE_TOPK = 2
EXPERT_DIM = 2816
N_DENSE = (DEPTH + 1) // 2
N_MOE = DEPTH // 2
EPS = 1e-6

DSA_SCALE = DSA_HEAD_DIM ** -0.5
IDX_SCALE = (IDX_HEADS * IDX_DIM) ** -0.5
MLA_SCALE = (MLA_NOPE + MLA_ROPE) ** -0.5

IN_SPLIT_WIDTHS = (DSA_HEADS * DSA_HEAD_DIM, DSA_HEADS * DSA_HEAD_DIM, DSA_HEADS * DSA_HEAD_DIM,
                   IDX_HEADS * IDX_DIM, IDX_DIM, IDX_HEADS,
                   BRANCH_W,
                   BRANCH_W, BRANCH_W, BRANCH_W,
                   MLA_Q_LORA, MLA_KV_LORA, MLA_ROPE)
IN_W = sum(IN_SPLIT_WIDTHS)
IN_SPLIT_POINTS = tuple(sum(IN_SPLIT_WIDTHS[:i + 1]) for i in range(len(IN_SPLIT_WIDTHS) - 1))

kernel_name = 'hybrid_streaming_encoder_step'


def rmsnorm(x, g):
    xf = x.astype(jnp.float32)
    y = xf * lax.rsqrt(jnp.mean(xf * xf, axis=-1, keepdims=True) + EPS)
    return (y * g.astype(jnp.float32)).astype(x.dtype)


def chunk_end(pos):
    return (pos // CHUNK + 1) * CHUNK


def rope_tables(pos):
    half = MLA_ROPE // 2
    inv = 1.0 / (ROPE_BASE ** (jnp.arange(half, dtype=jnp.float32) / half))
    ang = pos.astype(jnp.float32)[:, None] * inv[None, :]
    return jnp.cos(ang), jnp.sin(ang)


def apply_rope(x, cos, sin):
    half = x.shape[-1] // 2
    xf = x.astype(jnp.float32)
    x1, x2 = xf[..., :half], xf[..., half:]
    return jnp.concatenate([x1 * cos - x2 * sin, x2 * cos + x1 * sin], axis=-1).astype(x.dtype)


def map_query_blocks(fn, q_arrays, q_pos):
    B, T = q_arrays[0].shape[:2]
    qb = min(Q_BLOCK, T)
    nb = T // qb
    blk = tuple(jnp.moveaxis(a.reshape((B, nb, qb) + a.shape[2:]), 1, 0) for a in q_arrays)
    out = lax.map(fn, blk + (q_pos.reshape(nb, qb),))
    out = jnp.moveaxis(out, 0, 1)
    return out.reshape((B, T) + out.shape[3:])


def dsa_attention(q, k, v, iq, ik, iw, q_pos, top_k):
    k_pos = jnp.arange(k.shape[1])

    def block(args):
        qb, iqb, iwb, pb = args
        rel = jax.nn.relu(jnp.einsum('bqhi,bsi->bqhs', iqb, ik).astype(jnp.float32))
        score = jnp.einsum('bqh,bqhs->bqs', iwb.astype(jnp.float32), rel) * IDX_SCALE
        ok = k_pos[None, :] < chunk_end(pb)[:, None]
        score = jnp.where(ok[None], score, -jnp.inf)
        top_val, top_idx = lax.top_k(score, top_k)
        k_sel = jax.vmap(lambda kb, ib: kb[ib])(k, top_idx)
        v_sel = jax.vmap(lambda vb, ib: vb[ib])(v, top_idx)
        logits = jnp.einsum('bqhd,bqjhd->bhqj', qb, k_sel).astype(jnp.float32) * DSA_SCALE
        logits = jnp.where(jnp.isfinite(top_val)[:, None], logits, -jnp.inf)
        p = jax.nn.softmax(logits, axis=-1).astype(v.dtype)
        return jnp.einsum('bhqj,bqjhd->bqhd', p, v_sel)

    return map_query_blocks(block, (q, iq, iw), q_pos)


def mla_attention(q_lat, q_rope, ckv, krope, q_pos):
    k_pos = jnp.arange(ckv.shape[1])

    def block(args):
        ql, qr, pb = args
        logits = (jnp.einsum('bqhc,bsc->bhqs', ql, ckv).astype(jnp.float32)
                  + jnp.einsum('bqhr,bsr->bhqs', qr, krope).astype(jnp.float32)) * MLA_SCALE
        ok = k_pos[None, :] < chunk_end(pb)[:, None]
        logits = jnp.where(ok[None, None], logits, -jnp.inf)
        p = jax.nn.softmax(logits, axis=-1).astype(ckv.dtype)
        return jnp.einsum('bhqs,bsc->bqhc', p, ckv)

    return map_query_blocks(block, (q_lat, q_rope), q_pos)


def pool_mixer(u, prev, pos, w, scale):
    B, T, C = u.shape
    P = prev.shape[1]
    up = jnp.concatenate([prev, u], axis=1).astype(jnp.float32)
    cs = jnp.concatenate([jnp.zeros((B, 1, C), jnp.float32), jnp.cumsum(up, axis=1)], axis=1)
    end = cs[:, P + 1:P + 1 + T]
    means = []
    for g, win in enumerate(POOL_WINDOWS):
        sl = slice(g * POOL_GROUP, (g + 1) * POOL_GROUP)
        start = cs[:, P + 1 - win:P + 1 - win + T, sl]
        cnt = jnp.minimum(pos + 1, win).astype(jnp.float32)[None, :, None]
        means.append((end[..., sl] - start) / cnt)
    d = (jnp.concatenate(means, axis=-1) - u.astype(jnp.float32)).astype(u.dtype)
    d = d.reshape(B, T, len(POOL_WINDOWS), POOL_GROUP)
    y = jnp.einsum('btgi,gio->btgo', d, w).reshape(B, T, C)
    return y * scale


def short_conv(g, prev, w):
    gp = jnp.concatenate([prev, g], axis=1)
    return lax.conv_general_dilated(gp, w, window_strides=(1,), padding='VALID',
                                    dimension_numbers=('NWC', 'WIO', 'NWC'),
                                    feature_group_count=g.shape[-1])


def swiglu(h, w1, w3, w2):
    return (jax.nn.silu(h @ w1) * (h @ w3)) @ w2


def moe_ffn(h, rw, rb, w1, w3, w2):
    logits = (h @ rw + rb).astype(jnp.float32)
    top_v, top_i = lax.top_k(logits, MOE_TOPK)
    gates = jax.nn.softmax(top_v, axis=-1)
    dense_gate = jnp.sum(jax.nn.one_hot(top_i, N_EXPERTS, dtype=jnp.float32) * gates[..., None], axis=-2).astype(h.dtype)
    y = jnp.zeros_like(h)
    for e in range(N_EXPERTS):
        y = y + dense_gate[..., e:e + 1] * swiglu(h, w1[e], w3[e], w2[e])
    return y


def layer_forward(x, c, past, lp, fp, use_moe):
    B, T, _ = x.shape
    P = past['dsa_k'].shape[1]
    pos = P + jnp.arange(T)
    mod = (jax.nn.silu(c) @ lp['ada_w'] + lp['ada_b'])[:, None, :]
    sh1, sc1, g1, sh2, sc2, g2 = jnp.split(mod, 6, axis=-1)
    h = rmsnorm(x, lp['norm_mix_g']) * (1 + sc1) + sh1
    z = h @ lp['w_in']
    aq, ak, av, iq, ik, iw, pu, cb, cc, ch, mq, mkv, mkr = jnp.split(z, IN_SPLIT_POINTS, axis=-1)

    q = aq.reshape(B, T, DSA_HEADS, DSA_HEAD_DIM)
    k_new = ak.reshape(B, T, DSA_HEADS, DSA_HEAD_DIM)
    v_new = av.reshape(B, T, DSA_HEADS, DSA_HEAD_DIM)
    k_all = jnp.concatenate([past['dsa_k'], k_new], axis=1)
    v_all = jnp.concatenate([past['dsa_v'], v_new], axis=1)
    ik_all = jnp.concatenate([past['idx_k'], ik], axis=1)
    top_k = min(DSA_TOPK_MAX, (P + T) // 4)
    o_a = dsa_attention(q, k_all, v_all, iq.reshape(B, T, IDX_HEADS, IDX_DIM), ik_all, iw, pos, top_k)
    o_a = o_a.reshape(B, T, BRANCH_W)

    o_b = pool_mixer(pu, past['pool'], pos, lp['pool_w'], lp['pool_scale'])

    g_in = cc * ch
    o_c = cb * short_conv(g_in, past['conv'], lp['conv_w'])

    cq = (rmsnorm(mq, lp['mla_q_norm_g']) @ lp['mla_w_uq']).reshape(B, T, MLA_HEADS, MLA_NOPE + MLA_ROPE)
    q_nope, q_rope = cq[..., :MLA_NOPE], cq[..., MLA_NOPE:]
    cos, sin = rope_tables(pos)
    q_rope = apply_rope(q_rope, cos[:, None, :], sin[:, None, :])
    ckv_new = rmsnorm(mkv, lp['mla_kv_norm_g'])
    kr_new = apply_rope(mkr, cos, sin)
    q_lat = jnp.einsum('bthn,chn->bthc', q_nope, lp['mla_w_uk'])
    o_lat = mla_attention(q_lat, q_rope,
                          jnp.concatenate([past['ckv'], ckv_new], axis=1),
                          jnp.concatenate([past['krope'], kr_new], axis=1), pos)
    o_d = jnp.einsum('bthc,chv->bthv', o_lat, lp['mla_w_uv']).reshape(B, T, BRANCH_W)

    gates = jax.nn.sigmoid((h @ lp['w_gate'] + lp['b_gate']).astype(jnp.float32)).astype(x.dtype)
    gates = gates.reshape(B, T, N_BRANCH, D_MODEL)
    merged = jnp.zeros_like(x)
    for i, o in enumerate((o_a, o_b, o_c, o_d)):
        merged = merged + gates[:, :, i] * (o @ lp['w_branch'][i * BRANCH_W:(i + 1) * BRANCH_W])
    x = x + g1 * (merged @ lp['w_out'])

    h2 = rmsnorm(x, lp['norm_ffn_g']) * (1 + sc2) + sh2
    f = moe_ffn(h2, *fp) if use_moe else swiglu(h2, *fp)
    x = x + g2 * f

    pool_hist = jnp.concatenate([past['pool'], pu], axis=1)[:, -POOL_HIST:]
    conv_hist = jnp.concatenate([past['conv'], g_in], axis=1)[:, -(CONV_WIDTH - 1):]
    return x, (k_new, v_new, ik, ckv_new, kr_new, pool_hist, conv_hist)


def empty_past(B, dtype):
    return dict(dsa_k=jnp.zeros((B, 0, DSA_HEADS, DSA_HEAD_DIM), dtype),
                dsa_v=jnp.zeros((B, 0, DSA_HEADS, DSA_HEAD_DIM), dtype),
                idx_k=jnp.zeros((B, 0, IDX_DIM), dtype),
                ckv=jnp.zeros((B, 0, MLA_KV_LORA), dtype),
                krope=jnp.zeros((B, 0, MLA_ROPE), dtype),
                pool=jnp.zeros((B, POOL_HIST, BRANCH_W), dtype),
                conv=jnp.zeros((B, CONV_WIDTH - 1, BRANCH_W), dtype))


def setup_inputs(seed: int = 0) -> dict:
    key = jax.random.key(seed)
    ks = list(jax.random.split(key, 48))
    f32 = jnp.float32
    D = D_MODEL

    def nrm(shape, scale=1.0):
        return jax.random.normal(ks.pop(), shape, f32) * scale

    def gain(shape, noise=0.05):
        return 1.0 + noise * jax.random.normal(ks.pop(), shape, f32)

    return {
        'x_prompt': nrm((BATCH, SEQ, D)),
        'x_sample': nrm((DEC_BATCH, DEC_SEQ, D)),
        'c_prompt': nrm((BATCH, D)),
        'c_sample': nrm((DEC_BATCH, D)),
        'cache_dsa_k': nrm((DEPTH, DEC_BATCH, PAST_LEN, DSA_HEADS, DSA_HEAD_DIM)),
        'cache_dsa_v': nrm((DEPTH, DEC_BATCH, PAST_LEN, DSA_HEADS, DSA_HEAD_DIM)),
        'cache_dsa_idx_k': nrm((DEPTH, DEC_BATCH, PAST_LEN, IDX_DIM)),
        'cache_mla_ckv': nrm((DEPTH, DEC_BATCH, PAST_LEN, MLA_KV_LORA)),
        'cache_mla_krope': nrm((DEPTH, DEC_BATCH, PAST_LEN, MLA_ROPE)),
        'state_pool': nrm((DEPTH, DEC_BATCH, POOL_HIST, BRANCH_W)),
        'state_conv': nrm((DEPTH, DEC_BATCH, CONV_WIDTH - 1, BRANCH_W)),
        'ada_w': nrm((DEPTH, D, 6 * D), 0.5 * D ** -0.5),
        'ada_b': nrm((DEPTH, 6 * D), 0.02),
        'norm_mix_g': gain((DEPTH, D)),
        'norm_ffn_g': gain((DEPTH, D)),
        'w_in': nrm((DEPTH, D, IN_W), D ** -0.5),
        'mla_q_norm_g': gain((DEPTH, MLA_Q_LORA)),
        'mla_kv_norm_g': gain((DEPTH, MLA_KV_LORA)),
        'mla_w_uq': nrm((DEPTH, MLA_Q_LORA, MLA_HEADS * (MLA_NOPE + MLA_ROPE)), MLA_Q_LORA ** -0.5),
        'mla_w_uk': nrm((DEPTH, MLA_KV_LORA, MLA_HEADS, MLA_NOPE), MLA_KV_LORA ** -0.5),
        'mla_w_uv': nrm((DEPTH, MLA_KV_LORA, MLA_HEADS, MLA_V), MLA_KV_LORA ** -0.5),
        'pool_w': nrm((DEPTH, len(POOL_WINDOWS), POOL_GROUP, POOL_GROUP), POOL_GROUP ** -0.5),
        'pool_scale': gain((DEPTH, BRANCH_W), 0.1),
        'conv_w': nrm((DEPTH, CONV_WIDTH, 1, BRANCH_W), CONV_WIDTH ** -0.5),
        'w_gate': nrm((DEPTH, D, N_BRANCH * D), D ** -0.5),
        'b_gate': nrm((DEPTH, N_BRANCH * D), 0.02),
        'w_branch': nrm((DEPTH, N_BRANCH * BRANCH_W, D), BRANCH_W ** -0.5),
        'w_out': nrm((DEPTH, D, D), D ** -0.5),
        'ffn_w1': nrm((N_DENSE, D, FFN_DIM), D ** -0.5),
        'ffn_w3': nrm((N_DENSE, D, FFN_DIM), D ** -0.5),
        'ffn_w2': nrm((N_DENSE, FFN_DIM, D), FFN_DIM ** -0.5),
        'moe_router_w': nrm((N_MOE, D, N_EXPERTS), D ** -0.5),
        'moe_router_b': nrm((N_MOE, N_EXPERTS), 0.01),
        'moe_w1': nrm((N_MOE, N_EXPERTS, D, EXPERT_DIM), D ** -0.5),
        'moe_w3': nrm((N_MOE, N_EXPERTS, D, EXPERT_DIM), D ** -0.5),
        'moe_w2': nrm((N_MOE, N_EXPERTS, EXPERT_DIM, D), EXPERT_DIM ** -0.5),
        'final_norm_g': gain((D,)),
    }


def reference(x_prompt, x_sample, c_prompt, c_sample,
              cache_dsa_k, cache_dsa_v, cache_dsa_idx_k, cache_mla_ckv, cache_mla_krope, state_pool, state_conv,
              ada_w, ada_b, norm_mix_g, norm_ffn_g, w_in, mla_q_norm_g, mla_kv_norm_g,
              mla_w_uq, mla_w_uk, mla_w_uv, pool_w, pool_scale, conv_w,
              w_gate, b_gate, w_branch, w_out,
              ffn_w1, ffn_w3, ffn_w2, moe_router_w, moe_router_b, moe_w1, moe_w3, moe_w2,
              final_norm_g):
    def run(x, c, pasts):
        outs = [[] for _ in range(7)]
        for l in range(DEPTH):
            lp = dict(ada_w=ada_w[l], ada_b=ada_b[l], norm_mix_g=norm_mix_g[l], norm_ffn_g=norm_ffn_g[l],
                      w_in=w_in[l], mla_q_norm_g=mla_q_norm_g[l], mla_kv_norm_g=mla_kv_norm_g[l],
                      mla_w_uq=mla_w_uq[l], mla_w_uk=mla_w_uk[l], mla_w_uv=mla_w_uv[l],
                      pool_w=pool_w[l], pool_scale=pool_scale[l], conv_w=conv_w[l],
                      w_gate=w_gate[l], b_gate=b_gate[l], w_branch=w_branch[l], w_out=w_out[l])
            j = l // 2
            if l % 2 == 0:
                fp = (ffn_w1[j], ffn_w3[j], ffn_w2[j])
            else:
                fp = (moe_router_w[j], moe_router_b[j], moe_w1[j], moe_w3[j], moe_w2[j])
            x, rows = layer_forward(x, c, pasts[l], lp, fp, l % 2 == 1)
            for o, r in zip(outs, rows):
                o.append(r)
        return rmsnorm(x, final_norm_g), [jnp.stack(o) for o in outs]

    prompt_pasts = [empty_past(x_prompt.shape[0], x_prompt.dtype) for _ in range(DEPTH)]
    sample_pasts = [dict(dsa_k=cache_dsa_k[l], dsa_v=cache_dsa_v[l], idx_k=cache_dsa_idx_k[l],
                         ckv=cache_mla_ckv[l], krope=cache_mla_krope[l],
                         pool=state_pool[l], conv=state_conv[l]) for l in range(DEPTH)]
    y_prompt, pn = run(x_prompt, c_prompt, prompt_pasts)
    y_sample, sn = run(x_sample, c_sample, sample_pasts)
    return (y_prompt, y_sample, pn[0], pn[1], pn[2], pn[3], pn[4], pn[5], pn[6],
            sn[0], sn[1], sn[2], sn[3], sn[4], sn[5], sn[6])
```

```python
import functools

import numpy as np
import jax
import jax.numpy as jnp
from jax import lax
from jax.experimental import pallas as pl
from jax.experimental.pallas import tpu as pltpu

F32 = jnp.float32
BF16 = jnp.bfloat16
I32 = jnp.int32

D_MODEL = 1024
CHUNK = 64
N_BRANCH = 4
BRANCH_W = 512
DSA_HEADS = 8
DSA_HEAD_DIM = 64
IDX_HEADS = 4
IDX_DIM = 64
DSA_TOPK_MAX = 256
POOL_WINDOWS = (2, 4, 8, 16)
POOL_GROUP = BRANCH_W // 4
POOL_HIST = 15
CONV_WIDTH = 3
MLA_HEADS = 8
MLA_Q_LORA = 256
MLA_KV_LORA = 256
MLA_NOPE = 64
MLA_ROPE = 32
MLA_V = 64
ROPE_BASE = 10000.0
FFN_DIM = 2816
N_EXPERTS = 8
EPS = 1e-6

DSA_SCALE = DSA_HEAD_DIM ** -0.5
IDX_SCALE = (IDX_HEADS * IDX_DIM) ** -0.5
MLA_SCALE = (MLA_NOPE + MLA_ROPE) ** -0.5
LOG2E = 1.4426950408889634

IN_SPLIT_WIDTHS = (512, 512, 512, 256, 64, 4, 512, 512, 512, 512, 256, 256, 32)
IN_SPLIT_POINTS = tuple(int(sum(IN_SPLIT_WIDTHS[:i + 1])) for i in range(len(IN_SPLIT_WIDTHS) - 1))

C_AQ, C_AK, C_AV, C_PU, C_CB, C_CC, C_CH = 0, 512, 1024, 1536, 2048, 2560, 3072
C_IQ, C_MQ, C_MKV, C_IKW, C_MKR, ZW = 3584, 3840, 4096, 4352, 4480, 4608
IN_TN = 1536

NEG = -1e30
M_INIT = -1e29
INT_MIN = np.int32(-2 ** 31)
V7X_VMEM_LIMIT = 56 * 1024 * 1024


def _cparams(sem):
    return pltpu.CompilerParams(dimension_semantics=sem, vmem_limit_bytes=V7X_VMEM_LIMIT)


def _rms(x, g):
    return x * lax.rsqrt(jnp.mean(x * x, axis=-1, keepdims=True) + EPS) * g


def _sigmoid(x):
    return 1.0 / (1.0 + jnp.exp(-x))


def _const_spec(shape):
    nd = len(shape)
    return pl.BlockSpec(shape, lambda *_: (0,) * nd, pipeline_mode=pl.Buffered(1))


def _ada_body(c_ref, w_ref, b_ref, o_ref):
    c = c_ref[...]
    s = (c * _sigmoid(c)).astype(BF16)
    o_ref[...] = jnp.dot(s, w_ref[...], preferred_element_type=F32) + b_ref[...]


def _ada(c, w_bf, b):
    B, D = c.shape
    n = w_bf.shape[1]
    tn = 1536
    return pl.pallas_call(
        _ada_body,
        grid=(n // tn,),
        in_specs=[pl.BlockSpec((B, D), lambda j: (0, 0)),
                  pl.BlockSpec((D, tn), lambda j: (0, j)),
                  pl.BlockSpec((1, tn), lambda j: (0, j))],
        out_specs=pl.BlockSpec((B, tn), lambda j: (0, j)),
        out_shape=jax.ShapeDtypeStruct((B, n), F32),
        compiler_params=_cparams(("arbitrary",)),
        name="ada_mod",
    )(c, w_bf, b)


def _in_body(x_ref, sc_ref, sh_ref, g_ref, w_ref, z_ref, *rest, bb, tt, emit):
    if emit:
        kb_ref, vt_ref, qt_ref, iqt_ref, iwt_ref, ikb_ref, h_scr = rest
    else:
        (h_scr,) = rest
    j = pl.program_id(1)

    @pl.when(j == 0)
    def _():
        h = _rms(x_ref[...], g_ref[...]) * (1.0 + sc_ref[...]) + sh_ref[...]
        h_scr[...] = h.reshape(bb * tt, D_MODEL).astype(BF16)

    z = jnp.dot(h_scr[...], w_ref[...], preferred_element_type=F32)
    z_ref[...] = z

    if emit:
        @pl.when(j == 0)
        def _():
            kb_ref[...] = z[:, C_AK:C_AK + 512].astype(BF16)
            vt_ref[0, 0] = z[:, C_AV:C_AV + 512].T.astype(BF16)
            qt_ref[0] = z[:, C_AQ:C_AQ + 512].T.astype(BF16)

        @pl.when(j == 2)
        def _():
            o = 2 * IN_TN
            iqt_ref[0] = z[:, C_IQ - o:C_IQ - o + 256].T.astype(BF16)
            ikw = z[:, C_IKW - o:C_IKW - o + 128]
            ikb_ref[...] = ikw.astype(BF16)
            iwt_ref[0] = ikw.T[64:72, :]


def _k_in(x, sc, sh, g, w_bf, *, bb, tt, emit):
    B, T, D = x.shape
    nt = T // tt
    tm = bb * tt
    n_rows = (B // bb) * nt
    N = B * T
    xmap = lambda i, j: (i // nt, i % nt, 0)
    mmap = lambda i, j: (i // nt, 0, 0)
    in_specs = [pl.BlockSpec((bb, tt, D), xmap),
                pl.BlockSpec((bb, 1, D), mmap),
                pl.BlockSpec((bb, 1, D), mmap),
                _const_spec((1, 1, D)),
                pl.BlockSpec((D, IN_TN), lambda i, j: (0, j))]
    out_specs = [pl.BlockSpec((tm, IN_TN), lambda i, j: (i, j))]
    out_shape = [jax.ShapeDtypeStruct((N, ZW), F32)]
    if emit:
        assert bb == 1
        out_specs += [
            pl.BlockSpec((tm, 512), lambda i, j: (i, 0)),
            pl.BlockSpec((1, 1, 512, tt), lambda i, j: (i // nt, i % nt, 0, 0)),
            pl.BlockSpec((1, 512, tt), lambda i, j: (i // nt, 0, i % nt)),
            pl.BlockSpec((1, 256, tt), lambda i, j: (i // nt, 0, i % nt)),
            pl.BlockSpec((1, 8, tt), lambda i, j: (i // nt, 0, i % nt)),
            pl.BlockSpec((tm, 128), lambda i, j: (i, 0)),
        ]
        out_shape += [
            jax.ShapeDtypeStruct((N, 512), BF16),
            jax.ShapeDtypeStruct((B, nt, 512, tt), BF16),
            jax.ShapeDtypeStruct((B, 512, T), BF16),
            jax.ShapeDtypeStruct((B, 256, T), BF16),
            jax.ShapeDtypeStruct((B, 8, T), F32),
            jax.ShapeDtypeStruct((N, 128), BF16),
        ]
    return pl.pallas_call(
        functools.partial(_in_body, bb=bb, tt=tt, emit=emit),
        grid=(n_rows, ZW // IN_TN),
        in_specs=in_specs,
        out_specs=out_specs,
        out_shape=out_shape,
        scratch_shapes=[pltpu.VMEM((tm, D), BF16)],
        compiler_params=_cparams(("arbitrary", "arbitrary")),
        name="in_proj",
    )(x, sc, sh, g, w_bf)


def _pc_body(pu_ref, cb_ref, cc_ref, ch_ref, pup_ref, ccp_ref, chp_ref, ph_ref, cvh_ref,
             pw_ref, ps_ref, cw_ref, ob_ref, oc_ref, pho_ref, cho_ref, u_scr, g_scr, *, bb, tt, P):
    t = pl.program_id(1)
    first = t == 0
    pu = pu_ref[...]
    g = cc_ref[...] * ch_ref[...]
    u_scr[:, 0:16, :] = jnp.where(first, ph_ref[...], pup_ref[...])
    u_scr[:, 16:16 + tt, :] = pu
    g_scr[:, 0:16, :] = jnp.where(first, cvh_ref[...], ccp_ref[...] * chp_ref[...])
    g_scr[:, 16:16 + tt, :] = g

    pos = P + t * tt + lax.broadcasted_iota(I32, (1, tt, 1), 1)
    for gi, win in enumerate(POOL_WINDOWS):
        lo = gi * POOL_GROUP
        acc = pu[:, :, lo:lo + POOL_GROUP]
        for j in range(1, win):
            acc = acc + u_scr[:, 16 - j:16 - j + tt, lo:lo + POOL_GROUP]
        cnt = jnp.minimum(pos + 1, win).astype(F32)
        d = acc / cnt - pu[:, :, lo:lo + POOL_GROUP]
        y = jnp.dot(d.reshape(bb * tt, POOL_GROUP).astype(BF16), pw_ref[gi], preferred_element_type=F32)
        ob_ref[:, :, lo:lo + POOL_GROUP] = (y * ps_ref[:, lo:lo + POOL_GROUP]).reshape(bb, tt, POOL_GROUP)

    cw = cw_ref[...]
    conv = (cw[0:1, :] * g_scr[:, 14:14 + tt, :] + cw[1:2, :] * g_scr[:, 15:15 + tt, :] + cw[2:3, :] * g)
    oc_ref[...] = cb_ref[...] * conv
    pho_ref[...] = u_scr[:, tt:tt + 16, :]
    cho_ref[...] = g_scr[:, tt:tt + 16, :]


def _k_pc(z3, ph16, cvh16, pw_bf, ps, cw, *, bb, tt, P):
    B, T, _ = z3.shape
    nt = T // tt
    r = tt // 16

    def cur(c):
        return pl.BlockSpec((bb, tt, 512), lambda b, t: (b, t, c // 512))

    def prev(c):
        return pl.BlockSpec((bb, 16, 512), lambda b, t: (b, jnp.maximum(t * r - 1, 0), c // 512))

    hist = pl.BlockSpec((bb, 16, 512), lambda b, t: (b, 0, 0))
    return pl.pallas_call(
        functools.partial(_pc_body, bb=bb, tt=tt, P=P),
        grid=(B // bb, nt),
        in_specs=[cur(C_PU), cur(C_CB), cur(C_CC), cur(C_CH), prev(C_PU), prev(C_CC), prev(C_CH),
                  hist, hist, _const_spec((4, POOL_GROUP, POOL_GROUP)), _const_spec((1, 512)),
                  _const_spec((8, 512))],
        out_specs=[pl.BlockSpec((bb, tt, 512), lambda b, t: (b, t, 0)),
                   pl.BlockSpec((bb, tt, 512), lambda b, t: (b, t, 0)), hist, hist],
        out_shape=[jax.ShapeDtypeStruct((B, T, 512), F32), jax.ShapeDtypeStruct((B, T, 512), F32),
                   jax.ShapeDtypeStruct((B, 16, 512), F32), jax.ShapeDtypeStruct((B, 16, 512), F32)],
        scratch_shapes=[pltpu.VMEM((bb, tt + 16, 512), F32), pltpu.VMEM((bb, tt + 16, 512), F32)],
        compiler_params=_cparams(("arbitrary", "arbitrary")),
        name="pool_conv",
    )(z3, z3, z3, z3, z3, z3, z3, ph16, cvh16, pw_bf, ps, cw)


def _mp_body(mq_ref, mkv_ref, mkr_ref, cos_ref, sin_ref, gq_ref, gkv_ref, wuq_ref, wuk_ref,
             ql_ref, qr_ref, ckv_ref, ckvb_ref, kr_ref, krb_ref):
    a = _rms(mq_ref[...], gq_ref[...]).astype(BF16)
    cq = jnp.dot(a, wuq_ref[...], preferred_element_type=F32)
    cos = cos_ref[...]
    sin = sin_ref[...]
    qr_ref[...] = (cq[:, 512:768] * cos + cq[:, 768:1024] * sin).astype(BF16)
    for h in range(MLA_HEADS):
        qn = cq[:, h * MLA_NOPE:(h + 1) * MLA_NOPE].astype(BF16)
        ql_ref[:, h * MLA_KV_LORA:(h + 1) * MLA_KV_LORA] = jnp.dot(
            qn, wuk_ref[h], preferred_element_type=F32).astype(BF16)
    ckv = _rms(mkv_ref[...], gkv_ref[...])
    ckv_ref[...] = ckv
    ckvb_ref[...] = ckv.astype(BF16)
    mkr = mkr_ref[...]
    kr = mkr[:, 0:32] * cos[:, 0:32] + mkr[:, 32:64] * sin[:, 0:32]
    kr_ref[...] = kr
    krb_ref[...] = kr.astype(BF16)


def _k_mlaprep(z, cos_t, sin_t, gq, gkv, wuq_bf, wuk_bf, *, tm):
    N = z.shape[0]
    nr = cos_t.shape[0] // tm
    tab = pl.BlockSpec((tm, 256), lambda i: (i % nr, 0))
    row = lambda w: pl.BlockSpec((tm, w), lambda i: (i, 0))
    return pl.pallas_call(
        _mp_body,
        grid=(N // tm,),
        in_specs=[pl.BlockSpec((tm, 256), lambda i: (i, C_MQ // 256)),
                  pl.BlockSpec((tm, 256), lambda i: (i, C_MKV // 256)),
                  pl.BlockSpec((tm, 128), lambda i: (i, C_MKR // 128)),
                  tab, tab, _const_spec((1, 256)), _const_spec((1, 256)),
                  _const_spec((256, 1024)), _const_spec((MLA_HEADS, MLA_NOPE, MLA_KV_LORA))],
        out_specs=[row(2048), row(256), row(256), row(256), row(32), row(32)],
        out_shape=[jax.ShapeDtypeStruct((N, 2048), BF16), jax.ShapeDtypeStruct((N, 256), BF16),
                   jax.ShapeDtypeStruct((N, 256), F32), jax.ShapeDtypeStruct((N, 256), BF16),
                   jax.ShapeDtypeStruct((N, 32), F32), jax.ShapeDtypeStruct((N, 32), BF16)],
        compiler_params=_cparams(("arbitrary",)),
        name="mla_prep",
    )(z, z, z, cos_t, sin_t, gq, gkv, wuq_bf, wuk_bf)


def _dot_nt(a, b):
    return lax.dot_general(a, b, (((1,), (1,)), ((), ())), preferred_element_type=F32)


def _mla_body(ql_ref, qr_ref, ckv_ref, kr_ref, wuv_ref, o_ref, m_scr, l_scr, acc_scr,
              *, tq, P, L, KC, nkc_total):
    qi = pl.program_id(1)
    q0 = P + qi * tq
    rows = MLA_HEADS * tq
    ql = ql_ref[0]
    qr = qr_ref[0]
    Q = jnp.concatenate([ql[:, h * 256:(h + 1) * 256] for h in range(MLA_HEADS)], axis=0)
    R = jnp.concatenate([qr[:, h * 32:(h + 1) * 32] for h in range(MLA_HEADS)], axis=0)
    qpos = q0 + (lax.broadcasted_iota(I32, (rows, 1), 0) & (tq - 1))
    lim = jnp.minimum(((qpos >> 6) + 1) << 6, L)
    max_cend = (((q0 + tq - 1) >> 6) + 1) << 6
    nkc = jnp.minimum((max_cend + KC - 1) // KC, nkc_total)
    c = MLA_SCALE * LOG2E

    m_scr[...] = jnp.full((rows, 1), M_INIT, F32)
    l_scr[...] = jnp.zeros((rows, 1), F32)
    acc_scr[...] = jnp.zeros((rows, MLA_KV_LORA), F32)

    def step(kc, carry):
        k0 = pl.multiple_of(kc * KC, KC)
        ck = ckv_ref[0, pl.ds(k0, KC), :]
        kr = kr_ref[0, pl.ds(k0, KC), :]
        s = _dot_nt(Q, ck) + _dot_nt(R, kr)
        kpos = k0 + lax.broadcasted_iota(I32, (1, KC), 1)
        s = jnp.where(kpos < lim, s, NEG)
        m_old = m_scr[...]
        m_new = jnp.maximum(m_old, jnp.max(s, axis=-1, keepdims=True))
        alpha = jnp.exp2((m_old - m_new) * c)
        p = jnp.exp2((s - m_new) * c)
        l_scr[...] = alpha * l_scr[...] + jnp.sum(p, axis=-1, keepdims=True)
        acc_scr[...] = alpha * acc_scr[...] + jnp.dot(p.astype(BF16), ck, preferred_element_type=F32)
        m_scr[...] = m_new
        return carry

    lax.fori_loop(0, nkc, step, 0)
    o = (acc_scr[...] / l_scr[...]).astype(BF16)
    for h in range(MLA_HEADS):
        o_ref[0, :, h * MLA_V:(h + 1) * MLA_V] = jnp.dot(
            o[h * tq:(h + 1) * tq], wuv_ref[h], preferred_element_type=F32)


def _k_mla(ql3, qr3, ckvb, krb, wuv_bf, *, tq, P, L, KC):
    B, T, _ = ql3.shape
    Lp = ckvb.shape[1]
    nkc_total = Lp // KC
    rows = MLA_HEADS * tq
    return pl.pallas_call(
        functools.partial(_mla_body, tq=tq, P=P, L=L, KC=KC, nkc_total=nkc_total),
        grid=(B, T // tq),
        in_specs=[pl.BlockSpec((1, tq, 2048), lambda b, q: (b, q, 0)),
                  pl.BlockSpec((1, tq, 256), lambda b, q: (b, q, 0)),
                  pl.BlockSpec((1, Lp, 256), lambda b, q: (b, 0, 0)),
                  pl.BlockSpec((1, Lp, 32), lambda b, q: (b, 0, 0)),
                  _const_spec((MLA_HEADS, MLA_KV_LORA, MLA_V))],
        out_specs=pl.BlockSpec((1, tq, 512), lambda b, q: (b, q, 0)),
        out_shape=jax.ShapeDtypeStruct((B, T, 512), F32),
        scratch_shapes=[pltpu.VMEM((rows, 1), F32), pltpu.VMEM((rows, 1), F32),
                        pltpu.VMEM((rows, MLA_KV_LORA), F32)],
        compiler_params=_cparams(("arbitrary", "arbitrary")),
        name="mla_attn",
    )(ql3, qr3, ckvb, krb, wuv_bf)


def _fold8(x):
    n, w = x.shape
    return jnp.sum(x.reshape(n // 8, 8, w), axis=0)


def _dsa_body(qt_ref, iqt_ref, iwt_ref, kb_ref, vt_ref, ikb_ref, o_ref, s_scr,
              *, tq, P, L, KC, nkc_total, topk):
    qi = pl.program_id(1)
    q0 = P + qi * tq
    qpos = q0 + lax.broadcasted_iota(I32, (1, tq), 1)
    lim = jnp.minimum(((qpos >> 6) + 1) << 6, L)
    keff = jnp.minimum(lim, topk).astype(F32)
    max_cend = (((q0 + tq - 1) >> 6) + 1) << 6
    nkc = jnp.minimum((max_cend + KC - 1) // KC, nkc_total)

    iqt = iqt_ref[0]
    iwt = iwt_ref[0]
    zpad = jnp.zeros((IDX_DIM, tq), BF16)
    iq_h = [jnp.concatenate([iqt[h * IDX_DIM:(h + 1) * IDX_DIM], zpad], axis=0) for h in range(IDX_HEADS)]

    def score_step(kc, carry):
        k0 = pl.multiple_of(kc * KC, KC)
        ik = ikb_ref[0, pl.ds(k0, KC), :]
        s = jnp.zeros((KC, tq), F32)
        for h in range(IDX_HEADS):
            s = s + iwt[h:h + 1, :] * jnp.maximum(jnp.dot(ik, iq_h[h], preferred_element_type=F32), 0.0)
        s = s * IDX_SCALE
        kpos = k0 + lax.broadcasted_iota(I32, (KC, 1), 0)
        s = jnp.where(kpos < lim, s, -jnp.inf)
        b = lax.bitcast_convert_type(s, I32)
        b = jnp.where(b == INT_MIN, 0, b)
        s_scr[kc] = b ^ ((b >> 31) & np.int32(0x7FFFFFFF))
        return carry

    lax.fori_loop(0, nkc, score_step, 0)

    def bit_step(i, thr):
        cand = thr + jnp.left_shift(jnp.int32(1), 31 - i)

        def cnt_step(kc, acc):
            return acc + _fold8(jnp.where(s_scr[kc] >= cand, 1.0, 0.0))

        cnt = jnp.sum(lax.fori_loop(0, nkc, cnt_step, jnp.zeros((8, tq), F32)), axis=0, keepdims=True)
        return jnp.where(cnt >= keff, cand, thr)

    thr = lax.fori_loop(0, 32, bit_step, jnp.full((1, tq), INT_MIN, I32))

    def tie_cnt_step(kc, carry):
        key = s_scr[kc]
        return (carry[0] + _fold8(jnp.where(key > thr, 1.0, 0.0)),
                carry[1] + _fold8(jnp.where(key == thr, 1.0, 0.0)))

    z8 = jnp.zeros((8, tq), F32)
    n_gt, n_eq = lax.fori_loop(0, nkc, tie_cnt_step, (z8, z8))
    need = keff - jnp.sum(n_gt, axis=0, keepdims=True)
    surplus = jnp.max(jnp.sum(n_eq, axis=0, keepdims=True) - need)

    @pl.when(surplus < 0.5)
    def _():
        def fast_step(kc, carry):
            s_scr[kc] = lax.bitcast_convert_type(jnp.where(s_scr[kc] >= thr, 0.0, NEG), I32)
            return carry
        lax.fori_loop(0, nkc, fast_step, 0)

    @pl.when(surplus >= 0.5)
    def _():
        tri = jnp.where(lax.broadcasted_iota(I32, (KC, KC), 1) < lax.broadcasted_iota(I32, (KC, KC), 0),
                        1.0, 0.0).astype(BF16)

        def slow_step(kc, seen):
            key = s_scr[kc]
            eq = key == thr
            eqf = jnp.where(eq, 1.0, 0.0)
            rank = seen + jnp.dot(tri, eqf.astype(BF16), preferred_element_type=F32)
            bias = jnp.where(key > thr, 0.0, jnp.where(eq, jnp.where(rank < need, 0.0, NEG), NEG))
            s_scr[kc] = lax.bitcast_convert_type(bias, I32)
            return seen + jnp.sum(_fold8(eqf), axis=0, keepdims=True)
        lax.fori_loop(0, nkc, slow_step, jnp.zeros((1, tq), F32))

    c = DSA_SCALE * LOG2E
    qt = qt_ref[0]
    zq = jnp.zeros((DSA_HEAD_DIM, tq), BF16)
    outs = []
    for hp in range(DSA_HEADS // 2):
        r0 = hp * 2 * DSA_HEAD_DIM
        qa = qt[r0:r0 + DSA_HEAD_DIM]
        qb = qt[r0 + DSA_HEAD_DIM:r0 + 2 * DSA_HEAD_DIM]
        w = jnp.concatenate([jnp.concatenate([qa, zq], axis=1), jnp.concatenate([zq, qb], axis=1)], axis=0)

        def att_step(kc, carry, r0=r0, w=w):
            k0 = pl.multiple_of(kc * KC, KC)
            kblk = kb_ref[0, pl.ds(k0, KC), r0:r0 + 2 * DSA_HEAD_DIM]
            lg = jnp.dot(kblk, w, preferred_element_type=F32)
            bias = lax.bitcast_convert_type(s_scr[kc], F32)
            vt = vt_ref[0, kc]
            new = []
            for e in range(2):
                m_old, l_old, acc = carry[e]
                x = lg[:, e * tq:(e + 1) * tq] + bias
                m_new = jnp.maximum(m_old, jnp.max(x, axis=0, keepdims=True))
                alpha = jnp.exp2((m_old - m_new) * c)
                p = jnp.exp2((x - m_new) * c)
                l_new = alpha * l_old + jnp.sum(p, axis=0, keepdims=True)
                vh = vt[r0 + e * DSA_HEAD_DIM:r0 + (e + 1) * DSA_HEAD_DIM, :]
                acc = alpha * acc + jnp.dot(vh, p.astype(BF16), preferred_element_type=F32)
                new.append((m_new, l_new, acc))
            return tuple(new)

        init = tuple((jnp.full((1, tq), M_INIT, F32), jnp.zeros((1, tq), F32),
                      jnp.zeros((DSA_HEAD_DIM, tq), F32)) for _ in range(2))
        res = lax.fori_loop(0, nkc, att_step, init)
        for e in range(2):
            outs.append(res[e][2] / res[e][1])
    o_ref[0] = jnp.concatenate(outs, axis=0).T


def _k_dsa(qt, iqt, iwt, kb3, vt4, ikb3, *, tq, P, L, KC, topk):
    B, _, T = qt.shape
    Lp = kb3.shape[1]
    nkc_total = Lp // KC
    return pl.pallas_call(
        functools.partial(_dsa_body, tq=tq, P=P, L=L, KC=KC, nkc_total=nkc_total, topk=topk),
        grid=(B, T // tq),
        in_specs=[pl.BlockSpec((1, 512, tq), lambda b, q: (b, 0, q)),
                  pl.BlockSpec((1, 256, tq), lambda b, q: (b, 0, q)),
                  pl.BlockSpec((1, 8, tq), lambda b, q: (b, 0, q)),
                  pl.BlockSpec((1, Lp, 512), lambda b, q: (b, 0, 0)),
                  pl.BlockSpec((1, nkc_total, 512, KC), lambda b, q: (b, 0, 0, 0)),
                  pl.BlockSpec((1, Lp, 128), lambda b, q: (b, 0, 0))],
        out_specs=pl.BlockSpec((1, tq, 512), lambda b, q: (b, q, 0)),
        out_shape=jax.ShapeDtypeStruct((B, T, 512), F32),
        scratch_shapes=[pltpu.VMEM((nkc_total, KC, tq), I32)],
        compiler_params=_cparams(("arbitrary", "arbitrary")),
        name="dsa_attn",
    )(qt, iqt, iwt, kb3, vt4, ikb3)


def _merge_body(x_ref, sc_ref, sh_ref, g1_ref, ng_ref, oa_ref, ob_ref, oc_ref, od_ref,
                wg_ref, bg_ref, wb_ref, wo_ref, o_ref, *, bb, tt):
    x = x_ref[...]
    tm = bb * tt
    h = (_rms(x, ng_ref[...]) * (1.0 + sc_ref[...]) + sh_ref[...]).reshape(tm, D_MODEL).astype(BF16)
    merged = jnp.zeros((tm, D_MODEL), F32)
    for i, oref in enumerate((oa_ref, ob_ref, oc_ref, od_ref)):
        gate = _sigmoid(jnp.dot(h, wg_ref[:, i * D_MODEL:(i + 1) * D_MODEL], preferred_element_type=F32)
                        + bg_ref[:, i * D_MODEL:(i + 1) * D_MODEL])
        br = jnp.dot(oref[...].astype(BF16), wb_ref[i * BRANCH_W:(i + 1) * BRANCH_W, :],
                     preferred_element_type=F32)
        merged = merged + gate * br
    y = jnp.dot(merged.astype(BF16), wo_ref[...], preferred_element_type=F32)
    o_ref[...] = x + g1_ref[...] * y.reshape(bb, tt, D_MODEL)


def _k_merge(x, sc, sh, g1, ng, oa, ob, oc, od, wg_bf, bg, wb_bf, wo_bf, *, bb, tt):
    B, T, D = x.shape
    nt = T // tt
    tm = bb * tt
    xs = pl.BlockSpec((bb, tt, D), lambda i: (i // nt, i % nt, 0))
    ms = pl.BlockSpec((bb, 1, D), lambda i: (i // nt, 0, 0))
    os_ = pl.BlockSpec((tm, 512), lambda i: (i, 0))
    return pl.pallas_call(
        functools.partial(_merge_body, bb=bb, tt=tt),
        grid=((B // bb) * nt,),
        in_specs=[xs, ms, ms, ms, _const_spec((1, 1, D)), os_, os_, os_, os_,
                  _const_spec((D, 4 * D)), _const_spec((1, 4 * D)), _const_spec((4 * BRANCH_W, D)),
                  _const_spec((D, D))],
        out_specs=xs,
        out_shape=jax.ShapeDtypeStruct((B, T, D), F32),
        compiler_params=_cparams(("arbitrary",)),
        name="merge_out",
    )(x, sc, sh, g1, ng, oa, ob, oc, od, wg_bf, bg, wb_bf, wo_bf)


FC = FFN_DIM // 2


def _swiglu_chunk(h, w1, w3, w2):
    a = jnp.dot(h, w1, preferred_element_type=F32)
    b = jnp.dot(h, w3, preferred_element_type=F32)
    act = (a * _sigmoid(a) * b).astype(BF16)
    return jnp.dot(act, w2, preferred_element_type=F32)


def _finish(x, g2, f, fg_ref, o_ref, bb, tt):
    y = x + g2 * f.reshape(bb, tt, D_MODEL)
    if fg_ref is not None:
        y = _rms(y, fg_ref[...])
    o_ref[...] = y


def _ffn_body(x_ref, sc_ref, sh_ref, g2_ref, ng_ref, w1_ref, w3_ref, w2_ref, *rest, bb, tt, final):
    fg_ref, o_ref = rest if final else (None, rest[0])
    x = x_ref[...]
    tm = bb * tt
    h = (_rms(x, ng_ref[...]) * (1.0 + sc_ref[...]) + sh_ref[...]).reshape(tm, D_MODEL).astype(BF16)
    f = jnp.zeros((tm, D_MODEL), F32)
    for ci in range(FFN_DIM // FC):
        f = f + _swiglu_chunk(h, w1_ref[:, ci * FC:(ci + 1) * FC], w3_ref[:, ci * FC:(ci + 1) * FC],
                              w2_ref[ci * FC:(ci + 1) * FC, :])
    _finish(x, g2_ref[...], f, fg_ref, o_ref, bb, tt)


def _k_ffn(x, sc, sh, g2, ng, w1_bf, w3_bf, w2_bf, fg, *, bb, tt):
    B, T, D = x.shape
    nt = T // tt
    final = fg is not None
    xs = pl.BlockSpec((bb, tt, D), lambda i: (i // nt, i % nt, 0))
    ms = pl.BlockSpec((bb, 1, D), lambda i: (i // nt, 0, 0))
    in_specs = [xs, ms, ms, ms, _const_spec((1, 1, D)), _const_spec((D, FFN_DIM)),
                _const_spec((D, FFN_DIM)), _const_spec((FFN_DIM, D))]
    args = [x, sc, sh, g2, ng, w1_bf, w3_bf, w2_bf]
    if final:
        in_specs.append(_const_spec((1, 1, D)))
        args.append(fg)
    return pl.pallas_call(
        functools.partial(_ffn_body, bb=bb, tt=tt, final=final),
        grid=((B // bb) * nt,),
        in_specs=in_specs,
        out_specs=xs,
        out_shape=jax.ShapeDtypeStruct((B, T, D), F32),
        compiler_params=_cparams(("arbitrary",)),
        name="ffn_dense",
    )(*args)


def _moe_body(x_ref, sc_ref, sh_ref, g2_ref, ng_ref, rw_ref, rb_ref, w1_ref, w3_ref, w2_ref, *rest,
              bb, tt, final):
    if final:
        fg_ref, o_ref, h_scr, gate_scr, acc_scr = rest
    else:
        fg_ref = None
        o_ref, h_scr, gate_scr, acc_scr = rest
    e = pl.program_id(1)
    ci = pl.program_id(2)
    tm = bb * tt
    lane = lax.broadcasted_iota(I32, (tm, 128), 1)

    @pl.when((e == 0) & (ci == 0))
    def _():
        h = (_rms(x_ref[...], ng_ref[...]) * (1.0 + sc_ref[...]) + sh_ref[...]).reshape(tm, D_MODEL)
        h_scr[...] = h.astype(BF16)
        logits = jnp.dot(h, rw_ref[...], preferred_element_type=F32,
                         precision=lax.Precision.HIGHEST) + rb_ref[...]
        m1 = jnp.max(logits, axis=-1, keepdims=True)
        i1 = jnp.min(jnp.where(logits == m1, lane, 128), axis=-1, keepdims=True)
        rest_l = jnp.where(lane == i1, NEG, logits)
        m2 = jnp.max(rest_l, axis=-1, keepdims=True)
        i2 = jnp.min(jnp.where(rest_l == m2, lane, 128), axis=-1, keepdims=True)
        e2 = jnp.exp(m2 - m1)
        den = 1.0 + e2
        gate_scr[...] = jnp.where(lane == i1, 1.0 / den, jnp.where(lane == i2, e2 / den, 0.0))
        acc_scr[...] = jnp.zeros((tm, D_MODEL), F32)

    ge = jnp.sum(jnp.where(lane == e, gate_scr[...], 0.0), axis=-1, keepdims=True)
    acc_scr[...] += ge * _swiglu_chunk(h_scr[...], w1_ref[0], w3_ref[0], w2_ref[0])

    @pl.when((e == N_EXPERTS - 1) & (ci == FFN_DIM // FC - 1))
    def _():
        _finish(x_ref[...], g2_ref[...], acc_scr[...], fg_ref, o_ref, bb, tt)


def _k_moe(x, sc, sh, g2, ng, rw_pad, rb_pad, w1_bf, w3_bf, w2_bf, fg, *, bb, tt):
    B, T, D = x.shape
    nt = T // tt
    tm = bb * tt
    final = fg is not None
    xs = pl.BlockSpec((bb, tt, D), lambda i, e, c: (i // nt, i % nt, 0))
    ms = pl.BlockSpec((bb, 1, D), lambda i, e, c: (i // nt, 0, 0))
    in_specs = [xs, ms, ms, ms, _const_spec((1, 1, D)), _const_spec((D, 128)), _const_spec((1, 128)),
                pl.BlockSpec((1, D, FC), lambda i, e, c: (e, 0, c)),
                pl.BlockSpec((1, D, FC), lambda i, e, c: (e, 0, c)),
                pl.BlockSpec((1, FC, D), lambda i, e, c: (e, c, 0))]
    args = [x, sc, sh, g2, ng, rw_pad, rb_pad, w1_bf, w3_bf, w2_bf]
    if final:
        in_specs.append(_const_spec((1, 1, D)))
        args.append(fg)
    return pl.pallas_call(
        functools.partial(_moe_body, bb=bb, tt=tt, final=final),
        grid=((B // bb) * nt, N_EXPERTS, FFN_DIM // FC),
        in_specs=in_specs,
        out_specs=xs,
        out_shape=jax.ShapeDtypeStruct((B, T, D), F32),
        scratch_shapes=[pltpu.VMEM((tm, D), BF16), pltpu.VMEM((tm, 128), F32), pltpu.VMEM((tm, D), F32)],
        compiler_params=_cparams(("arbitrary", "arbitrary", "arbitrary")),
        name="ffn_moe",
    )(*args)


def _prep_w_in(w):
    aq, ak, av, iq, ik, iw, pu, cb, cc, ch, mq, mkv, mkr = jnp.split(w, IN_SPLIT_POINTS, axis=1)
    d = w.shape[0]
    mkr_sw = jnp.concatenate([mkr[:, 16:], mkr[:, :16]], axis=1)
    return jnp.concatenate([aq, ak, av, pu, cb, cc, ch, iq, mq, mkv, ik, iw, jnp.zeros((d, 60), w.dtype),
                            mkr, mkr_sw, jnp.zeros((d, 64), w.dtype)], axis=1).astype(BF16)


def _prep_w_uq(w):
    w3 = w.reshape(MLA_Q_LORA, MLA_HEADS, MLA_NOPE + MLA_ROPE)
    nope = w3[:, :, :MLA_NOPE].reshape(MLA_Q_LORA, MLA_HEADS * MLA_NOPE)
    rope = w3[:, :, MLA_NOPE:]
    rope_sw = jnp.concatenate([rope[:, :, 16:], rope[:, :, :16]], axis=-1)
    return jnp.concatenate([nope, rope.reshape(MLA_Q_LORA, -1), rope_sw.reshape(MLA_Q_LORA, -1)],
                           axis=1).astype(BF16)


def _rope_tables(pos):
    half = MLA_ROPE // 2
    inv = 1.0 / (ROPE_BASE ** (jnp.arange(half, dtype=F32) / half))
    ang = pos.astype(F32)[:, None] * inv[None, :]
    cos, sin = jnp.cos(ang), jnp.sin(ang)
    cos2 = jnp.tile(jnp.concatenate([cos, cos], axis=-1), (1, MLA_HEADS))
    sin2 = jnp.tile(jnp.concatenate([-sin, sin], axis=-1), (1, MLA_HEADS))
    return cos2, sin2


def _layer(x, mod, past, lw, fw, use_moe, final_g, cfg):
    B, T, D = x.shape
    bb, tt, KC, tq_dsa, tq_mla = cfg["bb"], cfg["tt"], cfg["KC"], cfg["tq_dsa"], cfg["tq_mla"]
    N = B * T
    P = 0 if past is None else past["dsa_k"].shape[1]
    L = P + T
    sh1, sc1, g1, sh2, sc2, g2 = [m.reshape(B, 1, D) for m in jnp.split(mod, 6, axis=-1)]
    ng1 = lw["norm_mix_g"].reshape(1, 1, D)
    ng2 = lw["norm_ffn_g"].reshape(1, 1, D)

    emit = past is None
    res = _k_in(x, sc1, sh1, ng1, lw["w_in"], bb=bb, tt=tt, emit=emit)
    z = res[0]
    z3 = z.reshape(B, T, ZW)
    k_new = z[:, C_AK:C_AK + 512].reshape(B, T, DSA_HEADS, DSA_HEAD_DIM)
    v_new = z[:, C_AV:C_AV + 512].reshape(B, T, DSA_HEADS, DSA_HEAD_DIM)
    ik_new = z3[:, :, C_IKW:C_IKW + IDX_DIM]

    if past is None:
        ph16 = jnp.zeros((B, 16, 512), F32)
        cvh16 = ph16
    else:
        ph16 = jnp.pad(past["pool"], ((0, 0), (1, 0), (0, 0)))
        cvh16 = jnp.pad(past["conv"], ((0, 0), (16 - (CONV_WIDTH - 1), 0), (0, 0)))
    o_b, o_c, ph_o, cvh_o = _k_pc(z3, ph16, cvh16, lw["pool_w"], lw["pool_scale"], lw["conv_w"],
                                  bb=bb, tt=tt, P=P)
    pool_hist = ph_o[:, 1:]
    conv_hist = cvh_o[:, 16 - (CONV_WIDTH - 1):]

    cos2, sin2 = _rope_tables(P + jnp.arange(T))
    if bb > 1:
        cos2, sin2 = jnp.tile(cos2, (bb, 1)), jnp.tile(sin2, (bb, 1))
    ql, qr, ckv_new, ckvb, kr_new, krb = _k_mlaprep(z, cos2, sin2, lw["mla_q_norm_g"], lw["mla_kv_norm_g"],
                                                    lw["mla_w_uq"], lw["mla_w_uk"], tm=bb * tt)

    topk = min(DSA_TOPK_MAX, L // 4)
    if past is None:
        kb3 = res[1].reshape(B, T, 512)
        vt4, qt, iqt, iwt = res[2], res[3], res[4], res[5]
        ikb3 = res[6].reshape(B, T, 128)
        ckvb3 = ckvb.reshape(B, T, 256)
        krb3 = krb.reshape(B, T, 32)
        Tq = T
    else:
        Lp = -(-L // KC) * KC
        padk = ((0, 0), (0, Lp - L), (0, 0))
        k_all = jnp.concatenate([past["dsa_k"].reshape(B, P, 512), k_new.reshape(B, T, 512)], axis=1)
        v_all = jnp.concatenate([past["dsa_v"].reshape(B, P, 512), v_new.reshape(B, T, 512)], axis=1)
        kb3 = jnp.pad(k_all.astype(BF16), padk)
        vt4 = jnp.pad(v_all.astype(BF16), padk).reshape(B, Lp // KC, KC, 512).transpose(0, 1, 3, 2)
        ik_all = jnp.concatenate([past["idx_k"], ik_new], axis=1).astype(BF16)
        ikb3 = jnp.pad(ik_all, ((0, 0), (0, Lp - L), (0, 128 - IDX_DIM)))
        Tq = tq_dsa
        padq = ((0, 0), (0, 0), (0, Tq - T))
        qt = jnp.pad(z3[:, :, C_AQ:C_AQ + 512].transpose(0, 2, 1).astype(BF16), padq)
        iqt = jnp.pad(z3[:, :, C_IQ:C_IQ + 256].transpose(0, 2, 1).astype(BF16), padq)
        iwt = jnp.pad(z3[:, :, C_IKW + 64:C_IKW + 72].transpose(0, 2, 1), padq)
        ckvb3 = jnp.pad(jnp.concatenate([past["ckv"].astype(BF16), ckvb.reshape(B, T, 256)], axis=1), padk)
        krb3 = jnp.pad(jnp.concatenate([past["krope"].astype(BF16), krb.reshape(B, T, 32)], axis=1), padk)

    o_a = _k_dsa(qt, iqt, iwt, kb3, vt4, ikb3, tq=tq_dsa, P=P, L=L, KC=KC, topk=topk)
    o_a = o_a[:, :T].reshape(N, 512)
    o_d = _k_mla(ql.reshape(B, T, 2048), qr.reshape(B, T, 256), ckvb3, krb3, lw["mla_w_uv"],
                 tq=tq_mla, P=P, L=L, KC=KC).reshape(N, 512)

    x1 = _k_merge(x, sc1, sh1, g1, ng1, o_a, o_b.reshape(N, 512), o_c.reshape(N, 512), o_d,
                  lw["w_gate"], lw["b_gate"], lw["w_branch"], lw["w_out"], bb=bb, tt=tt)
    fg = None if final_g is None else final_g.reshape(1, 1, D)
    if use_moe:
        x2 = _k_moe(x1, sc2, sh2, g2, ng2, fw["rw"], fw["rb"], fw["w1"], fw["w3"], fw["w2"], fg,
                    bb=cfg["bb_moe"], tt=cfg["tt_moe"])
    else:
        x2 = _k_ffn(x1, sc2, sh2, g2, ng2, fw["w1"], fw["w3"], fw["w2"], fg, bb=bb, tt=tt)
    rows = (k_new, v_new, ik_new, ckv_new.reshape(B, T, 256), kr_new.reshape(B, T, 32), pool_hist, conv_hist)
    return x2, rows


def _prep_layer(l, ada_w, ada_b, norm_mix_g, norm_ffn_g, w_in, mla_q_norm_g, mla_kv_norm_g,
                mla_w_uq, mla_w_uk, mla_w_uv, pool_w, pool_scale, conv_w, w_gate, b_gate, w_branch, w_out):
    return dict(
        ada_w=ada_w[l].astype(BF16), ada_b=ada_b[l].reshape(1, -1),
        norm_mix_g=norm_mix_g[l], norm_ffn_g=norm_ffn_g[l],
        w_in=_prep_w_in(w_in[l]),
        mla_q_norm_g=mla_q_norm_g[l].reshape(1, -1), mla_kv_norm_g=mla_kv_norm_g[l].reshape(1, -1),
        mla_w_uq=_prep_w_uq(mla_w_uq[l]),
        mla_w_uk=mla_w_uk[l].transpose(1, 2, 0).astype(BF16),
        mla_w_uv=mla_w_uv[l].transpose(1, 0, 2).astype(BF16),
        pool_w=pool_w[l].astype(BF16), pool_scale=pool_scale[l].reshape(1, -1),
        conv_w=jnp.pad(conv_w[l].reshape(CONV_WIDTH, BRANCH_W), ((0, 8 - CONV_WIDTH), (0, 0))),
        w_gate=w_gate[l].astype(BF16), b_gate=b_gate[l].reshape(1, -1),
        w_branch=w_branch[l].astype(BF16), w_out=w_out[l].astype(BF16))


def _cfg_for(B, T, has_past):
    if has_past:
        return dict(bb=B, tt=T, KC=512, tq_dsa=128, tq_mla=T, bb_moe=B, tt_moe=T)
    t = min(512, T)
    return dict(bb=1, tt=t, KC=t, tq_dsa=min(256, T), tq_mla=min(128, T), bb_moe=1, tt_moe=min(1024, T))


def _run(x, c, pasts, lws, fws, final_norm_g):
    B, T, _ = x.shape
    cfg = _cfg_for(B, T, pasts is not None)
    outs = [[] for _ in range(7)]
    depth = len(lws)
    for l in range(depth):
        lw = lws[l]
        mod = _ada(c, lw["ada_w"], lw["ada_b"])
        x, rows = _layer(x, mod, None if pasts is None else pasts[l], lw, fws[l], l % 2 == 1,
                         final_norm_g if l == depth - 1 else None, cfg)
        for o, r in zip(outs, rows):
            o.append(r)
    return x, [jnp.stack(o) for o in outs]


def kernel(x_prompt, x_sample, c_prompt, c_sample, cache_dsa_k, cache_dsa_v, cache_dsa_idx_k, cache_mla_ckv, cache_mla_krope, state_pool, state_conv, ada_w, ada_b, norm_mix_g, norm_ffn_g, w_in, mla_q_norm_g, mla_kv_norm_g, mla_w_uq, mla_w_uk, mla_w_uv, pool_w, pool_scale, conv_w, w_gate, b_gate, w_branch, w_out, ffn_w1, ffn_w3, ffn_w2, moe_router_w, moe_router_b, moe_w1, moe_w3, moe_w2, final_norm_g):
    depth = ada_w.shape[0]
    lws = [_prep_layer(l, ada_w, ada_b, norm_mix_g, norm_ffn_g, w_in, mla_q_norm_g, mla_kv_norm_g,
                       mla_w_uq, mla_w_uk, mla_w_uv, pool_w, pool_scale, conv_w, w_gate, b_gate,
                       w_branch, w_out) for l in range(depth)]
    fws = []
    for l in range(depth):
        j = l // 2
        if l % 2 == 0:
            fws.append(dict(w1=ffn_w1[j].astype(BF16), w3=ffn_w3[j].astype(BF16), w2=ffn_w2[j].astype(BF16)))
        else:
            fws.append(dict(
                rw=jnp.pad(moe_router_w[j], ((0, 0), (0, 128 - N_EXPERTS))),
                rb=jnp.pad(moe_router_b[j].reshape(1, -1), ((0, 0), (0, 128 - N_EXPERTS)), constant_values=NEG),
                w1=moe_w1[j].astype(BF16), w3=moe_w3[j].astype(BF16), w2=moe_w2[j].astype(BF16)))
    sample_pasts = [dict(dsa_k=cache_dsa_k[l], dsa_v=cache_dsa_v[l], idx_k=cache_dsa_idx_k[l],
                         ckv=cache_mla_ckv[l], krope=cache_mla_krope[l],
                         pool=state_pool[l], conv=state_conv[l]) for l in range(depth)]
    y_prompt, pn = _run(x_prompt, c_prompt, None, lws, fws, final_norm_g)
    y_sample, sn = _run(x_sample, c_sample, sample_pasts, lws, fws, final_norm_g)
    return (y_prompt, y_sample, pn[0], pn[1], pn[2], pn[3], pn[4], pn[5], pn[6],
            sn[0], sn[1], sn[2], sn[3], sn[4], sn[5], sn[6])
```

```python
import functools

import numpy as np
import jax
import jax.numpy as jnp
from jax import lax
from jax.experimental import pallas as pl
from jax.experimental.pallas import tpu as pltpu

F32 = jnp.float32
BF16 = jnp.bfloat16
I32 = jnp.int32

D_MODEL = 1024
CHUNK = 64
N_BRANCH = 4
BRANCH_W = 512
DSA_HEADS = 8
DSA_HEAD_DIM = 64
IDX_HEADS = 4
IDX_DIM = 64
DSA_TOPK_MAX = 256
POOL_WINDOWS = (2, 4, 8, 16)
POOL_GROUP = BRANCH_W // 4
POOL_HIST = 15
CONV_WIDTH = 3
MLA_HEADS = 8
MLA_Q_LORA = 256
MLA_KV_LORA = 256
MLA_NOPE = 64
MLA_ROPE = 32
MLA_V = 64
ROPE_BASE = 10000.0
FFN_DIM = 2816
N_EXPERTS = 8
EPS = 1e-6

DSA_SCALE = DSA_HEAD_DIM ** -0.5
IDX_SCALE = (IDX_HEADS * IDX_DIM) ** -0.5
MLA_SCALE = (MLA_NOPE + MLA_ROPE) ** -0.5
LOG2E = 1.4426950408889634

IN_SPLIT_WIDTHS = (512, 512, 512, 256, 64, 4, 512, 512, 512, 512, 256, 256, 32)
IN_SPLIT_POINTS = tuple(int(sum(IN_SPLIT_WIDTHS[:i + 1])) for i in range(len(IN_SPLIT_WIDTHS) - 1))

C_AQ, C_AK, C_AV, C_PU, C_CB, C_CC, C_CH = 0, 512, 1024, 1536, 2048, 2560, 3072
C_IQ, C_MQ, C_MKV, C_IKW, C_MKR, ZW = 3584, 3840, 4096, 4352, 4480, 4608
IN_TN = 1536

NEG = -1e30
M_INIT = -1e29
INT_MIN = np.int32(-2 ** 31)
V7X_VMEM_LIMIT = 56 * 1024 * 1024


def _cparams(sem):
    return pltpu.CompilerParams(dimension_semantics=sem, vmem_limit_bytes=V7X_VMEM_LIMIT)


def _rms(x, g):
    return x * lax.rsqrt(jnp.mean(x * x, axis=-1, keepdims=True) + EPS) * g


def _sigmoid(x):
    return 1.0 / (1.0 + jnp.exp(-x))


def _const_spec(shape):
    nd = len(shape)
    return pl.BlockSpec(shape, lambda *_: (0,) * nd, pipeline_mode=pl.Buffered(1))


def _ada_body(c_ref, w_ref, b_ref, o_ref):
    c = c_ref[...]
    s = (c * _sigmoid(c)).astype(BF16)
    o_ref[...] = jnp.dot(s, w_ref[...], preferred_element_type=F32) + b_ref[...]


def _ada(c, w_bf, b):
    B, D = c.shape
    n = w_bf.shape[1]
    tn = 1536
    return pl.pallas_call(
        _ada_body,
        grid=(n // tn,),
        in_specs=[pl.BlockSpec((B, D), lambda j: (0, 0)),
                  pl.BlockSpec((D, tn), lambda j: (0, j)),
                  pl.BlockSpec((1, tn), lambda j: (0, j))],
        out_specs=pl.BlockSpec((B, tn), lambda j: (0, j)),
        out_shape=jax.ShapeDtypeStruct((B, n), F32),
        compiler_params=_cparams(("arbitrary",)),
        name="ada_mod",
    )(c, w_bf, b)


def _in_body(x_ref, sc_ref, sh_ref, g_ref, w_ref, z_ref, *rest, bb, tt, emit):
    if emit:
        kb_ref, vt_ref, qt_ref, iqt_ref, iwt_ref, ikb_ref, h_scr = rest
    else:
        (h_scr,) = rest
    j = pl.program_id(1)

    @pl.when(j == 0)
    def _():
        h = _rms(x_ref[...], g_ref[...]) * (1.0 + sc_ref[...]) + sh_ref[...]
        h_scr[...] = h.reshape(bb * tt, D_MODEL).astype(BF16)

    z = jnp.dot(h_scr[...], w_ref[...], preferred_element_type=F32)
    z_ref[...] = z

    if emit:
        @pl.when(j == 0)
        def _():
            kb_ref[...] = z[:, C_AK:C_AK + 512].astype(BF16)
            vt_ref[0, 0] = z[:, C_AV:C_AV + 512].T.astype(BF16)
            qt_ref[0] = z[:, C_AQ:C_AQ + 512].T.astype(BF16)

        @pl.when(j == 2)
        def _():
            o = 2 * IN_TN
            iqt_ref[0] = z[:, C_IQ - o:C_IQ - o + 256].T.astype(BF16)
            ikw = z[:, C_IKW - o:C_IKW - o + 128]
            ikb_ref[...] = ikw.astype(BF16)
            iwt_ref[0] = ikw.T[64:72, :]


def _k_in(x, sc, sh, g, w_bf, *, bb, tt, emit):
    B, T, D = x.shape
    nt = T // tt
    tm = bb * tt
    n_rows = (B // bb) * nt
    N = B * T
    xmap = lambda i, j: (i // nt, i % nt, 0)
    mmap = lambda i, j: (i // nt, 0, 0)
    in_specs = [pl.BlockSpec((bb, tt, D), xmap),
                pl.BlockSpec((bb, 1, D), mmap),
                pl.BlockSpec((bb, 1, D), mmap),
                _const_spec((1, 1, D)),
                pl.BlockSpec((D, IN_TN), lambda i, j: (0, j))]
    out_specs = [pl.BlockSpec((tm, IN_TN), lambda i, j: (i, j))]
    out_shape = [jax.ShapeDtypeStruct((N, ZW), F32)]
    if emit:
        assert bb == 1
        out_specs += [
            pl.BlockSpec((tm, 512), lambda i, j: (i, 0)),
            pl.BlockSpec((1, 1, 512, tt), lambda i, j: (i // nt, i % nt, 0, 0)),
            pl.BlockSpec((1, 512, tt), lambda i, j: (i // nt, 0, i % nt)),
            pl.BlockSpec((1, 256, tt), lambda i, j: (i // nt, 0, i % nt)),
            pl.BlockSpec((1, 8, tt), lambda i, j: (i // nt, 0, i % nt)),
            pl.BlockSpec((tm, 128), lambda i, j: (i, 0)),
        ]
        out_shape += [
            jax.ShapeDtypeStruct((N, 512), BF16),
            jax.ShapeDtypeStruct((B, nt, 512, tt), BF16),
            jax.ShapeDtypeStruct((B, 512, T), BF16),
            jax.ShapeDtypeStruct((B, 256, T), BF16),
            jax.ShapeDtypeStruct((B, 8, T), F32),
            jax.ShapeDtypeStruct((N, 128), BF16),
        ]
    return pl.pallas_call(
        functools.partial(_in_body, bb=bb, tt=tt, emit=emit),
        grid=(n_rows, ZW // IN_TN),
        in_specs=in_specs,
        out_specs=out_specs,
        out_shape=out_shape,
        scratch_shapes=[pltpu.VMEM((tm, D), BF16)],
        compiler_params=_cparams(("arbitrary", "arbitrary")),
        name="in_proj",
    )(x, sc, sh, g, w_bf)


def _pc_body(pu_ref, cb_ref, cc_ref, ch_ref, pup_ref, ccp_ref, chp_ref, ph_ref, cvh_ref,
             pw_ref, ps_ref, cw_ref, ob_ref, oc_ref, pho_ref, cho_ref, u_scr, g_scr, *, bb, tt, P):
    t = pl.program_id(1)
    first = t == 0
    pu = pu_ref[...]
    g = cc_ref[...] * ch_ref[...]
    u_scr[:, 0:16, :] = jnp.where(first, ph_ref[...], pup_ref[...])
    u_scr[:, 16:16 + tt, :] = pu
    g_scr[:, 0:16, :] = jnp.where(first, cvh_ref[...], ccp_ref[...] * chp_ref[...])
    g_scr[:, 16:16 + tt, :] = g

    pos = P + t * tt + lax.broadcasted_iota(I32, (1, tt, 1), 1)
    for gi, win in enumerate(POOL_WINDOWS):
        lo = gi * POOL_GROUP
        acc = pu[:, :, lo:lo + POOL_GROUP]
        for j in range(1, win):
            acc = acc + u_scr[:, 16 - j:16 - j + tt, lo:lo + POOL_GROUP]
        cnt = jnp.minimum(pos + 1, win).astype(F32)
        d = acc / cnt - pu[:, :, lo:lo + POOL_GROUP]
        y = jnp.dot(d.reshape(bb * tt, POOL_GROUP).astype(BF16), pw_ref[gi], preferred_element_type=F32)
        ob_ref[:, :, lo:lo + POOL_GROUP] = (y * ps_ref[:, lo:lo + POOL_GROUP]).reshape(bb, tt, POOL_GROUP)

    cw = cw_ref[...]
    conv = (cw[0:1, :] * g_scr[:, 14:14 + tt, :] + cw[1:2, :] * g_scr[:, 15:15 + tt, :] + cw[2:3, :] * g)
    oc_ref[...] = cb_ref[...] * conv
    pho_ref[...] = u_scr[:, tt:tt + 16, :]
    cho_ref[...] = g_scr[:, tt:tt + 16, :]


def _k_pc(z3, ph16, cvh16, pw_bf, ps, cw, *, bb, tt, P):
    B, T, _ = z3.shape
    nt = T // tt
    r = tt // 16

    def cur(c):
        return pl.BlockSpec((bb, tt, 512), lambda b, t: (b, t, c // 512))

    def prev(c):
        return pl.BlockSpec((bb, 16, 512), lambda b, t: (b, jnp.maximum(t * r - 1, 0), c // 512))

    hist = pl.BlockSpec((bb, 16, 512), lambda b, t: (b, 0, 0))
    return pl.pallas_call(
        functools.partial(_pc_body, bb=bb, tt=tt, P=P),
        grid=(B // bb, nt),
        in_specs=[cur(C_PU), cur(C_CB), cur(C_CC), cur(C_CH), prev(C_PU), prev(C_CC), prev(C_CH),
                  hist, hist, _const_spec((4, POOL_GROUP, POOL_GROUP)), _const_spec((1, 512)),
                  _const_spec((8, 512))],
        out_specs=[pl.BlockSpec((bb, tt, 512), lambda b, t: (b, t, 0)),
                   pl.BlockSpec((bb, tt, 512), lambda b, t: (b, t, 0)), hist, hist],
        out_shape=[jax.ShapeDtypeStruct((B, T, 512), F32), jax.ShapeDtypeStruct((B, T, 512), F32),
                   jax.ShapeDtypeStruct((B, 16, 512), F32), jax.ShapeDtypeStruct((B, 16, 512), F32)],
        scratch_shapes=[pltpu.VMEM((bb, tt + 16, 512), F32), pltpu.VMEM((bb, tt + 16, 512), F32)],
        compiler_params=_cparams(("arbitrary", "arbitrary")),
        name="pool_conv",
    )(z3, z3, z3, z3, z3, z3, z3, ph16, cvh16, pw_bf, ps, cw)


def _mp_body(mq_ref, mkv_ref, mkr_ref, cos_ref, sin_ref, gq_ref, gkv_ref, wuq_ref, wukt_ref,
             qlt_ref, qrt_ref, ckv_ref, ckvb_ref, ckvt_ref, kr_ref, krb_ref):
    a = _rms(mq_ref[...], gq_ref[...]).astype(BF16)
    cq = jnp.dot(a, wuq_ref[...], preferred_element_type=F32)
    cqt = cq.T
    cos = cos_ref[...]
    sin = sin_ref[...]
    qrt_ref[...] = (cqt[512:768] * cos.T + cqt[768:1024] * sin.T).astype(BF16)
    for h in range(MLA_HEADS):
        qnt = cqt[h * MLA_NOPE:(h + 1) * MLA_NOPE].astype(BF16)
        qlt_ref[h * MLA_KV_LORA:(h + 1) * MLA_KV_LORA, :] = jnp.dot(
            wukt_ref[h], qnt, preferred_element_type=F32).astype(BF16)
    ckv = _rms(mkv_ref[...], gkv_ref[...])
    ckv_ref[...] = ckv
    ckvb_ref[...] = ckv.astype(BF16)
    ckvt_ref[0] = ckv.T.astype(BF16)
    mkr = mkr_ref[...]
    kr = mkr[:, 0:32] * cos[:, 0:32] + mkr[:, 32:64] * sin[:, 0:32]
    kr_ref[...] = kr
    krb_ref[...] = kr.astype(BF16)


def _k_mlaprep(z, cos_t, sin_t, gq, gkv, wuq_bf, wukt_bf, *, tm):
    N = z.shape[0]
    nr = cos_t.shape[0] // tm
    tab = pl.BlockSpec((tm, 256), lambda i: (i % nr, 0))
    row = lambda w: pl.BlockSpec((tm, w), lambda i: (i, 0))
    col = lambda h: pl.BlockSpec((h, tm), lambda i: (0, i))
    return pl.pallas_call(
        _mp_body,
        grid=(N // tm,),
        in_specs=[pl.BlockSpec((tm, 256), lambda i: (i, C_MQ // 256)),
                  pl.BlockSpec((tm, 256), lambda i: (i, C_MKV // 256)),
                  pl.BlockSpec((tm, 128), lambda i: (i, C_MKR // 128)),
                  tab, tab, _const_spec((1, 256)), _const_spec((1, 256)),
                  _const_spec((256, 1024)), _const_spec((MLA_HEADS, MLA_KV_LORA, MLA_NOPE))],
        out_specs=[col(2048), col(256), row(256), row(256),
                   pl.BlockSpec((1, 256, tm), lambda i: (i, 0, 0)), row(32), row(32)],
        out_shape=[jax.ShapeDtypeStruct((2048, N), BF16), jax.ShapeDtypeStruct((256, N), BF16),
                   jax.ShapeDtypeStruct((N, 256), F32), jax.ShapeDtypeStruct((N, 256), BF16),
                   jax.ShapeDtypeStruct((N // tm, 256, tm), BF16),
                   jax.ShapeDtypeStruct((N, 32), F32), jax.ShapeDtypeStruct((N, 32), BF16)],
        compiler_params=_cparams(("arbitrary",)),
        name="mla_prep",
    )(z, z, z, cos_t, sin_t, gq, gkv, wuq_bf, wukt_bf)


def _dot_nt(a, b):
    return lax.dot_general(a, b, (((1,), (1,)), ((), ())), preferred_element_type=F32)


MLA_SB = 128
MLA_CB = 256


def _mla_body(qlt_ref, qrt_ref, ckv_ref, kr_ref, ckvt_ref, wuvt_ref, o_ref,
              qt_scr, rt_scr, s_scr, pt_scr, m_scr, l_scr, al_scr, acc_scr,
              *, tq, P, L, KC, nkc_total, stacked):
    qi = pl.program_id(1)
    q0 = P + qi * tq
    cols_n = MLA_HEADS * tq
    if stacked:
        qt_scr[...] = qlt_ref[0]
        rt_scr[...] = qrt_ref[0]
    else:
        for h in range(MLA_HEADS):
            qt_scr[:, h * tq:(h + 1) * tq] = qlt_ref[h * 256:(h + 1) * 256, :]
            rt_scr[:, h * tq:(h + 1) * tq] = qrt_ref[h * 32:(h + 1) * 32, :]
    max_cend = (((q0 + tq - 1) >> 6) + 1) << 6
    nkc = jnp.minimum((max_cend + KC - 1) // KC, nkc_total)
    n_full = jnp.minimum(((q0 >> 6) + 1) << 6, L) // KC
    c = MLA_SCALE * LOG2E
    cb_w = min(MLA_CB, cols_n)

    m_scr[...] = jnp.full((1, cols_n), M_INIT, F32)
    l_scr[...] = jnp.zeros((1, cols_n), F32)
    acc_scr[...] = jnp.zeros((MLA_KV_LORA, cols_n), F32)

    def make_step(masked):
        def step(kc, carry):
            k0 = pl.multiple_of(kc * KC, KC)
            ck = ckv_ref[0, pl.ds(k0, KC), :]
            kr = kr_ref[0, pl.ds(k0, KC), :]
            s_scr[...] = (jnp.dot(ck, qt_scr[...], preferred_element_type=F32)
                          + jnp.dot(kr, rt_scr[...], preferred_element_type=F32))
            for cb in range(cols_n // cb_w):
                cols = slice(cb * cb_w, (cb + 1) * cb_w)
                if masked:
                    lane = cb * cb_w + lax.broadcasted_iota(I32, (1, cb_w), 1)
                    qpos = q0 + (lane & (tq - 1))
                    lim = jnp.minimum(((qpos >> 6) + 1) << 6, L)
                mx = jnp.full((8, cb_w), -jnp.inf, F32)
                for sb in range(KC // MLA_SB):
                    rs = slice(sb * MLA_SB, (sb + 1) * MLA_SB)
                    x = s_scr[rs, cols]
                    if masked:
                        kpos = k0 + sb * MLA_SB + lax.broadcasted_iota(I32, (MLA_SB, 1), 0)
                        x = jnp.where(kpos < lim, x, NEG)
                        s_scr[rs, cols] = x
                    mx = jnp.maximum(mx, _max8(x))
                m_old = m_scr[:, cols]
                m_new = jnp.maximum(m_old, jnp.max(mx, axis=0, keepdims=True))
                alpha = jnp.exp2((m_old - m_new) * c)
                sm = jnp.zeros((8, cb_w), F32)
                for sb in range(KC // MLA_SB):
                    rs = slice(sb * MLA_SB, (sb + 1) * MLA_SB)
                    p = jnp.exp2((s_scr[rs, cols] - m_new) * c)
                    sm = sm + _fold8(p)
                    pt_scr[rs, cols] = p.astype(BF16)
                l_scr[:, cols] = alpha * l_scr[:, cols] + jnp.sum(sm, axis=0, keepdims=True)
                m_scr[:, cols] = m_new
                al_scr[:, cols] = alpha
            acc_scr[...] = al_scr[...] * acc_scr[...] + jnp.dot(ckvt_ref[0, kc], pt_scr[...],
                                                                preferred_element_type=F32)
            return carry
        return step

    lax.fori_loop(0, n_full, make_step(False), 0)
    lax.fori_loop(n_full, nkc, make_step(True), 0)
    ot = (acc_scr[...] / l_scr[...]).astype(BF16)
    if stacked:
        for h in range(MLA_HEADS):
            o_ref[0, h * MLA_V:(h + 1) * MLA_V, :] = jnp.dot(wuvt_ref[h], ot, preferred_element_type=F32)
    else:
        outs = [jnp.dot(wuvt_ref[h], ot[:, h * tq:(h + 1) * tq], preferred_element_type=F32)
                for h in range(MLA_HEADS)]
        o_ref[0] = jnp.concatenate(outs, axis=0).T


def _k_mla(qlt, qrt, ckvb, krb, ckvt, wuvt_bf, *, B, T, tq, P, L, KC, stacked):
    Lp = ckvb.shape[1]
    nkc_total = Lp // KC
    cols_n = MLA_HEADS * tq
    nq = T // tq
    if stacked:
        q_specs = [pl.BlockSpec((1, MLA_KV_LORA, cols_n), lambda b, q: (b, 0, 0)),
                   pl.BlockSpec((1, MLA_ROPE, cols_n), lambda b, q: (b, 0, 0))]
        out_spec = pl.BlockSpec((1, MLA_HEADS * MLA_V, cols_n), lambda b, q: (b, 0, 0))
        out_shape = jax.ShapeDtypeStruct((B, MLA_HEADS * MLA_V, cols_n), F32)
    else:
        q_specs = [pl.BlockSpec((MLA_HEADS * MLA_KV_LORA, tq), lambda b, q: (0, b * nq + q)),
                   pl.BlockSpec((MLA_HEADS * MLA_ROPE, tq), lambda b, q: (0, b * nq + q))]
        out_spec = pl.BlockSpec((1, tq, 512), lambda b, q: (b, q, 0))
        out_shape = jax.ShapeDtypeStruct((B, T, 512), F32)
    return pl.pallas_call(
        functools.partial(_mla_body, tq=tq, P=P, L=L, KC=KC, nkc_total=nkc_total, stacked=stacked),
        grid=(B, nq),
        in_specs=q_specs + [pl.BlockSpec((1, Lp, 256), lambda b, q: (b, 0, 0)),
                            pl.BlockSpec((1, Lp, 32), lambda b, q: (b, 0, 0)),
                            pl.BlockSpec((1, nkc_total, 256, KC), lambda b, q: (b, 0, 0, 0)),
                            _const_spec((MLA_HEADS, MLA_V, MLA_KV_LORA))],
        out_specs=out_spec,
        out_shape=out_shape,
        scratch_shapes=[pltpu.VMEM((MLA_KV_LORA, cols_n), BF16), pltpu.VMEM((MLA_ROPE, cols_n), BF16),
                        pltpu.VMEM((KC, cols_n), F32), pltpu.VMEM((KC, cols_n), BF16),
                        pltpu.VMEM((1, cols_n), F32), pltpu.VMEM((1, cols_n), F32),
                        pltpu.VMEM((1, cols_n), F32), pltpu.VMEM((MLA_KV_LORA, cols_n), F32)],
        compiler_params=_cparams(("arbitrary", "arbitrary")),
        name="mla_attn",
    )(qlt, qrt, ckvb, krb, ckvt, wuvt_bf)


def _fold(x, r):
    n, w = x.shape
    return jnp.sum(x.reshape(n // r, r, w), axis=0)


def _fold8(x):
    return _fold(x, 8)


def _max8(x):
    n, w = x.shape
    return jnp.max(x.reshape(n // 8, 8, w), axis=0)


DSA_SB = 128


def _dsa_body(qt_ref, iqt_ref, iwt_ref, kb_ref, vt_ref, ikb_ref, o_ref,
              s_scr, w_scr, lg_scr, pt_scr, m_scr, l_scr, acc_scr, *, tq, P, L, KC, nkc_total, topk):
    qi = pl.program_id(1)
    q0 = P + qi * tq
    qpos = q0 + lax.broadcasted_iota(I32, (1, tq), 1)
    lim = jnp.minimum(((qpos >> 6) + 1) << 6, L)
    keff = jnp.minimum(lim, topk).astype(F32)
    max_cend = (((q0 + tq - 1) >> 6) + 1) << 6
    nkc = jnp.minimum((max_cend + KC - 1) // KC, nkc_total)

    iqt = iqt_ref[0]
    iwt = iwt_ref[0]
    zpad = jnp.zeros((IDX_DIM, tq), BF16)
    iq_h = [jnp.concatenate([iqt[h * IDX_DIM:(h + 1) * IDX_DIM], zpad], axis=0) for h in range(IDX_HEADS)]

    def score_step(kc, carry):
        k0 = pl.multiple_of(kc * KC, KC)
        ik = ikb_ref[0, pl.ds(k0, KC), :]
        for h in range(IDX_HEADS):
            lg_scr[h, :, 0:tq] = jnp.dot(ik, iq_h[h], preferred_element_type=F32)
        for sb in range(KC // DSA_SB):
            rs = slice(sb * DSA_SB, (sb + 1) * DSA_SB)
            s = jnp.zeros((DSA_SB, tq), F32)
            for h in range(IDX_HEADS):
                s = s + iwt[h:h + 1, :] * jnp.maximum(lg_scr[h, rs, 0:tq], 0.0)
            s = s * IDX_SCALE
            kpos = k0 + sb * DSA_SB + lax.broadcasted_iota(I32, (DSA_SB, 1), 0)
            s = jnp.where(kpos < lim, s, -jnp.inf)
            b = lax.bitcast_convert_type(s, I32)
            b = jnp.where(b == INT_MIN, 0, b)
            s_scr[kc, rs, :] = b ^ ((b >> 31) & np.int32(0x7FFFFFFF))
        return carry

    lax.fori_loop(0, nkc, score_step, 0)

    def bit_step(i, thr):
        cand = thr + jnp.left_shift(jnp.int32(1), 31 - i)

        def cnt_step(kc, acc):
            return acc + _fold(jnp.where(s_scr[kc] >= cand, 1.0, 0.0), 64)

        cnt = jnp.sum(lax.fori_loop(0, nkc, cnt_step, jnp.zeros((64, tq), F32)), axis=0, keepdims=True)
        return jnp.where(cnt >= keff, cand, thr)

    thr = lax.fori_loop(0, 32, bit_step, jnp.full((1, tq), INT_MIN, I32))

    def tie_cnt_step(kc, carry):
        key = s_scr[kc]
        return (carry[0] + _fold8(jnp.where(key > thr, 1.0, 0.0)),
                carry[1] + _fold8(jnp.where(key == thr, 1.0, 0.0)))

    z8 = jnp.zeros((8, tq), F32)
    n_gt, n_eq = lax.fori_loop(0, nkc, tie_cnt_step, (z8, z8))
    need = keff - jnp.sum(n_gt, axis=0, keepdims=True)
    surplus = jnp.max(jnp.sum(n_eq, axis=0, keepdims=True) - need)

    @pl.when(surplus < 0.5)
    def _():
        def fast_step(kc, carry):
            s_scr[kc] = lax.bitcast_convert_type(jnp.where(s_scr[kc] >= thr, 0.0, NEG), I32)
            return carry
        lax.fori_loop(0, nkc, fast_step, 0)

    @pl.when(surplus >= 0.5)
    def _():
        tri = jnp.where(lax.broadcasted_iota(I32, (KC, KC), 1) < lax.broadcasted_iota(I32, (KC, KC), 0),
                        1.0, 0.0).astype(BF16)

        def slow_step(kc, seen):
            key = s_scr[kc]
            eq = key == thr
            eqf = jnp.where(eq, 1.0, 0.0)
            rank = seen + jnp.dot(tri, eqf.astype(BF16), preferred_element_type=F32)
            bias = jnp.where(key > thr, 0.0, jnp.where(eq, jnp.where(rank < need, 0.0, NEG), NEG))
            s_scr[kc] = lax.bitcast_convert_type(bias, I32)
            return seen + jnp.sum(_fold8(eqf), axis=0, keepdims=True)
        lax.fori_loop(0, nkc, slow_step, jnp.zeros((1, tq), F32))

    c = DSA_SCALE * LOG2E
    hd = DSA_HEAD_DIM
    zq = jnp.zeros((hd, tq), BF16)
    for hp in range(DSA_HEADS // 2):
        r0 = hp * 2 * hd
        w_scr[hp, 0:hd, 0:tq] = qt_ref[0, r0:r0 + hd, :]
        w_scr[hp, 0:hd, tq:2 * tq] = zq
        w_scr[hp, hd:2 * hd, 0:tq] = zq
        w_scr[hp, hd:2 * hd, tq:2 * tq] = qt_ref[0, r0 + hd:r0 + 2 * hd, :]
    m_scr[...] = jnp.full((DSA_HEADS, tq), M_INIT, F32)
    l_scr[...] = jnp.zeros((DSA_HEADS, tq), F32)
    acc_scr[...] = jnp.zeros((DSA_HEADS * hd, tq), F32)
    nsb = KC // DSA_SB

    def att_step(kc, carry):
        k0 = pl.multiple_of(kc * KC, KC)
        for hp in range(DSA_HEADS // 2):
            r0 = hp * 2 * hd
            lg_scr[hp] = jnp.dot(kb_ref[0, pl.ds(k0, KC), r0:r0 + 2 * hd], w_scr[hp],
                                 preferred_element_type=F32)
            for e in range(2):
                h = 2 * hp + e
                cols = slice(e * tq, (e + 1) * tq)
                mx = jnp.full((8, tq), -jnp.inf, F32)
                for sb in range(nsb):
                    rs = slice(sb * DSA_SB, (sb + 1) * DSA_SB)
                    x = lg_scr[hp, rs, cols] + lax.bitcast_convert_type(s_scr[kc, rs, :], F32)
                    lg_scr[hp, rs, cols] = x
                    mx = jnp.maximum(mx, _max8(x))
                m_old = m_scr[h:h + 1, :]
                m_new = jnp.maximum(m_old, jnp.max(mx, axis=0, keepdims=True))
                alpha = jnp.exp2((m_old - m_new) * c)
                sm = jnp.zeros((8, tq), F32)
                for sb in range(nsb):
                    rs = slice(sb * DSA_SB, (sb + 1) * DSA_SB)
                    p = jnp.exp2((lg_scr[hp, rs, cols] - m_new) * c)
                    sm = sm + _fold8(p)
                    pt_scr[h, rs, :] = p.astype(BF16)
                l_scr[h:h + 1, :] = alpha * l_scr[h:h + 1, :] + jnp.sum(sm, axis=0, keepdims=True)
                m_scr[h:h + 1, :] = m_new
                hs = slice(h * hd, (h + 1) * hd)
                acc_scr[hs, :] = alpha * acc_scr[hs, :] + jnp.dot(vt_ref[0, kc, hs, :], pt_scr[h],
                                                                  preferred_element_type=F32)
        return carry

    lax.fori_loop(0, nkc, att_step, 0)
    outs = [acc_scr[h * hd:(h + 1) * hd, :] / l_scr[h:h + 1, :] for h in range(DSA_HEADS)]
    o_ref[0] = jnp.concatenate(outs, axis=0).T


def _k_dsa(qt, iqt, iwt, kb3, vt4, ikb3, *, tq, P, L, KC, topk):
    B, _, T = qt.shape
    Lp = kb3.shape[1]
    nkc_total = Lp // KC
    return pl.pallas_call(
        functools.partial(_dsa_body, tq=tq, P=P, L=L, KC=KC, nkc_total=nkc_total, topk=topk),
        grid=(B, T // tq),
        in_specs=[pl.BlockSpec((1, 512, tq), lambda b, q: (b, 0, q)),
                  pl.BlockSpec((1, 256, tq), lambda b, q: (b, 0, q)),
                  pl.BlockSpec((1, 8, tq), lambda b, q: (b, 0, q)),
                  pl.BlockSpec((1, Lp, 512), lambda b, q: (b, 0, 0)),
                  pl.BlockSpec((1, nkc_total, 512, KC), lambda b, q: (b, 0, 0, 0)),
                  pl.BlockSpec((1, Lp, 128), lambda b, q: (b, 0, 0))],
        out_specs=pl.BlockSpec((1, tq, 512), lambda b, q: (b, q, 0)),
        out_shape=jax.ShapeDtypeStruct((B, T, 512), F32),
        scratch_shapes=[pltpu.VMEM((nkc_total, KC, tq), I32),
                        pltpu.VMEM((DSA_HEADS // 2, 2 * DSA_HEAD_DIM, 2 * tq), BF16),
                        pltpu.VMEM((DSA_HEADS // 2, KC, 2 * tq), F32),
                        pltpu.VMEM((DSA_HEADS, KC, tq), BF16),
                        pltpu.VMEM((DSA_HEADS, tq), F32), pltpu.VMEM((DSA_HEADS, tq), F32),
                        pltpu.VMEM((DSA_HEADS * DSA_HEAD_DIM, tq), F32)],
        compiler_params=_cparams(("arbitrary", "arbitrary")),
        name="dsa_attn",
    )(qt, iqt, iwt, kb3, vt4, ikb3)


def _merge_body(x_ref, sc_ref, sh_ref, g1_ref, ng_ref, oa_ref, ob_ref, oc_ref, od_ref,
                wg_ref, bg_ref, wb_ref, wo_ref, o_ref, *, bb, tt):
    x = x_ref[...]
    tm = bb * tt
    h = (_rms(x, ng_ref[...]) * (1.0 + sc_ref[...]) + sh_ref[...]).reshape(tm, D_MODEL).astype(BF16)
    merged = jnp.zeros((tm, D_MODEL), F32)
    for i, oref in enumerate((oa_ref, ob_ref, oc_ref, od_ref)):
        gate = _sigmoid(jnp.dot(h, wg_ref[:, i * D_MODEL:(i + 1) * D_MODEL], preferred_element_type=F32)
                        + bg_ref[:, i * D_MODEL:(i + 1) * D_MODEL])
        br = jnp.dot(oref[...].astype(BF16), wb_ref[i * BRANCH_W:(i + 1) * BRANCH_W, :],
                     preferred_element_type=F32)
        merged = merged + gate * br
    y = jnp.dot(merged.astype(BF16), wo_ref[...], preferred_element_type=F32)
    o_ref[...] = x + g1_ref[...] * y.reshape(bb, tt, D_MODEL)


def _k_merge(x, sc, sh, g1, ng, oa, ob, oc, od, wg_bf, bg, wb_bf, wo_bf, *, bb, tt):
    B, T, D = x.shape
    nt = T // tt
    tm = bb * tt
    xs = pl.BlockSpec((bb, tt, D), lambda i: (i // nt, i % nt, 0))
    ms = pl.BlockSpec((bb, 1, D), lambda i: (i // nt, 0, 0))
    os_ = pl.BlockSpec((tm, 512), lambda i: (i, 0))
    return pl.pallas_call(
        functools.partial(_merge_body, bb=bb, tt=tt),
        grid=((B // bb) * nt,),
        in_specs=[xs, ms, ms, ms, _const_spec((1, 1, D)), os_, os_, os_, os_,
                  _const_spec((D, 4 * D)), _const_spec((1, 4 * D)), _const_spec((4 * BRANCH_W, D)),
                  _const_spec((D, D))],
        out_specs=xs,
        out_shape=jax.ShapeDtypeStruct((B, T, D), F32),
        compiler_params=_cparams(("arbitrary",)),
        name="merge_out",
    )(x, sc, sh, g1, ng, oa, ob, oc, od, wg_bf, bg, wb_bf, wo_bf)


FC = FFN_DIM // 2


def _swiglu_chunk(h, w1, w3, w2):
    a = jnp.dot(h, w1, preferred_element_type=F32)
    b = jnp.dot(h, w3, preferred_element_type=F32)
    act = (a * _sigmoid(a) * b).astype(BF16)
    return jnp.dot(act, w2, preferred_element_type=F32)


def _finish(x, g2, f, fg_ref, o_ref, bb, tt):
    y = x + g2 * f.reshape(bb, tt, D_MODEL)
    if fg_ref is not None:
        y = _rms(y, fg_ref[...])
    o_ref[...] = y


def _ffn_body(x_ref, sc_ref, sh_ref, g2_ref, ng_ref, w1_ref, w3_ref, w2_ref, *rest, bb, tt, final):
    fg_ref, o_ref = rest if final else (None, rest[0])
    x = x_ref[...]
    tm = bb * tt
    h = (_rms(x, ng_ref[...]) * (1.0 + sc_ref[...]) + sh_ref[...]).reshape(tm, D_MODEL).astype(BF16)
    f = jnp.zeros((tm, D_MODEL), F32)
    for ci in range(FFN_DIM // FC):
        f = f + _swiglu_chunk(h, w1_ref[:, ci * FC:(ci + 1) * FC], w3_ref[:, ci * FC:(ci + 1) * FC],
                              w2_ref[ci * FC:(ci + 1) * FC, :])
    _finish(x, g2_ref[...], f, fg_ref, o_ref, bb, tt)


def _k_ffn(x, sc, sh, g2, ng, w1_bf, w3_bf, w2_bf, fg, *, bb, tt):
    B, T, D = x.shape
    nt = T // tt
    final = fg is not None
    xs = pl.BlockSpec((bb, tt, D), lambda i: (i // nt, i % nt, 0))
    ms = pl.BlockSpec((bb, 1, D), lambda i: (i // nt, 0, 0))
    in_specs = [xs, ms, ms, ms, _const_spec((1, 1, D)), _const_spec((D, FFN_DIM)),
                _const_spec((D, FFN_DIM)), _const_spec((FFN_DIM, D))]
    args = [x, sc, sh, g2, ng, w1_bf, w3_bf, w2_bf]
    if final:
        in_specs.append(_const_spec((1, 1, D)))
        args.append(fg)
    return pl.pallas_call(
        functools.partial(_ffn_body, bb=bb, tt=tt, final=final),
        grid=((B // bb) * nt,),
        in_specs=in_specs,
        out_specs=xs,
        out_shape=jax.ShapeDtypeStruct((B, T, D), F32),
        compiler_params=_cparams(("arbitrary",)),
        name="ffn_dense",
    )(*args)


def _moe_body(x_ref, sc_ref, sh_ref, g2_ref, ng_ref, rw_ref, rb_ref, w1_ref, w3_ref, w2_ref, *rest,
              bb, tt, final):
    if final:
        fg_ref, o_ref, h_scr, gate_scr, acc_scr = rest
    else:
        fg_ref = None
        o_ref, h_scr, gate_scr, acc_scr = rest
    e = pl.program_id(1)
    ci = pl.program_id(2)
    tm = bb * tt
    lane = lax.broadcasted_iota(I32, (tm, 128), 1)

    @pl.when((e == 0) & (ci == 0))
    def _():
        h = (_rms(x_ref[...], ng_ref[...]) * (1.0 + sc_ref[...]) + sh_ref[...]).reshape(tm, D_MODEL)
        h_scr[...] = h.astype(BF16)
        logits = jnp.dot(h, rw_ref[...], preferred_element_type=F32,
                         precision=lax.Precision.HIGHEST) + rb_ref[...]
        m1 = jnp.max(logits, axis=-1, keepdims=True)
        i1 = jnp.min(jnp.where(logits == m1, lane, 128), axis=-1, keepdims=True)
        rest_l = jnp.where(lane == i1, NEG, logits)
        m2 = jnp.max(rest_l, axis=-1, keepdims=True)
        i2 = jnp.min(jnp.where(rest_l == m2, lane, 128), axis=-1, keepdims=True)
        e2 = jnp.exp(m2 - m1)
        den = 1.0 + e2
        gate_scr[...] = jnp.where(lane == i1, 1.0 / den, jnp.where(lane == i2, e2 / den, 0.0))
        acc_scr[...] = jnp.zeros((tm, D_MODEL), F32)

    ge = jnp.sum(jnp.where(lane == e, gate_scr[...], 0.0), axis=-1, keepdims=True)
    acc_scr[...] += ge * _swiglu_chunk(h_scr[...], w1_ref[0], w3_ref[0], w2_ref[0])

    @pl.when((e == N_EXPERTS - 1) & (ci == FFN_DIM // FC - 1))
    def _():
        _finish(x_ref[...], g2_ref[...], acc_scr[...], fg_ref, o_ref, bb, tt)


def _k_moe(x, sc, sh, g2, ng, rw_pad, rb_pad, w1_bf, w3_bf, w2_bf, fg, *, bb, tt):
    B, T, D = x.shape
    nt = T // tt
    tm = bb * tt
    final = fg is not None
    xs = pl.BlockSpec((bb, tt, D), lambda i, e, c: (i // nt, i % nt, 0))
    ms = pl.BlockSpec((bb, 1, D), lambda i, e, c: (i // nt, 0, 0))
    in_specs = [xs, ms, ms, ms, _const_spec((1, 1, D)), _const_spec((D, 128)), _const_spec((1, 128)),
                pl.BlockSpec((1, D, FC), lambda i, e, c: (e, 0, c)),
                pl.BlockSpec((1, D, FC), lambda i, e, c: (e, 0, c)),
                pl.BlockSpec((1, FC, D), lambda i, e, c: (e, c, 0))]
    args = [x, sc, sh, g2, ng, rw_pad, rb_pad, w1_bf, w3_bf, w2_bf]
    if final:
        in_specs.append(_const_spec((1, 1, D)))
        args.append(fg)
    return pl.pallas_call(
        functools.partial(_moe_body, bb=bb, tt=tt, final=final),
        grid=((B // bb) * nt, N_EXPERTS, FFN_DIM // FC),
        in_specs=in_specs,
        out_specs=xs,
        out_shape=jax.ShapeDtypeStruct((B, T, D), F32),
        scratch_shapes=[pltpu.VMEM((tm, D), BF16), pltpu.VMEM((tm, 128), F32), pltpu.VMEM((tm, D), F32)],
        compiler_params=_cparams(("arbitrary", "arbitrary", "arbitrary")),
        name="ffn_moe",
    )(*args)


def _prep_w_in(w):
    aq, ak, av, iq, ik, iw, pu, cb, cc, ch, mq, mkv, mkr = jnp.split(w, IN_SPLIT_POINTS, axis=1)
    d = w.shape[0]
    mkr_sw = jnp.concatenate([mkr[:, 16:], mkr[:, :16]], axis=1)
    return jnp.concatenate([aq, ak, av, pu, cb, cc, ch, iq, mq, mkv, ik, iw, jnp.zeros((d, 60), w.dtype),
                            mkr, mkr_sw, jnp.zeros((d, 64), w.dtype)], axis=1).astype(BF16)


def _prep_w_uq(w):
    w3 = w.reshape(MLA_Q_LORA, MLA_HEADS, MLA_NOPE + MLA_ROPE)
    nope = w3[:, :, :MLA_NOPE].reshape(MLA_Q_LORA, MLA_HEADS * MLA_NOPE)
    rope = w3[:, :, MLA_NOPE:]
    rope_sw = jnp.concatenate([rope[:, :, 16:], rope[:, :, :16]], axis=-1)
    return jnp.concatenate([nope, rope.reshape(MLA_Q_LORA, -1), rope_sw.reshape(MLA_Q_LORA, -1)],
                           axis=1).astype(BF16)


def _rope_tables(pos):
    half = MLA_ROPE // 2
    inv = 1.0 / (ROPE_BASE ** (jnp.arange(half, dtype=F32) / half))
    ang = pos.astype(F32)[:, None] * inv[None, :]
    cos, sin = jnp.cos(ang), jnp.sin(ang)
    cos2 = jnp.tile(jnp.concatenate([cos, cos], axis=-1), (1, MLA_HEADS))
    sin2 = jnp.tile(jnp.concatenate([-sin, sin], axis=-1), (1, MLA_HEADS))
    return cos2, sin2


def _layer(x, mod, past, lw, fw, use_moe, final_g, cfg):
    B, T, D = x.shape
    bb, tt, KC, tq_dsa, tq_mla = cfg["bb"], cfg["tt"], cfg["KC"], cfg["tq_dsa"], cfg["tq_mla"]
    N = B * T
    P = 0 if past is None else past["dsa_k"].shape[1]
    L = P + T
    sh1, sc1, g1, sh2, sc2, g2 = [m.reshape(B, 1, D) for m in jnp.split(mod, 6, axis=-1)]
    ng1 = lw["norm_mix_g"].reshape(1, 1, D)
    ng2 = lw["norm_ffn_g"].reshape(1, 1, D)

    emit = past is None
    res = _k_in(x, sc1, sh1, ng1, lw["w_in"], bb=bb, tt=tt, emit=emit)
    z = res[0]
    z3 = z.reshape(B, T, ZW)
    k_new = z[:, C_AK:C_AK + 512].reshape(B, T, DSA_HEADS, DSA_HEAD_DIM)
    v_new = z[:, C_AV:C_AV + 512].reshape(B, T, DSA_HEADS, DSA_HEAD_DIM)
    ik_new = z3[:, :, C_IKW:C_IKW + IDX_DIM]

    if past is None:
        ph16 = jnp.zeros((B, 16, 512), F32)
        cvh16 = ph16
    else:
        ph16 = jnp.pad(past["pool"], ((0, 0), (1, 0), (0, 0)))
        cvh16 = jnp.pad(past["conv"], ((0, 0), (16 - (CONV_WIDTH - 1), 0), (0, 0)))
    o_b, o_c, ph_o, cvh_o = _k_pc(z3, ph16, cvh16, lw["pool_w"], lw["pool_scale"], lw["conv_w"],
                                  bb=bb, tt=tt, P=P)
    pool_hist = ph_o[:, 1:]
    conv_hist = cvh_o[:, 16 - (CONV_WIDTH - 1):]

    cos2, sin2 = _rope_tables(P + jnp.arange(T))
    if bb > 1:
        cos2, sin2 = jnp.tile(cos2, (bb, 1)), jnp.tile(sin2, (bb, 1))
    qlt, qrt, ckv_new, ckvb, ckvt, kr_new, krb = _k_mlaprep(
        z, cos2, sin2, lw["mla_q_norm_g"], lw["mla_kv_norm_g"], lw["mla_w_uq"], lw["mla_w_uk"], tm=bb * tt)

    topk = min(DSA_TOPK_MAX, L // 4)
    if past is None:
        kb3 = res[1].reshape(B, T, 512)
        vt4, qt, iqt, iwt = res[2], res[3], res[4], res[5]
        ikb3 = res[6].reshape(B, T, 128)
        ckvb3 = ckvb.reshape(B, T, 256)
        krb3 = krb.reshape(B, T, 32)
        Tq = T
    else:
        Lp = -(-L // KC) * KC
        padk = ((0, 0), (0, Lp - L), (0, 0))
        k_all = jnp.concatenate([past["dsa_k"].reshape(B, P, 512), k_new.reshape(B, T, 512)], axis=1)
        v_all = jnp.concatenate([past["dsa_v"].reshape(B, P, 512), v_new.reshape(B, T, 512)], axis=1)
        kb3 = jnp.pad(k_all.astype(BF16), padk)
        vt4 = jnp.pad(v_all.astype(BF16), padk).reshape(B, Lp // KC, KC, 512).transpose(0, 1, 3, 2)
        ik_all = jnp.concatenate([past["idx_k"], ik_new], axis=1).astype(BF16)
        ikb3 = jnp.pad(ik_all, ((0, 0), (0, Lp - L), (0, 128 - IDX_DIM)))
        Tq = tq_dsa
        padq = ((0, 0), (0, 0), (0, Tq - T))
        qt = jnp.pad(z3[:, :, C_AQ:C_AQ + 512].transpose(0, 2, 1).astype(BF16), padq)
        iqt = jnp.pad(z3[:, :, C_IQ:C_IQ + 256].transpose(0, 2, 1).astype(BF16), padq)
        iwt = jnp.pad(z3[:, :, C_IKW + 64:C_IKW + 72].transpose(0, 2, 1), padq)
        ckvb3 = jnp.pad(jnp.concatenate([past["ckv"].astype(BF16), ckvb.reshape(B, T, 256)], axis=1), padk)
        krb3 = jnp.pad(jnp.concatenate([past["krope"].astype(BF16), krb.reshape(B, T, 32)], axis=1), padk)

    o_a = _k_dsa(qt, iqt, iwt, kb3, vt4, ikb3, tq=tq_dsa, P=P, L=L, KC=KC, topk=topk)
    o_a = o_a[:, :T].reshape(N, 512)
    if past is None:
        ckvt4 = ckvt.reshape(B, T // KC, 256, KC)
        o_d = _k_mla(qlt, qrt, ckvb3, krb3, ckvt4, lw["mla_w_uv"], B=B, T=T, tq=tq_mla, P=P, L=L, KC=KC,
                     stacked=False).reshape(N, 512)
    else:
        qs = qlt.reshape(MLA_HEADS, 256, B, T).transpose(2, 1, 0, 3).reshape(B, 256, MLA_HEADS * T)
        rs = qrt.reshape(MLA_HEADS, 32, B, T).transpose(2, 1, 0, 3).reshape(B, 32, MLA_HEADS * T)
        ckvt4 = ckvb3.reshape(B, ckvb3.shape[1] // KC, KC, 256).transpose(0, 1, 3, 2)
        o5 = _k_mla(qs, rs, ckvb3, krb3, ckvt4, lw["mla_w_uv"], B=B, T=T, tq=tq_mla, P=P, L=L, KC=KC,
                    stacked=True).reshape(B, MLA_HEADS, MLA_V, MLA_HEADS, T)
        o_d = jnp.stack([o5[:, h, :, h, :] for h in range(MLA_HEADS)], axis=1)
        o_d = o_d.transpose(0, 3, 1, 2).reshape(N, 512)

    x1 = _k_merge(x, sc1, sh1, g1, ng1, o_a, o_b.reshape(N, 512), o_c.reshape(N, 512), o_d,
                  lw["w_gate"], lw["b_gate"], lw["w_branch"], lw["w_out"], bb=bb, tt=tt)
    fg = None if final_g is None else final_g.reshape(1, 1, D)
    if use_moe:
        x2 = _k_moe(x1, sc2, sh2, g2, ng2, fw["rw"], fw["rb"], fw["w1"], fw["w3"], fw["w2"], fg,
                    bb=cfg["bb_moe"], tt=cfg["tt_moe"])
    else:
        x2 = _k_ffn(x1, sc2, sh2, g2, ng2, fw["w1"], fw["w3"], fw["w2"], fg, bb=bb, tt=tt)
    rows = (k_new, v_new, ik_new, ckv_new.reshape(B, T, 256), kr_new.reshape(B, T, 32), pool_hist, conv_hist)
    return x2, rows


def _prep_layer(l, ada_w, ada_b, norm_mix_g, norm_ffn_g, w_in, mla_q_norm_g, mla_kv_norm_g,
                mla_w_uq, mla_w_uk, mla_w_uv, pool_w, pool_scale, conv_w, w_gate, b_gate, w_branch, w_out):
    return dict(
        ada_w=ada_w[l].astype(BF16), ada_b=ada_b[l].reshape(1, -1),
        norm_mix_g=norm_mix_g[l], norm_ffn_g=norm_ffn_g[l],
        w_in=_prep_w_in(w_in[l]),
        mla_q_norm_g=mla_q_norm_g[l].reshape(1, -1), mla_kv_norm_g=mla_kv_norm_g[l].reshape(1, -1),
        mla_w_uq=_prep_w_uq(mla_w_uq[l]),
        mla_w_uk=mla_w_uk[l].transpose(1, 0, 2).astype(BF16),
        mla_w_uv=mla_w_uv[l].transpose(1, 2, 0).astype(BF16),
        pool_w=pool_w[l].astype(BF16), pool_scale=pool_scale[l].reshape(1, -1),
        conv_w=jnp.pad(conv_w[l].reshape(CONV_WIDTH, BRANCH_W), ((0, 8 - CONV_WIDTH), (0, 0))),
        w_gate=w_gate[l].astype(BF16), b_gate=b_gate[l].reshape(1, -1),
        w_branch=w_branch[l].astype(BF16), w_out=w_out[l].astype(BF16))


def _cfg_for(B, T, has_past):
    if has_past:
        return dict(bb=B, tt=T, KC=512, tq_dsa=128, tq_mla=T, bb_moe=B, tt_moe=T)
    t = min(512, T)
    return dict(bb=1, tt=t, KC=t, tq_dsa=min(256, T), tq_mla=min(128, T), bb_moe=1, tt_moe=min(1024, T))


def _run(x, c, pasts, lws, fws, final_norm_g):
    B, T, _ = x.shape
    cfg = _cfg_for(B, T, pasts is not None)
    outs = [[] for _ in range(7)]
    depth = len(lws)
    for l in range(depth):
        lw = lws[l]
        mod = _ada(c, lw["ada_w"], lw["ada_b"])
        x, rows = _layer(x, mod, None if pasts is None else pasts[l], lw, fws[l], l % 2 == 1,
                         final_norm_g if l == depth - 1 else None, cfg)
        for o, r in zip(outs, rows):
            o.append(r)
    return x, [jnp.stack(o) for o in outs]


def kernel(x_prompt, x_sample, c_prompt, c_sample, cache_dsa_k, cache_dsa_v, cache_dsa_idx_k, cache_mla_ckv, cache_mla_krope, state_pool, state_conv, ada_w, ada_b, norm_mix_g, norm_ffn_g, w_in, mla_q_norm_g, mla_kv_norm_g, mla_w_uq, mla_w_uk, mla_w_uv, pool_w, pool_scale, conv_w, w_gate, b_gate, w_branch, w_out, ffn_w1, ffn_w3, ffn_w2, moe_router_w, moe_router_b, moe_w1, moe_w3, moe_w2, final_norm_g):
    depth = ada_w.shape[0]
    lws = [_prep_layer(l, ada_w, ada_b, norm_mix_g, norm_ffn_g, w_in, mla_q_norm_g, mla_kv_norm_g,
                       mla_w_uq, mla_w_uk, mla_w_uv, pool_w, pool_scale, conv_w, w_gate, b_gate,
                       w_branch, w_out) for l in range(depth)]
    fws = []
    for l in range(depth):
        j = l // 2
        if l % 2 == 0:
            fws.append(dict(w1=ffn_w1[j].astype(BF16), w3=ffn_w3[j].astype(BF16), w2=ffn_w2[j].astype(BF16)))
        else:
            fws.append(dict(
                rw=jnp.pad(moe_router_w[j], ((0, 0), (0, 128 - N_EXPERTS))),
                rb=jnp.pad(moe_router_b[j].reshape(1, -1), ((0, 0), (0, 128 - N_EXPERTS)), constant_values=NEG),
                w1=moe_w1[j].astype(BF16), w3=moe_w3[j].astype(BF16), w2=moe_w2[j].astype(BF16)))
    sample_pasts = [dict(dsa_k=cache_dsa_k[l], dsa_v=cache_dsa_v[l], idx_k=cache_dsa_idx_k[l],
                         ckv=cache_mla_ckv[l], krope=cache_mla_krope[l],
                         pool=state_pool[l], conv=state_conv[l]) for l in range(depth)]
    y_prompt, pn = _run(x_prompt, c_prompt, None, lws, fws, final_norm_g)
    y_sample, sn = _run(x_sample, c_sample, sample_pasts, lws, fws, final_norm_g)
    return (y_prompt, y_sample, pn[0], pn[1], pn[2], pn[3], pn[4], pn[5], pn[6],
            sn[0], sn[1], sn[2], sn[3], sn[4], sn[5], sn[6])
```

```python
import functools

import numpy as np
import jax
import jax.numpy as jnp
from jax import lax
from jax.experimental import pallas as pl
from jax.experimental.pallas import tpu as pltpu

F32 = jnp.float32
BF16 = jnp.bfloat16
I32 = jnp.int32

D_MODEL = 1024
CHUNK = 64
N_BRANCH = 4
BRANCH_W = 512
DSA_HEADS = 8
DSA_HEAD_DIM = 64
IDX_HEADS = 4
IDX_DIM = 64
DSA_TOPK_MAX = 256
POOL_WINDOWS = (2, 4, 8, 16)
POOL_GROUP = BRANCH_W // 4
POOL_HIST = 15
CONV_WIDTH = 3
MLA_HEADS = 8
MLA_Q_LORA = 256
MLA_KV_LORA = 256
MLA_NOPE = 64
MLA_ROPE = 32
MLA_V = 64
ROPE_BASE = 10000.0
FFN_DIM = 2816
N_EXPERTS = 8
EPS = 1e-6

DSA_SCALE = DSA_HEAD_DIM ** -0.5
IDX_SCALE = (IDX_HEADS * IDX_DIM) ** -0.5
MLA_SCALE = (MLA_NOPE + MLA_ROPE) ** -0.5
LOG2E = 1.4426950408889634

IN_SPLIT_WIDTHS = (512, 512, 512, 256, 64, 4, 512, 512, 512, 512, 256, 256, 32)
IN_SPLIT_POINTS = tuple(int(sum(IN_SPLIT_WIDTHS[:i + 1])) for i in range(len(IN_SPLIT_WIDTHS) - 1))

C_AQ, C_AK, C_AV, C_PU, C_CB, C_CC, C_CH = 0, 512, 1024, 1536, 2048, 2560, 3072
C_IQ, C_MQ, C_MKV, C_IKW, C_MKR, ZW = 3584, 3840, 4096, 4352, 4480, 4608
IN_TN = 1536

NEG = -1e30
M_INIT = -1e29
INT_MIN = np.int32(-2 ** 31)
V7X_VMEM_LIMIT = 56 * 1024 * 1024


def _cparams(sem):
    return pltpu.CompilerParams(dimension_semantics=sem, vmem_limit_bytes=V7X_VMEM_LIMIT)


def _rms(x, g):
    return x * lax.rsqrt(jnp.mean(x * x, axis=-1, keepdims=True) + EPS) * g


def _sigmoid(x):
    return 1.0 / (1.0 + jnp.exp(-x))


def _const_spec(shape):
    nd = len(shape)
    return pl.BlockSpec(shape, lambda *_: (0,) * nd, pipeline_mode=pl.Buffered(1))


def _ada_body(c_ref, w_ref, b_ref, o_ref):
    c = c_ref[...]
    s = (c * _sigmoid(c)).astype(BF16)
    o_ref[...] = jnp.dot(s, w_ref[...], preferred_element_type=F32) + b_ref[...]


def _ada(c, w_bf, b):
    B, D = c.shape
    n = w_bf.shape[1]
    tn = 1536
    return pl.pallas_call(
        _ada_body,
        grid=(n // tn,),
        in_specs=[pl.BlockSpec((B, D), lambda j: (0, 0)),
                  pl.BlockSpec((D, tn), lambda j: (0, j)),
                  pl.BlockSpec((1, tn), lambda j: (0, j))],
        out_specs=pl.BlockSpec((B, tn), lambda j: (0, j)),
        out_shape=jax.ShapeDtypeStruct((B, n), F32),
        compiler_params=_cparams(("arbitrary",)),
        name="ada_mod",
    )(c, w_bf, b)


def _in_body(x_ref, sc_ref, sh_ref, g_ref, w_ref, z_ref, *rest, bb, tt, emit):
    if emit:
        kb_ref, vt_ref, qt_ref, iqt_ref, iwt_ref, ikb_ref, h_scr = rest
    else:
        (h_scr,) = rest
    j = pl.program_id(1)

    @pl.when(j == 0)
    def _():
        h = _rms(x_ref[...], g_ref[...]) * (1.0 + sc_ref[...]) + sh_ref[...]
        h_scr[...] = h.reshape(bb * tt, D_MODEL).astype(BF16)

    z = jnp.dot(h_scr[...], w_ref[...], preferred_element_type=F32)
    z_ref[...] = z

    if emit:
        @pl.when(j == 0)
        def _():
            kb_ref[...] = z[:, C_AK:C_AK + 512].astype(BF16)
            vt_ref[0, 0] = z[:, C_AV:C_AV + 512].T.astype(BF16)
            qt_ref[0] = z[:, C_AQ:C_AQ + 512].T.astype(BF16)

        @pl.when(j == 2)
        def _():
            o = 2 * IN_TN
            iqt_ref[0] = z[:, C_IQ - o:C_IQ - o + 256].T.astype(BF16)
            ikw = z[:, C_IKW - o:C_IKW - o + 128]
            ikb_ref[...] = ikw.astype(BF16)
            iwt_ref[0] = ikw.T[64:72, :]


def _k_in(x, sc, sh, g, w_bf, *, bb, tt, emit):
    B, T, D = x.shape
    nt = T // tt
    tm = bb * tt
    n_rows = (B // bb) * nt
    N = B * T
    xmap = lambda i, j: (i // nt, i % nt, 0)
    mmap = lambda i, j: (i // nt, 0, 0)
    in_specs = [pl.BlockSpec((bb, tt, D), xmap),
                pl.BlockSpec((bb, 1, D), mmap),
                pl.BlockSpec((bb, 1, D), mmap),
                _const_spec((1, 1, D)),
                pl.BlockSpec((D, IN_TN), lambda i, j: (0, j))]
    out_specs = [pl.BlockSpec((tm, IN_TN), lambda i, j: (i, j))]
    out_shape = [jax.ShapeDtypeStruct((N, ZW), F32)]
    if emit:
        assert bb == 1
        out_specs += [
            pl.BlockSpec((tm, 512), lambda i, j: (i, 0)),
            pl.BlockSpec((1, 1, 512, tt), lambda i, j: (i // nt, i % nt, 0, 0)),
            pl.BlockSpec((1, 512, tt), lambda i, j: (i // nt, 0, i % nt)),
            pl.BlockSpec((1, 256, tt), lambda i, j: (i // nt, 0, i % nt)),
            pl.BlockSpec((1, 8, tt), lambda i, j: (i // nt, 0, i % nt)),
            pl.BlockSpec((tm, 128), lambda i, j: (i, 0)),
        ]
        out_shape += [
            jax.ShapeDtypeStruct((N, 512), BF16),
            jax.ShapeDtypeStruct((B, nt, 512, tt), BF16),
            jax.ShapeDtypeStruct((B, 512, T), BF16),
            jax.ShapeDtypeStruct((B, 256, T), BF16),
            jax.ShapeDtypeStruct((B, 8, T), F32),
            jax.ShapeDtypeStruct((N, 128), BF16),
        ]
    return pl.pallas_call(
        functools.partial(_in_body, bb=bb, tt=tt, emit=emit),
        grid=(n_rows, ZW // IN_TN),
        in_specs=in_specs,
        out_specs=out_specs,
        out_shape=out_shape,
        scratch_shapes=[pltpu.VMEM((tm, D), BF16)],
        compiler_params=_cparams(("arbitrary", "arbitrary")),
        name="in_proj",
    )(x, sc, sh, g, w_bf)


def _pc_body(pu_ref, cb_ref, cc_ref, ch_ref, pup_ref, ccp_ref, chp_ref, ph_ref, cvh_ref,
             pw_ref, ps_ref, cw_ref, ob_ref, oc_ref, pho_ref, cho_ref, u_scr, g_scr, *, bb, tt, P):
    t = pl.program_id(1)
    first = t == 0
    pu = pu_ref[...]
    g = cc_ref[...] * ch_ref[...]
    u_scr[:, 0:16, :] = jnp.where(first, ph_ref[...], pup_ref[...])
    u_scr[:, 16:16 + tt, :] = pu
    g_scr[:, 0:16, :] = jnp.where(first, cvh_ref[...], ccp_ref[...] * chp_ref[...])
    g_scr[:, 16:16 + tt, :] = g

    pos = P + t * tt + lax.broadcasted_iota(I32, (1, tt, 1), 1)
    for gi, win in enumerate(POOL_WINDOWS):
        lo = gi * POOL_GROUP
        acc = pu[:, :, lo:lo + POOL_GROUP]
        for j in range(1, win):
            acc = acc + u_scr[:, 16 - j:16 - j + tt, lo:lo + POOL_GROUP]
        cnt = jnp.minimum(pos + 1, win).astype(F32)
        d = acc / cnt - pu[:, :, lo:lo + POOL_GROUP]
        y = jnp.dot(d.reshape(bb * tt, POOL_GROUP).astype(BF16), pw_ref[gi], preferred_element_type=F32)
        ob_ref[:, :, lo:lo + POOL_GROUP] = (y * ps_ref[:, lo:lo + POOL_GROUP]).reshape(bb, tt, POOL_GROUP)

    cw = cw_ref[...]
    conv = (cw[0:1, :] * g_scr[:, 14:14 + tt, :] + cw[1:2, :] * g_scr[:, 15:15 + tt, :] + cw[2:3, :] * g)
    oc_ref[...] = cb_ref[...] * conv
    pho_ref[...] = u_scr[:, tt:tt + 16, :]
    cho_ref[...] = g_scr[:, tt:tt + 16, :]


def _k_pc(z3, ph16, cvh16, pw_bf, ps, cw, *, bb, tt, P):
    B, T, _ = z3.shape
    nt = T // tt
    r = tt // 16

    def cur(c):
        return pl.BlockSpec((bb, tt, 512), lambda b, t: (b, t, c // 512))

    def prev(c):
        return pl.BlockSpec((bb, 16, 512), lambda b, t: (b, jnp.maximum(t * r - 1, 0), c // 512))

    hist = pl.BlockSpec((bb, 16, 512), lambda b, t: (b, 0, 0))
    return pl.pallas_call(
        functools.partial(_pc_body, bb=bb, tt=tt, P=P),
        grid=(B // bb, nt),
        in_specs=[cur(C_PU), cur(C_CB), cur(C_CC), cur(C_CH), prev(C_PU), prev(C_CC), prev(C_CH),
                  hist, hist, _const_spec((4, POOL_GROUP, POOL_GROUP)), _const_spec((1, 512)),
                  _const_spec((8, 512))],
        out_specs=[pl.BlockSpec((bb, tt, 512), lambda b, t: (b, t, 0)),
                   pl.BlockSpec((bb, tt, 512), lambda b, t: (b, t, 0)), hist, hist],
        out_shape=[jax.ShapeDtypeStruct((B, T, 512), F32), jax.ShapeDtypeStruct((B, T, 512), F32),
                   jax.ShapeDtypeStruct((B, 16, 512), F32), jax.ShapeDtypeStruct((B, 16, 512), F32)],
        scratch_shapes=[pltpu.VMEM((bb, tt + 16, 512), F32), pltpu.VMEM((bb, tt + 16, 512), F32)],
        compiler_params=_cparams(("arbitrary", "arbitrary")),
        name="pool_conv",
    )(z3, z3, z3, z3, z3, z3, z3, ph16, cvh16, pw_bf, ps, cw)


def _mp_body(mq_ref, mkv_ref, mkr_ref, cos_ref, sin_ref, gq_ref, gkv_ref, wuq_ref, wukt_ref,
             qlt_ref, qrt_ref, ckv_ref, ckvb_ref, ckvt_ref, kr_ref, krb_ref):
    a = _rms(mq_ref[...], gq_ref[...]).astype(BF16)
    cq = jnp.dot(a, wuq_ref[...], preferred_element_type=F32)
    cqt = cq.T
    cos = cos_ref[...]
    sin = sin_ref[...]
    qrt_ref[...] = (cqt[512:768] * cos.T + cqt[768:1024] * sin.T).astype(BF16)
    for h in range(MLA_HEADS):
        qnt = cqt[h * MLA_NOPE:(h + 1) * MLA_NOPE].astype(BF16)
        qlt_ref[h * MLA_KV_LORA:(h + 1) * MLA_KV_LORA, :] = jnp.dot(
            wukt_ref[h], qnt, preferred_element_type=F32).astype(BF16)
    ckv = _rms(mkv_ref[...], gkv_ref[...])
    ckv_ref[...] = ckv
    ckvb_ref[...] = ckv.astype(BF16)
    ckvt_ref[0] = ckv.T.astype(BF16)
    mkr = mkr_ref[...]
    kr = mkr[:, 0:32] * cos[:, 0:32] + mkr[:, 32:64] * sin[:, 0:32]
    kr_ref[...] = kr
    krb_ref[...] = kr.astype(BF16)


def _k_mlaprep(z, cos_t, sin_t, gq, gkv, wuq_bf, wukt_bf, *, tm):
    N = z.shape[0]
    nr = cos_t.shape[0] // tm
    tab = pl.BlockSpec((tm, 256), lambda i: (i % nr, 0))
    row = lambda w: pl.BlockSpec((tm, w), lambda i: (i, 0))
    col = lambda h: pl.BlockSpec((h, tm), lambda i: (0, i))
    return pl.pallas_call(
        _mp_body,
        grid=(N // tm,),
        in_specs=[pl.BlockSpec((tm, 256), lambda i: (i, C_MQ // 256)),
                  pl.BlockSpec((tm, 256), lambda i: (i, C_MKV // 256)),
                  pl.BlockSpec((tm, 128), lambda i: (i, C_MKR // 128)),
                  tab, tab, _const_spec((1, 256)), _const_spec((1, 256)),
                  _const_spec((256, 1024)), _const_spec((MLA_HEADS, MLA_KV_LORA, MLA_NOPE))],
        out_specs=[col(2048), col(256), row(256), row(256),
                   pl.BlockSpec((1, 256, tm), lambda i: (i, 0, 0)), row(32), row(32)],
        out_shape=[jax.ShapeDtypeStruct((2048, N), BF16), jax.ShapeDtypeStruct((256, N), BF16),
                   jax.ShapeDtypeStruct((N, 256), F32), jax.ShapeDtypeStruct((N, 256), BF16),
                   jax.ShapeDtypeStruct((N // tm, 256, tm), BF16),
                   jax.ShapeDtypeStruct((N, 32), F32), jax.ShapeDtypeStruct((N, 32), BF16)],
        compiler_params=_cparams(("arbitrary",)),
        name="mla_prep",
    )(z, z, z, cos_t, sin_t, gq, gkv, wuq_bf, wukt_bf)


def _dot_nt(a, b):
    return lax.dot_general(a, b, (((1,), (1,)), ((), ())), preferred_element_type=F32)


MLA_SB = 128
MLA_CB = 256


def _mla_body(qlt_ref, qrt_ref, ckv_ref, kr_ref, ckvt_ref, wuvt_ref, o_ref,
              qt_scr, rt_scr, s_scr, pt_scr, m_scr, l_scr, al_scr, acc_scr,
              *, tq, P, L, KC, nkc_total, stacked):
    qi = pl.program_id(1)
    q0 = P + qi * tq
    cols_n = MLA_HEADS * tq
    if stacked:
        qt_scr[...] = qlt_ref[0]
        rt_scr[...] = qrt_ref[0]
    else:
        for h in range(MLA_HEADS):
            qt_scr[:, h * tq:(h + 1) * tq] = qlt_ref[h * 256:(h + 1) * 256, :]
            rt_scr[:, h * tq:(h + 1) * tq] = qrt_ref[h * 32:(h + 1) * 32, :]
    max_cend = (((q0 + tq - 1) >> 6) + 1) << 6
    nkc = jnp.minimum((max_cend + KC - 1) // KC, nkc_total)
    n_full = jnp.minimum(((q0 >> 6) + 1) << 6, L) // KC
    c = MLA_SCALE * LOG2E
    cb_w = min(MLA_CB, cols_n)

    m_scr[...] = jnp.full((1, cols_n), M_INIT, F32)
    l_scr[...] = jnp.zeros((1, cols_n), F32)
    acc_scr[...] = jnp.zeros((MLA_KV_LORA, cols_n), F32)

    def make_step(masked):
        def step(kc, carry):
            k0 = pl.multiple_of(kc * KC, KC)
            ck = ckv_ref[0, pl.ds(k0, KC), :]
            kr = kr_ref[0, pl.ds(k0, KC), :]
            s_scr[...] = (jnp.dot(ck, qt_scr[...], preferred_element_type=F32)
                          + jnp.dot(kr, rt_scr[...], preferred_element_type=F32))
            for cb in range(cols_n // cb_w):
                cols = slice(cb * cb_w, (cb + 1) * cb_w)
                if masked:
                    lane = cb * cb_w + lax.broadcasted_iota(I32, (1, cb_w), 1)
                    qpos = q0 + (lane & (tq - 1))
                    lim = jnp.minimum(((qpos >> 6) + 1) << 6, L)
                mx = jnp.full((8, cb_w), -jnp.inf, F32)
                for sb in range(KC // MLA_SB):
                    rs = slice(sb * MLA_SB, (sb + 1) * MLA_SB)
                    x = s_scr[rs, cols]
                    if masked:
                        kpos = k0 + sb * MLA_SB + lax.broadcasted_iota(I32, (MLA_SB, 1), 0)
                        x = jnp.where(kpos < lim, x, NEG)
                        s_scr[rs, cols] = x
                    mx = jnp.maximum(mx, _max8(x))
                m_old = m_scr[:, cols]
                m_new = jnp.maximum(m_old, jnp.max(mx, axis=0, keepdims=True))
                alpha = jnp.exp2((m_old - m_new) * c)
                sm = jnp.zeros((8, cb_w), F32)
                for sb in range(KC // MLA_SB):
                    rs = slice(sb * MLA_SB, (sb + 1) * MLA_SB)
                    p = jnp.exp2((s_scr[rs, cols] - m_new) * c)
                    sm = sm + _fold8(p)
                    pt_scr[rs, cols] = p.astype(BF16)
                l_scr[:, cols] = alpha * l_scr[:, cols] + jnp.sum(sm, axis=0, keepdims=True)
                m_scr[:, cols] = m_new
                al_scr[:, cols] = alpha
            acc_scr[...] = al_scr[...] * acc_scr[...] + jnp.dot(ckvt_ref[0, kc], pt_scr[...],
                                                                preferred_element_type=F32)
            return carry
        return step

    lax.fori_loop(0, n_full, make_step(False), 0)
    lax.fori_loop(n_full, nkc, make_step(True), 0)
    ot = (acc_scr[...] / l_scr[...]).astype(BF16)
    if stacked:
        for h in range(MLA_HEADS):
            o_ref[0, h * MLA_V:(h + 1) * MLA_V, :] = jnp.dot(wuvt_ref[h], ot, preferred_element_type=F32)
    else:
        outs = [jnp.dot(wuvt_ref[h], ot[:, h * tq:(h + 1) * tq], preferred_element_type=F32)
                for h in range(MLA_HEADS)]
        o_ref[0] = jnp.concatenate(outs, axis=0).T


def _k_mla(qlt, qrt, ckvb, krb, ckvt, wuvt_bf, *, B, T, tq, P, L, KC, stacked):
    Lp = ckvb.shape[1]
    nkc_total = Lp // KC
    cols_n = MLA_HEADS * tq
    nq = T // tq
    if stacked:
        q_specs = [pl.BlockSpec((1, MLA_KV_LORA, cols_n), lambda b, q: (b, 0, 0)),
                   pl.BlockSpec((1, MLA_ROPE, cols_n), lambda b, q: (b, 0, 0))]
        out_spec = pl.BlockSpec((1, MLA_HEADS * MLA_V, cols_n), lambda b, q: (b, 0, 0))
        out_shape = jax.ShapeDtypeStruct((B, MLA_HEADS * MLA_V, cols_n), F32)
    else:
        q_specs = [pl.BlockSpec((MLA_HEADS * MLA_KV_LORA, tq), lambda b, q: (0, b * nq + q)),
                   pl.BlockSpec((MLA_HEADS * MLA_ROPE, tq), lambda b, q: (0, b * nq + q))]
        out_spec = pl.BlockSpec((1, tq, 512), lambda b, q: (b, q, 0))
        out_shape = jax.ShapeDtypeStruct((B, T, 512), F32)
    return pl.pallas_call(
        functools.partial(_mla_body, tq=tq, P=P, L=L, KC=KC, nkc_total=nkc_total, stacked=stacked),
        grid=(B, nq),
        in_specs=q_specs + [pl.BlockSpec((1, Lp, 256), lambda b, q: (b, 0, 0)),
                            pl.BlockSpec((1, Lp, 32), lambda b, q: (b, 0, 0)),
                            pl.BlockSpec((1, nkc_total, 256, KC), lambda b, q: (b, 0, 0, 0)),
                            _const_spec((MLA_HEADS, MLA_V, MLA_KV_LORA))],
        out_specs=out_spec,
        out_shape=out_shape,
        scratch_shapes=[pltpu.VMEM((MLA_KV_LORA, cols_n), BF16), pltpu.VMEM((MLA_ROPE, cols_n), BF16),
                        pltpu.VMEM((KC, cols_n), F32), pltpu.VMEM((KC, cols_n), BF16),
                        pltpu.VMEM((1, cols_n), F32), pltpu.VMEM((1, cols_n), F32),
                        pltpu.VMEM((1, cols_n), F32), pltpu.VMEM((MLA_KV_LORA, cols_n), F32)],
        compiler_params=_cparams(("arbitrary", "arbitrary")),
        name="mla_attn",
    )(qlt, qrt, ckvb, krb, ckvt, wuvt_bf)


def _fold(x, r):
    n, w = x.shape
    return jnp.sum(x.reshape(n // r, r, w), axis=0)


def _fold8(x):
    return _fold(x, 8)


def _max8(x):
    n, w = x.shape
    return jnp.max(x.reshape(n // 8, 8, w), axis=0)


DSA_SB = 128


def _dsa_body(qt_ref, iqt_ref, iwt_ref, kb_ref, vt_ref, ikb_ref, o_ref,
              s_scr, w_scr, lg_scr, pt_scr, m_scr, l_scr, acc_scr, *, tq, P, L, KC, nkc_total, topk):
    qi = pl.program_id(1)
    q0 = P + qi * tq
    qpos = q0 + lax.broadcasted_iota(I32, (1, tq), 1)
    lim = jnp.minimum(((qpos >> 6) + 1) << 6, L)
    keff = jnp.minimum(lim, topk).astype(F32)
    max_cend = (((q0 + tq - 1) >> 6) + 1) << 6
    nkc = jnp.minimum((max_cend + KC - 1) // KC, nkc_total)

    iqt = iqt_ref[0]
    iwt = iwt_ref[0]
    zpad = jnp.zeros((IDX_DIM, tq), BF16)
    iq_h = [jnp.concatenate([iqt[h * IDX_DIM:(h + 1) * IDX_DIM], zpad], axis=0) for h in range(IDX_HEADS)]

    def score_step(kc, carry):
        k0 = pl.multiple_of(kc * KC, KC)
        ik = ikb_ref[0, pl.ds(k0, KC), :]
        for h in range(IDX_HEADS):
            lg_scr[h, :, 0:tq] = jnp.dot(ik, iq_h[h], preferred_element_type=F32)
        for sb in range(KC // DSA_SB):
            rs = slice(sb * DSA_SB, (sb + 1) * DSA_SB)
            s = jnp.zeros((DSA_SB, tq), F32)
            for h in range(IDX_HEADS):
                s = s + iwt[h:h + 1, :] * jnp.maximum(lg_scr[h, rs, 0:tq], 0.0)
            s = s * IDX_SCALE
            kpos = k0 + sb * DSA_SB + lax.broadcasted_iota(I32, (DSA_SB, 1), 0)
            s = jnp.where(kpos < lim, s, -jnp.inf)
            b = lax.bitcast_convert_type(s, I32)
            b = jnp.where(b == INT_MIN, 0, b)
            s_scr[kc, rs, :] = b ^ ((b >> 31) & np.int32(0x7FFFFFFF))
        return carry

    lax.fori_loop(0, nkc, score_step, 0)

    def bit_step(i, thr):
        cand = thr + jnp.left_shift(jnp.int32(1), 31 - i)

        def cnt_step(kc, acc):
            return acc + _fold(jnp.where(s_scr[kc] >= cand, 1.0, 0.0), 64)

        cnt = jnp.sum(lax.fori_loop(0, nkc, cnt_step, jnp.zeros((64, tq), F32)), axis=0, keepdims=True)
        return jnp.where(cnt >= keff, cand, thr)

    thr = lax.fori_loop(0, 32, bit_step, jnp.full((1, tq), INT_MIN, I32))

    def tie_cnt_step(kc, carry):
        key = s_scr[kc]
        return (carry[0] + _fold8(jnp.where(key > thr, 1.0, 0.0)),
                carry[1] + _fold8(jnp.where(key == thr, 1.0, 0.0)))

    z8 = jnp.zeros((8, tq), F32)
    n_gt, n_eq = lax.fori_loop(0, nkc, tie_cnt_step, (z8, z8))
    need = keff - jnp.sum(n_gt, axis=0, keepdims=True)
    surplus = jnp.max(jnp.sum(n_eq, axis=0, keepdims=True) - need)

    @pl.when(surplus < 0.5)
    def _():
        def fast_step(kc, carry):
            s_scr[kc] = lax.bitcast_convert_type(jnp.where(s_scr[kc] >= thr, 0.0, NEG), I32)
            return carry
        lax.fori_loop(0, nkc, fast_step, 0)

    @pl.when(surplus >= 0.5)
    def _():
        tri = jnp.where(lax.broadcasted_iota(I32, (KC, KC), 1) < lax.broadcasted_iota(I32, (KC, KC), 0),
                        1.0, 0.0).astype(BF16)

        def slow_step(kc, seen):
            key = s_scr[kc]
            eq = key == thr
            eqf = jnp.where(eq, 1.0, 0.0)
            rank = seen + jnp.dot(tri, eqf.astype(BF16), preferred_element_type=F32)
            bias = jnp.where(key > thr, 0.0, jnp.where(eq, jnp.where(rank < need, 0.0, NEG), NEG))
            s_scr[kc] = lax.bitcast_convert_type(bias, I32)
            return seen + jnp.sum(_fold8(eqf), axis=0, keepdims=True)
        lax.fori_loop(0, nkc, slow_step, jnp.zeros((1, tq), F32))

    c = DSA_SCALE * LOG2E
    hd = DSA_HEAD_DIM
    zq = jnp.zeros((hd, tq), BF16)
    for hp in range(DSA_HEADS // 2):
        r0 = hp * 2 * hd
        w_scr[hp, 0:hd, 0:tq] = qt_ref[0, r0:r0 + hd, :]
        w_scr[hp, 0:hd, tq:2 * tq] = zq
        w_scr[hp, hd:2 * hd, 0:tq] = zq
        w_scr[hp, hd:2 * hd, tq:2 * tq] = qt_ref[0, r0 + hd:r0 + 2 * hd, :]
    m_scr[...] = jnp.full((DSA_HEADS, tq), M_INIT, F32)
    l_scr[...] = jnp.zeros((DSA_HEADS, tq), F32)
    acc_scr[...] = jnp.zeros((DSA_HEADS * hd, tq), F32)
    nsb = KC // DSA_SB

    def att_step(kc, carry):
        k0 = pl.multiple_of(kc * KC, KC)
        for hp in range(DSA_HEADS // 2):
            r0 = hp * 2 * hd
            lg_scr[hp] = jnp.dot(kb_ref[0, pl.ds(k0, KC), r0:r0 + 2 * hd], w_scr[hp],
                                 preferred_element_type=F32)
            for e in range(2):
                h = 2 * hp + e
                cols = slice(e * tq, (e + 1) * tq)
                mx = jnp.full((8, tq), -jnp.inf, F32)
                for sb in range(nsb):
                    rs = slice(sb * DSA_SB, (sb + 1) * DSA_SB)
                    x = lg_scr[hp, rs, cols] + lax.bitcast_convert_type(s_scr[kc, rs, :], F32)
                    lg_scr[hp, rs, cols] = x
                    mx = jnp.maximum(mx, _max8(x))
                m_old = m_scr[h:h + 1, :]
                m_new = jnp.maximum(m_old, jnp.max(mx, axis=0, keepdims=True))
                alpha = jnp.exp2((m_old - m_new) * c)
                sm = jnp.zeros((8, tq), F32)
                for sb in range(nsb):
                    rs = slice(sb * DSA_SB, (sb + 1) * DSA_SB)
                    p = jnp.exp2((lg_scr[hp, rs, cols] - m_new) * c)
                    sm = sm + _fold8(p)
                    pt_scr[h, rs, :] = p.astype(BF16)
                l_scr[h:h + 1, :] = alpha * l_scr[h:h + 1, :] + jnp.sum(sm, axis=0, keepdims=True)
                m_scr[h:h + 1, :] = m_new
                hs = slice(h * hd, (h + 1) * hd)
                acc_scr[hs, :] = alpha * acc_scr[hs, :] + jnp.dot(vt_ref[0, kc, hs, :], pt_scr[h],
                                                                  preferred_element_type=F32)
        return carry

    lax.fori_loop(0, nkc, att_step, 0)
    outs = [acc_scr[h * hd:(h + 1) * hd, :] / l_scr[h:h + 1, :] for h in range(DSA_HEADS)]
    o_ref[0] = jnp.concatenate(outs, axis=0).T


def _k_dsa(qt, iqt, iwt, kb3, vt4, ikb3, *, tq, P, L, KC, topk):
    B, _, T = qt.shape
    Lp = kb3.shape[1]
    nkc_total = Lp // KC
    return pl.pallas_call(
        functools.partial(_dsa_body, tq=tq, P=P, L=L, KC=KC, nkc_total=nkc_total, topk=topk),
        grid=(B, T // tq),
        in_specs=[pl.BlockSpec((1, 512, tq), lambda b, q: (b, 0, q)),
                  pl.BlockSpec((1, 256, tq), lambda b, q: (b, 0, q)),
                  pl.BlockSpec((1, 8, tq), lambda b, q: (b, 0, q)),
                  pl.BlockSpec((1, Lp, 512), lambda b, q: (b, 0, 0)),
                  pl.BlockSpec((1, nkc_total, 512, KC), lambda b, q: (b, 0, 0, 0)),
                  pl.BlockSpec((1, Lp, 128), lambda b, q: (b, 0, 0))],
        out_specs=pl.BlockSpec((1, tq, 512), lambda b, q: (b, q, 0)),
        out_shape=jax.ShapeDtypeStruct((B, T, 512), F32),
        scratch_shapes=[pltpu.VMEM((nkc_total, KC, tq), I32),
                        pltpu.VMEM((DSA_HEADS // 2, 2 * DSA_HEAD_DIM, 2 * tq), BF16),
                        pltpu.VMEM((DSA_HEADS // 2, KC, 2 * tq), F32),
                        pltpu.VMEM((DSA_HEADS, KC, tq), BF16),
                        pltpu.VMEM((DSA_HEADS, tq), F32), pltpu.VMEM((DSA_HEADS, tq), F32),
                        pltpu.VMEM((DSA_HEADS * DSA_HEAD_DIM, tq), F32)],
        compiler_params=_cparams(("arbitrary", "arbitrary")),
        name="dsa_attn",
    )(qt, iqt, iwt, kb3, vt4, ikb3)


def _merge_body(x_ref, sc_ref, sh_ref, g1_ref, ng_ref, oa_ref, ob_ref, oc_ref, od_ref,
                wg_ref, bg_ref, wb_ref, wo_ref, o_ref, *, bb, tt):
    x = x_ref[...]
    tm = bb * tt
    h = (_rms(x, ng_ref[...]) * (1.0 + sc_ref[...]) + sh_ref[...]).reshape(tm, D_MODEL).astype(BF16)
    merged = jnp.zeros((tm, D_MODEL), F32)
    for i, oref in enumerate((oa_ref, ob_ref, oc_ref, od_ref)):
        gate = _sigmoid(jnp.dot(h, wg_ref[:, i * D_MODEL:(i + 1) * D_MODEL], preferred_element_type=F32)
                        + bg_ref[:, i * D_MODEL:(i + 1) * D_MODEL])
        br = jnp.dot(oref[...].astype(BF16), wb_ref[i * BRANCH_W:(i + 1) * BRANCH_W, :],
                     preferred_element_type=F32)
        merged = merged + gate * br
    y = jnp.dot(merged.astype(BF16), wo_ref[...], preferred_element_type=F32)
    o_ref[...] = x + g1_ref[...] * y.reshape(bb, tt, D_MODEL)


def _k_merge(x, sc, sh, g1, ng, oa, ob, oc, od, wg_bf, bg, wb_bf, wo_bf, *, bb, tt):
    B, T, D = x.shape
    nt = T // tt
    tm = bb * tt
    xs = pl.BlockSpec((bb, tt, D), lambda i: (i // nt, i % nt, 0))
    ms = pl.BlockSpec((bb, 1, D), lambda i: (i // nt, 0, 0))
    os_ = pl.BlockSpec((tm, 512), lambda i: (i, 0))
    return pl.pallas_call(
        functools.partial(_merge_body, bb=bb, tt=tt),
        grid=((B // bb) * nt,),
        in_specs=[xs, ms, ms, ms, _const_spec((1, 1, D)), os_, os_, os_, os_,
                  _const_spec((D, 4 * D)), _const_spec((1, 4 * D)), _const_spec((4 * BRANCH_W, D)),
                  _const_spec((D, D))],
        out_specs=xs,
        out_shape=jax.ShapeDtypeStruct((B, T, D), F32),
        compiler_params=_cparams(("arbitrary",)),
        name="merge_out",
    )(x, sc, sh, g1, ng, oa, ob, oc, od, wg_bf, bg, wb_bf, wo_bf)


FC = FFN_DIM // 2
MOE_RB = 256


def _swiglu_chunk(h, w1, w3, w2):
    a = jnp.dot(h, w1, preferred_element_type=F32)
    b = jnp.dot(h, w3, preferred_element_type=F32)
    act = (a * _sigmoid(a) * b).astype(BF16)
    return jnp.dot(act, w2, preferred_element_type=F32)


def _finish(x, g2, f, fg_ref, o_ref, bb, tt):
    y = x + g2 * f.reshape(bb, tt, D_MODEL)
    if fg_ref is not None:
        y = _rms(y, fg_ref[...])
    o_ref[...] = y


def _ffn_body(x_ref, sc_ref, sh_ref, g2_ref, ng_ref, w1_ref, w3_ref, w2_ref, *rest, bb, tt, final):
    fg_ref, o_ref = rest if final else (None, rest[0])
    x = x_ref[...]
    tm = bb * tt
    h = (_rms(x, ng_ref[...]) * (1.0 + sc_ref[...]) + sh_ref[...]).reshape(tm, D_MODEL).astype(BF16)
    f = jnp.zeros((tm, D_MODEL), F32)
    for ci in range(FFN_DIM // FC):
        f = f + _swiglu_chunk(h, w1_ref[:, ci * FC:(ci + 1) * FC], w3_ref[:, ci * FC:(ci + 1) * FC],
                              w2_ref[ci * FC:(ci + 1) * FC, :])
    _finish(x, g2_ref[...], f, fg_ref, o_ref, bb, tt)


def _k_ffn(x, sc, sh, g2, ng, w1_bf, w3_bf, w2_bf, fg, *, bb, tt):
    B, T, D = x.shape
    nt = T // tt
    final = fg is not None
    xs = pl.BlockSpec((bb, tt, D), lambda i: (i // nt, i % nt, 0))
    ms = pl.BlockSpec((bb, 1, D), lambda i: (i // nt, 0, 0))
    in_specs = [xs, ms, ms, ms, _const_spec((1, 1, D)), _const_spec((D, FFN_DIM)),
                _const_spec((D, FFN_DIM)), _const_spec((FFN_DIM, D))]
    args = [x, sc, sh, g2, ng, w1_bf, w3_bf, w2_bf]
    if final:
        in_specs.append(_const_spec((1, 1, D)))
        args.append(fg)
    return pl.pallas_call(
        functools.partial(_ffn_body, bb=bb, tt=tt, final=final),
        grid=((B // bb) * nt,),
        in_specs=in_specs,
        out_specs=xs,
        out_shape=jax.ShapeDtypeStruct((B, T, D), F32),
        compiler_params=_cparams(("arbitrary",)),
        name="ffn_dense",
    )(*args)


def _moe_body(x_ref, sc_ref, sh_ref, g2_ref, ng_ref, rw_ref, rb_ref, w1_ref, w3_ref, w2_ref, *rest,
              bb, tt, final):
    if final:
        fg_ref, o_ref, h_scr, gate_scr, rank_scr, rankt_scr, xg_scr, y_scr, acc_scr = rest
    else:
        fg_ref = None
        o_ref, h_scr, gate_scr, rank_scr, rankt_scr, xg_scr, y_scr, acc_scr = rest
    e = pl.program_id(1)
    ci = pl.program_id(2)
    last_c = FFN_DIM // FC - 1
    tm = bb * tt
    lane = lax.broadcasted_iota(I32, (tm, 128), 1)

    @pl.when((e == 0) & (ci == 0))
    def _():
        h = (_rms(x_ref[...], ng_ref[...]) * (1.0 + sc_ref[...]) + sh_ref[...]).reshape(tm, D_MODEL)
        h_scr[...] = h.astype(BF16)
        logits = jnp.dot(h, rw_ref[...], preferred_element_type=F32,
                         precision=lax.Precision.HIGHEST) + rb_ref[...]
        m1 = jnp.max(logits, axis=-1, keepdims=True)
        i1 = jnp.min(jnp.where(logits == m1, lane, 128), axis=-1, keepdims=True)
        rest_l = jnp.where(lane == i1, NEG, logits)
        m2 = jnp.max(rest_l, axis=-1, keepdims=True)
        i2 = jnp.min(jnp.where(rest_l == m2, lane, 128), axis=-1, keepdims=True)
        e2 = jnp.exp(m2 - m1)
        den = 1.0 + e2
        gate_scr[...] = jnp.where(lane == i1, 1.0 / den, jnp.where(lane == i2, e2 / den, 0.0))
        sel = jnp.where(lane == i1, 1.0, jnp.where(lane == i2, 1.0, 0.0))
        tri = jnp.where(lax.broadcasted_iota(I32, (tm, tm), 1) < lax.broadcasted_iota(I32, (tm, tm), 0),
                        1.0, 0.0).astype(BF16)
        rank = jnp.where(sel > 0.5, jnp.dot(tri, sel.astype(BF16), preferred_element_type=F32), -1.0)
        rank_scr[...] = rank
        rankt_scr[...] = rank.T
        acc_scr[...] = jnp.zeros((tm, D_MODEL), F32)

    is_e = lane == e
    gate_col = jnp.sum(jnp.where(is_e, gate_scr[...], 0.0), axis=-1, keepdims=True)
    rank_col = jnp.sum(jnp.where(is_e, rank_scr[...], 0.0), axis=-1, keepdims=True)
    rank_row = rankt_scr[pl.ds(e, 1), :]
    n_e = jnp.max(rank_row).astype(I32) + 1
    nblk = (n_e + MOE_RB - 1) // MOE_RB

    def blk_step(b, carry):
        base = (b * MOE_RB).astype(F32)

        @pl.when(ci == 0)
        def _():
            slot = base + lax.broadcasted_iota(I32, (MOE_RB, 1), 0).astype(F32)
            pick = jnp.where(rank_row == slot, 1.0, 0.0).astype(BF16)
            xg_scr[b] = jnp.dot(pick, h_scr[...], preferred_element_type=F32).astype(BF16)

        part = _swiglu_chunk(xg_scr[b], w1_ref[0], w3_ref[0], w2_ref[0])

        @pl.when(ci == 0)
        def _():
            y_scr[b] = part

        @pl.when((ci > 0) & (ci < last_c))
        def _():
            y_scr[b] += part

        @pl.when(ci == last_c)
        def _():
            y = y_scr[b] + part
            y_hi = y.astype(BF16)
            y_lo = (y - y_hi.astype(F32)).astype(BF16)
            slot = base + lax.broadcasted_iota(I32, (1, MOE_RB), 1).astype(F32)
            put = jnp.where(rank_col == slot, 1.0, 0.0).astype(BF16)
            acc_scr[...] += gate_col * (jnp.dot(put, y_hi, preferred_element_type=F32)
                                        + jnp.dot(put, y_lo, preferred_element_type=F32))
        return carry

    lax.fori_loop(0, nblk, blk_step, 0)

    @pl.when((e == N_EXPERTS - 1) & (ci == last_c))
    def _():
        _finish(x_ref[...], g2_ref[...], acc_scr[...], fg_ref, o_ref, bb, tt)


def _k_moe(x, sc, sh, g2, ng, rw_pad, rb_pad, w1_bf, w3_bf, w2_bf, fg, *, bb, tt):
    B, T, D = x.shape
    nt = T // tt
    tm = bb * tt
    final = fg is not None
    xs = pl.BlockSpec((bb, tt, D), lambda i, e, c: (i // nt, i % nt, 0))
    xin = pl.BlockSpec((bb, tt, D), lambda i, e, c: (i // nt, i % nt, 0), pipeline_mode=pl.Buffered(1))
    ms = pl.BlockSpec((bb, 1, D), lambda i, e, c: (i // nt, 0, 0))
    in_specs = [xin, ms, ms, ms, _const_spec((1, 1, D)), _const_spec((D, 128)), _const_spec((1, 128)),
                pl.BlockSpec((1, D, FC), lambda i, e, c: (e, 0, c)),
                pl.BlockSpec((1, D, FC), lambda i, e, c: (e, 0, c)),
                pl.BlockSpec((1, FC, D), lambda i, e, c: (e, c, 0))]
    args = [x, sc, sh, g2, ng, rw_pad, rb_pad, w1_bf, w3_bf, w2_bf]
    if final:
        in_specs.append(_const_spec((1, 1, D)))
        args.append(fg)
    return pl.pallas_call(
        functools.partial(_moe_body, bb=bb, tt=tt, final=final),
        grid=((B // bb) * nt, N_EXPERTS, FFN_DIM // FC),
        in_specs=in_specs,
        out_specs=xs,
        out_shape=jax.ShapeDtypeStruct((B, T, D), F32),
        scratch_shapes=[pltpu.VMEM((tm, D), BF16), pltpu.VMEM((tm, 128), F32), pltpu.VMEM((tm, 128), F32),
                        pltpu.VMEM((128, tm), F32), pltpu.VMEM((pl.cdiv(tm, MOE_RB), MOE_RB, D), BF16),
                        pltpu.VMEM((pl.cdiv(tm, MOE_RB), MOE_RB, D), F32), pltpu.VMEM((tm, D), F32)],
        compiler_params=_cparams(("arbitrary", "arbitrary", "arbitrary")),
        name="ffn_moe",
    )(*args)


def _prep_w_in(w):
    aq, ak, av, iq, ik, iw, pu, cb, cc, ch, mq, mkv, mkr = jnp.split(w, IN_SPLIT_POINTS, axis=1)
    d = w.shape[0]
    mkr_sw = jnp.concatenate([mkr[:, 16:], mkr[:, :16]], axis=1)
    return jnp.concatenate([aq, ak, av, pu, cb, cc, ch, iq, mq, mkv, ik, iw, jnp.zeros((d, 60), w.dtype),
                            mkr, mkr_sw, jnp.zeros((d, 64), w.dtype)], axis=1).astype(BF16)


def _prep_w_uq(w):
    w3 = w.reshape(MLA_Q_LORA, MLA_HEADS, MLA_NOPE + MLA_ROPE)
    nope = w3[:, :, :MLA_NOPE].reshape(MLA_Q_LORA, MLA_HEADS * MLA_NOPE)
    rope = w3[:, :, MLA_NOPE:]
    rope_sw = jnp.concatenate([rope[:, :, 16:], rope[:, :, :16]], axis=-1)
    return jnp.concatenate([nope, rope.reshape(MLA_Q_LORA, -1), rope_sw.reshape(MLA_Q_LORA, -1)],
                           axis=1).astype(BF16)


def _rope_tables(pos):
    half = MLA_ROPE // 2
    inv = 1.0 / (ROPE_BASE ** (jnp.arange(half, dtype=F32) / half))
    ang = pos.astype(F32)[:, None] * inv[None, :]
    cos, sin = jnp.cos(ang), jnp.sin(ang)
    cos2 = jnp.tile(jnp.concatenate([cos, cos], axis=-1), (1, MLA_HEADS))
    sin2 = jnp.tile(jnp.concatenate([-sin, sin], axis=-1), (1, MLA_HEADS))
    return cos2, sin2


def _layer(x, mod, past, lw, fw, use_moe, final_g, cfg):
    B, T, D = x.shape
    bb, tt, KC, tq_dsa, tq_mla = cfg["bb"], cfg["tt"], cfg["KC"], cfg["tq_dsa"], cfg["tq_mla"]
    N = B * T
    P = 0 if past is None else past["dsa_k"].shape[1]
    L = P + T
    sh1, sc1, g1, sh2, sc2, g2 = [m.reshape(B, 1, D) for m in jnp.split(mod, 6, axis=-1)]
    ng1 = lw["norm_mix_g"].reshape(1, 1, D)
    ng2 = lw["norm_ffn_g"].reshape(1, 1, D)

    emit = past is None
    res = _k_in(x, sc1, sh1, ng1, lw["w_in"], bb=bb, tt=tt, emit=emit)
    z = res[0]
    z3 = z.reshape(B, T, ZW)
    k_new = z[:, C_AK:C_AK + 512].reshape(B, T, DSA_HEADS, DSA_HEAD_DIM)
    v_new = z[:, C_AV:C_AV + 512].reshape(B, T, DSA_HEADS, DSA_HEAD_DIM)
    ik_new = z3[:, :, C_IKW:C_IKW + IDX_DIM]

    if past is None:
        ph16 = jnp.zeros((B, 16, 512), F32)
        cvh16 = ph16
    else:
        ph16 = jnp.pad(past["pool"], ((0, 0), (1, 0), (0, 0)))
        cvh16 = jnp.pad(past["conv"], ((0, 0), (16 - (CONV_WIDTH - 1), 0), (0, 0)))
    o_b, o_c, ph_o, cvh_o = _k_pc(z3, ph16, cvh16, lw["pool_w"], lw["pool_scale"], lw["conv_w"],
                                  bb=bb, tt=tt, P=P)
    pool_hist = ph_o[:, 1:]
    conv_hist = cvh_o[:, 16 - (CONV_WIDTH - 1):]

    cos2, sin2 = _rope_tables(P + jnp.arange(T))
    if bb > 1:
        cos2, sin2 = jnp.tile(cos2, (bb, 1)), jnp.tile(sin2, (bb, 1))
    qlt, qrt, ckv_new, ckvb, ckvt, kr_new, krb = _k_mlaprep(
        z, cos2, sin2, lw["mla_q_norm_g"], lw["mla_kv_norm_g"], lw["mla_w_uq"], lw["mla_w_uk"], tm=bb * tt)

    topk = min(DSA_TOPK_MAX, L // 4)
    if past is None:
        kb3 = res[1].reshape(B, T, 512)
        vt4, qt, iqt, iwt = res[2], res[3], res[4], res[5]
        ikb3 = res[6].reshape(B, T, 128)
        ckvb3 = ckvb.reshape(B, T, 256)
        krb3 = krb.reshape(B, T, 32)
        Tq = T
    else:
        Lp = -(-L // KC) * KC
        padk = ((0, 0), (0, Lp - L), (0, 0))
        k_all = jnp.concatenate([past["dsa_k"].reshape(B, P, 512), k_new.reshape(B, T, 512)], axis=1)
        v_all = jnp.concatenate([past["dsa_v"].reshape(B, P, 512), v_new.reshape(B, T, 512)], axis=1)
        kb3 = jnp.pad(k_all.astype(BF16), padk)
        vt4 = jnp.pad(v_all.astype(BF16), padk).reshape(B, Lp // KC, KC, 512).transpose(0, 1, 3, 2)
        ik_all = jnp.concatenate([past["idx_k"], ik_new], axis=1).astype(BF16)
        ikb3 = jnp.pad(ik_all, ((0, 0), (0, Lp - L), (0, 128 - IDX_DIM)))
        Tq = tq_dsa
        padq = ((0, 0), (0, 0), (0, Tq - T))
        qt = jnp.pad(z3[:, :, C_AQ:C_AQ + 512].transpose(0, 2, 1).astype(BF16), padq)
        iqt = jnp.pad(z3[:, :, C_IQ:C_IQ + 256].transpose(0, 2, 1).astype(BF16), padq)
        iwt = jnp.pad(z3[:, :, C_IKW + 64:C_IKW + 72].transpose(0, 2, 1), padq)
        ckvb3 = jnp.pad(jnp.concatenate([past["ckv"].astype(BF16), ckvb.reshape(B, T, 256)], axis=1), padk)
        krb3 = jnp.pad(jnp.concatenate([past["krope"].astype(BF16), krb.reshape(B, T, 32)], axis=1), padk)

    o_a = _k_dsa(qt, iqt, iwt, kb3, vt4, ikb3, tq=tq_dsa, P=P, L=L, KC=KC, topk=topk)
    o_a = o_a[:, :T].reshape(N, 512)
    if past is None:
        ckvt4 = ckvt.reshape(B, T // KC, 256, KC)
        o_d = _k_mla(qlt, qrt, ckvb3, krb3, ckvt4, lw["mla_w_uv"], B=B, T=T, tq=tq_mla, P=P, L=L, KC=KC,
                     stacked=False).reshape(N, 512)
    else:
        qs = qlt.reshape(MLA_HEADS, 256, B, T).transpose(2, 1, 0, 3).reshape(B, 256, MLA_HEADS * T)
        rs = qrt.reshape(MLA_HEADS, 32, B, T).transpose(2, 1, 0, 3).reshape(B, 32, MLA_HEADS * T)
        ckvt4 = ckvb3.reshape(B, ckvb3.shape[1] // KC, KC, 256).transpose(0, 1, 3, 2)
        o5 = _k_mla(qs, rs, ckvb3, krb3, ckvt4, lw["mla_w_uv"], B=B, T=T, tq=tq_mla, P=P, L=L, KC=KC,
                    stacked=True).reshape(B, MLA_HEADS, MLA_V, MLA_HEADS, T)
        o_d = jnp.stack([o5[:, h, :, h, :] for h in range(MLA_HEADS)], axis=1)
        o_d = o_d.transpose(0, 3, 1, 2).reshape(N, 512)

    x1 = _k_merge(x, sc1, sh1, g1, ng1, o_a, o_b.reshape(N, 512), o_c.reshape(N, 512), o_d,
                  lw["w_gate"], lw["b_gate"], lw["w_branch"], lw["w_out"], bb=bb, tt=tt)
    fg = None if final_g is None else final_g.reshape(1, 1, D)
    if use_moe:
        x2 = _k_moe(x1, sc2, sh2, g2, ng2, fw["rw"], fw["rb"], fw["w1"], fw["w3"], fw["w2"], fg,
                    bb=cfg["bb_moe"], tt=cfg["tt_moe"])
    else:
        x2 = _k_ffn(x1, sc2, sh2, g2, ng2, fw["w1"], fw["w3"], fw["w2"], fg, bb=bb, tt=tt)
    rows = (k_new, v_new, ik_new, ckv_new.reshape(B, T, 256), kr_new.reshape(B, T, 32), pool_hist, conv_hist)
    return x2, rows


def _prep_layer(l, ada_w, ada_b, norm_mix_g, norm_ffn_g, w_in, mla_q_norm_g, mla_kv_norm_g,
                mla_w_uq, mla_w_uk, mla_w_uv, pool_w, pool_scale, conv_w, w_gate, b_gate, w_branch, w_out):
    return dict(
        ada_w=ada_w[l].astype(BF16), ada_b=ada_b[l].reshape(1, -1),
        norm_mix_g=norm_mix_g[l], norm_ffn_g=norm_ffn_g[l],
        w_in=_prep_w_in(w_in[l]),
        mla_q_norm_g=mla_q_norm_g[l].reshape(1, -1), mla_kv_norm_g=mla_kv_norm_g[l].reshape(1, -1),
        mla_w_uq=_prep_w_uq(mla_w_uq[l]),
        mla_w_uk=mla_w_uk[l].transpose(1, 0, 2).astype(BF16),
        mla_w_uv=mla_w_uv[l].transpose(1, 2, 0).astype(BF16),
        pool_w=pool_w[l].astype(BF16), pool_scale=pool_scale[l].reshape(1, -1),
        conv_w=jnp.pad(conv_w[l].reshape(CONV_WIDTH, BRANCH_W), ((0, 8 - CONV_WIDTH), (0, 0))),
        w_gate=w_gate[l].astype(BF16), b_gate=b_gate[l].reshape(1, -1),
        w_branch=w_branch[l].astype(BF16), w_out=w_out[l].astype(BF16))


def _cfg_for(B, T, has_past):
    if has_past:
        return dict(bb=B, tt=T, KC=512, tq_dsa=128, tq_mla=T, bb_moe=B, tt_moe=T)
    t = min(512, T)
    return dict(bb=1, tt=t, KC=t, tq_dsa=min(256, T), tq_mla=min(128, T), bb_moe=1, tt_moe=min(1024, T))


def _run(x, c, pasts, lws, fws, final_norm_g):
    B, T, _ = x.shape
    cfg = _cfg_for(B, T, pasts is not None)
    outs = [[] for _ in range(7)]
    depth = len(lws)
    for l in range(depth):
        lw = lws[l]
        mod = _ada(c, lw["ada_w"], lw["ada_b"])
        x, rows = _layer(x, mod, None if pasts is None else pasts[l], lw, fws[l], l % 2 == 1,
                         final_norm_g if l == depth - 1 else None, cfg)
        for o, r in zip(outs, rows):
            o.append(r)
    return x, [jnp.stack(o) for o in outs]


def kernel(x_prompt, x_sample, c_prompt, c_sample, cache_dsa_k, cache_dsa_v, cache_dsa_idx_k, cache_mla_ckv, cache_mla_krope, state_pool, state_conv, ada_w, ada_b, norm_mix_g, norm_ffn_g, w_in, mla_q_norm_g, mla_kv_norm_g, mla_w_uq, mla_w_uk, mla_w_uv, pool_w, pool_scale, conv_w, w_gate, b_gate, w_branch, w_out, ffn_w1, ffn_w3, ffn_w2, moe_router_w, moe_router_b, moe_w1, moe_w3, moe_w2, final_norm_g):
    depth = ada_w.shape[0]
    lws = [_prep_layer(l, ada_w, ada_b, norm_mix_g, norm_ffn_g, w_in, mla_q_norm_g, mla_kv_norm_g,
                       mla_w_uq, mla_w_uk, mla_w_uv, pool_w, pool_scale, conv_w, w_gate, b_gate,
                       w_branch, w_out) for l in range(depth)]
    fws = []
    for l in range(depth):
        j = l // 2
        if l % 2 == 0:
            fws.append(dict(w1=ffn_w1[j].astype(BF16), w3=ffn_w3[j].astype(BF16), w2=ffn_w2[j].astype(BF16)))
        else:
            fws.append(dict(
                rw=jnp.pad(moe_router_w[j], ((0, 0), (0, 128 - N_EXPERTS))),
                rb=jnp.pad(moe_router_b[j].reshape(1, -1), ((0, 0), (0, 128 - N_EXPERTS)), constant_values=NEG),
                w1=moe_w1[j].astype(BF16), w3=moe_w3[j].astype(BF16), w2=moe_w2[j].astype(BF16)))
    sample_pasts = [dict(dsa_k=cache_dsa_k[l], dsa_v=cache_dsa_v[l], idx_k=cache_dsa_idx_k[l],
                         ckv=cache_mla_ckv[l], krope=cache_mla_krope[l],
                         pool=state_pool[l], conv=state_conv[l]) for l in range(depth)]
    y_prompt, pn = _run(x_prompt, c_prompt, None, lws, fws, final_norm_g)
    y_sample, sn = _run(x_sample, c_sample, sample_pasts, lws, fws, final_norm_g)
    return (y_prompt, y_sample, pn[0], pn[1], pn[2], pn[3], pn[4], pn[5], pn[6],
            sn[0], sn[1], sn[2], sn[3], sn[4], sn[5], sn[6])
```

```python
import functools

import numpy as np
import jax
import jax.numpy as jnp
from jax import lax
from jax.experimental import pallas as pl
from jax.experimental.pallas import tpu as pltpu

F32 = jnp.float32
BF16 = jnp.bfloat16
I32 = jnp.int32

D_MODEL = 1024
CHUNK = 64
N_BRANCH = 4
BRANCH_W = 512
DSA_HEADS = 8
DSA_HEAD_DIM = 64
IDX_HEADS = 4
IDX_DIM = 64
DSA_TOPK_MAX = 256
POOL_WINDOWS = (2, 4, 8, 16)
POOL_GROUP = BRANCH_W // 4
POOL_HIST = 15
CONV_WIDTH = 3
MLA_HEADS = 8
MLA_Q_LORA = 256
MLA_KV_LORA = 256
MLA_NOPE = 64
MLA_ROPE = 32
MLA_V = 64
ROPE_BASE = 10000.0
FFN_DIM = 2816
N_EXPERTS = 8
EPS = 1e-6

DSA_SCALE = DSA_HEAD_DIM ** -0.5
IDX_SCALE = (IDX_HEADS * IDX_DIM) ** -0.5
MLA_SCALE = (MLA_NOPE + MLA_ROPE) ** -0.5
LOG2E = 1.4426950408889634

IN_SPLIT_WIDTHS = (512, 512, 512, 256, 64, 4, 512, 512, 512, 512, 256, 256, 32)
IN_SPLIT_POINTS = tuple(int(sum(IN_SPLIT_WIDTHS[:i + 1])) for i in range(len(IN_SPLIT_WIDTHS) - 1))

C_AQ, C_AK, C_AV, C_PU, C_CB, C_CC, C_CH = 0, 512, 1024, 1536, 2048, 2560, 3072
C_IQ, C_MQ, C_MKV, C_IKW, C_MKR, ZW = 3584, 3840, 4096, 4352, 4480, 4608
IN_TN = 1536
Z_OFF = IN_TN
DSA_VR = DSA_HEAD_DIM + 16
MLA_VR = MLA_KV_LORA + 16

NEG = -1e30
M_INIT = -1e29
INT_MIN = np.int32(-2 ** 31)
V7X_VMEM_LIMIT = 56 * 1024 * 1024


def _cparams(sem):
    return pltpu.CompilerParams(dimension_semantics=sem, vmem_limit_bytes=V7X_VMEM_LIMIT)


def _rms(x, g):
    return x * lax.rsqrt(jnp.mean(x * x, axis=-1, keepdims=True) + EPS) * g


def _sigmoid(x):
    return 1.0 / (1.0 + jnp.exp(-x))


def _const_spec(shape):
    nd = len(shape)
    return pl.BlockSpec(shape, lambda *_: (0,) * nd, pipeline_mode=pl.Buffered(1))


def _ada_body(c_ref, w_ref, b_ref, o_ref):
    c = c_ref[...]
    s = (c * _sigmoid(c)).astype(BF16)
    o_ref[...] = jnp.dot(s, w_ref[...], preferred_element_type=F32) + b_ref[...]


def _ada(c, w_bf, b):
    B, D = c.shape
    n = w_bf.shape[1]
    tn = 1536
    return pl.pallas_call(
        _ada_body,
        grid=(n // tn,),
        in_specs=[pl.BlockSpec((B, D), lambda j: (0, 0)),
                  pl.BlockSpec((D, tn), lambda j: (0, j)),
                  pl.BlockSpec((1, tn), lambda j: (0, j))],
        out_specs=pl.BlockSpec((B, tn), lambda j: (0, j)),
        out_shape=jax.ShapeDtypeStruct((B, n), F32),
        compiler_params=_cparams(("arbitrary",)),
        name="ada_mod",
    )(c, w_bf, b)


def _in_body(x_ref, sc_ref, sh_ref, g_ref, w_ref, z_ref, *rest, bb, tt, emit, kc):
    if emit:
        kn_ref, vn_ref, kb_ref, vt_ref, qt_ref, iqt_ref, iwt_ref, ikb_ref, h_scr = rest
    else:
        (h_scr,) = rest
    j = pl.program_id(1)

    @pl.when(j == 0)
    def _():
        h = _rms(x_ref[...], g_ref[...]) * (1.0 + sc_ref[...]) + sh_ref[...]
        h_scr[...] = h.reshape(bb * tt, D_MODEL).astype(BF16)

    z = jnp.dot(h_scr[...], w_ref[...], preferred_element_type=F32)

    if not emit:
        z_ref[...] = z
    else:
        @pl.when(j > 0)
        def _():
            z_ref[...] = z

        @pl.when(j == 0)
        def _():
            k = z[:, C_AK:C_AK + 512]
            v = z[:, C_AV:C_AV + 512]
            kn_ref[...] = k
            vn_ref[...] = v
            kb_ref[...] = k.astype(BF16)
            vt = v.T.astype(BF16)
            ones = jnp.ones((DSA_VR - DSA_HEAD_DIM, kc), BF16)
            for c in range(tt // kc):
                for h in range(DSA_HEADS):
                    vt_ref[0, c, h * DSA_VR:h * DSA_VR + DSA_HEAD_DIM, :] = vt[
                        h * DSA_HEAD_DIM:(h + 1) * DSA_HEAD_DIM, c * kc:(c + 1) * kc]
                    vt_ref[0, c, h * DSA_VR + DSA_HEAD_DIM:(h + 1) * DSA_VR, :] = ones
            qt_ref[0] = (z[:, C_AQ:C_AQ + 512] * (DSA_SCALE * LOG2E)).T.astype(BF16)

        @pl.when(j == 2)
        def _():
            o = 2 * IN_TN
            iqt_ref[0] = z[:, C_IQ - o:C_IQ - o + 256].T.astype(BF16)
            ikw = z[:, C_IKW - o:C_IKW - o + 128]
            ikb_ref[...] = ikw.astype(BF16)
            iwt_ref[0] = ikw.T[64:72, :] * IDX_SCALE


def _k_in(x, sc, sh, g, w_bf, *, bb, tt, emit, kc):
    B, T, D = x.shape
    nt = T // tt
    tm = bb * tt
    n_rows = (B // bb) * nt
    N = B * T
    xmap = lambda i, j: (i // nt, i % nt, 0)
    mmap = lambda i, j: (i // nt, 0, 0)
    in_specs = [pl.BlockSpec((bb, tt, D), xmap),
                pl.BlockSpec((bb, 1, D), mmap),
                pl.BlockSpec((bb, 1, D), mmap),
                _const_spec((1, 1, D)),
                pl.BlockSpec((D, IN_TN), lambda i, j: (0, j))]
    if emit:
        assert bb == 1
        out_specs = [pl.BlockSpec((tm, IN_TN), lambda i, j: (i, jnp.maximum(j - 1, 0)))]
        out_shape = [jax.ShapeDtypeStruct((N, ZW - Z_OFF), F32)]
        out_specs += [
            pl.BlockSpec((tm, 512), lambda i, j: (i, 0)),
            pl.BlockSpec((tm, 512), lambda i, j: (i, 0)),
            pl.BlockSpec((tm, 512), lambda i, j: (i, 0)),
            pl.BlockSpec((1, tt // kc, DSA_HEADS * DSA_VR, kc), lambda i, j: (i // nt, i % nt, 0, 0)),
            pl.BlockSpec((1, 512, tt), lambda i, j: (i // nt, 0, i % nt)),
            pl.BlockSpec((1, 256, tt), lambda i, j: (i // nt, 0, i % nt)),
            pl.BlockSpec((1, 8, tt), lambda i, j: (i // nt, 0, i % nt)),
            pl.BlockSpec((tm, 128), lambda i, j: (i, 0)),
        ]
        out_shape += [
            jax.ShapeDtypeStruct((N, 512), F32),
            jax.ShapeDtypeStruct((N, 512), F32),
            jax.ShapeDtypeStruct((N, 512), BF16),
            jax.ShapeDtypeStruct((B, T // kc, DSA_HEADS * DSA_VR, kc), BF16),
            jax.ShapeDtypeStruct((B, 512, T), BF16),
            jax.ShapeDtypeStruct((B, 256, T), BF16),
            jax.ShapeDtypeStruct((B, 8, T), F32),
            jax.ShapeDtypeStruct((N, 128), BF16),
        ]
    else:
        out_specs = [pl.BlockSpec((tm, IN_TN), lambda i, j: (i, j))]
        out_shape = [jax.ShapeDtypeStruct((N, ZW), F32)]
    return pl.pallas_call(
        functools.partial(_in_body, bb=bb, tt=tt, emit=emit, kc=kc),
        grid=(n_rows, ZW // IN_TN),
        in_specs=in_specs,
        out_specs=out_specs,
        out_shape=out_shape,
        scratch_shapes=[pltpu.VMEM((tm, D), BF16)],
        compiler_params=_cparams(("arbitrary", "arbitrary")),
        name="in_proj",
    )(x, sc, sh, g, w_bf)


def _pc_body(pu_ref, cb_ref, cc_ref, ch_ref, pup_ref, ccp_ref, chp_ref, ph_ref, cvh_ref,
             pw_ref, ps_ref, cw_ref, ob_ref, oc_ref, pho_ref, cho_ref, u_scr, g_scr, *, bb, tt, P):
    t = pl.program_id(1)
    first = t == 0
    pu = pu_ref[...]
    g = cc_ref[...] * ch_ref[...]
    u_scr[:, 0:16, :] = jnp.where(first, ph_ref[...], pup_ref[...])
    u_scr[:, 16:16 + tt, :] = pu
    g_scr[:, 0:16, :] = jnp.where(first, cvh_ref[...], ccp_ref[...] * chp_ref[...])
    g_scr[:, 16:16 + tt, :] = g

    pos = P + t * tt + lax.broadcasted_iota(I32, (1, tt, 1), 1)
    for gi, win in enumerate(POOL_WINDOWS):
        lo = gi * POOL_GROUP
        acc = pu[:, :, lo:lo + POOL_GROUP]
        for j in range(1, win):
            acc = acc + u_scr[:, 16 - j:16 - j + tt, lo:lo + POOL_GROUP]
        cnt = jnp.minimum(pos + 1, win).astype(F32)
        d = acc / cnt - pu[:, :, lo:lo + POOL_GROUP]
        y = jnp.dot(d.reshape(bb * tt, POOL_GROUP).astype(BF16), pw_ref[gi], preferred_element_type=F32)
        ob_ref[:, :, lo:lo + POOL_GROUP] = (y * ps_ref[:, lo:lo + POOL_GROUP]).reshape(bb, tt, POOL_GROUP)

    cw = cw_ref[...]
    conv = (cw[0:1, :] * g_scr[:, 14:14 + tt, :] + cw[1:2, :] * g_scr[:, 15:15 + tt, :] + cw[2:3, :] * g)
    oc_ref[...] = cb_ref[...] * conv
    pho_ref[...] = u_scr[:, tt:tt + 16, :]
    cho_ref[...] = g_scr[:, tt:tt + 16, :]


def _k_pc(z3, ph16, cvh16, pw_bf, ps, cw, *, bb, tt, P, zoff):
    B, T, _ = z3.shape
    nt = T // tt
    r = tt // 16

    def cur(c):
        return pl.BlockSpec((bb, tt, 512), lambda b, t: (b, t, (c - zoff) // 512))

    def prev(c):
        return pl.BlockSpec((bb, 16, 512), lambda b, t: (b, jnp.maximum(t * r - 1, 0), (c - zoff) // 512))

    hist = pl.BlockSpec((bb, 16, 512), lambda b, t: (b, 0, 0))
    return pl.pallas_call(
        functools.partial(_pc_body, bb=bb, tt=tt, P=P),
        grid=(B // bb, nt),
        in_specs=[cur(C_PU), cur(C_CB), cur(C_CC), cur(C_CH), prev(C_PU), prev(C_CC), prev(C_CH),
                  hist, hist, _const_spec((4, POOL_GROUP, POOL_GROUP)), _const_spec((1, 512)),
                  _const_spec((8, 512))],
        out_specs=[pl.BlockSpec((bb, tt, 512), lambda b, t: (b, t, 0)),
                   pl.BlockSpec((bb, tt, 512), lambda b, t: (b, t, 0)), hist, hist],
        out_shape=[jax.ShapeDtypeStruct((B, T, 512), F32), jax.ShapeDtypeStruct((B, T, 512), F32),
                   jax.ShapeDtypeStruct((B, 16, 512), F32), jax.ShapeDtypeStruct((B, 16, 512), F32)],
        scratch_shapes=[pltpu.VMEM((bb, tt + 16, 512), F32), pltpu.VMEM((bb, tt + 16, 512), F32)],
        compiler_params=_cparams(("arbitrary", "arbitrary")),
        name="pool_conv",
    )(z3, z3, z3, z3, z3, z3, z3, ph16, cvh16, pw_bf, ps, cw)


def _mp_body(mq_ref, mkv_ref, mkr_ref, cos_ref, sin_ref, gq_ref, gkv_ref, wuq_ref, wukt_ref,
             qlt_ref, qrt_ref, ckv_ref, ckvb_ref, ckvt_ref, kr_ref, krb_ref):
    a = _rms(mq_ref[...], gq_ref[...]).astype(BF16)
    cq = jnp.dot(a, wuq_ref[...], preferred_element_type=F32)
    cqt = cq.T
    cos = cos_ref[...]
    sin = sin_ref[...]
    c = MLA_SCALE * LOG2E
    qrt_ref[...] = ((cqt[512:768] * cos.T + cqt[768:1024] * sin.T) * c).astype(BF16)
    for h in range(MLA_HEADS):
        qnt = cqt[h * MLA_NOPE:(h + 1) * MLA_NOPE].astype(BF16)
        qlt_ref[h * MLA_KV_LORA:(h + 1) * MLA_KV_LORA, :] = (jnp.dot(
            wukt_ref[h], qnt, preferred_element_type=F32) * c).astype(BF16)
    ckv = _rms(mkv_ref[...], gkv_ref[...])
    ckv_ref[...] = ckv
    ckvb_ref[...] = ckv.astype(BF16)
    ckvt_ref[0, 0:MLA_KV_LORA, :] = ckv.T.astype(BF16)
    ckvt_ref[0, MLA_KV_LORA:MLA_VR, :] = jnp.ones((MLA_VR - MLA_KV_LORA, ckv.shape[0]), BF16)
    mkr = mkr_ref[...]
    kr = mkr[:, 0:32] * cos[:, 0:32] + mkr[:, 32:64] * sin[:, 0:32]
    kr_ref[...] = kr
    krb_ref[...] = kr.astype(BF16)


def _k_mlaprep(z, cos_t, sin_t, gq, gkv, wuq_bf, wukt_bf, *, tm, zoff):
    N = z.shape[0]
    nr = cos_t.shape[0] // tm
    tab = pl.BlockSpec((tm, 256), lambda i: (i % nr, 0))
    row = lambda w: pl.BlockSpec((tm, w), lambda i: (i, 0))
    col = lambda h: pl.BlockSpec((h, tm), lambda i: (0, i))
    return pl.pallas_call(
        _mp_body,
        grid=(N // tm,),
        in_specs=[pl.BlockSpec((tm, 256), lambda i: (i, (C_MQ - zoff) // 256)),
                  pl.BlockSpec((tm, 256), lambda i: (i, (C_MKV - zoff) // 256)),
                  pl.BlockSpec((tm, 128), lambda i: (i, (C_MKR - zoff) // 128)),
                  tab, tab, _const_spec((1, 256)), _const_spec((1, 256)),
                  _const_spec((256, 1024)), _const_spec((MLA_HEADS, MLA_KV_LORA, MLA_NOPE))],
        out_specs=[col(2048), col(256), row(256), row(256),
                   pl.BlockSpec((1, MLA_VR, tm), lambda i: (i, 0, 0)), row(32), row(32)],
        out_shape=[jax.ShapeDtypeStruct((2048, N), BF16), jax.ShapeDtypeStruct((256, N), BF16),
                   jax.ShapeDtypeStruct((N, 256), F32), jax.ShapeDtypeStruct((N, 256), BF16),
                   jax.ShapeDtypeStruct((N // tm, MLA_VR, tm), BF16),
                   jax.ShapeDtypeStruct((N, 32), F32), jax.ShapeDtypeStruct((N, 32), BF16)],
        compiler_params=_cparams(("arbitrary",)),
        name="mla_prep",
    )(z, z, z, cos_t, sin_t, gq, gkv, wuq_bf, wukt_bf)


def _dot_nt(a, b):
    return lax.dot_general(a, b, (((1,), (1,)), ((), ())), preferred_element_type=F32)


MLA_SB = 128
MLA_CB = 256


def _mla_body(qlt_ref, qrt_ref, ckv_ref, kr_ref, ckvt_ref, wuvt_ref, o_ref,
              qt_scr, rt_scr, s_scr, pt_scr, m_scr, acc_scr,
              *, tq, P, L, KC, nkc_total, stacked):
    qi = pl.program_id(1)
    q0 = P + qi * tq
    cols_n = MLA_HEADS * tq
    if stacked:
        qt_scr[...] = qlt_ref[0]
        rt_scr[...] = qrt_ref[0]
    else:
        for h in range(MLA_HEADS):
            qt_scr[:, h * tq:(h + 1) * tq] = qlt_ref[h * 256:(h + 1) * 256, :]
            rt_scr[:, h * tq:(h + 1) * tq] = qrt_ref[h * 32:(h + 1) * 32, :]
    max_cend = (((q0 + tq - 1) >> 6) + 1) << 6
    nkc = jnp.minimum((max_cend + KC - 1) // KC, nkc_total)
    n_full = jnp.minimum(((q0 >> 6) + 1) << 6, L) // KC
    cb_w = min(MLA_CB, cols_n)

    m_scr[...] = jnp.full((1, cols_n), M_INIT, F32)
    acc_scr[...] = jnp.zeros((MLA_VR, cols_n), F32)

    def make_step(masked):
        def step(kc, carry):
            k0 = pl.multiple_of(kc * KC, KC)
            ck = ckv_ref[0, pl.ds(k0, KC), :]
            kr = kr_ref[0, pl.ds(k0, KC), :]
            s_scr[...] = (jnp.dot(ck, qt_scr[...], preferred_element_type=F32)
                          + jnp.dot(kr, rt_scr[...], preferred_element_type=F32))
            for cb in range(cols_n // cb_w):
                cols = slice(cb * cb_w, (cb + 1) * cb_w)
                if masked:
                    lane = cb * cb_w + lax.broadcasted_iota(I32, (1, cb_w), 1)
                    qpos = q0 + (lane & (tq - 1))
                    lim = jnp.minimum(((qpos >> 6) + 1) << 6, L)
                mx = jnp.full((8, cb_w), -jnp.inf, F32)
                for sb in range(KC // MLA_SB):
                    rs = slice(sb * MLA_SB, (sb + 1) * MLA_SB)
                    x = s_scr[rs, cols]
                    if masked:
                        kpos = k0 + sb * MLA_SB + lax.broadcasted_iota(I32, (MLA_SB, 1), 0)
                        x = jnp.where(kpos < lim, x, NEG)
                        s_scr[rs, cols] = x
                    mx = jnp.maximum(mx, _max8(x))
                m_old = m_scr[:, cols]
                m_new = jnp.maximum(m_old, jnp.max(mx, axis=0, keepdims=True))
                for sb in range(KC // MLA_SB):
                    rs = slice(sb * MLA_SB, (sb + 1) * MLA_SB)
                    pt_scr[rs, cols] = jnp.exp2(s_scr[rs, cols] - m_new).astype(BF16)
                m_scr[:, cols] = m_new
                acc_scr[:, cols] = (jnp.exp2(m_old - m_new) * acc_scr[:, cols]
                                    + jnp.dot(ckvt_ref[0, kc], pt_scr[:, cols], preferred_element_type=F32))
            return carry
        return step

    lax.fori_loop(0, n_full, make_step(False), 0)
    lax.fori_loop(n_full, nkc, make_step(True), 0)
    ot = (acc_scr[0:MLA_KV_LORA, :] / acc_scr[MLA_KV_LORA:MLA_KV_LORA + 1, :]).astype(BF16)
    if stacked:
        for h in range(MLA_HEADS):
            o_ref[0, h * MLA_V:(h + 1) * MLA_V, :] = jnp.dot(wuvt_ref[h], ot, preferred_element_type=F32)
    else:
        outs = [jnp.dot(wuvt_ref[h], ot[:, h * tq:(h + 1) * tq], preferred_element_type=F32)
                for h in range(MLA_HEADS)]
        o_ref[0] = jnp.concatenate(outs, axis=0).T


def _k_mla(qlt, qrt, ckvb, krb, ckvt, wuvt_bf, *, B, T, tq, P, L, KC, stacked):
    Lp = ckvb.shape[1]
    nkc_total = Lp // KC
    cols_n = MLA_HEADS * tq
    nq = T // tq
    if stacked:
        q_specs = [pl.BlockSpec((1, MLA_KV_LORA, cols_n), lambda b, q: (b, 0, 0)),
                   pl.BlockSpec((1, MLA_ROPE, cols_n), lambda b, q: (b, 0, 0))]
        out_spec = pl.BlockSpec((1, MLA_HEADS * MLA_V, cols_n), lambda b, q: (b, 0, 0))
        out_shape = jax.ShapeDtypeStruct((B, MLA_HEADS * MLA_V, cols_n), F32)
    else:
        q_specs = [pl.BlockSpec((MLA_HEADS * MLA_KV_LORA, tq), lambda b, q: (0, b * nq + q)),
                   pl.BlockSpec((MLA_HEADS * MLA_ROPE, tq), lambda b, q: (0, b * nq + q))]
        out_spec = pl.BlockSpec((1, tq, 512), lambda b, q: (b, q, 0))
        out_shape = jax.ShapeDtypeStruct((B, T, 512), F32)
    return pl.pallas_call(
        functools.partial(_mla_body, tq=tq, P=P, L=L, KC=KC, nkc_total=nkc_total, stacked=stacked),
        grid=(B, nq),
        in_specs=q_specs + [pl.BlockSpec((1, Lp, 256), lambda b, q: (b, 0, 0)),
                            pl.BlockSpec((1, Lp, 32), lambda b, q: (b, 0, 0)),
                            pl.BlockSpec((1, nkc_total, MLA_VR, KC), lambda b, q: (b, 0, 0, 0)),
                            _const_spec((MLA_HEADS, MLA_V, MLA_KV_LORA))],
        out_specs=out_spec,
        out_shape=out_shape,
        scratch_shapes=[pltpu.VMEM((MLA_KV_LORA, cols_n), BF16), pltpu.VMEM((MLA_ROPE, cols_n), BF16),
                        pltpu.VMEM((KC, cols_n), F32), pltpu.VMEM((KC, cols_n), BF16),
                        pltpu.VMEM((1, cols_n), F32), pltpu.VMEM((MLA_VR, cols_n), F32)],
        compiler_params=_cparams(("arbitrary", "arbitrary")),
        name="mla_attn",
    )(qlt, qrt, ckvb, krb, ckvt, wuvt_bf)


def _fold(x, r):
    n, w = x.shape
    return jnp.sum(x.reshape(n // r, r, w), axis=0)


def _fold8(x):
    return _fold(x, 8)


def _max8(x):
    n, w = x.shape
    return jnp.max(x.reshape(n // 8, 8, w), axis=0)


DSA_SB = 128


def _dsa_body(qt_ref, iqt_ref, iwt_ref, kb_ref, vt_ref, ikb_ref, o_ref,
              s_scr, w_scr, lg_scr, pt_scr, m_scr, acc_scr, *, tq, P, L, KC, nkc_total, topk):
    qi = pl.program_id(1)
    q0 = P + qi * tq
    qpos = q0 + lax.broadcasted_iota(I32, (1, tq), 1)
    lim = jnp.minimum(((qpos >> 6) + 1) << 6, L)
    keff = jnp.minimum(lim, topk).astype(F32)
    max_cend = (((q0 + tq - 1) >> 6) + 1) << 6
    nkc = jnp.minimum((max_cend + KC - 1) // KC, nkc_total)

    iqt = iqt_ref[0]
    iwt = iwt_ref[0]
    zpad = jnp.zeros((IDX_DIM, tq), BF16)
    iq_h = [jnp.concatenate([iqt[h * IDX_DIM:(h + 1) * IDX_DIM], zpad], axis=0) for h in range(IDX_HEADS)]

    def score_step(kc, carry):
        k0 = pl.multiple_of(kc * KC, KC)
        ik = ikb_ref[0, pl.ds(k0, KC), :]
        for h in range(IDX_HEADS):
            lg_scr[h, :, 0:tq] = jnp.dot(ik, iq_h[h], preferred_element_type=F32)
        for sb in range(KC // DSA_SB):
            rs = slice(sb * DSA_SB, (sb + 1) * DSA_SB)
            s = jnp.zeros((DSA_SB, tq), F32)
            for h in range(IDX_HEADS):
                s = s + iwt[h:h + 1, :] * jnp.maximum(lg_scr[h, rs, 0:tq], 0.0)
            kpos = k0 + sb * DSA_SB + lax.broadcasted_iota(I32, (DSA_SB, 1), 0)
            s = jnp.where(kpos < lim, s, -jnp.inf)
            b = lax.bitcast_convert_type(s, I32)
            b = jnp.where(b == INT_MIN, 0, b)
            s_scr[kc, rs, :] = b ^ ((b >> 31) & np.int32(0x7FFFFFFF))
        return carry

    lax.fori_loop(0, nkc, score_step, 0)

    def bit_step(i, thr):
        cand = thr + jnp.left_shift(jnp.int32(1), 31 - i)

        def cnt_step(kc, acc):
            return acc + _fold(jnp.where(s_scr[kc] >= cand, 1.0, 0.0), 64)

        cnt = jnp.sum(lax.fori_loop(0, nkc, cnt_step, jnp.zeros((64, tq), F32)), axis=0, keepdims=True)
        return jnp.where(cnt >= keff, cand, thr)

    thr = lax.fori_loop(0, 32, bit_step, jnp.full((1, tq), INT_MIN, I32))

    def tie_cnt_step(kc, carry):
        key = s_scr[kc]
        return (carry[0] + _fold8(jnp.where(key > thr, 1.0, 0.0)),
                carry[1] + _fold8(jnp.where(key == thr, 1.0, 0.0)))

    z8 = jnp.zeros((8, tq), F32)
    n_gt, n_eq = lax.fori_loop(0, nkc, tie_cnt_step, (z8, z8))
    need = keff - jnp.sum(n_gt, axis=0, keepdims=True)
    surplus = jnp.max(jnp.sum(n_eq, axis=0, keepdims=True) - need)

    @pl.when(surplus < 0.5)
    def _():
        def fast_step(kc, carry):
            s_scr[kc] = lax.bitcast_convert_type(jnp.where(s_scr[kc] >= thr, 0.0, NEG), I32)
            return carry
        lax.fori_loop(0, nkc, fast_step, 0)

    @pl.when(surplus >= 0.5)
    def _():
        tri = jnp.where(lax.broadcasted_iota(I32, (KC, KC), 1) < lax.broadcasted_iota(I32, (KC, KC), 0),
                        1.0, 0.0).astype(BF16)

        def slow_step(kc, seen):
            key = s_scr[kc]
            eq = key == thr
            eqf = jnp.where(eq, 1.0, 0.0)
            rank = seen + jnp.dot(tri, eqf.astype(BF16), preferred_element_type=F32)
            bias = jnp.where(key > thr, 0.0, jnp.where(eq, jnp.where(rank < need, 0.0, NEG), NEG))
            s_scr[kc] = lax.bitcast_convert_type(bias, I32)
            return seen + jnp.sum(_fold8(eqf), axis=0, keepdims=True)
        lax.fori_loop(0, nkc, slow_step, jnp.zeros((1, tq), F32))

    hd = DSA_HEAD_DIM
    zq = jnp.zeros((hd, tq), BF16)
    for hp in range(DSA_HEADS // 2):
        r0 = hp * 2 * hd
        w_scr[hp, 0:hd, 0:tq] = qt_ref[0, r0:r0 + hd, :]
        w_scr[hp, 0:hd, tq:2 * tq] = zq
        w_scr[hp, hd:2 * hd, 0:tq] = zq
        w_scr[hp, hd:2 * hd, tq:2 * tq] = qt_ref[0, r0 + hd:r0 + 2 * hd, :]
    m_scr[...] = jnp.full((DSA_HEADS, tq), M_INIT, F32)
    acc_scr[...] = jnp.zeros((DSA_HEADS * DSA_VR, tq), F32)
    nsb = KC // DSA_SB

    nhp = DSA_HEADS // 2

    def att_step(kc, carry):
        k0 = pl.multiple_of(kc * KC, KC)
        for hp in range(nhp):
            r0 = hp * 2 * hd
            lg_scr[hp] = jnp.dot(kb_ref[0, pl.ds(k0, KC), r0:r0 + 2 * hd], w_scr[hp],
                                 preferred_element_type=F32)
        for hp in range(nhp):
            li = hp
            for e in range(2):
                h = 2 * hp + e
                cols = slice(e * tq, (e + 1) * tq)
                mx = jnp.full((8, tq), -jnp.inf, F32)
                for sb in range(nsb):
                    rs = slice(sb * DSA_SB, (sb + 1) * DSA_SB)
                    x = lg_scr[li, rs, cols] + lax.bitcast_convert_type(s_scr[kc, rs, :], F32)
                    lg_scr[li, rs, cols] = x
                    mx = jnp.maximum(mx, _max8(x))
                m_old = m_scr[h:h + 1, :]
                m_new = jnp.maximum(m_old, jnp.max(mx, axis=0, keepdims=True))
                alpha = jnp.exp2(m_old - m_new)
                for sb in range(nsb):
                    rs = slice(sb * DSA_SB, (sb + 1) * DSA_SB)
                    pt_scr[h, rs, :] = jnp.exp2(lg_scr[li, rs, cols] - m_new).astype(BF16)
                m_scr[h:h + 1, :] = m_new
                hs = slice(h * DSA_VR, (h + 1) * DSA_VR)
                acc_scr[hs, :] = alpha * acc_scr[hs, :] + jnp.dot(vt_ref[0, kc, hs, :], pt_scr[h],
                                                                  preferred_element_type=F32)
        return carry

    lax.fori_loop(0, nkc, att_step, 0)
    outs = [acc_scr[h * DSA_VR:h * DSA_VR + hd, :] / acc_scr[h * DSA_VR + hd:h * DSA_VR + hd + 1, :]
            for h in range(DSA_HEADS)]
    o_ref[0] = jnp.concatenate(outs, axis=0).T


def _k_dsa(qt, iqt, iwt, kb3, vt4, ikb3, *, tq, P, L, KC, topk):
    B, _, T = qt.shape
    Lp = kb3.shape[1]
    nkc_total = Lp // KC
    return pl.pallas_call(
        functools.partial(_dsa_body, tq=tq, P=P, L=L, KC=KC, nkc_total=nkc_total, topk=topk),
        grid=(B, T // tq),
        in_specs=[pl.BlockSpec((1, 512, tq), lambda b, q: (b, 0, q)),
                  pl.BlockSpec((1, 256, tq), lambda b, q: (b, 0, q)),
                  pl.BlockSpec((1, 8, tq), lambda b, q: (b, 0, q)),
                  pl.BlockSpec((1, Lp, 512), lambda b, q: (b, 0, 0)),
                  pl.BlockSpec((1, nkc_total, DSA_HEADS * DSA_VR, KC), lambda b, q: (b, 0, 0, 0)),
                  pl.BlockSpec((1, Lp, 128), lambda b, q: (b, 0, 0))],
        out_specs=pl.BlockSpec((1, tq, 512), lambda b, q: (b, q, 0)),
        out_shape=jax.ShapeDtypeStruct((B, T, 512), F32),
        scratch_shapes=[pltpu.VMEM((nkc_total, KC, tq), I32),
                        pltpu.VMEM((DSA_HEADS // 2, 2 * DSA_HEAD_DIM, 2 * tq), BF16),
                        pltpu.VMEM((DSA_HEADS // 2, KC, 2 * tq), F32),
                        pltpu.VMEM((DSA_HEADS, KC, tq), BF16),
                        pltpu.VMEM((DSA_HEADS, tq), F32),
                        pltpu.VMEM((DSA_HEADS * DSA_VR, tq), F32)],
        compiler_params=_cparams(("arbitrary", "arbitrary")),
        name="dsa_attn",
    )(qt, iqt, iwt, kb3, vt4, ikb3)


def _merge_body(x_ref, sc_ref, sh_ref, g1_ref, ng_ref, oa_ref, ob_ref, oc_ref, od_ref,
                wg_ref, bg_ref, wb_ref, wo_ref, o_ref, *, bb, tt):
    x = x_ref[...]
    tm = bb * tt
    h = (_rms(x, ng_ref[...]) * (1.0 + sc_ref[...]) + sh_ref[...]).reshape(tm, D_MODEL).astype(BF16)
    merged = jnp.zeros((tm, D_MODEL), F32)
    for i, oref in enumerate((oa_ref, ob_ref, oc_ref, od_ref)):
        gate = _sigmoid(jnp.dot(h, wg_ref[:, i * D_MODEL:(i + 1) * D_MODEL], preferred_element_type=F32)
                        + bg_ref[:, i * D_MODEL:(i + 1) * D_MODEL])
        br = jnp.dot(oref[...].astype(BF16), wb_ref[i * BRANCH_W:(i + 1) * BRANCH_W, :],
                     preferred_element_type=F32)
        merged = merged + gate * br
    y = jnp.dot(merged.astype(BF16), wo_ref[...], preferred_element_type=F32)
    o_ref[...] = x + g1_ref[...] * y.reshape(bb, tt, D_MODEL)


def _k_merge(x, sc, sh, g1, ng, oa, ob, oc, od, wg_bf, bg, wb_bf, wo_bf, *, bb, tt):
    B, T, D = x.shape
    nt = T // tt
    tm = bb * tt
    xs = pl.BlockSpec((bb, tt, D), lambda i: (i // nt, i % nt, 0))
    ms = pl.BlockSpec((bb, 1, D), lambda i: (i // nt, 0, 0))
    os_ = pl.BlockSpec((tm, 512), lambda i: (i, 0))
    return pl.pallas_call(
        functools.partial(_merge_body, bb=bb, tt=tt),
        grid=((B // bb) * nt,),
        in_specs=[xs, ms, ms, ms, _const_spec((1, 1, D)), os_, os_, os_, os_,
                  _const_spec((D, 4 * D)), _const_spec((1, 4 * D)), _const_spec((4 * BRANCH_W, D)),
                  _const_spec((D, D))],
        out_specs=xs,
        out_shape=jax.ShapeDtypeStruct((B, T, D), F32),
        compiler_params=_cparams(("arbitrary",)),
        name="merge_out",
    )(x, sc, sh, g1, ng, oa, ob, oc, od, wg_bf, bg, wb_bf, wo_bf)


FC = FFN_DIM // 2
MOE_RB = 256


def _swiglu_chunk(h, w1, w3, w2):
    a = jnp.dot(h, w1, preferred_element_type=F32)
    b = jnp.dot(h, w3, preferred_element_type=F32)
    act = (a * _sigmoid(a) * b).astype(BF16)
    return jnp.dot(act, w2, preferred_element_type=F32)


def _finish(x, g2, f, fg_ref, o_ref, bb, tt):
    y = x + g2 * f.reshape(bb, tt, D_MODEL)
    if fg_ref is not None:
        y = _rms(y, fg_ref[...])
    o_ref[...] = y


def _ffn_body(x_ref, sc_ref, sh_ref, g2_ref, ng_ref, w1_ref, w3_ref, w2_ref, *rest, bb, tt, final):
    fg_ref, o_ref = rest if final else (None, rest[0])
    x = x_ref[...]
    tm = bb * tt
    h = (_rms(x, ng_ref[...]) * (1.0 + sc_ref[...]) + sh_ref[...]).reshape(tm, D_MODEL).astype(BF16)
    f = jnp.zeros((tm, D_MODEL), F32)
    for ci in range(FFN_DIM // FC):
        f = f + _swiglu_chunk(h, w1_ref[:, ci * FC:(ci + 1) * FC], w3_ref[:, ci * FC:(ci + 1) * FC],
                              w2_ref[ci * FC:(ci + 1) * FC, :])
    _finish(x, g2_ref[...], f, fg_ref, o_ref, bb, tt)


def _k_ffn(x, sc, sh, g2, ng, w1_bf, w3_bf, w2_bf, fg, *, bb, tt):
    B, T, D = x.shape
    nt = T // tt
    final = fg is not None
    xs = pl.BlockSpec((bb, tt, D), lambda i: (i // nt, i % nt, 0))
    ms = pl.BlockSpec((bb, 1, D), lambda i: (i // nt, 0, 0))
    in_specs = [xs, ms, ms, ms, _const_spec((1, 1, D)), _const_spec((D, FFN_DIM)),
                _const_spec((D, FFN_DIM)), _const_spec((FFN_DIM, D))]
    args = [x, sc, sh, g2, ng, w1_bf, w3_bf, w2_bf]
    if final:
        in_specs.append(_const_spec((1, 1, D)))
        args.append(fg)
    return pl.pallas_call(
        functools.partial(_ffn_body, bb=bb, tt=tt, final=final),
        grid=((B // bb) * nt,),
        in_specs=in_specs,
        out_specs=xs,
        out_shape=jax.ShapeDtypeStruct((B, T, D), F32),
        compiler_params=_cparams(("arbitrary",)),
        name="ffn_dense",
    )(*args)


def _moe_body(x_ref, sc_ref, sh_ref, g2_ref, ng_ref, rw_ref, rb_ref, w1_ref, w3_ref, w2_ref, *rest,
              bb, tt, final):
    if final:
        fg_ref, o_ref, h_scr, gate_scr, rank_scr, rankt_scr, xg_scr, y_scr, acc_scr = rest
    else:
        fg_ref = None
        o_ref, h_scr, gate_scr, rank_scr, rankt_scr, xg_scr, y_scr, acc_scr = rest
    e = pl.program_id(1)
    ci = pl.program_id(2)
    last_c = FFN_DIM // FC - 1
    tm = bb * tt
    lane = lax.broadcasted_iota(I32, (tm, 128), 1)

    @pl.when((e == 0) & (ci == 0))
    def _():
        h = (_rms(x_ref[...], ng_ref[...]) * (1.0 + sc_ref[...]) + sh_ref[...]).reshape(tm, D_MODEL)
        h_scr[...] = h.astype(BF16)
        logits = jnp.dot(h, rw_ref[...], preferred_element_type=F32,
                         precision=lax.Precision.HIGHEST) + rb_ref[...]
        m1 = jnp.max(logits, axis=-1, keepdims=True)
        i1 = jnp.min(jnp.where(logits == m1, lane, 128), axis=-1, keepdims=True)
        rest_l = jnp.where(lane == i1, NEG, logits)
        m2 = jnp.max(rest_l, axis=-1, keepdims=True)
        i2 = jnp.min(jnp.where(rest_l == m2, lane, 128), axis=-1, keepdims=True)
        e2 = jnp.exp(m2 - m1)
        den = 1.0 + e2
        gate_scr[...] = jnp.where(lane == i1, 1.0 / den, jnp.where(lane == i2, e2 / den, 0.0))
        sel = jnp.where(lane == i1, 1.0, jnp.where(lane == i2, 1.0, 0.0))
        tri = jnp.where(lax.broadcasted_iota(I32, (tm, tm), 1) < lax.broadcasted_iota(I32, (tm, tm), 0),
                        1.0, 0.0).astype(BF16)
        rank = jnp.where(sel > 0.5, jnp.dot(tri, sel.astype(BF16), preferred_element_type=F32), -1.0)
        rank_scr[...] = rank
        rankt_scr[...] = rank.T
        acc_scr[...] = jnp.zeros((tm, D_MODEL), F32)

    is_e = lane == e
    gate_col = jnp.sum(jnp.where(is_e, gate_scr[...], 0.0), axis=-1, keepdims=True)
    rank_col = jnp.sum(jnp.where(is_e, rank_scr[...], 0.0), axis=-1, keepdims=True)
    rank_row = rankt_scr[pl.ds(e, 1), :]
    n_e = jnp.max(rank_row).astype(I32) + 1
    nblk = (n_e + MOE_RB - 1) // MOE_RB

    def blk_step(b, carry):
        base = (b * MOE_RB).astype(F32)

        @pl.when(ci == 0)
        def _():
            slot = base + lax.broadcasted_iota(I32, (MOE_RB, 1), 0).astype(F32)
            pick = jnp.where(rank_row == slot, 1.0, 0.0).astype(BF16)
            xg_scr[b] = jnp.dot(pick, h_scr[...], preferred_element_type=F32).astype(BF16)

        part = _swiglu_chunk(xg_scr[b], w1_ref[0], w3_ref[0], w2_ref[0])

        @pl.when(ci == 0)
        def _():
            y_scr[b] = part

        @pl.when((ci > 0) & (ci < last_c))
        def _():
            y_scr[b] += part

        @pl.when(ci == last_c)
        def _():
            y = y_scr[b] + part
            y_hi = y.astype(BF16)
            y_lo = (y - y_hi.astype(F32)).astype(BF16)
            slot = base + lax.broadcasted_iota(I32, (1, MOE_RB), 1).astype(F32)
            put = jnp.where(rank_col == slot, 1.0, 0.0).astype(BF16)
            acc_scr[...] += gate_col * (jnp.dot(put, y_hi, preferred_element_type=F32)
                                        + jnp.dot(put, y_lo, preferred_element_type=F32))
        return carry

    lax.fori_loop(0, nblk, blk_step, 0)

    @pl.when((e == N_EXPERTS - 1) & (ci == last_c))
    def _():
        _finish(x_ref[...], g2_ref[...], acc_scr[...], fg_ref, o_ref, bb, tt)


def _k_moe(x, sc, sh, g2, ng, rw_pad, rb_pad, w1_bf, w3_bf, w2_bf, fg, *, bb, tt):
    B, T, D = x.shape
    nt = T // tt
    tm = bb * tt
    final = fg is not None
    xs = pl.BlockSpec((bb, tt, D), lambda i, e, c: (i // nt, i % nt, 0))
    xin = pl.BlockSpec((bb, tt, D), lambda i, e, c: (i // nt, i % nt, 0), pipeline_mode=pl.Buffered(1))
    ms = pl.BlockSpec((bb, 1, D), lambda i, e, c: (i // nt, 0, 0))
    in_specs = [xin, ms, ms, ms, _const_spec((1, 1, D)), _const_spec((D, 128)), _const_spec((1, 128)),
                pl.BlockSpec((1, D, FC), lambda i, e, c: (e, 0, c)),
                pl.BlockSpec((1, D, FC), lambda i, e, c: (e, 0, c)),
                pl.BlockSpec((1, FC, D), lambda i, e, c: (e, c, 0))]
    args = [x, sc, sh, g2, ng, rw_pad, rb_pad, w1_bf, w3_bf, w2_bf]
    if final:
        in_specs.append(_const_spec((1, 1, D)))
        args.append(fg)
    return pl.pallas_call(
        functools.partial(_moe_body, bb=bb, tt=tt, final=final),
        grid=((B // bb) * nt, N_EXPERTS, FFN_DIM // FC),
        in_specs=in_specs,
        out_specs=xs,
        out_shape=jax.ShapeDtypeStruct((B, T, D), F32),
        scratch_shapes=[pltpu.VMEM((tm, D), BF16), pltpu.VMEM((tm, 128), F32), pltpu.VMEM((tm, 128), F32),
                        pltpu.VMEM((128, tm), F32), pltpu.VMEM((pl.cdiv(tm, MOE_RB), MOE_RB, D), BF16),
                        pltpu.VMEM((pl.cdiv(tm, MOE_RB), MOE_RB, D), F32), pltpu.VMEM((tm, D), F32)],
        compiler_params=_cparams(("arbitrary", "arbitrary", "arbitrary")),
        name="ffn_moe",
    )(*args)


def _prep_w_in(w):
    aq, ak, av, iq, ik, iw, pu, cb, cc, ch, mq, mkv, mkr = jnp.split(w, IN_SPLIT_POINTS, axis=1)
    d = w.shape[0]
    mkr_sw = jnp.concatenate([mkr[:, 16:], mkr[:, :16]], axis=1)
    return jnp.concatenate([aq, ak, av, pu, cb, cc, ch, iq, mq, mkv, ik, iw, jnp.zeros((d, 60), w.dtype),
                            mkr, mkr_sw, jnp.zeros((d, 64), w.dtype)], axis=1).astype(BF16)


def _prep_w_uq(w):
    w3 = w.reshape(MLA_Q_LORA, MLA_HEADS, MLA_NOPE + MLA_ROPE)
    nope = w3[:, :, :MLA_NOPE].reshape(MLA_Q_LORA, MLA_HEADS * MLA_NOPE)
    rope = w3[:, :, MLA_NOPE:]
    rope_sw = jnp.concatenate([rope[:, :, 16:], rope[:, :, :16]], axis=-1)
    return jnp.concatenate([nope, rope.reshape(MLA_Q_LORA, -1), rope_sw.reshape(MLA_Q_LORA, -1)],
                           axis=1).astype(BF16)


def _rope_tables(pos):
    half = MLA_ROPE // 2
    inv = 1.0 / (ROPE_BASE ** (jnp.arange(half, dtype=F32) / half))
    ang = pos.astype(F32)[:, None] * inv[None, :]
    cos, sin = jnp.cos(ang), jnp.sin(ang)
    cos2 = jnp.tile(jnp.concatenate([cos, cos], axis=-1), (1, MLA_HEADS))
    sin2 = jnp.tile(jnp.concatenate([-sin, sin], axis=-1), (1, MLA_HEADS))
    return cos2, sin2


def _layer(x, mod, past, lw, fw, use_moe, final_g, cfg):
    B, T, D = x.shape
    bb, tt, KC, tq_dsa, tq_mla = cfg["bb"], cfg["tt"], cfg["KC"], cfg["tq_dsa"], cfg["tq_mla"]
    KCD = cfg["KCD"]
    N = B * T
    P = 0 if past is None else past["dsa_k"].shape[1]
    L = P + T
    sh1, sc1, g1, sh2, sc2, g2 = [m.reshape(B, 1, D) for m in jnp.split(mod, 6, axis=-1)]
    ng1 = lw["norm_mix_g"].reshape(1, 1, D)
    ng2 = lw["norm_ffn_g"].reshape(1, 1, D)

    emit = past is None
    res = _k_in(x, sc1, sh1, ng1, lw["w_in"], bb=bb, tt=tt, emit=emit, kc=KCD)
    z = res[0]
    zoff = Z_OFF if emit else 0
    z3 = z.reshape(B, T, ZW - zoff)
    if emit:
        k_new = res[1].reshape(B, T, DSA_HEADS, DSA_HEAD_DIM)
        v_new = res[2].reshape(B, T, DSA_HEADS, DSA_HEAD_DIM)
    else:
        k_new = z[:, C_AK:C_AK + 512].reshape(B, T, DSA_HEADS, DSA_HEAD_DIM)
        v_new = z[:, C_AV:C_AV + 512].reshape(B, T, DSA_HEADS, DSA_HEAD_DIM)
    ik_new = z3[:, :, C_IKW - zoff:C_IKW - zoff + IDX_DIM]

    if past is None:
        ph16 = jnp.zeros((B, 16, 512), F32)
        cvh16 = ph16
    else:
        ph16 = jnp.pad(past["pool"], ((0, 0), (1, 0), (0, 0)))
        cvh16 = jnp.pad(past["conv"], ((0, 0), (16 - (CONV_WIDTH - 1), 0), (0, 0)))
    o_b, o_c, ph_o, cvh_o = _k_pc(z3, ph16, cvh16, lw["pool_w"], lw["pool_scale"], lw["conv_w"],
                                  bb=bb, tt=tt, P=P, zoff=zoff)
    pool_hist = ph_o[:, 1:]
    conv_hist = cvh_o[:, 16 - (CONV_WIDTH - 1):]

    cos2, sin2 = _rope_tables(P + jnp.arange(T))
    if bb > 1:
        cos2, sin2 = jnp.tile(cos2, (bb, 1)), jnp.tile(sin2, (bb, 1))
    qlt, qrt, ckv_new, ckvb, ckvt, kr_new, krb = _k_mlaprep(
        z, cos2, sin2, lw["mla_q_norm_g"], lw["mla_kv_norm_g"], lw["mla_w_uq"], lw["mla_w_uk"], tm=bb * tt,
        zoff=zoff)

    topk = min(DSA_TOPK_MAX, L // 4)
    if past is None:
        kb3 = res[3].reshape(B, T, 512)
        vt4, qt, iqt, iwt = res[4], res[5], res[6], res[7]
        ikb3 = res[8].reshape(B, T, 128)
        ckvb3 = ckvb.reshape(B, T, 256)
        krb3 = krb.reshape(B, T, 32)
        Tq = T
    else:
        Lp = -(-L // KCD) * KCD
        padk = ((0, 0), (0, Lp - L), (0, 0))
        k_all = jnp.concatenate([past["dsa_k"].reshape(B, P, 512), k_new.reshape(B, T, 512)], axis=1)
        v_all = jnp.concatenate([past["dsa_v"].reshape(B, P, 512), v_new.reshape(B, T, 512)], axis=1)
        kb3 = jnp.pad(k_all.astype(BF16), padk)
        vt5 = jnp.pad(v_all.astype(BF16), padk).reshape(B, Lp // KCD, KCD, DSA_HEADS, DSA_HEAD_DIM)
        vt5 = jnp.concatenate([vt5.transpose(0, 1, 3, 4, 2),
                               jnp.ones((B, Lp // KCD, DSA_HEADS, DSA_VR - DSA_HEAD_DIM, KCD), BF16)], axis=3)
        vt4 = vt5.reshape(B, Lp // KCD, DSA_HEADS * DSA_VR, KCD)
        ik_all = jnp.concatenate([past["idx_k"], ik_new], axis=1).astype(BF16)
        ikb3 = jnp.pad(ik_all, ((0, 0), (0, Lp - L), (0, 128 - IDX_DIM)))
        Tq = tq_dsa
        padq = ((0, 0), (0, 0), (0, Tq - T))
        qt = jnp.pad((z3[:, :, C_AQ:C_AQ + 512] * (DSA_SCALE * LOG2E)).transpose(0, 2, 1).astype(BF16), padq)
        iqt = jnp.pad(z3[:, :, C_IQ:C_IQ + 256].transpose(0, 2, 1).astype(BF16), padq)
        iwt = jnp.pad(z3[:, :, C_IKW + 64:C_IKW + 72].transpose(0, 2, 1) * IDX_SCALE, padq)
        Lm = -(-L // KC) * KC
        padk = ((0, 0), (0, Lm - L), (0, 0))
        ckvb3 = jnp.pad(jnp.concatenate([past["ckv"].astype(BF16), ckvb.reshape(B, T, 256)], axis=1), padk)
        krb3 = jnp.pad(jnp.concatenate([past["krope"].astype(BF16), krb.reshape(B, T, 32)], axis=1), padk)

    o_a = _k_dsa(qt, iqt, iwt, kb3, vt4, ikb3, tq=tq_dsa, P=P, L=L, KC=KCD, topk=topk)
    o_a = o_a[:, :T].reshape(N, 512)
    if past is None:
        ckvt4 = ckvt.reshape(B, T // KC, MLA_VR, KC)
        o_d = _k_mla(qlt, qrt, ckvb3, krb3, ckvt4, lw["mla_w_uv"], B=B, T=T, tq=tq_mla, P=P, L=L, KC=KC,
                     stacked=False).reshape(N, 512)
    else:
        qs = qlt.reshape(MLA_HEADS, 256, B, T).transpose(2, 1, 0, 3).reshape(B, 256, MLA_HEADS * T)
        rs = qrt.reshape(MLA_HEADS, 32, B, T).transpose(2, 1, 0, 3).reshape(B, 32, MLA_HEADS * T)
        nk = ckvb3.shape[1] // KC
        ckvt4 = jnp.concatenate([ckvb3.reshape(B, nk, KC, 256).transpose(0, 1, 3, 2),
                                 jnp.ones((B, nk, MLA_VR - MLA_KV_LORA, KC), BF16)], axis=2)
        o5 = _k_mla(qs, rs, ckvb3, krb3, ckvt4, lw["mla_w_uv"], B=B, T=T, tq=tq_mla, P=P, L=L, KC=KC,
                    stacked=True).reshape(B, MLA_HEADS, MLA_V, MLA_HEADS, T)
        o_d = jnp.stack([o5[:, h, :, h, :] for h in range(MLA_HEADS)], axis=1)
        o_d = o_d.transpose(0, 3, 1, 2).reshape(N, 512)

    x1 = _k_merge(x, sc1, sh1, g1, ng1, o_a, o_b.reshape(N, 512), o_c.reshape(N, 512), o_d,
                  lw["w_gate"], lw["b_gate"], lw["w_branch"], lw["w_out"], bb=bb, tt=tt)
    fg = None if final_g is None else final_g.reshape(1, 1, D)
    if use_moe:
        x2 = _k_moe(x1, sc2, sh2, g2, ng2, fw["rw"], fw["rb"], fw["w1"], fw["w3"], fw["w2"], fg,
                    bb=cfg["bb_moe"], tt=cfg["tt_moe"])
    else:
        x2 = _k_ffn(x1, sc2, sh2, g2, ng2, fw["w1"], fw["w3"], fw["w2"], fg, bb=bb, tt=tt)
    rows = (k_new, v_new, ik_new, ckv_new.reshape(B, T, 256), kr_new.reshape(B, T, 32), pool_hist, conv_hist)
    return x2, rows


def _prep_layer(l, ada_w, ada_b, norm_mix_g, norm_ffn_g, w_in, mla_q_norm_g, mla_kv_norm_g,
                mla_w_uq, mla_w_uk, mla_w_uv, pool_w, pool_scale, conv_w, w_gate, b_gate, w_branch, w_out):
    return dict(
        ada_w=ada_w[l].astype(BF16), ada_b=ada_b[l].reshape(1, -1),
        norm_mix_g=norm_mix_g[l], norm_ffn_g=norm_ffn_g[l],
        w_in=_prep_w_in(w_in[l]),
        mla_q_norm_g=mla_q_norm_g[l].reshape(1, -1), mla_kv_norm_g=mla_kv_norm_g[l].reshape(1, -1),
        mla_w_uq=_prep_w_uq(mla_w_uq[l]),
        mla_w_uk=mla_w_uk[l].transpose(1, 0, 2).astype(BF16),
        mla_w_uv=mla_w_uv[l].transpose(1, 2, 0).astype(BF16),
        pool_w=pool_w[l].astype(BF16), pool_scale=pool_scale[l].reshape(1, -1),
        conv_w=jnp.pad(conv_w[l].reshape(CONV_WIDTH, BRANCH_W), ((0, 8 - CONV_WIDTH), (0, 0))),
        w_gate=w_gate[l].astype(BF16), b_gate=b_gate[l].reshape(1, -1),
        w_branch=w_branch[l].astype(BF16), w_out=w_out[l].astype(BF16))


def _cfg_for(B, T, has_past):
    if has_past:
        return dict(bb=B, tt=T, KC=512, KCD=256, tq_dsa=128, tq_mla=T, bb_moe=B, tt_moe=T)
    t = min(512, T)
    return dict(bb=1, tt=t, KC=t, KCD=min(256, T), tq_dsa=min(256, T), tq_mla=min(128, T), bb_moe=1,
                tt_moe=min(1024, T))


def _run(x, c, pasts, lws, fws, final_norm_g):
    B, T, _ = x.shape
    cfg = _cfg_for(B, T, pasts is not None)
    outs = [[] for _ in range(7)]
    depth = len(lws)
    for l in range(depth):
        lw = lws[l]
        mod = _ada(c, lw["ada_w"], lw["ada_b"])
        x, rows = _layer(x, mod, None if pasts is None else pasts[l], lw, fws[l], l % 2 == 1,
                         final_norm_g if l == depth - 1 else None, cfg)
        for o, r in zip(outs, rows):
            o.append(r)
    return x, [jnp.stack(o) for o in outs]


def kernel(x_prompt, x_sample, c_prompt, c_sample, cache_dsa_k, cache_dsa_v, cache_dsa_idx_k, cache_mla_ckv, cache_mla_krope, state_pool, state_conv, ada_w, ada_b, norm_mix_g, norm_ffn_g, w_in, mla_q_norm_g, mla_kv_norm_g, mla_w_uq, mla_w_uk, mla_w_uv, pool_w, pool_scale, conv_w, w_gate, b_gate, w_branch, w_out, ffn_w1, ffn_w3, ffn_w2, moe_router_w, moe_router_b, moe_w1, moe_w3, moe_w2, final_norm_g):
    depth = ada_w.shape[0]
    lws = [_prep_layer(l, ada_w, ada_b, norm_mix_g, norm_ffn_g, w_in, mla_q_norm_g, mla_kv_norm_g,
                       mla_w_uq, mla_w_uk, mla_w_uv, pool_w, pool_scale, conv_w, w_gate, b_gate,
                       w_branch, w_out) for l in range(depth)]
    fws = []
    for l in range(depth):
        j = l // 2
        if l % 2 == 0:
            fws.append(dict(w1=ffn_w1[j].astype(BF16), w3=ffn_w3[j].astype(BF16), w2=ffn_w2[j].astype(BF16)))
        else:
            fws.append(dict(
                rw=jnp.pad(moe_router_w[j], ((0, 0), (0, 128 - N_EXPERTS))),
                rb=jnp.pad(moe_router_b[j].reshape(1, -1), ((0, 0), (0, 128 - N_EXPERTS)), constant_values=NEG),
                w1=moe_w1[j].astype(BF16), w3=moe_w3[j].astype(BF16), w2=moe_w2[j].astype(BF16)))
    sample_pasts = [dict(dsa_k=cache_dsa_k[l], dsa_v=cache_dsa_v[l], idx_k=cache_dsa_idx_k[l],
                         ckv=cache_mla_ckv[l], krope=cache_mla_krope[l],
                         pool=state_pool[l], conv=state_conv[l]) for l in range(depth)]
    y_prompt, pn = _run(x_prompt, c_prompt, None, lws, fws, final_norm_g)
    y_sample, sn = _run(x_sample, c_sample, sample_pasts, lws, fws, final_norm_g)
    return (y_prompt, y_sample, pn[0], pn[1], pn[2], pn[3], pn[4], pn[5], pn[6],
            sn[0], sn[1], sn[2], sn[3], sn[4], sn[5], sn[6])
```

```python
import functools

import numpy as np
import jax
import jax.numpy as jnp
from jax import lax
from jax.experimental import pallas as pl
from jax.experimental.pallas import tpu as pltpu

F32 = jnp.float32
BF16 = jnp.bfloat16
I32 = jnp.int32

D_MODEL = 1024
CHUNK = 64
N_BRANCH = 4
BRANCH_W = 512
DSA_HEADS = 8
DSA_HEAD_DIM = 64
IDX_HEADS = 4
IDX_DIM = 64
DSA_TOPK_MAX = 256
POOL_WINDOWS = (2, 4, 8, 16)
POOL_GROUP = BRANCH_W // 4
POOL_HIST = 15
CONV_WIDTH = 3
MLA_HEADS = 8
MLA_Q_LORA = 256
MLA_KV_LORA = 256
MLA_NOPE = 64
MLA_ROPE = 32
MLA_V = 64
ROPE_BASE = 10000.0
FFN_DIM = 2816
N_EXPERTS = 8
EPS = 1e-6

DSA_SCALE = DSA_HEAD_DIM ** -0.5
IDX_SCALE = (IDX_HEADS * IDX_DIM) ** -0.5
MLA_SCALE = (MLA_NOPE + MLA_ROPE) ** -0.5
LOG2E = 1.4426950408889634

IN_SPLIT_WIDTHS = (512, 512, 512, 256, 64, 4, 512, 512, 512, 512, 256, 256, 32)
IN_SPLIT_POINTS = tuple(int(sum(IN_SPLIT_WIDTHS[:i + 1])) for i in range(len(IN_SPLIT_WIDTHS) - 1))

C_AQ, C_AK, C_AV, C_PU, C_CB, C_CC, C_CH = 0, 512, 1024, 1536, 2048, 2560, 3072
C_IQ, C_MQ, C_MKV, C_IKW, C_MKR, ZW = 3584, 3840, 4096, 4352, 4480, 4608
IN_TN = 1536
Z_OFF = IN_TN
DSA_VR = DSA_HEAD_DIM + 16
MLA_VR = MLA_KV_LORA + 16

NEG = -1e30
M_INIT = -1e29
INT_MIN = np.int32(-2 ** 31)
V7X_VMEM_LIMIT = 56 * 1024 * 1024


def _cparams(sem):
    return pltpu.CompilerParams(dimension_semantics=sem, vmem_limit_bytes=V7X_VMEM_LIMIT)


def _rms(x, g):
    return x * lax.rsqrt(jnp.mean(x * x, axis=-1, keepdims=True) + EPS) * g


def _sigmoid(x):
    return 1.0 / (1.0 + jnp.exp(-x))


def _const_spec(shape):
    nd = len(shape)
    return pl.BlockSpec(shape, lambda *_: (0,) * nd, pipeline_mode=pl.Buffered(1))


def _ada_body(c_ref, w_ref, b_ref, o_ref):
    c = c_ref[...]
    s = (c * _sigmoid(c)).astype(BF16)
    o_ref[...] = jnp.dot(s, w_ref[...], preferred_element_type=F32) + b_ref[...]


def _ada(c, w_bf, b):
    B, D = c.shape
    n = w_bf.shape[1]
    tn = 1536
    return pl.pallas_call(
        _ada_body,
        grid=(n // tn,),
        in_specs=[pl.BlockSpec((B, D), lambda j: (0, 0)),
                  pl.BlockSpec((D, tn), lambda j: (0, j)),
                  pl.BlockSpec((1, tn), lambda j: (0, j))],
        out_specs=pl.BlockSpec((B, tn), lambda j: (0, j)),
        out_shape=jax.ShapeDtypeStruct((B, n), F32),
        compiler_params=_cparams(("arbitrary",)),
        name="ada_mod",
    )(c, w_bf, b)


def _in_body(x_ref, sc_ref, sh_ref, g_ref, w_ref, z_ref, *rest, bb, tt, emit, kc):
    if emit:
        kn_ref, vn_ref, kb_ref, vt_ref, qt_ref, iqt_ref, iwt_ref, ikb_ref, h_scr = rest
    else:
        (h_scr,) = rest
    j = pl.program_id(1)

    @pl.when(j == 0)
    def _():
        h = _rms(x_ref[...], g_ref[...]) * (1.0 + sc_ref[...]) + sh_ref[...]
        h_scr[...] = h.reshape(bb * tt, D_MODEL).astype(BF16)

    z = jnp.dot(h_scr[...], w_ref[...], preferred_element_type=F32)

    if not emit:
        z_ref[...] = z
    else:
        @pl.when(j > 0)
        def _():
            z_ref[...] = z

        @pl.when(j == 0)
        def _():
            k = z[:, C_AK:C_AK + 512]
            v = z[:, C_AV:C_AV + 512]
            kn_ref[...] = k
            vn_ref[...] = v
            kb_ref[...] = k.astype(BF16)
            vt = v.T.astype(BF16)
            ones = jnp.ones((DSA_VR - DSA_HEAD_DIM, kc), BF16)
            for c in range(tt // kc):
                for h in range(DSA_HEADS):
                    vt_ref[0, c, h * DSA_VR:h * DSA_VR + DSA_HEAD_DIM, :] = vt[
                        h * DSA_HEAD_DIM:(h + 1) * DSA_HEAD_DIM, c * kc:(c + 1) * kc]
                    vt_ref[0, c, h * DSA_VR + DSA_HEAD_DIM:(h + 1) * DSA_VR, :] = ones
            qt_ref[0] = (z[:, C_AQ:C_AQ + 512] * (DSA_SCALE * LOG2E)).T.astype(BF16)

        @pl.when(j == 2)
        def _():
            o = 2 * IN_TN
            iqt_ref[0] = z[:, C_IQ - o:C_IQ - o + 256].T.astype(BF16)
            ikw = z[:, C_IKW - o:C_IKW - o + 128]
            ikb_ref[...] = ikw.astype(BF16)
            iwt_ref[0] = ikw.T[64:72, :] * IDX_SCALE


def _k_in(x, sc, sh, g, w_bf, *, bb, tt, emit, kc):
    B, T, D = x.shape
    nt = T // tt
    tm = bb * tt
    n_rows = (B // bb) * nt
    N = B * T
    xmap = lambda i, j: (i // nt, i % nt, 0)
    mmap = lambda i, j: (i // nt, 0, 0)
    in_specs = [pl.BlockSpec((bb, tt, D), xmap),
                pl.BlockSpec((bb, 1, D), mmap),
                pl.BlockSpec((bb, 1, D), mmap),
                _const_spec((1, 1, D)),
                pl.BlockSpec((D, IN_TN), lambda i, j: (0, j))]
    if emit:
        assert bb == 1
        out_specs = [pl.BlockSpec((tm, IN_TN), lambda i, j: (i, jnp.maximum(j - 1, 0)))]
        out_shape = [jax.ShapeDtypeStruct((N, ZW - Z_OFF), F32)]
        out_specs += [
            pl.BlockSpec((tm, 512), lambda i, j: (i, 0)),
            pl.BlockSpec((tm, 512), lambda i, j: (i, 0)),
            pl.BlockSpec((tm, 512), lambda i, j: (i, 0)),
            pl.BlockSpec((1, tt // kc, DSA_HEADS * DSA_VR, kc), lambda i, j: (i // nt, i % nt, 0, 0)),
            pl.BlockSpec((1, 512, tt), lambda i, j: (i // nt, 0, i % nt)),
            pl.BlockSpec((1, 256, tt), lambda i, j: (i // nt, 0, i % nt)),
            pl.BlockSpec((1, 8, tt), lambda i, j: (i // nt, 0, i % nt)),
            pl.BlockSpec((tm, 128), lambda i, j: (i, 0)),
        ]
        out_shape += [
            jax.ShapeDtypeStruct((N, 512), F32),
            jax.ShapeDtypeStruct((N, 512), F32),
            jax.ShapeDtypeStruct((N, 512), BF16),
            jax.ShapeDtypeStruct((B, T // kc, DSA_HEADS * DSA_VR, kc), BF16),
            jax.ShapeDtypeStruct((B, 512, T), BF16),
            jax.ShapeDtypeStruct((B, 256, T), BF16),
            jax.ShapeDtypeStruct((B, 8, T), F32),
            jax.ShapeDtypeStruct((N, 128), BF16),
        ]
    else:
        out_specs = [pl.BlockSpec((tm, IN_TN), lambda i, j: (i, j))]
        out_shape = [jax.ShapeDtypeStruct((N, ZW), F32)]
    return pl.pallas_call(
        functools.partial(_in_body, bb=bb, tt=tt, emit=emit, kc=kc),
        grid=(n_rows, ZW // IN_TN),
        in_specs=in_specs,
        out_specs=out_specs,
        out_shape=out_shape,
        scratch_shapes=[pltpu.VMEM((tm, D), BF16)],
        compiler_params=_cparams(("arbitrary", "arbitrary")),
        name="in_proj",
    )(x, sc, sh, g, w_bf)


def _pc_body(pu_ref, cb_ref, cc_ref, ch_ref, pup_ref, ccp_ref, chp_ref, ph_ref, cvh_ref,
             pw_ref, ps_ref, cw_ref, ob_ref, oc_ref, pho_ref, cho_ref, u_scr, g_scr, *, bb, tt, P):
    t = pl.program_id(1)
    first = t == 0
    pu = pu_ref[...]
    g = cc_ref[...] * ch_ref[...]
    u_scr[:, 0:16, :] = jnp.where(first, ph_ref[...], pup_ref[...])
    u_scr[:, 16:16 + tt, :] = pu
    g_scr[:, 0:16, :] = jnp.where(first, cvh_ref[...], ccp_ref[...] * chp_ref[...])
    g_scr[:, 16:16 + tt, :] = g

    pos = P + t * tt + lax.broadcasted_iota(I32, (1, tt, 1), 1)
    for gi, win in enumerate(POOL_WINDOWS):
        lo = gi * POOL_GROUP
        acc = pu[:, :, lo:lo + POOL_GROUP]
        for j in range(1, win):
            acc = acc + u_scr[:, 16 - j:16 - j + tt, lo:lo + POOL_GROUP]
        cnt = jnp.minimum(pos + 1, win).astype(F32)
        d = acc / cnt - pu[:, :, lo:lo + POOL_GROUP]
        y = jnp.dot(d.reshape(bb * tt, POOL_GROUP).astype(BF16), pw_ref[gi], preferred_element_type=F32)
        ob_ref[:, :, lo:lo + POOL_GROUP] = (y * ps_ref[:, lo:lo + POOL_GROUP]).reshape(bb, tt, POOL_GROUP)

    cw = cw_ref[...]
    conv = (cw[0:1, :] * g_scr[:, 14:14 + tt, :] + cw[1:2, :] * g_scr[:, 15:15 + tt, :] + cw[2:3, :] * g)
    oc_ref[...] = cb_ref[...] * conv
    pho_ref[...] = u_scr[:, tt:tt + 16, :]
    cho_ref[...] = g_scr[:, tt:tt + 16, :]


def _k_pc(z3, ph16, cvh16, pw_bf, ps, cw, *, bb, tt, P, zoff):
    B, T, _ = z3.shape
    nt = T // tt
    r = tt // 16

    def cur(c):
        return pl.BlockSpec((bb, tt, 512), lambda b, t: (b, t, (c - zoff) // 512))

    def prev(c):
        return pl.BlockSpec((bb, 16, 512), lambda b, t: (b, jnp.maximum(t * r - 1, 0), (c - zoff) // 512))

    hist = pl.BlockSpec((bb, 16, 512), lambda b, t: (b, 0, 0))
    return pl.pallas_call(
        functools.partial(_pc_body, bb=bb, tt=tt, P=P),
        grid=(B // bb, nt),
        in_specs=[cur(C_PU), cur(C_CB), cur(C_CC), cur(C_CH), prev(C_PU), prev(C_CC), prev(C_CH),
                  hist, hist, _const_spec((4, POOL_GROUP, POOL_GROUP)), _const_spec((1, 512)),
                  _const_spec((8, 512))],
        out_specs=[pl.BlockSpec((bb, tt, 512), lambda b, t: (b, t, 0)),
                   pl.BlockSpec((bb, tt, 512), lambda b, t: (b, t, 0)), hist, hist],
        out_shape=[jax.ShapeDtypeStruct((B, T, 512), F32), jax.ShapeDtypeStruct((B, T, 512), F32),
                   jax.ShapeDtypeStruct((B, 16, 512), F32), jax.ShapeDtypeStruct((B, 16, 512), F32)],
        scratch_shapes=[pltpu.VMEM((bb, tt + 16, 512), F32), pltpu.VMEM((bb, tt + 16, 512), F32)],
        compiler_params=_cparams(("arbitrary", "arbitrary")),
        name="pool_conv",
    )(z3, z3, z3, z3, z3, z3, z3, ph16, cvh16, pw_bf, ps, cw)


def _mp_body(mq_ref, mkv_ref, mkr_ref, cos_ref, sin_ref, gq_ref, gkv_ref, wuq_ref, wukt_ref,
             qlt_ref, qrt_ref, ckv_ref, ckvb_ref, ckvt_ref, kr_ref, krb_ref):
    a = _rms(mq_ref[...], gq_ref[...]).astype(BF16)
    cq = jnp.dot(a, wuq_ref[...], preferred_element_type=F32)
    cqt = cq.T
    cos = cos_ref[...]
    sin = sin_ref[...]
    c = MLA_SCALE * LOG2E
    qrt_ref[...] = ((cqt[512:768] * cos.T + cqt[768:1024] * sin.T) * c).astype(BF16)
    for h in range(MLA_HEADS):
        qnt = cqt[h * MLA_NOPE:(h + 1) * MLA_NOPE].astype(BF16)
        qlt_ref[h * MLA_KV_LORA:(h + 1) * MLA_KV_LORA, :] = (jnp.dot(
            wukt_ref[h], qnt, preferred_element_type=F32) * c).astype(BF16)
    ckv = _rms(mkv_ref[...], gkv_ref[...])
    ckv_ref[...] = ckv
    ckvb_ref[...] = ckv.astype(BF16)
    ckvt_ref[0, 0:MLA_KV_LORA, :] = ckv.T.astype(BF16)
    ckvt_ref[0, MLA_KV_LORA:MLA_VR, :] = jnp.ones((MLA_VR - MLA_KV_LORA, ckv.shape[0]), BF16)
    mkr = mkr_ref[...]
    kr = mkr[:, 0:32] * cos[:, 0:32] + mkr[:, 32:64] * sin[:, 0:32]
    kr_ref[...] = kr
    krb_ref[...] = kr.astype(BF16)


def _k_mlaprep(z, cos_t, sin_t, gq, gkv, wuq_bf, wukt_bf, *, tm, zoff):
    N = z.shape[0]
    nr = cos_t.shape[0] // tm
    tab = pl.BlockSpec((tm, 256), lambda i: (i % nr, 0))
    row = lambda w: pl.BlockSpec((tm, w), lambda i: (i, 0))
    col = lambda h: pl.BlockSpec((h, tm), lambda i: (0, i))
    return pl.pallas_call(
        _mp_body,
        grid=(N // tm,),
        in_specs=[pl.BlockSpec((tm, 256), lambda i: (i, (C_MQ - zoff) // 256)),
                  pl.BlockSpec((tm, 256), lambda i: (i, (C_MKV - zoff) // 256)),
                  pl.BlockSpec((tm, 128), lambda i: (i, (C_MKR - zoff) // 128)),
                  tab, tab, _const_spec((1, 256)), _const_spec((1, 256)),
                  _const_spec((256, 1024)), _const_spec((MLA_HEADS, MLA_KV_LORA, MLA_NOPE))],
        out_specs=[col(2048), col(256), row(256), row(256),
                   pl.BlockSpec((1, MLA_VR, tm), lambda i: (i, 0, 0)), row(32), row(32)],
        out_shape=[jax.ShapeDtypeStruct((2048, N), BF16), jax.ShapeDtypeStruct((256, N), BF16),
                   jax.ShapeDtypeStruct((N, 256), F32), jax.ShapeDtypeStruct((N, 256), BF16),
                   jax.ShapeDtypeStruct((N // tm, MLA_VR, tm), BF16),
                   jax.ShapeDtypeStruct((N, 32), F32), jax.ShapeDtypeStruct((N, 32), BF16)],
        compiler_params=_cparams(("arbitrary",)),
        name="mla_prep",
    )(z, z, z, cos_t, sin_t, gq, gkv, wuq_bf, wukt_bf)


def _dot_nt(a, b):
    return lax.dot_general(a, b, (((1,), (1,)), ((), ())), preferred_element_type=F32)


MLA_SB = 128
MLA_CB = 256


def _mla_body(qlt_ref, qrt_ref, ckv_ref, kr_ref, ckvt_ref, wuvt_ref, o_ref,
              qt_scr, rt_scr, s_scr, pt_scr, m_scr, acc_scr,
              *, tq, P, L, KC, nkc_total, stacked):
    qi = pl.program_id(1)
    q0 = P + qi * tq
    cols_n = MLA_HEADS * tq
    if stacked:
        qt_scr[...] = qlt_ref[0]
        rt_scr[...] = qrt_ref[0]
    else:
        for h in range(MLA_HEADS):
            qt_scr[:, h * tq:(h + 1) * tq] = qlt_ref[h * 256:(h + 1) * 256, :]
            rt_scr[:, h * tq:(h + 1) * tq] = qrt_ref[h * 32:(h + 1) * 32, :]
    max_cend = (((q0 + tq - 1) >> 6) + 1) << 6
    nkc = jnp.minimum((max_cend + KC - 1) // KC, nkc_total)
    n_full = jnp.minimum(((q0 >> 6) + 1) << 6, L) // KC
    cb_w = min(MLA_CB, cols_n)

    m_scr[...] = jnp.full((1, cols_n), M_INIT, F32)
    acc_scr[...] = jnp.zeros((MLA_VR, cols_n), F32)

    def make_step(masked):
        def step(kc, carry):
            k0 = pl.multiple_of(kc * KC, KC)
            ck = ckv_ref[0, pl.ds(k0, KC), :]
            kr = kr_ref[0, pl.ds(k0, KC), :]
            s_scr[...] = (jnp.dot(ck, qt_scr[...], preferred_element_type=F32)
                          + jnp.dot(kr, rt_scr[...], preferred_element_type=F32))
            for cb in range(cols_n // cb_w):
                cols = slice(cb * cb_w, (cb + 1) * cb_w)
                if masked:
                    lane = cb * cb_w + lax.broadcasted_iota(I32, (1, cb_w), 1)
                    qpos = q0 + (lane & (tq - 1))
                    lim = jnp.minimum(((qpos >> 6) + 1) << 6, L)
                mx = jnp.full((8, cb_w), -jnp.inf, F32)
                for sb in range(KC // MLA_SB):
                    rs = slice(sb * MLA_SB, (sb + 1) * MLA_SB)
                    x = s_scr[rs, cols]
                    if masked:
                        kpos = k0 + sb * MLA_SB + lax.broadcasted_iota(I32, (MLA_SB, 1), 0)
                        x = jnp.where(kpos < lim, x, NEG)
                        s_scr[rs, cols] = x
                    mx = jnp.maximum(mx, _max8(x))
                m_old = m_scr[:, cols]
                m_new = jnp.maximum(m_old, jnp.max(mx, axis=0, keepdims=True))
                for sb in range(KC // MLA_SB):
                    rs = slice(sb * MLA_SB, (sb + 1) * MLA_SB)
                    pt_scr[rs, cols] = jnp.exp2(s_scr[rs, cols] - m_new).astype(BF16)
                m_scr[:, cols] = m_new
                acc_scr[:, cols] = (jnp.exp2(m_old - m_new) * acc_scr[:, cols]
                                    + jnp.dot(ckvt_ref[0, kc], pt_scr[:, cols], preferred_element_type=F32))
            return carry
        return step

    lax.fori_loop(0, n_full, make_step(False), 0)
    lax.fori_loop(n_full, nkc, make_step(True), 0)
    ot = (acc_scr[0:MLA_KV_LORA, :] / acc_scr[MLA_KV_LORA:MLA_KV_LORA + 1, :]).astype(BF16)
    if stacked:
        for h in range(MLA_HEADS):
            o_ref[0, h * MLA_V:(h + 1) * MLA_V, :] = jnp.dot(wuvt_ref[h], ot, preferred_element_type=F32)
    else:
        outs = [jnp.dot(wuvt_ref[h], ot[:, h * tq:(h + 1) * tq], preferred_element_type=F32)
                for h in range(MLA_HEADS)]
        o_ref[0] = jnp.concatenate(outs, axis=0).T


def _k_mla(qlt, qrt, ckvb, krb, ckvt, wuvt_bf, *, B, T, tq, P, L, KC, stacked):
    Lp = ckvb.shape[1]
    nkc_total = Lp // KC
    cols_n = MLA_HEADS * tq
    nq = T // tq
    if stacked:
        q_specs = [pl.BlockSpec((1, MLA_KV_LORA, cols_n), lambda b, q: (b, 0, 0)),
                   pl.BlockSpec((1, MLA_ROPE, cols_n), lambda b, q: (b, 0, 0))]
        out_spec = pl.BlockSpec((1, MLA_HEADS * MLA_V, cols_n), lambda b, q: (b, 0, 0))
        out_shape = jax.ShapeDtypeStruct((B, MLA_HEADS * MLA_V, cols_n), F32)
    else:
        q_specs = [pl.BlockSpec((MLA_HEADS * MLA_KV_LORA, tq), lambda b, q: (0, b * nq + q)),
                   pl.BlockSpec((MLA_HEADS * MLA_ROPE, tq), lambda b, q: (0, b * nq + q))]
        out_spec = pl.BlockSpec((1, tq, 512), lambda b, q: (b, q, 0))
        out_shape = jax.ShapeDtypeStruct((B, T, 512), F32)
    return pl.pallas_call(
        functools.partial(_mla_body, tq=tq, P=P, L=L, KC=KC, nkc_total=nkc_total, stacked=stacked),
        grid=(B, nq),
        in_specs=q_specs + [pl.BlockSpec((1, Lp, 256), lambda b, q: (b, 0, 0)),
                            pl.BlockSpec((1, Lp, 32), lambda b, q: (b, 0, 0)),
                            pl.BlockSpec((1, nkc_total, MLA_VR, KC), lambda b, q: (b, 0, 0, 0)),
                            _const_spec((MLA_HEADS, MLA_V, MLA_KV_LORA))],
        out_specs=out_spec,
        out_shape=out_shape,
        scratch_shapes=[pltpu.VMEM((MLA_KV_LORA, cols_n), BF16), pltpu.VMEM((MLA_ROPE, cols_n), BF16),
                        pltpu.VMEM((KC, cols_n), F32), pltpu.VMEM((KC, cols_n), BF16),
                        pltpu.VMEM((1, cols_n), F32), pltpu.VMEM((MLA_VR, cols_n), F32)],
        compiler_params=_cparams(("arbitrary", "arbitrary")),
        name="mla_attn",
    )(qlt, qrt, ckvb, krb, ckvt, wuvt_bf)


def _fold(x, r):
    n, w = x.shape
    return jnp.sum(x.reshape(n // r, r, w), axis=0)


def _fold8(x):
    return _fold(x, 8)


def _max8(x):
    n, w = x.shape
    return jnp.max(x.reshape(n // 8, 8, w), axis=0)


DSA_SB_ELEMS = 128 * 256


def _dsa_body(qt_ref, iqt_ref, iwt_ref, kb_ref, vt_ref, ikb_ref, o_ref,
              s_scr, w_scr, lg_scr, pt_scr, m_scr, acc_scr, *, tq, P, L, KC, nkc_total, topk):
    qi = pl.program_id(1)
    q0 = P + qi * tq
    sbk = min(KC, DSA_SB_ELEMS // tq)
    acc_r = sbk // 2
    qpos = q0 + lax.broadcasted_iota(I32, (1, tq), 1)
    lim = jnp.minimum(((qpos >> 6) + 1) << 6, L)
    keff = jnp.minimum(lim, topk).astype(F32)
    max_cend = (((q0 + tq - 1) >> 6) + 1) << 6
    nkc = jnp.minimum((max_cend + KC - 1) // KC, nkc_total)

    iqt = iqt_ref[0]
    iwt = iwt_ref[0]
    zpad = jnp.zeros((IDX_DIM, tq), BF16)
    iq_h = [jnp.concatenate([iqt[h * IDX_DIM:(h + 1) * IDX_DIM], zpad], axis=0) for h in range(IDX_HEADS)]

    def score_step(kc, carry):
        k0 = pl.multiple_of(kc * KC, KC)
        ik = ikb_ref[0, pl.ds(k0, KC), :]
        for h in range(IDX_HEADS):
            lg_scr[h, :, 0:tq] = jnp.dot(ik, iq_h[h], preferred_element_type=F32)
        for sb in range(KC // sbk):
            rs = slice(sb * sbk, (sb + 1) * sbk)
            s = jnp.zeros((sbk, tq), F32)
            for h in range(IDX_HEADS):
                s = s + iwt[h:h + 1, :] * jnp.maximum(lg_scr[h, rs, 0:tq], 0.0)
            kpos = k0 + sb * sbk + lax.broadcasted_iota(I32, (sbk, 1), 0)
            s = jnp.where(kpos < lim, s, -jnp.inf)
            b = lax.bitcast_convert_type(s, I32)
            b = jnp.where(b == INT_MIN, 0, b)
            s_scr[kc, rs, :] = b ^ ((b >> 31) & np.int32(0x7FFFFFFF))
        return carry

    lax.fori_loop(0, nkc, score_step, 0)

    def bit_step(i, thr):
        cand = thr + jnp.left_shift(jnp.int32(1), 31 - i)

        def cnt_step(kc, acc):
            return acc + _fold(jnp.where(s_scr[kc] >= cand, 1.0, 0.0), acc_r)

        cnt = jnp.sum(lax.fori_loop(0, nkc, cnt_step, jnp.zeros((acc_r, tq), F32)), axis=0, keepdims=True)
        return jnp.where(cnt >= keff, cand, thr)

    thr = lax.fori_loop(0, 32, bit_step, jnp.full((1, tq), INT_MIN, I32))

    def tie_cnt_step(kc, carry):
        key = s_scr[kc]
        return (carry[0] + _fold8(jnp.where(key > thr, 1.0, 0.0)),
                carry[1] + _fold8(jnp.where(key == thr, 1.0, 0.0)))

    z8 = jnp.zeros((8, tq), F32)
    n_gt, n_eq = lax.fori_loop(0, nkc, tie_cnt_step, (z8, z8))
    need = keff - jnp.sum(n_gt, axis=0, keepdims=True)
    surplus = jnp.max(jnp.sum(n_eq, axis=0, keepdims=True) - need)

    @pl.when(surplus < 0.5)
    def _():
        def fast_step(kc, carry):
            s_scr[kc] = lax.bitcast_convert_type(jnp.where(s_scr[kc] >= thr, 0.0, NEG), I32)
            return carry
        lax.fori_loop(0, nkc, fast_step, 0)

    @pl.when(surplus >= 0.5)
    def _():
        tri = jnp.where(lax.broadcasted_iota(I32, (KC, KC), 1) < lax.broadcasted_iota(I32, (KC, KC), 0),
                        1.0, 0.0).astype(BF16)

        def slow_step(kc, seen):
            key = s_scr[kc]
            eq = key == thr
            eqf = jnp.where(eq, 1.0, 0.0)
            rank = seen + jnp.dot(tri, eqf.astype(BF16), preferred_element_type=F32)
            bias = jnp.where(key > thr, 0.0, jnp.where(eq, jnp.where(rank < need, 0.0, NEG), NEG))
            s_scr[kc] = lax.bitcast_convert_type(bias, I32)
            return seen + jnp.sum(_fold8(eqf), axis=0, keepdims=True)
        lax.fori_loop(0, nkc, slow_step, jnp.zeros((1, tq), F32))

    hd = DSA_HEAD_DIM
    zq = jnp.zeros((hd, tq), BF16)
    for hp in range(DSA_HEADS // 2):
        r0 = hp * 2 * hd
        w_scr[hp, 0:hd, 0:tq] = qt_ref[0, r0:r0 + hd, :]
        w_scr[hp, 0:hd, tq:2 * tq] = zq
        w_scr[hp, hd:2 * hd, 0:tq] = zq
        w_scr[hp, hd:2 * hd, tq:2 * tq] = qt_ref[0, r0 + hd:r0 + 2 * hd, :]
    m_scr[...] = jnp.full((DSA_HEADS, tq), M_INIT, F32)
    acc_scr[...] = jnp.zeros((DSA_HEADS * DSA_VR, tq), F32)
    nsb = KC // sbk

    nhp = DSA_HEADS // 2

    def att_step(kc, carry):
        k0 = pl.multiple_of(kc * KC, KC)
        for hp in range(nhp):
            r0 = hp * 2 * hd
            lg_scr[hp] = jnp.dot(kb_ref[0, pl.ds(k0, KC), r0:r0 + 2 * hd], w_scr[hp],
                                 preferred_element_type=F32)
        for hp in range(nhp):
            li = hp
            for e in range(2):
                h = 2 * hp + e
                cols = slice(e * tq, (e + 1) * tq)
                mx = jnp.full((8, tq), -jnp.inf, F32)
                for sb in range(nsb):
                    rs = slice(sb * sbk, (sb + 1) * sbk)
                    x = lg_scr[li, rs, cols] + lax.bitcast_convert_type(s_scr[kc, rs, :], F32)
                    lg_scr[li, rs, cols] = x
                    mx = jnp.maximum(mx, _max8(x))
                m_old = m_scr[h:h + 1, :]
                m_new = jnp.maximum(m_old, jnp.max(mx, axis=0, keepdims=True))
                alpha = jnp.exp2(m_old - m_new)
                for sb in range(nsb):
                    rs = slice(sb * sbk, (sb + 1) * sbk)
                    pt_scr[h, rs, :] = jnp.exp2(lg_scr[li, rs, cols] - m_new).astype(BF16)
                m_scr[h:h + 1, :] = m_new
                hs = slice(h * DSA_VR, (h + 1) * DSA_VR)
                acc_scr[hs, :] = alpha * acc_scr[hs, :] + jnp.dot(vt_ref[0, kc, hs, :], pt_scr[h],
                                                                  preferred_element_type=F32)
        return carry

    lax.fori_loop(0, nkc, att_step, 0)
    outs = [acc_scr[h * DSA_VR:h * DSA_VR + hd, :] / acc_scr[h * DSA_VR + hd:h * DSA_VR + hd + 1, :]
            for h in range(DSA_HEADS)]
    o_ref[0] = jnp.concatenate(outs, axis=0).T


def _k_dsa(qt, iqt, iwt, kb3, vt4, ikb3, *, tq, P, L, KC, topk):
    B, _, T = qt.shape
    Lp = kb3.shape[1]
    nkc_total = Lp // KC
    return pl.pallas_call(
        functools.partial(_dsa_body, tq=tq, P=P, L=L, KC=KC, nkc_total=nkc_total, topk=topk),
        grid=(B, T // tq),
        in_specs=[pl.BlockSpec((1, 512, tq), lambda b, q: (b, 0, q)),
                  pl.BlockSpec((1, 256, tq), lambda b, q: (b, 0, q)),
                  pl.BlockSpec((1, 8, tq), lambda b, q: (b, 0, q)),
                  pl.BlockSpec((1, Lp, 512), lambda b, q: (b, 0, 0)),
                  pl.BlockSpec((1, nkc_total, DSA_HEADS * DSA_VR, KC), lambda b, q: (b, 0, 0, 0)),
                  pl.BlockSpec((1, Lp, 128), lambda b, q: (b, 0, 0))],
        out_specs=pl.BlockSpec((1, tq, 512), lambda b, q: (b, q, 0)),
        out_shape=jax.ShapeDtypeStruct((B, T, 512), F32),
        scratch_shapes=[pltpu.VMEM((nkc_total, KC, tq), I32),
                        pltpu.VMEM((DSA_HEADS // 2, 2 * DSA_HEAD_DIM, 2 * tq), BF16),
                        pltpu.VMEM((DSA_HEADS // 2, KC, 2 * tq), F32),
                        pltpu.VMEM((DSA_HEADS, KC, tq), BF16),
                        pltpu.VMEM((DSA_HEADS, tq), F32),
                        pltpu.VMEM((DSA_HEADS * DSA_VR, tq), F32)],
        compiler_params=_cparams(("arbitrary", "arbitrary")),
        name="dsa_attn",
    )(qt, iqt, iwt, kb3, vt4, ikb3)


def _merge_body(x_ref, sc_ref, sh_ref, g1_ref, ng_ref, oa_ref, ob_ref, oc_ref, od_ref,
                wg_ref, bg_ref, wb_ref, wo_ref, o_ref, *, bb, tt):
    x = x_ref[...]
    tm = bb * tt
    h = (_rms(x, ng_ref[...]) * (1.0 + sc_ref[...]) + sh_ref[...]).reshape(tm, D_MODEL).astype(BF16)
    merged = jnp.zeros((tm, D_MODEL), F32)
    for i, oref in enumerate((oa_ref, ob_ref, oc_ref, od_ref)):
        gate = _sigmoid(jnp.dot(h, wg_ref[:, i * D_MODEL:(i + 1) * D_MODEL], preferred_element_type=F32)
                        + bg_ref[:, i * D_MODEL:(i + 1) * D_MODEL])
        br = jnp.dot(oref[...].astype(BF16), wb_ref[i * BRANCH_W:(i + 1) * BRANCH_W, :],
                     preferred_element_type=F32)
        merged = merged + gate * br
    y = jnp.dot(merged.astype(BF16), wo_ref[...], preferred_element_type=F32)
    o_ref[...] = x + g1_ref[...] * y.reshape(bb, tt, D_MODEL)


def _k_merge(x, sc, sh, g1, ng, oa, ob, oc, od, wg_bf, bg, wb_bf, wo_bf, *, bb, tt):
    B, T, D = x.shape
    nt = T // tt
    tm = bb * tt
    xs = pl.BlockSpec((bb, tt, D), lambda i: (i // nt, i % nt, 0))
    ms = pl.BlockSpec((bb, 1, D), lambda i: (i // nt, 0, 0))
    os_ = pl.BlockSpec((tm, 512), lambda i: (i, 0))
    return pl.pallas_call(
        functools.partial(_merge_body, bb=bb, tt=tt),
        grid=((B // bb) * nt,),
        in_specs=[xs, ms, ms, ms, _const_spec((1, 1, D)), os_, os_, os_, os_,
                  _const_spec((D, 4 * D)), _const_spec((1, 4 * D)), _const_spec((4 * BRANCH_W, D)),
                  _const_spec((D, D))],
        out_specs=xs,
        out_shape=jax.ShapeDtypeStruct((B, T, D), F32),
        compiler_params=_cparams(("arbitrary",)),
        name="merge_out",
    )(x, sc, sh, g1, ng, oa, ob, oc, od, wg_bf, bg, wb_bf, wo_bf)


FC = FFN_DIM // 2
MOE_RB = 256


def _swiglu_chunk(h, w1, w3, w2):
    a = jnp.dot(h, w1, preferred_element_type=F32)
    b = jnp.dot(h, w3, preferred_element_type=F32)
    act = (a * _sigmoid(a) * b).astype(BF16)
    return jnp.dot(act, w2, preferred_element_type=F32)


def _finish(x, g2, f, fg_ref, o_ref, bb, tt):
    y = x + g2 * f.reshape(bb, tt, D_MODEL)
    if fg_ref is not None:
        y = _rms(y, fg_ref[...])
    o_ref[...] = y


def _ffn_body(x_ref, sc_ref, sh_ref, g2_ref, ng_ref, w1_ref, w3_ref, w2_ref, *rest, bb, tt, final):
    fg_ref, o_ref = rest if final else (None, rest[0])
    x = x_ref[...]
    tm = bb * tt
    h = (_rms(x, ng_ref[...]) * (1.0 + sc_ref[...]) + sh_ref[...]).reshape(tm, D_MODEL).astype(BF16)
    f = jnp.zeros((tm, D_MODEL), F32)
    for ci in range(FFN_DIM // FC):
        f = f + _swiglu_chunk(h, w1_ref[:, ci * FC:(ci + 1) * FC], w3_ref[:, ci * FC:(ci + 1) * FC],
                              w2_ref[ci * FC:(ci + 1) * FC, :])
    _finish(x, g2_ref[...], f, fg_ref, o_ref, bb, tt)


def _k_ffn(x, sc, sh, g2, ng, w1_bf, w3_bf, w2_bf, fg, *, bb, tt):
    B, T, D = x.shape
    nt = T // tt
    final = fg is not None
    xs = pl.BlockSpec((bb, tt, D), lambda i: (i // nt, i % nt, 0))
    ms = pl.BlockSpec((bb, 1, D), lambda i: (i // nt, 0, 0))
    in_specs = [xs, ms, ms, ms, _const_spec((1, 1, D)), _const_spec((D, FFN_DIM)),
                _const_spec((D, FFN_DIM)), _const_spec((FFN_DIM, D))]
    args = [x, sc, sh, g2, ng, w1_bf, w3_bf, w2_bf]
    if final:
        in_specs.append(_const_spec((1, 1, D)))
        args.append(fg)
    return pl.pallas_call(
        functools.partial(_ffn_body, bb=bb, tt=tt, final=final),
        grid=((B // bb) * nt,),
        in_specs=in_specs,
        out_specs=xs,
        out_shape=jax.ShapeDtypeStruct((B, T, D), F32),
        compiler_params=_cparams(("arbitrary",)),
        name="ffn_dense",
    )(*args)


def _moe_body(x_ref, sc_ref, sh_ref, g2_ref, ng_ref, rw_ref, rb_ref, w1_ref, w3_ref, w2_ref, *rest,
              bb, tt, final):
    if final:
        fg_ref, o_ref, h_scr, gate_scr, rank_scr, rankt_scr, xg_scr, y_scr, acc_scr = rest
    else:
        fg_ref = None
        o_ref, h_scr, gate_scr, rank_scr, rankt_scr, xg_scr, y_scr, acc_scr = rest
    e = pl.program_id(1)
    ci = pl.program_id(2)
    last_c = FFN_DIM // FC - 1
    tm = bb * tt
    lane = lax.broadcasted_iota(I32, (tm, 128), 1)

    @pl.when((e == 0) & (ci == 0))
    def _():
        h = (_rms(x_ref[...], ng_ref[...]) * (1.0 + sc_ref[...]) + sh_ref[...]).reshape(tm, D_MODEL)
        h_scr[...] = h.astype(BF16)
        logits = jnp.dot(h, rw_ref[...], preferred_element_type=F32,
                         precision=lax.Precision.HIGHEST) + rb_ref[...]
        m1 = jnp.max(logits, axis=-1, keepdims=True)
        i1 = jnp.min(jnp.where(logits == m1, lane, 128), axis=-1, keepdims=True)
        rest_l = jnp.where(lane == i1, NEG, logits)
        m2 = jnp.max(rest_l, axis=-1, keepdims=True)
        i2 = jnp.min(jnp.where(rest_l == m2, lane, 128), axis=-1, keepdims=True)
        e2 = jnp.exp(m2 - m1)
        den = 1.0 + e2
        gate_scr[...] = jnp.where(lane == i1, 1.0 / den, jnp.where(lane == i2, e2 / den, 0.0))
        sel = jnp.where(lane == i1, 1.0, jnp.where(lane == i2, 1.0, 0.0))
        tri = jnp.where(lax.broadcasted_iota(I32, (tm, tm), 1) < lax.broadcasted_iota(I32, (tm, tm), 0),
                        1.0, 0.0).astype(BF16)
        rank = jnp.where(sel > 0.5, jnp.dot(tri, sel.astype(BF16), preferred_element_type=F32), -1.0)
        rank_scr[...] = rank
        rankt_scr[...] = rank.T
        acc_scr[...] = jnp.zeros((tm, D_MODEL), F32)

    is_e = lane == e
    gate_col = jnp.sum(jnp.where(is_e, gate_scr[...], 0.0), axis=-1, keepdims=True)
    rank_col = jnp.sum(jnp.where(is_e, rank_scr[...], 0.0), axis=-1, keepdims=True)
    rank_row = rankt_scr[pl.ds(e, 1), :]
    n_e = jnp.max(rank_row).astype(I32) + 1
    nblk = (n_e + MOE_RB - 1) // MOE_RB

    def blk_step(b, carry):
        base = (b * MOE_RB).astype(F32)

        @pl.when(ci == 0)
        def _():
            slot = base + lax.broadcasted_iota(I32, (MOE_RB, 1), 0).astype(F32)
            pick = jnp.where(rank_row == slot, 1.0, 0.0).astype(BF16)
            xg_scr[b] = jnp.dot(pick, h_scr[...], preferred_element_type=F32).astype(BF16)

        part = _swiglu_chunk(xg_scr[b], w1_ref[0], w3_ref[0], w2_ref[0])

        @pl.when(ci == 0)
        def _():
            y_scr[b] = part

        @pl.when((ci > 0) & (ci < last_c))
        def _():
            y_scr[b] += part

        @pl.when(ci == last_c)
        def _():
            y = y_scr[b] + part
            y_hi = y.astype(BF16)
            y_lo = (y - y_hi.astype(F32)).astype(BF16)
            slot = base + lax.broadcasted_iota(I32, (1, MOE_RB), 1).astype(F32)
            put = jnp.where(rank_col == slot, 1.0, 0.0).astype(BF16)
            acc_scr[...] += gate_col * (jnp.dot(put, y_hi, preferred_element_type=F32)
                                        + jnp.dot(put, y_lo, preferred_element_type=F32))
        return carry

    lax.fori_loop(0, nblk, blk_step, 0)

    @pl.when((e == N_EXPERTS - 1) & (ci == last_c))
    def _():
        _finish(x_ref[...], g2_ref[...], acc_scr[...], fg_ref, o_ref, bb, tt)


def _k_moe(x, sc, sh, g2, ng, rw_pad, rb_pad, w1_bf, w3_bf, w2_bf, fg, *, bb, tt):
    B, T, D = x.shape
    nt = T // tt
    tm = bb * tt
    final = fg is not None
    xs = pl.BlockSpec((bb, tt, D), lambda i, e, c: (i // nt, i % nt, 0))
    xin = pl.BlockSpec((bb, tt, D), lambda i, e, c: (i // nt, i % nt, 0), pipeline_mode=pl.Buffered(1))
    ms = pl.BlockSpec((bb, 1, D), lambda i, e, c: (i // nt, 0, 0))
    in_specs = [xin, ms, ms, ms, _const_spec((1, 1, D)), _const_spec((D, 128)), _const_spec((1, 128)),
                pl.BlockSpec((1, D, FC), lambda i, e, c: (e, 0, c)),
                pl.BlockSpec((1, D, FC), lambda i, e, c: (e, 0, c)),
                pl.BlockSpec((1, FC, D), lambda i, e, c: (e, c, 0))]
    args = [x, sc, sh, g2, ng, rw_pad, rb_pad, w1_bf, w3_bf, w2_bf]
    if final:
        in_specs.append(_const_spec((1, 1, D)))
        args.append(fg)
    return pl.pallas_call(
        functools.partial(_moe_body, bb=bb, tt=tt, final=final),
        grid=((B // bb) * nt, N_EXPERTS, FFN_DIM // FC),
        in_specs=in_specs,
        out_specs=xs,
        out_shape=jax.ShapeDtypeStruct((B, T, D), F32),
        scratch_shapes=[pltpu.VMEM((tm, D), BF16), pltpu.VMEM((tm, 128), F32), pltpu.VMEM((tm, 128), F32),
                        pltpu.VMEM((128, tm), F32), pltpu.VMEM((pl.cdiv(tm, MOE_RB), MOE_RB, D), BF16),
                        pltpu.VMEM((pl.cdiv(tm, MOE_RB), MOE_RB, D), F32), pltpu.VMEM((tm, D), F32)],
        compiler_params=_cparams(("arbitrary", "arbitrary", "arbitrary")),
        name="ffn_moe",
    )(*args)


def _prep_w_in(w):
    aq, ak, av, iq, ik, iw, pu, cb, cc, ch, mq, mkv, mkr = jnp.split(w, IN_SPLIT_POINTS, axis=1)
    d = w.shape[0]
    mkr_sw = jnp.concatenate([mkr[:, 16:], mkr[:, :16]], axis=1)
    return jnp.concatenate([aq, ak, av, pu, cb, cc, ch, iq, mq, mkv, ik, iw, jnp.zeros((d, 60), w.dtype),
                            mkr, mkr_sw, jnp.zeros((d, 64), w.dtype)], axis=1).astype(BF16)


def _prep_w_uq(w):
    w3 = w.reshape(MLA_Q_LORA, MLA_HEADS, MLA_NOPE + MLA_ROPE)
    nope = w3[:, :, :MLA_NOPE].reshape(MLA_Q_LORA, MLA_HEADS * MLA_NOPE)
    rope = w3[:, :, MLA_NOPE:]
    rope_sw = jnp.concatenate([rope[:, :, 16:], rope[:, :, :16]], axis=-1)
    return jnp.concatenate([nope, rope.reshape(MLA_Q_LORA, -1), rope_sw.reshape(MLA_Q_LORA, -1)],
                           axis=1).astype(BF16)


def _rope_tables(pos):
    half = MLA_ROPE // 2
    inv = 1.0 / (ROPE_BASE ** (jnp.arange(half, dtype=F32) / half))
    ang = pos.astype(F32)[:, None] * inv[None, :]
    cos, sin = jnp.cos(ang), jnp.sin(ang)
    cos2 = jnp.tile(jnp.concatenate([cos, cos], axis=-1), (1, MLA_HEADS))
    sin2 = jnp.tile(jnp.concatenate([-sin, sin], axis=-1), (1, MLA_HEADS))
    return cos2, sin2


def _layer(x, mod, past, lw, fw, use_moe, final_g, cfg):
    B, T, D = x.shape
    bb, tt, KC, tq_dsa, tq_mla = cfg["bb"], cfg["tt"], cfg["KC"], cfg["tq_dsa"], cfg["tq_mla"]
    KCD = cfg["KCD"]
    N = B * T
    P = 0 if past is None else past["dsa_k"].shape[1]
    L = P + T
    sh1, sc1, g1, sh2, sc2, g2 = [m.reshape(B, 1, D) for m in jnp.split(mod, 6, axis=-1)]
    ng1 = lw["norm_mix_g"].reshape(1, 1, D)
    ng2 = lw["norm_ffn_g"].reshape(1, 1, D)

    emit = past is None
    res = _k_in(x, sc1, sh1, ng1, lw["w_in"], bb=bb, tt=tt, emit=emit, kc=KCD)
    z = res[0]
    zoff = Z_OFF if emit else 0
    z3 = z.reshape(B, T, ZW - zoff)
    if emit:
        k_new = res[1].reshape(B, T, DSA_HEADS, DSA_HEAD_DIM)
        v_new = res[2].reshape(B, T, DSA_HEADS, DSA_HEAD_DIM)
    else:
        k_new = z[:, C_AK:C_AK + 512].reshape(B, T, DSA_HEADS, DSA_HEAD_DIM)
        v_new = z[:, C_AV:C_AV + 512].reshape(B, T, DSA_HEADS, DSA_HEAD_DIM)
    ik_new = z3[:, :, C_IKW - zoff:C_IKW - zoff + IDX_DIM]

    if past is None:
        ph16 = jnp.zeros((B, 16, 512), F32)
        cvh16 = ph16
    else:
        ph16 = jnp.pad(past["pool"], ((0, 0), (1, 0), (0, 0)))
        cvh16 = jnp.pad(past["conv"], ((0, 0), (16 - (CONV_WIDTH - 1), 0), (0, 0)))
    o_b, o_c, ph_o, cvh_o = _k_pc(z3, ph16, cvh16, lw["pool_w"], lw["pool_scale"], lw["conv_w"],
                                  bb=bb, tt=tt, P=P, zoff=zoff)
    pool_hist = ph_o[:, 1:]
    conv_hist = cvh_o[:, 16 - (CONV_WIDTH - 1):]

    cos2, sin2 = _rope_tables(P + jnp.arange(T))
    if bb > 1:
        cos2, sin2 = jnp.tile(cos2, (bb, 1)), jnp.tile(sin2, (bb, 1))
    qlt, qrt, ckv_new, ckvb, ckvt, kr_new, krb = _k_mlaprep(
        z, cos2, sin2, lw["mla_q_norm_g"], lw["mla_kv_norm_g"], lw["mla_w_uq"], lw["mla_w_uk"], tm=bb * tt,
        zoff=zoff)

    topk = min(DSA_TOPK_MAX, L // 4)
    if past is None:
        kb3 = res[3].reshape(B, T, 512)
        vt4, qt, iqt, iwt = res[4], res[5], res[6], res[7]
        ikb3 = res[8].reshape(B, T, 128)
        ckvb3 = ckvb.reshape(B, T, 256)
        krb3 = krb.reshape(B, T, 32)
        Tq = T
    else:
        Lp = -(-L // KCD) * KCD
        padk = ((0, 0), (0, Lp - L), (0, 0))
        k_all = jnp.concatenate([past["dsa_k"].reshape(B, P, 512), k_new.reshape(B, T, 512)], axis=1)
        v_all = jnp.concatenate([past["dsa_v"].reshape(B, P, 512), v_new.reshape(B, T, 512)], axis=1)
        kb3 = jnp.pad(k_all.astype(BF16), padk)
        vt5 = jnp.pad(v_all.astype(BF16), padk).reshape(B, Lp // KCD, KCD, DSA_HEADS, DSA_HEAD_DIM)
        vt5 = jnp.concatenate([vt5.transpose(0, 1, 3, 4, 2),
                               jnp.ones((B, Lp // KCD, DSA_HEADS, DSA_VR - DSA_HEAD_DIM, KCD), BF16)], axis=3)
        vt4 = vt5.reshape(B, Lp // KCD, DSA_HEADS * DSA_VR, KCD)
        ik_all = jnp.concatenate([past["idx_k"], ik_new], axis=1).astype(BF16)
        ikb3 = jnp.pad(ik_all, ((0, 0), (0, Lp - L), (0, 128 - IDX_DIM)))
        Tq = tq_dsa
        padq = ((0, 0), (0, 0), (0, Tq - T))
        qt = jnp.pad((z3[:, :, C_AQ:C_AQ + 512] * (DSA_SCALE * LOG2E)).transpose(0, 2, 1).astype(BF16), padq)
        iqt = jnp.pad(z3[:, :, C_IQ:C_IQ + 256].transpose(0, 2, 1).astype(BF16), padq)
        iwt = jnp.pad(z3[:, :, C_IKW + 64:C_IKW + 72].transpose(0, 2, 1) * IDX_SCALE, padq)
        Lm = -(-L // KC) * KC
        padk = ((0, 0), (0, Lm - L), (0, 0))
        ckvb3 = jnp.pad(jnp.concatenate([past["ckv"].astype(BF16), ckvb.reshape(B, T, 256)], axis=1), padk)
        krb3 = jnp.pad(jnp.concatenate([past["krope"].astype(BF16), krb.reshape(B, T, 32)], axis=1), padk)

    o_a = _k_dsa(qt, iqt, iwt, kb3, vt4, ikb3, tq=tq_dsa, P=P, L=L, KC=KCD, topk=topk)
    o_a = o_a[:, :T].reshape(N, 512)
    if past is None:
        ckvt4 = ckvt.reshape(B, T // KC, MLA_VR, KC)
        o_d = _k_mla(qlt, qrt, ckvb3, krb3, ckvt4, lw["mla_w_uv"], B=B, T=T, tq=tq_mla, P=P, L=L, KC=KC,
                     stacked=False).reshape(N, 512)
    else:
        qs = qlt.reshape(MLA_HEADS, 256, B, T).transpose(2, 1, 0, 3).reshape(B, 256, MLA_HEADS * T)
        rs = qrt.reshape(MLA_HEADS, 32, B, T).transpose(2, 1, 0, 3).reshape(B, 32, MLA_HEADS * T)
        nk = ckvb3.shape[1] // KC
        ckvt4 = jnp.concatenate([ckvb3.reshape(B, nk, KC, 256).transpose(0, 1, 3, 2),
                                 jnp.ones((B, nk, MLA_VR - MLA_KV_LORA, KC), BF16)], axis=2)
        o5 = _k_mla(qs, rs, ckvb3, krb3, ckvt4, lw["mla_w_uv"], B=B, T=T, tq=tq_mla, P=P, L=L, KC=KC,
                    stacked=True).reshape(B, MLA_HEADS, MLA_V, MLA_HEADS, T)
        o_d = jnp.stack([o5[:, h, :, h, :] for h in range(MLA_HEADS)], axis=1)
        o_d = o_d.transpose(0, 3, 1, 2).reshape(N, 512)

    x1 = _k_merge(x, sc1, sh1, g1, ng1, o_a, o_b.reshape(N, 512), o_c.reshape(N, 512), o_d,
                  lw["w_gate"], lw["b_gate"], lw["w_branch"], lw["w_out"], bb=bb, tt=tt)
    fg = None if final_g is None else final_g.reshape(1, 1, D)
    if use_moe:
        x2 = _k_moe(x1, sc2, sh2, g2, ng2, fw["rw"], fw["rb"], fw["w1"], fw["w3"], fw["w2"], fg,
                    bb=cfg["bb_moe"], tt=cfg["tt_moe"])
    else:
        x2 = _k_ffn(x1, sc2, sh2, g2, ng2, fw["w1"], fw["w3"], fw["w2"], fg, bb=bb, tt=tt)
    rows = (k_new, v_new, ik_new, ckv_new.reshape(B, T, 256), kr_new.reshape(B, T, 32), pool_hist, conv_hist)
    return x2, rows


def _prep_layer(l, ada_w, ada_b, norm_mix_g, norm_ffn_g, w_in, mla_q_norm_g, mla_kv_norm_g,
                mla_w_uq, mla_w_uk, mla_w_uv, pool_w, pool_scale, conv_w, w_gate, b_gate, w_branch, w_out):
    return dict(
        ada_w=ada_w[l].astype(BF16), ada_b=ada_b[l].reshape(1, -1),
        norm_mix_g=norm_mix_g[l], norm_ffn_g=norm_ffn_g[l],
        w_in=_prep_w_in(w_in[l]),
        mla_q_norm_g=mla_q_norm_g[l].reshape(1, -1), mla_kv_norm_g=mla_kv_norm_g[l].reshape(1, -1),
        mla_w_uq=_prep_w_uq(mla_w_uq[l]),
        mla_w_uk=mla_w_uk[l].transpose(1, 0, 2).astype(BF16),
        mla_w_uv=mla_w_uv[l].transpose(1, 2, 0).astype(BF16),
        pool_w=pool_w[l].astype(BF16), pool_scale=pool_scale[l].reshape(1, -1),
        conv_w=jnp.pad(conv_w[l].reshape(CONV_WIDTH, BRANCH_W), ((0, 8 - CONV_WIDTH), (0, 0))),
        w_gate=w_gate[l].astype(BF16), b_gate=b_gate[l].reshape(1, -1),
        w_branch=w_branch[l].astype(BF16), w_out=w_out[l].astype(BF16))


def _cfg_for(B, T, has_past):
    if has_past:
        return dict(bb=B, tt=T, KC=512, KCD=512, tq_dsa=128, tq_mla=T, bb_moe=B, tt_moe=T)
    t = min(512, T)
    return dict(bb=1, tt=t, KC=t, KCD=t, tq_dsa=t, tq_mla=min(128, T), bb_moe=1, tt_moe=min(1024, T))


def _run(x, c, pasts, lws, fws, final_norm_g):
    B, T, _ = x.shape
    cfg = _cfg_for(B, T, pasts is not None)
    outs = [[] for _ in range(7)]
    depth = len(lws)
    for l in range(depth):
        lw = lws[l]
        mod = _ada(c, lw["ada_w"], lw["ada_b"])
        x, rows = _layer(x, mod, None if pasts is None else pasts[l], lw, fws[l], l % 2 == 1,
                         final_norm_g if l == depth - 1 else None, cfg)
        for o, r in zip(outs, rows):
            o.append(r)
    return x, [jnp.stack(o) for o in outs]


def kernel(x_prompt, x_sample, c_prompt, c_sample, cache_dsa_k, cache_dsa_v, cache_dsa_idx_k, cache_mla_ckv, cache_mla_krope, state_pool, state_conv, ada_w, ada_b, norm_mix_g, norm_ffn_g, w_in, mla_q_norm_g, mla_kv_norm_g, mla_w_uq, mla_w_uk, mla_w_uv, pool_w, pool_scale, conv_w, w_gate, b_gate, w_branch, w_out, ffn_w1, ffn_w3, ffn_w2, moe_router_w, moe_router_b, moe_w1, moe_w3, moe_w2, final_norm_g):
    depth = ada_w.shape[0]
    lws = [_prep_layer(l, ada_w, ada_b, norm_mix_g, norm_ffn_g, w_in, mla_q_norm_g, mla_kv_norm_g,
                       mla_w_uq, mla_w_uk, mla_w_uv, pool_w, pool_scale, conv_w, w_gate, b_gate,
                       w_branch, w_out) for l in range(depth)]
    fws = []
    for l in range(depth):
        j = l // 2
        if l % 2 == 0:
            fws.append(dict(w1=ffn_w1[j].astype(BF16), w3=ffn_w3[j].astype(BF16), w2=ffn_w2[j].astype(BF16)))
        else:
            fws.append(dict(
                rw=jnp.pad(moe_router_w[j], ((0, 0), (0, 128 - N_EXPERTS))),
                rb=jnp.pad(moe_router_b[j].reshape(1, -1), ((0, 0), (0, 128 - N_EXPERTS)), constant_values=NEG),
                w1=moe_w1[j].astype(BF16), w3=moe_w3[j].astype(BF16), w2=moe_w2[j].astype(BF16)))
    sample_pasts = [dict(dsa_k=cache_dsa_k[l], dsa_v=cache_dsa_v[l], idx_k=cache_dsa_idx_k[l],
                         ckv=cache_mla_ckv[l], krope=cache_mla_krope[l],
                         pool=state_pool[l], conv=state_conv[l]) for l in range(depth)]
    y_prompt, pn = _run(x_prompt, c_prompt, None, lws, fws, final_norm_g)
    y_sample, sn = _run(x_sample, c_sample, sample_pasts, lws, fws, final_norm_g)
    return (y_prompt, y_sample, pn[0], pn[1], pn[2], pn[3], pn[4], pn[5], pn[6],
            sn[0], sn[1], sn[2], sn[3], sn[4], sn[5], sn[6])
```

```python
import functools

import numpy as np
import jax
import jax.numpy as jnp
from jax import lax
from jax.experimental import pallas as pl
from jax.experimental.pallas import tpu as pltpu

F32 = jnp.float32
BF16 = jnp.bfloat16
I32 = jnp.int32

D_MODEL = 1024
CHUNK = 64
N_BRANCH = 4
BRANCH_W = 512
DSA_HEADS = 8
DSA_HEAD_DIM = 64
IDX_HEADS = 4
IDX_DIM = 64
DSA_TOPK_MAX = 256
POOL_WINDOWS = (2, 4, 8, 16)
POOL_GROUP = BRANCH_W // 4
POOL_HIST = 15
CONV_WIDTH = 3
MLA_HEADS = 8
MLA_Q_LORA = 256
MLA_KV_LORA = 256
MLA_NOPE = 64
MLA_ROPE = 32
MLA_V = 64
ROPE_BASE = 10000.0
FFN_DIM = 2816
N_EXPERTS = 8
EPS = 1e-6

DSA_SCALE = DSA_HEAD_DIM ** -0.5
IDX_SCALE = (IDX_HEADS * IDX_DIM) ** -0.5
MLA_SCALE = (MLA_NOPE + MLA_ROPE) ** -0.5
LOG2E = 1.4426950408889634

IN_SPLIT_WIDTHS = (512, 512, 512, 256, 64, 4, 512, 512, 512, 512, 256, 256, 32)
IN_SPLIT_POINTS = tuple(int(sum(IN_SPLIT_WIDTHS[:i + 1])) for i in range(len(IN_SPLIT_WIDTHS) - 1))

C_AQ, C_AK, C_AV, C_PU, C_CB, C_CC, C_CH = 0, 512, 1024, 1536, 2048, 2560, 3072
C_IQ, C_MQ, C_MKV, C_IKW, C_MKR, ZW = 3584, 3840, 4096, 4352, 4480, 4608
IN_TN = 1536
Z_OFF = IN_TN
DSA_VR = DSA_HEAD_DIM + 16
MLA_VR = MLA_KV_LORA + 16

NEG = -1e30
M_INIT = -1e29
INT_MIN = np.int32(-2 ** 31)
V7X_VMEM_LIMIT = 56 * 1024 * 1024


def _cparams(sem):
    return pltpu.CompilerParams(dimension_semantics=sem, vmem_limit_bytes=V7X_VMEM_LIMIT)


def _rms(x, g):
    return x * lax.rsqrt(jnp.mean(x * x, axis=-1, keepdims=True) + EPS) * g


def _sigmoid(x):
    return 1.0 / (1.0 + jnp.exp(-x))


def _const_spec(shape):
    nd = len(shape)
    return pl.BlockSpec(shape, lambda *_: (0,) * nd, pipeline_mode=pl.Buffered(1))


def _ada_body(c_ref, w_ref, b_ref, o_ref):
    c = c_ref[...]
    s = (c * _sigmoid(c)).astype(BF16)
    o_ref[...] = jnp.dot(s, w_ref[...], preferred_element_type=F32) + b_ref[...]


def _ada(c, w_bf, b):
    B, D = c.shape
    n = w_bf.shape[1]
    tn = 1536
    return pl.pallas_call(
        _ada_body,
        grid=(n // tn,),
        in_specs=[pl.BlockSpec((B, D), lambda j: (0, 0)),
                  pl.BlockSpec((D, tn), lambda j: (0, j)),
                  pl.BlockSpec((1, tn), lambda j: (0, j))],
        out_specs=pl.BlockSpec((B, tn), lambda j: (0, j)),
        out_shape=jax.ShapeDtypeStruct((B, n), F32),
        compiler_params=_cparams(("arbitrary",)),
        name="ada_mod",
    )(c, w_bf, b)


def _in_body(x_ref, sc_ref, sh_ref, g_ref, w_ref, z_ref, *rest, bb, tt, emit, kc):
    if emit:
        kn_ref, vn_ref, kb_ref, vt_ref, qt_ref, iqt_ref, iwt_ref, ikb_ref, h_scr = rest
    else:
        (h_scr,) = rest
    j = pl.program_id(1)

    @pl.when(j == 0)
    def _():
        h = _rms(x_ref[...], g_ref[...]) * (1.0 + sc_ref[...]) + sh_ref[...]
        h_scr[...] = h.reshape(bb * tt, D_MODEL).astype(BF16)

    z = jnp.dot(h_scr[...], w_ref[...], preferred_element_type=F32)

    if not emit:
        z_ref[...] = z
    else:
        @pl.when(j > 0)
        def _():
            z_ref[...] = z

        @pl.when(j == 0)
        def _():
            k = z[:, C_AK:C_AK + 512]
            v = z[:, C_AV:C_AV + 512]
            kn_ref[...] = k
            vn_ref[...] = v
            kb_ref[...] = k.astype(BF16)
            vt = v.T.astype(BF16)
            ones = jnp.ones((DSA_VR - DSA_HEAD_DIM, kc), BF16)
            for c in range(tt // kc):
                for h in range(DSA_HEADS):
                    vt_ref[0, c, h * DSA_VR:h * DSA_VR + DSA_HEAD_DIM, :] = vt[
                        h * DSA_HEAD_DIM:(h + 1) * DSA_HEAD_DIM, c * kc:(c + 1) * kc]
                    vt_ref[0, c, h * DSA_VR + DSA_HEAD_DIM:(h + 1) * DSA_VR, :] = ones
            qt_ref[0] = (z[:, C_AQ:C_AQ + 512] * (DSA_SCALE * LOG2E)).T.astype(BF16)

        @pl.when(j == 2)
        def _():
            o = 2 * IN_TN
            iqt_ref[0] = z[:, C_IQ - o:C_IQ - o + 256].T.astype(BF16)
            ikw = z[:, C_IKW - o:C_IKW - o + 128]
            ikb_ref[...] = ikw.astype(BF16)
            iwt_ref[0] = ikw.T[64:72, :] * IDX_SCALE


def _k_in(x, sc, sh, g, w_bf, *, bb, tt, emit, kc):
    B, T, D = x.shape
    nt = T // tt
    tm = bb * tt
    n_rows = (B // bb) * nt
    N = B * T
    xmap = lambda i, j: (i // nt, i % nt, 0)
    mmap = lambda i, j: (i // nt, 0, 0)
    in_specs = [pl.BlockSpec((bb, tt, D), xmap),
                pl.BlockSpec((bb, 1, D), mmap),
                pl.BlockSpec((bb, 1, D), mmap),
                _const_spec((1, 1, D)),
                pl.BlockSpec((D, IN_TN), lambda i, j: (0, j))]
    if emit:
        assert bb == 1
        out_specs = [pl.BlockSpec((tm, IN_TN), lambda i, j: (i, jnp.maximum(j - 1, 0)))]
        out_shape = [jax.ShapeDtypeStruct((N, ZW - Z_OFF), F32)]
        out_specs += [
            pl.BlockSpec((tm, 512), lambda i, j: (i, 0)),
            pl.BlockSpec((tm, 512), lambda i, j: (i, 0)),
            pl.BlockSpec((tm, 512), lambda i, j: (i, 0)),
            pl.BlockSpec((1, tt // kc, DSA_HEADS * DSA_VR, kc), lambda i, j: (i // nt, i % nt, 0, 0)),
            pl.BlockSpec((1, 512, tt), lambda i, j: (i // nt, 0, i % nt)),
            pl.BlockSpec((1, 256, tt), lambda i, j: (i // nt, 0, i % nt)),
            pl.BlockSpec((1, 8, tt), lambda i, j: (i // nt, 0, i % nt)),
            pl.BlockSpec((tm, 128), lambda i, j: (i, 0)),
        ]
        out_shape += [
            jax.ShapeDtypeStruct((N, 512), F32),
            jax.ShapeDtypeStruct((N, 512), F32),
            jax.ShapeDtypeStruct((N, 512), BF16),
            jax.ShapeDtypeStruct((B, T // kc, DSA_HEADS * DSA_VR, kc), BF16),
            jax.ShapeDtypeStruct((B, 512, T), BF16),
            jax.ShapeDtypeStruct((B, 256, T), BF16),
            jax.ShapeDtypeStruct((B, 8, T), F32),
            jax.ShapeDtypeStruct((N, 128), BF16),
        ]
    else:
        out_specs = [pl.BlockSpec((tm, IN_TN), lambda i, j: (i, j))]
        out_shape = [jax.ShapeDtypeStruct((N, ZW), F32)]
    return pl.pallas_call(
        functools.partial(_in_body, bb=bb, tt=tt, emit=emit, kc=kc),
        grid=(n_rows, ZW // IN_TN),
        in_specs=in_specs,
        out_specs=out_specs,
        out_shape=out_shape,
        scratch_shapes=[pltpu.VMEM((tm, D), BF16)],
        compiler_params=_cparams(("arbitrary", "arbitrary")),
        name="in_proj",
    )(x, sc, sh, g, w_bf)


def _pc_body(pu_ref, cb_ref, cc_ref, ch_ref, pup_ref, ccp_ref, chp_ref, ph_ref, cvh_ref,
             pw_ref, ps_ref, cw_ref, ob_ref, oc_ref, pho_ref, cho_ref, u_scr, g_scr, *, bb, tt, P):
    t = pl.program_id(1)
    first = t == 0
    pu = pu_ref[...]
    g = cc_ref[...] * ch_ref[...]
    u_scr[:, 0:16, :] = jnp.where(first, ph_ref[...], pup_ref[...])
    u_scr[:, 16:16 + tt, :] = pu
    g_scr[:, 0:16, :] = jnp.where(first, cvh_ref[...], ccp_ref[...] * chp_ref[...])
    g_scr[:, 16:16 + tt, :] = g

    pos = P + t * tt + lax.broadcasted_iota(I32, (1, tt, 1), 1)
    for gi, win in enumerate(POOL_WINDOWS):
        lo = gi * POOL_GROUP
        acc = pu[:, :, lo:lo + POOL_GROUP]
        for j in range(1, win):
            acc = acc + u_scr[:, 16 - j:16 - j + tt, lo:lo + POOL_GROUP]
        cnt = jnp.minimum(pos + 1, win).astype(F32)
        d = acc / cnt - pu[:, :, lo:lo + POOL_GROUP]
        y = jnp.dot(d.reshape(bb * tt, POOL_GROUP).astype(BF16), pw_ref[gi], preferred_element_type=F32)
        ob_ref[:, :, lo:lo + POOL_GROUP] = (y * ps_ref[:, lo:lo + POOL_GROUP]).reshape(bb, tt, POOL_GROUP)

    cw = cw_ref[...]
    conv = (cw[0:1, :] * g_scr[:, 14:14 + tt, :] + cw[1:2, :] * g_scr[:, 15:15 + tt, :] + cw[2:3, :] * g)
    oc_ref[...] = cb_ref[...] * conv
    pho_ref[...] = u_scr[:, tt:tt + 16, :]
    cho_ref[...] = g_scr[:, tt:tt + 16, :]


def _k_pc(z3, ph16, cvh16, pw_bf, ps, cw, *, bb, tt, P, zoff):
    B, T, _ = z3.shape
    nt = T // tt
    r = tt // 16

    def cur(c):
        return pl.BlockSpec((bb, tt, 512), lambda b, t: (b, t, (c - zoff) // 512))

    def prev(c):
        return pl.BlockSpec((bb, 16, 512), lambda b, t: (b, jnp.maximum(t * r - 1, 0), (c - zoff) // 512))

    hist = pl.BlockSpec((bb, 16, 512), lambda b, t: (b, 0, 0))
    return pl.pallas_call(
        functools.partial(_pc_body, bb=bb, tt=tt, P=P),
        grid=(B // bb, nt),
        in_specs=[cur(C_PU), cur(C_CB), cur(C_CC), cur(C_CH), prev(C_PU), prev(C_CC), prev(C_CH),
                  hist, hist, _const_spec((4, POOL_GROUP, POOL_GROUP)), _const_spec((1, 512)),
                  _const_spec((8, 512))],
        out_specs=[pl.BlockSpec((bb, tt, 512), lambda b, t: (b, t, 0)),
                   pl.BlockSpec((bb, tt, 512), lambda b, t: (b, t, 0)), hist, hist],
        out_shape=[jax.ShapeDtypeStruct((B, T, 512), F32), jax.ShapeDtypeStruct((B, T, 512), F32),
                   jax.ShapeDtypeStruct((B, 16, 512), F32), jax.ShapeDtypeStruct((B, 16, 512), F32)],
        scratch_shapes=[pltpu.VMEM((bb, tt + 16, 512), F32), pltpu.VMEM((bb, tt + 16, 512), F32)],
        compiler_params=_cparams(("arbitrary", "arbitrary")),
        name="pool_conv",
    )(z3, z3, z3, z3, z3, z3, z3, ph16, cvh16, pw_bf, ps, cw)


def _mp_body(mq_ref, mkv_ref, mkr_ref, cos_ref, sin_ref, gq_ref, gkv_ref, wuq_ref, wukt_ref,
             qlt_ref, qrt_ref, ckv_ref, ckvb_ref, ckvt_ref, kr_ref, krb_ref):
    a = _rms(mq_ref[...], gq_ref[...]).astype(BF16)
    cq = jnp.dot(a, wuq_ref[...], preferred_element_type=F32)
    cqt = cq.T
    cos = cos_ref[...]
    sin = sin_ref[...]
    c = MLA_SCALE * LOG2E
    qrt_ref[...] = ((cqt[512:768] * cos.T + cqt[768:1024] * sin.T) * c).astype(BF16)
    for h in range(MLA_HEADS):
        qnt = cqt[h * MLA_NOPE:(h + 1) * MLA_NOPE].astype(BF16)
        qlt_ref[h * MLA_KV_LORA:(h + 1) * MLA_KV_LORA, :] = (jnp.dot(
            wukt_ref[h], qnt, preferred_element_type=F32) * c).astype(BF16)
    ckv = _rms(mkv_ref[...], gkv_ref[...])
    ckv_ref[...] = ckv
    ckvb_ref[...] = ckv.astype(BF16)
    ckvt_ref[0, 0:MLA_KV_LORA, :] = ckv.T.astype(BF16)
    ckvt_ref[0, MLA_KV_LORA:MLA_VR, :] = jnp.ones((MLA_VR - MLA_KV_LORA, ckv.shape[0]), BF16)
    mkr = mkr_ref[...]
    kr = mkr[:, 0:32] * cos[:, 0:32] + mkr[:, 32:64] * sin[:, 0:32]
    kr_ref[...] = kr
    krb_ref[...] = kr.astype(BF16)


def _k_mlaprep(z, cos_t, sin_t, gq, gkv, wuq_bf, wukt_bf, *, tm, zoff):
    N = z.shape[0]
    nr = cos_t.shape[0] // tm
    tab = pl.BlockSpec((tm, 256), lambda i: (i % nr, 0))
    row = lambda w: pl.BlockSpec((tm, w), lambda i: (i, 0))
    col = lambda h: pl.BlockSpec((h, tm), lambda i: (0, i))
    return pl.pallas_call(
        _mp_body,
        grid=(N // tm,),
        in_specs=[pl.BlockSpec((tm, 256), lambda i: (i, (C_MQ - zoff) // 256)),
                  pl.BlockSpec((tm, 256), lambda i: (i, (C_MKV - zoff) // 256)),
                  pl.BlockSpec((tm, 128), lambda i: (i, (C_MKR - zoff) // 128)),
                  tab, tab, _const_spec((1, 256)), _const_spec((1, 256)),
                  _const_spec((256, 1024)), _const_spec((MLA_HEADS, MLA_KV_LORA, MLA_NOPE))],
        out_specs=[col(2048), col(256), row(256), row(256),
                   pl.BlockSpec((1, MLA_VR, tm), lambda i: (i, 0, 0)), row(32), row(32)],
        out_shape=[jax.ShapeDtypeStruct((2048, N), BF16), jax.ShapeDtypeStruct((256, N), BF16),
                   jax.ShapeDtypeStruct((N, 256), F32), jax.ShapeDtypeStruct((N, 256), BF16),
                   jax.ShapeDtypeStruct((N // tm, MLA_VR, tm), BF16),
                   jax.ShapeDtypeStruct((N, 32), F32), jax.ShapeDtypeStruct((N, 32), BF16)],
        compiler_params=_cparams(("arbitrary",)),
        name="mla_prep",
    )(z, z, z, cos_t, sin_t, gq, gkv, wuq_bf, wukt_bf)


def _dot_nt(a, b):
    return lax.dot_general(a, b, (((1,), (1,)), ((), ())), preferred_element_type=F32)


MLA_SB = 128
MLA_CB = 256


def _mla_body(qlt_ref, qrt_ref, ckv_ref, kr_ref, ckvt_ref, wuvt_ref, o_ref,
              qt_scr, rt_scr, s_scr, pt_scr, m_scr, acc_scr,
              *, tq, P, L, KC, nkc_total, stacked):
    qi = pl.program_id(1)
    q0 = P + qi * tq
    cols_n = MLA_HEADS * tq
    if stacked:
        qt_scr[...] = qlt_ref[0]
        rt_scr[...] = qrt_ref[0]
    else:
        for h in range(MLA_HEADS):
            qt_scr[:, h * tq:(h + 1) * tq] = qlt_ref[h * 256:(h + 1) * 256, :]
            rt_scr[:, h * tq:(h + 1) * tq] = qrt_ref[h * 32:(h + 1) * 32, :]
    max_cend = (((q0 + tq - 1) >> 6) + 1) << 6
    nkc = jnp.minimum((max_cend + KC - 1) // KC, nkc_total)
    n_full = jnp.minimum(((q0 >> 6) + 1) << 6, L) // KC
    cb_w = min(MLA_CB, cols_n)

    m_scr[...] = jnp.full((1, cols_n), M_INIT, F32)
    acc_scr[...] = jnp.zeros((MLA_VR, cols_n), F32)

    def make_step(masked):
        def step(kc, carry):
            k0 = pl.multiple_of(kc * KC, KC)
            ck = ckv_ref[0, pl.ds(k0, KC), :]
            kr = kr_ref[0, pl.ds(k0, KC), :]
            s_scr[...] = (jnp.dot(ck, qt_scr[...], preferred_element_type=F32)
                          + jnp.dot(kr, rt_scr[...], preferred_element_type=F32))
            for cb in range(cols_n // cb_w):
                cols = slice(cb * cb_w, (cb + 1) * cb_w)
                if masked:
                    lane = cb * cb_w + lax.broadcasted_iota(I32, (1, cb_w), 1)
                    qpos = q0 + (lane & (tq - 1))
                    lim = jnp.minimum(((qpos >> 6) + 1) << 6, L)
                mx = jnp.full((8, cb_w), -jnp.inf, F32)
                for sb in range(KC // MLA_SB):
                    rs = slice(sb * MLA_SB, (sb + 1) * MLA_SB)
                    x = s_scr[rs, cols]
                    if masked:
                        kpos = k0 + sb * MLA_SB + lax.broadcasted_iota(I32, (MLA_SB, 1), 0)
                        x = jnp.where(kpos < lim, x, NEG)
                        s_scr[rs, cols] = x
                    mx = jnp.maximum(mx, _max8(x))
                m_old = m_scr[:, cols]
                m_new = jnp.maximum(m_old, jnp.max(mx, axis=0, keepdims=True))
                for sb in range(KC // MLA_SB):
                    rs = slice(sb * MLA_SB, (sb + 1) * MLA_SB)
                    pt_scr[rs, cols] = jnp.exp2(s_scr[rs, cols] - m_new).astype(BF16)
                m_scr[:, cols] = m_new
                acc_scr[:, cols] = (jnp.exp2(m_old - m_new) * acc_scr[:, cols]
                                    + jnp.dot(ckvt_ref[0, kc], pt_scr[:, cols], preferred_element_type=F32))
            return carry
        return step

    lax.fori_loop(0, n_full, make_step(False), 0)
    lax.fori_loop(n_full, nkc, make_step(True), 0)
    ot = (acc_scr[0:MLA_KV_LORA, :] / acc_scr[MLA_KV_LORA:MLA_KV_LORA + 1, :]).astype(BF16)
    if stacked:
        for h in range(MLA_HEADS):
            o_ref[0, h * MLA_V:(h + 1) * MLA_V, :] = jnp.dot(wuvt_ref[h], ot, preferred_element_type=F32)
    else:
        outs = [jnp.dot(wuvt_ref[h], ot[:, h * tq:(h + 1) * tq], preferred_element_type=F32)
                for h in range(MLA_HEADS)]
        o_ref[0] = jnp.concatenate(outs, axis=0).T


def _k_mla(qlt, qrt, ckvb, krb, ckvt, wuvt_bf, *, B, T, tq, P, L, KC, stacked):
    Lp = ckvb.shape[1]
    nkc_total = Lp // KC
    cols_n = MLA_HEADS * tq
    nq = T // tq
    if stacked:
        q_specs = [pl.BlockSpec((1, MLA_KV_LORA, cols_n), lambda b, q: (b, 0, 0)),
                   pl.BlockSpec((1, MLA_ROPE, cols_n), lambda b, q: (b, 0, 0))]
        out_spec = pl.BlockSpec((1, MLA_HEADS * MLA_V, cols_n), lambda b, q: (b, 0, 0))
        out_shape = jax.ShapeDtypeStruct((B, MLA_HEADS * MLA_V, cols_n), F32)
    else:
        q_specs = [pl.BlockSpec((MLA_HEADS * MLA_KV_LORA, tq), lambda b, q: (0, b * nq + q)),
                   pl.BlockSpec((MLA_HEADS * MLA_ROPE, tq), lambda b, q: (0, b * nq + q))]
        out_spec = pl.BlockSpec((1, tq, 512), lambda b, q: (b, q, 0))
        out_shape = jax.ShapeDtypeStruct((B, T, 512), F32)
    return pl.pallas_call(
        functools.partial(_mla_body, tq=tq, P=P, L=L, KC=KC, nkc_total=nkc_total, stacked=stacked),
        grid=(B, nq),
        in_specs=q_specs + [pl.BlockSpec((1, Lp, 256), lambda b, q: (b, 0, 0)),
                            pl.BlockSpec((1, Lp, 32), lambda b, q: (b, 0, 0)),
                            pl.BlockSpec((1, nkc_total, MLA_VR, KC), lambda b, q: (b, 0, 0, 0)),
                            _const_spec((MLA_HEADS, MLA_V, MLA_KV_LORA))],
        out_specs=out_spec,
        out_shape=out_shape,
        scratch_shapes=[pltpu.VMEM((MLA_KV_LORA, cols_n), BF16), pltpu.VMEM((MLA_ROPE, cols_n), BF16),
                        pltpu.VMEM((KC, cols_n), F32), pltpu.VMEM((KC, cols_n), BF16),
                        pltpu.VMEM((1, cols_n), F32), pltpu.VMEM((MLA_VR, cols_n), F32)],
        compiler_params=_cparams(("arbitrary", "arbitrary")),
        name="mla_attn",
    )(qlt, qrt, ckvb, krb, ckvt, wuvt_bf)


def _fold(x, r):
    n, w = x.shape
    return jnp.sum(x.reshape(n // r, r, w), axis=0)


def _fold8(x):
    return _fold(x, 8)


def _max8(x):
    n, w = x.shape
    return jnp.max(x.reshape(n // 8, 8, w), axis=0)


DSA_SB_ELEMS = 128 * 256


def _dsa_body(qt_ref, iqt_ref, iwt_ref, kb_ref, vt_ref, ikb_ref, o_ref,
              s_scr, w_scr, lg_scr, pt_scr, m_scr, acc_scr, *, tq, P, L, KC, nkc_total, topk):
    qi = pl.program_id(1)
    q0 = P + qi * tq
    sbk = min(KC, DSA_SB_ELEMS // tq)
    acc_r = sbk // 2
    qpos = q0 + lax.broadcasted_iota(I32, (1, tq), 1)
    lim = jnp.minimum(((qpos >> 6) + 1) << 6, L)
    keff = jnp.minimum(lim, topk).astype(F32)
    max_cend = (((q0 + tq - 1) >> 6) + 1) << 6
    nkc = jnp.minimum((max_cend + KC - 1) // KC, nkc_total)

    iqt = iqt_ref[0]
    iwt = iwt_ref[0]
    zpad = jnp.zeros((IDX_DIM, tq), BF16)
    iq_h = [jnp.concatenate([iqt[h * IDX_DIM:(h + 1) * IDX_DIM], zpad], axis=0) for h in range(IDX_HEADS)]

    def score_step(kc, carry):
        k0 = pl.multiple_of(kc * KC, KC)
        ik = ikb_ref[0, pl.ds(k0, KC), :]
        for h in range(IDX_HEADS):
            lg_scr[h, :, 0:tq] = jnp.dot(ik, iq_h[h], preferred_element_type=F32)
        for sb in range(KC // sbk):
            rs = slice(sb * sbk, (sb + 1) * sbk)
            s = jnp.zeros((sbk, tq), F32)
            for h in range(IDX_HEADS):
                s = s + iwt[h:h + 1, :] * jnp.maximum(lg_scr[h, rs, 0:tq], 0.0)
            kpos = k0 + sb * sbk + lax.broadcasted_iota(I32, (sbk, 1), 0)
            s = jnp.where(kpos < lim, s, -jnp.inf)
            b = lax.bitcast_convert_type(s, I32)
            b = jnp.where(b == INT_MIN, 0, b)
            s_scr[kc, rs, :] = b ^ ((b >> 31) & np.int32(0x7FFFFFFF))
        return carry

    lax.fori_loop(0, nkc, score_step, 0)

    def bit_step(i, thr):
        cand = thr + jnp.left_shift(jnp.int32(1), 31 - i)

        def cnt_step(kc, acc):
            return acc + _fold(jnp.where(s_scr[kc] >= cand, 1.0, 0.0), acc_r)

        cnt = jnp.sum(lax.fori_loop(0, nkc, cnt_step, jnp.zeros((acc_r, tq), F32)), axis=0, keepdims=True)
        return jnp.where(cnt >= keff, cand, thr)

    thr = lax.fori_loop(0, 32, bit_step, jnp.full((1, tq), INT_MIN, I32))

    def tie_cnt_step(kc, carry):
        key = s_scr[kc]
        return (carry[0] + _fold8(jnp.where(key > thr, 1.0, 0.0)),
                carry[1] + _fold8(jnp.where(key == thr, 1.0, 0.0)))

    z8 = jnp.zeros((8, tq), F32)
    n_gt, n_eq = lax.fori_loop(0, nkc, tie_cnt_step, (z8, z8))
    need = keff - jnp.sum(n_gt, axis=0, keepdims=True)
    surplus = jnp.max(jnp.sum(n_eq, axis=0, keepdims=True) - need)

    @pl.when(surplus < 0.5)
    def _():
        def fast_step(kc, carry):
            s_scr[kc] = lax.bitcast_convert_type(jnp.where(s_scr[kc] >= thr, 0.0, NEG), I32)
            return carry
        lax.fori_loop(0, nkc, fast_step, 0)

    @pl.when(surplus >= 0.5)
    def _():
        tri = jnp.where(lax.broadcasted_iota(I32, (KC, KC), 1) < lax.broadcasted_iota(I32, (KC, KC), 0),
                        1.0, 0.0).astype(BF16)

        def slow_step(kc, seen):
            key = s_scr[kc]
            eq = key == thr
            eqf = jnp.where(eq, 1.0, 0.0)
            rank = seen + jnp.dot(tri, eqf.astype(BF16), preferred_element_type=F32)
            bias = jnp.where(key > thr, 0.0, jnp.where(eq, jnp.where(rank < need, 0.0, NEG), NEG))
            s_scr[kc] = lax.bitcast_convert_type(bias, I32)
            return seen + jnp.sum(_fold8(eqf), axis=0, keepdims=True)
        lax.fori_loop(0, nkc, slow_step, jnp.zeros((1, tq), F32))

    hd = DSA_HEAD_DIM
    zq = jnp.zeros((hd, tq), BF16)
    for hp in range(DSA_HEADS // 2):
        r0 = hp * 2 * hd
        w_scr[hp, 0:hd, 0:tq] = qt_ref[0, r0:r0 + hd, :]
        w_scr[hp, 0:hd, tq:2 * tq] = zq
        w_scr[hp, hd:2 * hd, 0:tq] = zq
        w_scr[hp, hd:2 * hd, tq:2 * tq] = qt_ref[0, r0 + hd:r0 + 2 * hd, :]
    m_scr[...] = jnp.full((DSA_HEADS, tq), M_INIT, F32)
    acc_scr[...] = jnp.zeros((DSA_HEADS * DSA_VR, tq), F32)
    nsb = KC // sbk

    nhp = DSA_HEADS // 2

    def att_step(kc, carry):
        k0 = pl.multiple_of(kc * KC, KC)
        for hp in range(nhp):
            r0 = hp * 2 * hd
            lg_scr[hp] = jnp.dot(kb_ref[0, pl.ds(k0, KC), r0:r0 + 2 * hd], w_scr[hp],
                                 preferred_element_type=F32)
        for hp in range(nhp):
            li = hp
            for e in range(2):
                h = 2 * hp + e
                cols = slice(e * tq, (e + 1) * tq)
                mx = jnp.full((8, tq), -jnp.inf, F32)
                for sb in range(nsb):
                    rs = slice(sb * sbk, (sb + 1) * sbk)
                    x = lg_scr[li, rs, cols] + lax.bitcast_convert_type(s_scr[kc, rs, :], F32)
                    lg_scr[li, rs, cols] = x
                    mx = jnp.maximum(mx, _max8(x))
                m_old = m_scr[h:h + 1, :]
                m_new = jnp.maximum(m_old, jnp.max(mx, axis=0, keepdims=True))
                alpha = jnp.exp2(m_old - m_new)
                for sb in range(nsb):
                    rs = slice(sb * sbk, (sb + 1) * sbk)
                    pt_scr[h, rs, :] = jnp.exp2(lg_scr[li, rs, cols] - m_new).astype(BF16)
                m_scr[h:h + 1, :] = m_new
                hs = slice(h * DSA_VR, (h + 1) * DSA_VR)
                acc_scr[hs, :] = alpha * acc_scr[hs, :] + jnp.dot(vt_ref[0, kc, hs, :], pt_scr[h],
                                                                  preferred_element_type=F32)
        return carry

    lax.fori_loop(0, nkc, att_step, 0)
    outs = [acc_scr[h * DSA_VR:h * DSA_VR + hd, :] / acc_scr[h * DSA_VR + hd:h * DSA_VR + hd + 1, :]
            for h in range(DSA_HEADS)]
    o_ref[0] = jnp.concatenate(outs, axis=0).T


def _k_dsa(qt, iqt, iwt, kb3, vt4, ikb3, *, tq, P, L, KC, topk):
    B, _, T = qt.shape
    Lp = kb3.shape[1]
    nkc_total = Lp // KC
    return pl.pallas_call(
        functools.partial(_dsa_body, tq=tq, P=P, L=L, KC=KC, nkc_total=nkc_total, topk=topk),
        grid=(B, T // tq),
        in_specs=[pl.BlockSpec((1, 512, tq), lambda b, q: (b, 0, q)),
                  pl.BlockSpec((1, 256, tq), lambda b, q: (b, 0, q)),
                  pl.BlockSpec((1, 8, tq), lambda b, q: (b, 0, q)),
                  pl.BlockSpec((1, Lp, 512), lambda b, q: (b, 0, 0)),
                  pl.BlockSpec((1, nkc_total, DSA_HEADS * DSA_VR, KC), lambda b, q: (b, 0, 0, 0)),
                  pl.BlockSpec((1, Lp, 128), lambda b, q: (b, 0, 0))],
        out_specs=pl.BlockSpec((1, tq, 512), lambda b, q: (b, q, 0)),
        out_shape=jax.ShapeDtypeStruct((B, T, 512), F32),
        scratch_shapes=[pltpu.VMEM((nkc_total, KC, tq), I32),
                        pltpu.VMEM((DSA_HEADS // 2, 2 * DSA_HEAD_DIM, 2 * tq), BF16),
                        pltpu.VMEM((DSA_HEADS // 2, KC, 2 * tq), F32),
                        pltpu.VMEM((DSA_HEADS, KC, tq), BF16),
                        pltpu.VMEM((DSA_HEADS, tq), F32),
                        pltpu.VMEM((DSA_HEADS * DSA_VR, tq), F32)],
        compiler_params=_cparams(("arbitrary", "arbitrary")),
        name="dsa_attn",
    )(qt, iqt, iwt, kb3, vt4, ikb3)


def _merge_body(x_ref, sc_ref, sh_ref, g1_ref, ng_ref, oa_ref, ob_ref, oc_ref, od_ref,
                wg_ref, bg_ref, wb_ref, wo_ref, o_ref, *, bb, tt):
    x = x_ref[...]
    tm = bb * tt
    h = (_rms(x, ng_ref[...]) * (1.0 + sc_ref[...]) + sh_ref[...]).reshape(tm, D_MODEL).astype(BF16)
    merged = jnp.zeros((tm, D_MODEL), F32)
    for i, oref in enumerate((oa_ref, ob_ref, oc_ref, od_ref)):
        gate = _sigmoid(jnp.dot(h, wg_ref[:, i * D_MODEL:(i + 1) * D_MODEL], preferred_element_type=F32)
                        + bg_ref[:, i * D_MODEL:(i + 1) * D_MODEL])
        br = jnp.dot(oref[...].astype(BF16), wb_ref[i * BRANCH_W:(i + 1) * BRANCH_W, :],
                     preferred_element_type=F32)
        merged = merged + gate * br
    y = jnp.dot(merged.astype(BF16), wo_ref[...], preferred_element_type=F32)
    o_ref[...] = x + g1_ref[...] * y.reshape(bb, tt, D_MODEL)


def _k_merge(x, sc, sh, g1, ng, oa, ob, oc, od, wg_bf, bg, wb_bf, wo_bf, *, bb, tt):
    B, T, D = x.shape
    nt = T // tt
    tm = bb * tt
    xs = pl.BlockSpec((bb, tt, D), lambda i: (i // nt, i % nt, 0))
    ms = pl.BlockSpec((bb, 1, D), lambda i: (i // nt, 0, 0))
    os_ = pl.BlockSpec((tm, 512), lambda i: (i, 0))
    return pl.pallas_call(
        functools.partial(_merge_body, bb=bb, tt=tt),
        grid=((B // bb) * nt,),
        in_specs=[xs, ms, ms, ms, _const_spec((1, 1, D)), os_, os_, os_, os_,
                  _const_spec((D, 4 * D)), _const_spec((1, 4 * D)), _const_spec((4 * BRANCH_W, D)),
                  _const_spec((D, D))],
        out_specs=xs,
        out_shape=jax.ShapeDtypeStruct((B, T, D), F32),
        compiler_params=_cparams(("arbitrary",)),
        name="merge_out",
    )(x, sc, sh, g1, ng, oa, ob, oc, od, wg_bf, bg, wb_bf, wo_bf)


FC = FFN_DIM // 2
MOE_RB = 256


def _swiglu_chunk(h, w1, w3, w2):
    a = jnp.dot(h, w1, preferred_element_type=F32)
    b = jnp.dot(h, w3, preferred_element_type=F32)
    act = (a * _sigmoid(a) * b).astype(BF16)
    return jnp.dot(act, w2, preferred_element_type=F32)


def _finish(x, g2, f, fg_ref, o_ref, bb, tt):
    y = x + g2 * f.reshape(bb, tt, D_MODEL)
    if fg_ref is not None:
        y = _rms(y, fg_ref[...])
    o_ref[...] = y


def _ffn_body(x_ref, sc_ref, sh_ref, g2_ref, ng_ref, w1_ref, w3_ref, w2_ref, *rest, bb, tt, final):
    fg_ref, o_ref = rest if final else (None, rest[0])
    x = x_ref[...]
    tm = bb * tt
    h = (_rms(x, ng_ref[...]) * (1.0 + sc_ref[...]) + sh_ref[...]).reshape(tm, D_MODEL).astype(BF16)
    f = jnp.zeros((tm, D_MODEL), F32)
    for ci in range(FFN_DIM // FC):
        f = f + _swiglu_chunk(h, w1_ref[:, ci * FC:(ci + 1) * FC], w3_ref[:, ci * FC:(ci + 1) * FC],
                              w2_ref[ci * FC:(ci + 1) * FC, :])
    _finish(x, g2_ref[...], f, fg_ref, o_ref, bb, tt)


def _k_ffn(x, sc, sh, g2, ng, w1_bf, w3_bf, w2_bf, fg, *, bb, tt):
    B, T, D = x.shape
    nt = T // tt
    final = fg is not None
    xs = pl.BlockSpec((bb, tt, D), lambda i: (i // nt, i % nt, 0))
    ms = pl.BlockSpec((bb, 1, D), lambda i: (i // nt, 0, 0))
    in_specs = [xs, ms, ms, ms, _const_spec((1, 1, D)), _const_spec((D, FFN_DIM)),
                _const_spec((D, FFN_DIM)), _const_spec((FFN_DIM, D))]
    args = [x, sc, sh, g2, ng, w1_bf, w3_bf, w2_bf]
    if final:
        in_specs.append(_const_spec((1, 1, D)))
        args.append(fg)
    return pl.pallas_call(
        functools.partial(_ffn_body, bb=bb, tt=tt, final=final),
        grid=((B // bb) * nt,),
        in_specs=in_specs,
        out_specs=xs,
        out_shape=jax.ShapeDtypeStruct((B, T, D), F32),
        compiler_params=_cparams(("arbitrary",)),
        name="ffn_dense",
    )(*args)


def _moe_body(x_ref, sc_ref, sh_ref, g2_ref, ng_ref, rw_ref, rb_ref, w1_ref, w3_ref, w2_ref, *rest,
              bb, tt, final):
    if final:
        fg_ref, o_ref, h_scr, gate_scr, rank_scr, rankt_scr, xg_scr, y_scr, acc_scr = rest
    else:
        fg_ref = None
        o_ref, h_scr, gate_scr, rank_scr, rankt_scr, xg_scr, y_scr, acc_scr = rest
    e = pl.program_id(1)
    ci = pl.program_id(2)
    last_c = FFN_DIM // FC - 1
    tm = bb * tt
    lane = lax.broadcasted_iota(I32, (tm, 128), 1)

    @pl.when((e == 0) & (ci == 0))
    def _():
        h = (_rms(x_ref[...], ng_ref[...]) * (1.0 + sc_ref[...]) + sh_ref[...]).reshape(tm, D_MODEL)
        h_scr[...] = h.astype(BF16)
        logits = jnp.dot(h, rw_ref[...], preferred_element_type=F32,
                         precision=lax.Precision.HIGHEST) + rb_ref[...]
        m1 = jnp.max(logits, axis=-1, keepdims=True)
        i1 = jnp.min(jnp.where(logits == m1, lane, 128), axis=-1, keepdims=True)
        rest_l = jnp.where(lane == i1, NEG, logits)
        m2 = jnp.max(rest_l, axis=-1, keepdims=True)
        i2 = jnp.min(jnp.where(rest_l == m2, lane, 128), axis=-1, keepdims=True)
        e2 = jnp.exp(m2 - m1)
        den = 1.0 + e2
        gate_scr[...] = jnp.where(lane == i1, 1.0 / den, jnp.where(lane == i2, e2 / den, 0.0))
        sel = jnp.where(lane == i1, 1.0, jnp.where(lane == i2, 1.0, 0.0))
        tri = jnp.where(lax.broadcasted_iota(I32, (tm, tm), 1) < lax.broadcasted_iota(I32, (tm, tm), 0),
                        1.0, 0.0).astype(BF16)
        rank = jnp.where(sel > 0.5, jnp.dot(tri, sel.astype(BF16), preferred_element_type=F32), -1.0)
        rank_scr[...] = rank
        rankt_scr[...] = rank.T
        acc_scr[...] = jnp.zeros((tm, D_MODEL), F32)

    is_e = lane == e
    gate_col = jnp.sum(jnp.where(is_e, gate_scr[...], 0.0), axis=-1, keepdims=True)
    rank_col = jnp.sum(jnp.where(is_e, rank_scr[...], 0.0), axis=-1, keepdims=True)
    rank_row = rankt_scr[pl.ds(e, 1), :]
    n_e = jnp.max(rank_row).astype(I32) + 1
    nblk = (n_e + MOE_RB - 1) // MOE_RB

    def blk_step(b, carry):
        base = (b * MOE_RB).astype(F32)

        @pl.when(ci == 0)
        def _():
            slot = base + lax.broadcasted_iota(I32, (MOE_RB, 1), 0).astype(F32)
            pick = jnp.where(rank_row == slot, 1.0, 0.0).astype(BF16)
            xg_scr[b] = jnp.dot(pick, h_scr[...], preferred_element_type=F32).astype(BF16)

        part = _swiglu_chunk(xg_scr[b], w1_ref[0], w3_ref[0], w2_ref[0])

        @pl.when(ci == 0)
        def _():
            y_scr[b] = part

        @pl.when((ci > 0) & (ci < last_c))
        def _():
            y_scr[b] += part

        @pl.when(ci == last_c)
        def _():
            y = y_scr[b] + part
            y_hi = y.astype(BF16)
            y_lo = (y - y_hi.astype(F32)).astype(BF16)
            slot = base + lax.broadcasted_iota(I32, (1, MOE_RB), 1).astype(F32)
            put = jnp.where(rank_col == slot, 1.0, 0.0).astype(BF16)
            acc_scr[...] += gate_col * (jnp.dot(put, y_hi, preferred_element_type=F32)
                                        + jnp.dot(put, y_lo, preferred_element_type=F32))
        return carry

    lax.fori_loop(0, nblk, blk_step, 0)

    @pl.when((e == N_EXPERTS - 1) & (ci == last_c))
    def _():
        _finish(x_ref[...], g2_ref[...], acc_scr[...], fg_ref, o_ref, bb, tt)


def _k_moe(x, sc, sh, g2, ng, rw_pad, rb_pad, w1_bf, w3_bf, w2_bf, fg, *, bb, tt):
    B, T, D = x.shape
    nt = T // tt
    tm = bb * tt
    final = fg is not None
    xs = pl.BlockSpec((bb, tt, D), lambda i, e, c: (i // nt, i % nt, 0))
    xin = pl.BlockSpec((bb, tt, D), lambda i, e, c: (i // nt, i % nt, 0), pipeline_mode=pl.Buffered(1))
    ms = pl.BlockSpec((bb, 1, D), lambda i, e, c: (i // nt, 0, 0))
    in_specs = [xin, ms, ms, ms, _const_spec((1, 1, D)), _const_spec((D, 128)), _const_spec((1, 128)),
                pl.BlockSpec((1, D, FC), lambda i, e, c: (e, 0, c)),
                pl.BlockSpec((1, D, FC), lambda i, e, c: (e, 0, c)),
                pl.BlockSpec((1, FC, D), lambda i, e, c: (e, c, 0))]
    args = [x, sc, sh, g2, ng, rw_pad, rb_pad, w1_bf, w3_bf, w2_bf]
    if final:
        in_specs.append(_const_spec((1, 1, D)))
        args.append(fg)
    return pl.pallas_call(
        functools.partial(_moe_body, bb=bb, tt=tt, final=final),
        grid=((B // bb) * nt, N_EXPERTS, FFN_DIM // FC),
        in_specs=in_specs,
        out_specs=xs,
        out_shape=jax.ShapeDtypeStruct((B, T, D), F32),
        scratch_shapes=[pltpu.VMEM((tm, D), BF16), pltpu.VMEM((tm, 128), F32), pltpu.VMEM((tm, 128), F32),
                        pltpu.VMEM((128, tm), F32), pltpu.VMEM((pl.cdiv(tm, MOE_RB), MOE_RB, D), BF16),
                        pltpu.VMEM((pl.cdiv(tm, MOE_RB), MOE_RB, D), F32), pltpu.VMEM((tm, D), F32)],
        compiler_params=_cparams(("arbitrary", "arbitrary", "arbitrary")),
        name="ffn_moe",
    )(*args)


def _cast_body(x_ref, o_ref):
    o_ref[...] = x_ref[...].astype(BF16)


def _cast_bf16(w, start=0, count=None):
    _, R, C = w.shape
    E = w.shape[0] if count is None else count
    rb = 256
    return pl.pallas_call(
        _cast_body,
        grid=(E, R // rb),
        in_specs=[pl.BlockSpec((1, rb, C), lambda e, r: (start + e, r, 0))],
        out_specs=pl.BlockSpec((1, rb, C), lambda e, r: (e, r, 0)),
        out_shape=jax.ShapeDtypeStruct((E, R, C), BF16),
        compiler_params=_cparams(("arbitrary", "arbitrary")),
        name="cast_bf16",
    )(w)


def _dsa_keys_body(ck_ref, cv_ref, ci_ref, nk_ref, nv_ref, ni_ref, kb_ref, vt_ref, ikb_ref, *, nc, kc_rows):
    cached = pl.program_id(1) < nc
    k = jnp.where(cached, ck_ref[0], nk_ref[0])
    v = jnp.where(cached, cv_ref[0], nv_ref[0])
    ik = jnp.where(cached, ci_ref[0], ni_ref[0])
    kb_ref[0] = k.astype(BF16)
    vt = v.T.astype(BF16)
    ones = jnp.ones((DSA_VR - DSA_HEAD_DIM, kc_rows), BF16)
    for h in range(DSA_HEADS):
        vt_ref[0, 0, h * DSA_VR:h * DSA_VR + DSA_HEAD_DIM, :] = vt[h * DSA_HEAD_DIM:(h + 1) * DSA_HEAD_DIM]
        vt_ref[0, 0, h * DSA_VR + DSA_HEAD_DIM:(h + 1) * DSA_VR, :] = ones
    ikb_ref[0, :, 0:IDX_DIM] = ik.astype(BF16)
    ikb_ref[0, :, IDX_DIM:128] = jnp.zeros((kc_rows, 128 - IDX_DIM), BF16)


def _k_dsa_keys(ck, cv, ci, nk, nv, ni, *, KC, row0):
    B = nk.shape[0]
    P = ck.shape[1]
    assert P % KC == 0
    nc = P // KC
    cblk = lambda w: pl.BlockSpec((1, KC, w), lambda b, c: (row0 + b, jnp.minimum(c, nc - 1), 0))
    nblk = lambda w: pl.BlockSpec((1, KC, w), lambda b, c: (b, 0, 0))
    return pl.pallas_call(
        functools.partial(_dsa_keys_body, nc=nc, kc_rows=KC),
        grid=(B, nc + 1),
        in_specs=[cblk(512), cblk(512), cblk(IDX_DIM), nblk(512), nblk(512), nblk(IDX_DIM)],
        out_specs=[pl.BlockSpec((1, KC, 512), lambda b, c: (b, c, 0)),
                   pl.BlockSpec((1, 1, DSA_HEADS * DSA_VR, KC), lambda b, c: (b, c, 0, 0)),
                   pl.BlockSpec((1, KC, 128), lambda b, c: (b, c, 0))],
        out_shape=[jax.ShapeDtypeStruct((B, P + KC, 512), BF16),
                   jax.ShapeDtypeStruct((B, nc + 1, DSA_HEADS * DSA_VR, KC), BF16),
                   jax.ShapeDtypeStruct((B, P + KC, 128), BF16)],
        compiler_params=_cparams(("arbitrary", "arbitrary")),
        name="dsa_keys",
    )(ck, cv, ci, nk, nv, ni)


def _mla_keys_body(cc_ref, cr_ref, nc_ref, nr_ref, cb_ref, ct_ref, rb_ref, *, nc, kc_rows):
    cached = pl.program_id(1) < nc
    ckv = jnp.where(cached, cc_ref[0], nc_ref[0])
    kr = jnp.where(cached, cr_ref[0], nr_ref[0])
    cb_ref[0] = ckv.astype(BF16)
    ct_ref[0, 0, 0:MLA_KV_LORA, :] = ckv.T.astype(BF16)
    ct_ref[0, 0, MLA_KV_LORA:MLA_VR, :] = jnp.ones((MLA_VR - MLA_KV_LORA, kc_rows), BF16)
    rb_ref[0] = kr.astype(BF16)


def _k_mla_keys(cc, cr, ncv, nr, *, KC, row0):
    B = ncv.shape[0]
    P = cc.shape[1]
    assert P % KC == 0
    nc = P // KC
    cblk = lambda w: pl.BlockSpec((1, KC, w), lambda b, c: (row0 + b, jnp.minimum(c, nc - 1), 0))
    nblk = lambda w: pl.BlockSpec((1, KC, w), lambda b, c: (b, 0, 0))
    return pl.pallas_call(
        functools.partial(_mla_keys_body, nc=nc, kc_rows=KC),
        grid=(B, nc + 1),
        in_specs=[cblk(MLA_KV_LORA), cblk(MLA_ROPE), nblk(MLA_KV_LORA), nblk(MLA_ROPE)],
        out_specs=[pl.BlockSpec((1, KC, MLA_KV_LORA), lambda b, c: (b, c, 0)),
                   pl.BlockSpec((1, 1, MLA_VR, KC), lambda b, c: (b, c, 0, 0)),
                   pl.BlockSpec((1, KC, MLA_ROPE), lambda b, c: (b, c, 0))],
        out_shape=[jax.ShapeDtypeStruct((B, P + KC, MLA_KV_LORA), BF16),
                   jax.ShapeDtypeStruct((B, nc + 1, MLA_VR, KC), BF16),
                   jax.ShapeDtypeStruct((B, P + KC, MLA_ROPE), BF16)],
        compiler_params=_cparams(("arbitrary", "arbitrary")),
        name="mla_keys",
    )(cc, cr, ncv, nr)


def _prep_w_in(w):
    aq, ak, av, iq, ik, iw, pu, cb, cc, ch, mq, mkv, mkr = jnp.split(w, IN_SPLIT_POINTS, axis=1)
    d = w.shape[0]
    mkr_sw = jnp.concatenate([mkr[:, 16:], mkr[:, :16]], axis=1)
    return jnp.concatenate([aq, ak, av, pu, cb, cc, ch, iq, mq, mkv, ik, iw, jnp.zeros((d, 60), w.dtype),
                            mkr, mkr_sw, jnp.zeros((d, 64), w.dtype)], axis=1).astype(BF16)


def _prep_w_uq(w):
    w3 = w.reshape(MLA_Q_LORA, MLA_HEADS, MLA_NOPE + MLA_ROPE)
    nope = w3[:, :, :MLA_NOPE].reshape(MLA_Q_LORA, MLA_HEADS * MLA_NOPE)
    rope = w3[:, :, MLA_NOPE:]
    rope_sw = jnp.concatenate([rope[:, :, 16:], rope[:, :, :16]], axis=-1)
    return jnp.concatenate([nope, rope.reshape(MLA_Q_LORA, -1), rope_sw.reshape(MLA_Q_LORA, -1)],
                           axis=1).astype(BF16)


def _rope_tables(pos):
    half = MLA_ROPE // 2
    inv = 1.0 / (ROPE_BASE ** (jnp.arange(half, dtype=F32) / half))
    ang = pos.astype(F32)[:, None] * inv[None, :]
    cos, sin = jnp.cos(ang), jnp.sin(ang)
    cos2 = jnp.tile(jnp.concatenate([cos, cos], axis=-1), (1, MLA_HEADS))
    sin2 = jnp.tile(jnp.concatenate([-sin, sin], axis=-1), (1, MLA_HEADS))
    return cos2, sin2


def _layer(x, mod, past, lw, fw, use_moe, final_g, cfg):
    B, T, D = x.shape
    bb, tt, KC, tq_dsa, tq_mla = cfg["bb"], cfg["tt"], cfg["KC"], cfg["tq_dsa"], cfg["tq_mla"]
    KCD = cfg["KCD"]
    N = B * T
    P = 0 if past is None else past["dsa_k"].shape[1]
    L = P + T
    sh1, sc1, g1, sh2, sc2, g2 = [m.reshape(B, 1, D) for m in jnp.split(mod, 6, axis=-1)]
    ng1 = lw["norm_mix_g"].reshape(1, 1, D)
    ng2 = lw["norm_ffn_g"].reshape(1, 1, D)

    emit = past is None
    res = _k_in(x, sc1, sh1, ng1, lw["w_in"], bb=bb, tt=tt, emit=emit, kc=KCD)
    z = res[0]
    zoff = Z_OFF if emit else 0
    z3 = z.reshape(B, T, ZW - zoff)
    if emit:
        k_new = res[1].reshape(B, T, DSA_HEADS, DSA_HEAD_DIM)
        v_new = res[2].reshape(B, T, DSA_HEADS, DSA_HEAD_DIM)
    else:
        k_new = z[:, C_AK:C_AK + 512].reshape(B, T, DSA_HEADS, DSA_HEAD_DIM)
        v_new = z[:, C_AV:C_AV + 512].reshape(B, T, DSA_HEADS, DSA_HEAD_DIM)
    ik_new = z3[:, :, C_IKW - zoff:C_IKW - zoff + IDX_DIM]

    if past is None:
        ph16 = jnp.zeros((B, 16, 512), F32)
        cvh16 = ph16
    else:
        ph16 = jnp.pad(past["pool"], ((0, 0), (1, 0), (0, 0)))
        cvh16 = jnp.pad(past["conv"], ((0, 0), (16 - (CONV_WIDTH - 1), 0), (0, 0)))
    o_b, o_c, ph_o, cvh_o = _k_pc(z3, ph16, cvh16, lw["pool_w"], lw["pool_scale"], lw["conv_w"],
                                  bb=bb, tt=tt, P=P, zoff=zoff)
    pool_hist = ph_o[:, 1:]
    conv_hist = cvh_o[:, 16 - (CONV_WIDTH - 1):]

    cos2, sin2 = _rope_tables(P + jnp.arange(T))
    if bb > 1:
        cos2, sin2 = jnp.tile(cos2, (bb, 1)), jnp.tile(sin2, (bb, 1))
    qlt, qrt, ckv_new, ckvb, ckvt, kr_new, krb = _k_mlaprep(
        z, cos2, sin2, lw["mla_q_norm_g"], lw["mla_kv_norm_g"], lw["mla_w_uq"], lw["mla_w_uk"], tm=bb * tt,
        zoff=zoff)

    topk = min(DSA_TOPK_MAX, L // 4)
    if past is None:
        kb3 = res[3].reshape(B, T, 512)
        vt4, qt, iqt, iwt = res[4], res[5], res[6], res[7]
        ikb3 = res[8].reshape(B, T, 128)
        ckvb3 = ckvb.reshape(B, T, 256)
        krb3 = krb.reshape(B, T, 32)
        Tq = T
    else:
        assert T <= KCD and T <= KC
        padn = lambda a, kc: jnp.pad(a, ((0, 0), (0, kc - T), (0, 0)))
        kb3, vt4, ikb3 = _k_dsa_keys(past["dsa_k"], past["dsa_v"], past["idx_k"],
                                     padn(k_new.reshape(B, T, 512), KCD), padn(v_new.reshape(B, T, 512), KCD),
                                     padn(ik_new, KCD), KC=KCD, row0=past["row0"])
        Tq = tq_dsa
        padq = ((0, 0), (0, 0), (0, Tq - T))
        qt = jnp.pad((z3[:, :, C_AQ:C_AQ + 512] * (DSA_SCALE * LOG2E)).transpose(0, 2, 1).astype(BF16), padq)
        iqt = jnp.pad(z3[:, :, C_IQ:C_IQ + 256].transpose(0, 2, 1).astype(BF16), padq)
        iwt = jnp.pad(z3[:, :, C_IKW + 64:C_IKW + 72].transpose(0, 2, 1) * IDX_SCALE, padq)
        ckvb3, ckvt4, krb3 = _k_mla_keys(past["ckv"], past["krope"], padn(ckv_new.reshape(B, T, 256), KC),
                                         padn(kr_new.reshape(B, T, 32), KC), KC=KC, row0=past["row0"])

    o_a = _k_dsa(qt, iqt, iwt, kb3, vt4, ikb3, tq=tq_dsa, P=P, L=L, KC=KCD, topk=topk)
    o_a = o_a[:, :T].reshape(N, 512)
    if past is None:
        ckvt4 = ckvt.reshape(B, T // KC, MLA_VR, KC)
        o_d = _k_mla(qlt, qrt, ckvb3, krb3, ckvt4, lw["mla_w_uv"], B=B, T=T, tq=tq_mla, P=P, L=L, KC=KC,
                     stacked=False).reshape(N, 512)
    else:
        qs = qlt.reshape(MLA_HEADS, 256, B, T).transpose(2, 1, 0, 3).reshape(B, 256, MLA_HEADS * T)
        rs = qrt.reshape(MLA_HEADS, 32, B, T).transpose(2, 1, 0, 3).reshape(B, 32, MLA_HEADS * T)
        o5 = _k_mla(qs, rs, ckvb3, krb3, ckvt4, lw["mla_w_uv"], B=B, T=T, tq=tq_mla, P=P, L=L, KC=KC,
                    stacked=True).reshape(B, MLA_HEADS, MLA_V, MLA_HEADS, T)
        o_d = jnp.stack([o5[:, h, :, h, :] for h in range(MLA_HEADS)], axis=1)
        o_d = o_d.transpose(0, 3, 1, 2).reshape(N, 512)

    x1 = _k_merge(x, sc1, sh1, g1, ng1, o_a, o_b.reshape(N, 512), o_c.reshape(N, 512), o_d,
                  lw["w_gate"], lw["b_gate"], lw["w_branch"], lw["w_out"], bb=bb, tt=tt)
    fg = None if final_g is None else final_g.reshape(1, 1, D)
    if use_moe:
        x2 = _k_moe(x1, sc2, sh2, g2, ng2, fw["rw"], fw["rb"], fw["w1"], fw["w3"], fw["w2"], fg,
                    bb=cfg["bb_moe"], tt=cfg["tt_moe"])
    else:
        x2 = _k_ffn(x1, sc2, sh2, g2, ng2, fw["w1"], fw["w3"], fw["w2"], fg, bb=bb, tt=tt)
    rows = (k_new, v_new, ik_new, ckv_new.reshape(B, T, 256), kr_new.reshape(B, T, 32), pool_hist, conv_hist)
    return x2, rows


def _prep_layer(l, ada_w, ada_b, norm_mix_g, norm_ffn_g, w_in, mla_q_norm_g, mla_kv_norm_g,
                mla_w_uq, mla_w_uk, mla_w_uv, pool_w, pool_scale, conv_w, w_gate, b_gate, w_branch, w_out):
    return dict(
        ada_w=_cast_bf16(ada_w, l, 1)[0], ada_b=ada_b[l].reshape(1, -1),
        norm_mix_g=norm_mix_g[l], norm_ffn_g=norm_ffn_g[l],
        w_in=_prep_w_in(w_in[l]),
        mla_q_norm_g=mla_q_norm_g[l].reshape(1, -1), mla_kv_norm_g=mla_kv_norm_g[l].reshape(1, -1),
        mla_w_uq=_prep_w_uq(mla_w_uq[l]),
        mla_w_uk=mla_w_uk[l].transpose(1, 0, 2).astype(BF16),
        mla_w_uv=mla_w_uv[l].transpose(1, 2, 0).astype(BF16),
        pool_w=pool_w[l].astype(BF16), pool_scale=pool_scale[l].reshape(1, -1),
        conv_w=jnp.pad(conv_w[l].reshape(CONV_WIDTH, BRANCH_W), ((0, 8 - CONV_WIDTH), (0, 0))),
        w_gate=_cast_bf16(w_gate, l, 1)[0], b_gate=b_gate[l].reshape(1, -1),
        w_branch=_cast_bf16(w_branch, l, 1)[0], w_out=_cast_bf16(w_out, l, 1)[0])


def _cfg_for(B, T, has_past):
    if has_past:
        return dict(bb=B, tt=T, KC=512, KCD=512, tq_dsa=128, tq_mla=T, bb_moe=B, tt_moe=T)
    t = min(512, T)
    return dict(bb=1, tt=t, KC=t, KCD=t, tq_dsa=t, tq_mla=min(128, T), bb_moe=1, tt_moe=min(1024, T))


def _run(x, c, pasts, lws, fws, final_norm_g):
    B, T, _ = x.shape
    cfg = _cfg_for(B, T, pasts is not None)
    outs = [[] for _ in range(7)]
    depth = len(lws)
    for l in range(depth):
        lw = lws[l]
        mod = _ada(c, lw["ada_w"], lw["ada_b"])
        x, rows = _layer(x, mod, None if pasts is None else pasts[l], lw, fws[l], l % 2 == 1,
                         final_norm_g if l == depth - 1 else None, cfg)
        for o, r in zip(outs, rows):
            o.append(r)
    return x, [jnp.stack(o) for o in outs]


def kernel(x_prompt, x_sample, c_prompt, c_sample, cache_dsa_k, cache_dsa_v, cache_dsa_idx_k, cache_mla_ckv, cache_mla_krope, state_pool, state_conv, ada_w, ada_b, norm_mix_g, norm_ffn_g, w_in, mla_q_norm_g, mla_kv_norm_g, mla_w_uq, mla_w_uk, mla_w_uv, pool_w, pool_scale, conv_w, w_gate, b_gate, w_branch, w_out, ffn_w1, ffn_w3, ffn_w2, moe_router_w, moe_router_b, moe_w1, moe_w3, moe_w2, final_norm_g):
    depth = ada_w.shape[0]
    lws = [_prep_layer(l, ada_w, ada_b, norm_mix_g, norm_ffn_g, w_in, mla_q_norm_g, mla_kv_norm_g,
                       mla_w_uq, mla_w_uk, mla_w_uv, pool_w, pool_scale, conv_w, w_gate, b_gate,
                       w_branch, w_out) for l in range(depth)]
    fws = []
    for l in range(depth):
        j = l // 2
        if l % 2 == 0:
            fws.append(dict(w1=_cast_bf16(ffn_w1, j, 1)[0], w3=_cast_bf16(ffn_w3, j, 1)[0],
                            w2=_cast_bf16(ffn_w2, j, 1)[0]))
        else:
            fws.append(dict(
                rw=jnp.pad(moe_router_w[j], ((0, 0), (0, 128 - N_EXPERTS))),
                rb=jnp.pad(moe_router_b[j].reshape(1, -1), ((0, 0), (0, 128 - N_EXPERTS)), constant_values=NEG),
                w1=_cast_bf16(moe_w1.reshape((-1,) + moe_w1.shape[2:]), j * N_EXPERTS, N_EXPERTS),
                w3=_cast_bf16(moe_w3.reshape((-1,) + moe_w3.shape[2:]), j * N_EXPERTS, N_EXPERTS),
                w2=_cast_bf16(moe_w2.reshape((-1,) + moe_w2.shape[2:]), j * N_EXPERTS, N_EXPERTS)))
    nb, plen = cache_dsa_k.shape[1], cache_dsa_k.shape[2]
    flat = lambda a: a.reshape(depth * nb, plen, -1)
    sample_pasts = [dict(dsa_k=flat(cache_dsa_k), dsa_v=flat(cache_dsa_v), idx_k=flat(cache_dsa_idx_k),
                         ckv=flat(cache_mla_ckv), krope=flat(cache_mla_krope), row0=l * nb,
                         pool=state_pool[l], conv=state_conv[l]) for l in range(depth)]
    y_prompt, pn = _run(x_prompt, c_prompt, None, lws, fws, final_norm_g)
    y_sample, sn = _run(x_sample, c_sample, sample_pasts, lws, fws, final_norm_g)
    return (y_prompt, y_sample, pn[0], pn[1], pn[2], pn[3], pn[4], pn[5], pn[6],
            sn[0], sn[1], sn[2], sn[3], sn[4], sn[5], sn[6])
```

```python
import functools

import numpy as np
import jax
import jax.numpy as jnp
from jax import lax
from jax.experimental import pallas as pl
from jax.experimental.pallas import tpu as pltpu

F32 = jnp.float32
BF16 = jnp.bfloat16
I32 = jnp.int32

D_MODEL = 1024
CHUNK = 64
N_BRANCH = 4
BRANCH_W = 512
DSA_HEADS = 8
DSA_HEAD_DIM = 64
IDX_HEADS = 4
IDX_DIM = 64
DSA_TOPK_MAX = 256
POOL_WINDOWS = (2, 4, 8, 16)
POOL_GROUP = BRANCH_W // 4
POOL_HIST = 15
CONV_WIDTH = 3
MLA_HEADS = 8
MLA_Q_LORA = 256
MLA_KV_LORA = 256
MLA_NOPE = 64
MLA_ROPE = 32
MLA_V = 64
ROPE_BASE = 10000.0
FFN_DIM = 2816
N_EXPERTS = 8
EPS = 1e-6

DSA_SCALE = DSA_HEAD_DIM ** -0.5
IDX_SCALE = (IDX_HEADS * IDX_DIM) ** -0.5
MLA_SCALE = (MLA_NOPE + MLA_ROPE) ** -0.5
LOG2E = 1.4426950408889634

IN_SPLIT_WIDTHS = (512, 512, 512, 256, 64, 4, 512, 512, 512, 512, 256, 256, 32)
IN_SPLIT_POINTS = tuple(int(sum(IN_SPLIT_WIDTHS[:i + 1])) for i in range(len(IN_SPLIT_WIDTHS) - 1))

C_AQ, C_AK, C_AV, C_PU, C_CB, C_CC, C_CH = 0, 512, 1024, 1536, 2048, 2560, 3072
C_IQ, C_MQ, C_MKV, C_IKW, C_MKR, ZW = 3584, 3840, 4096, 4352, 4480, 4608
IN_TN = 1536
Z_OFF = IN_TN
DSA_VR = DSA_HEAD_DIM + 16
MLA_VR = MLA_KV_LORA + 16

NEG = -1e30
M_INIT = -1e29
INT_MIN = np.int32(-2 ** 31)
V7X_VMEM_LIMIT = 56 * 1024 * 1024


def _cparams(sem):
    return pltpu.CompilerParams(dimension_semantics=sem, vmem_limit_bytes=V7X_VMEM_LIMIT)


def _rms(x, g):
    return x * lax.rsqrt(jnp.mean(x * x, axis=-1, keepdims=True) + EPS) * g


def _sigmoid(x):
    return 1.0 / (1.0 + jnp.exp(-x))


def _const_spec(shape):
    nd = len(shape)
    return pl.BlockSpec(shape, lambda *_: (0,) * nd, pipeline_mode=pl.Buffered(1))


def _ada_body(c_ref, w_ref, b_ref, o_ref):
    c = c_ref[...]
    s = (c * _sigmoid(c)).astype(BF16)
    o_ref[...] = jnp.dot(s, w_ref[...], preferred_element_type=F32) + b_ref[...]


def _ada(c, w_bf, b):
    B, D = c.shape
    n = w_bf.shape[1]
    tn = 1536
    return pl.pallas_call(
        _ada_body,
        grid=(n // tn,),
        in_specs=[pl.BlockSpec((B, D), lambda j: (0, 0)),
                  pl.BlockSpec((D, tn), lambda j: (0, j)),
                  pl.BlockSpec((1, tn), lambda j: (0, j))],
        out_specs=pl.BlockSpec((B, tn), lambda j: (0, j)),
        out_shape=jax.ShapeDtypeStruct((B, n), F32),
        compiler_params=_cparams(("arbitrary",)),
        name="ada_mod",
    )(c, w_bf, b)


def _in_body(x_ref, sc_ref, sh_ref, g_ref, w_ref, z_ref, *rest, bb, tt, emit, kc):
    if emit:
        kn_ref, vn_ref, kb_ref, vt_ref, qt_ref, iqt_ref, iwt_ref, ikb_ref, h_scr = rest
    else:
        (h_scr,) = rest
    j = pl.program_id(1)

    @pl.when(j == 0)
    def _():
        h = _rms(x_ref[...], g_ref[...]) * (1.0 + sc_ref[...]) + sh_ref[...]
        h_scr[...] = h.reshape(bb * tt, D_MODEL).astype(BF16)

    z = jnp.dot(h_scr[...], w_ref[...], preferred_element_type=F32)

    if not emit:
        z_ref[...] = z
    else:
        @pl.when(j > 0)
        def _():
            z_ref[...] = z

        @pl.when(j == 0)
        def _():
            k = z[:, C_AK:C_AK + 512]
            v = z[:, C_AV:C_AV + 512]
            kn_ref[...] = k
            vn_ref[...] = v
            kb_ref[...] = k.astype(BF16)
            vt = v.T.astype(BF16)
            ones = jnp.ones((DSA_VR - DSA_HEAD_DIM, kc), BF16)
            for c in range(tt // kc):
                for h in range(DSA_HEADS):
                    vt_ref[0, c, h * DSA_VR:h * DSA_VR + DSA_HEAD_DIM, :] = vt[
                        h * DSA_HEAD_DIM:(h + 1) * DSA_HEAD_DIM, c * kc:(c + 1) * kc]
                    vt_ref[0, c, h * DSA_VR + DSA_HEAD_DIM:(h + 1) * DSA_VR, :] = ones
            qt_ref[0] = (z[:, C_AQ:C_AQ + 512] * (DSA_SCALE * LOG2E)).T.astype(BF16)

        @pl.when(j == 2)
        def _():
            o = 2 * IN_TN
            iqt_ref[0] = z[:, C_IQ - o:C_IQ - o + 256].T.astype(BF16)
            ikw = z[:, C_IKW - o:C_IKW - o + 128]
            ikb_ref[...] = ikw.astype(BF16)
            iwt_ref[0] = ikw.T[64:72, :] * IDX_SCALE


def _k_in(x, sc, sh, g, w_bf, *, bb, tt, emit, kc):
    B, T, D = x.shape
    nt = T // tt
    tm = bb * tt
    n_rows = (B // bb) * nt
    N = B * T
    xmap = lambda i, j: (i // nt, i % nt, 0)
    mmap = lambda i, j: (i // nt, 0, 0)
    in_specs = [pl.BlockSpec((bb, tt, D), xmap),
                pl.BlockSpec((bb, 1, D), mmap),
                pl.BlockSpec((bb, 1, D), mmap),
                _const_spec((1, 1, D)),
                pl.BlockSpec((D, IN_TN), lambda i, j: (0, j))]
    if emit:
        assert bb == 1
        out_specs = [pl.BlockSpec((tm, IN_TN), lambda i, j: (i, jnp.maximum(j - 1, 0)))]
        out_shape = [jax.ShapeDtypeStruct((N, ZW - Z_OFF), F32)]
        out_specs += [
            pl.BlockSpec((tm, 512), lambda i, j: (i, 0)),
            pl.BlockSpec((tm, 512), lambda i, j: (i, 0)),
            pl.BlockSpec((tm, 512), lambda i, j: (i, 0)),
            pl.BlockSpec((1, tt // kc, DSA_HEADS * DSA_VR, kc), lambda i, j: (i // nt, i % nt, 0, 0)),
            pl.BlockSpec((1, 512, tt), lambda i, j: (i // nt, 0, i % nt)),
            pl.BlockSpec((1, 256, tt), lambda i, j: (i // nt, 0, i % nt)),
            pl.BlockSpec((1, 8, tt), lambda i, j: (i // nt, 0, i % nt)),
            pl.BlockSpec((tm, 128), lambda i, j: (i, 0)),
        ]
        out_shape += [
            jax.ShapeDtypeStruct((N, 512), F32),
            jax.ShapeDtypeStruct((N, 512), F32),
            jax.ShapeDtypeStruct((N, 512), BF16),
            jax.ShapeDtypeStruct((B, T // kc, DSA_HEADS * DSA_VR, kc), BF16),
            jax.ShapeDtypeStruct((B, 512, T), BF16),
            jax.ShapeDtypeStruct((B, 256, T), BF16),
            jax.ShapeDtypeStruct((B, 8, T), F32),
            jax.ShapeDtypeStruct((N, 128), BF16),
        ]
    else:
        out_specs = [pl.BlockSpec((tm, IN_TN), lambda i, j: (i, j))]
        out_shape = [jax.ShapeDtypeStruct((N, ZW), F32)]
    return pl.pallas_call(
        functools.partial(_in_body, bb=bb, tt=tt, emit=emit, kc=kc),
        grid=(n_rows, ZW // IN_TN),
        in_specs=in_specs,
        out_specs=out_specs,
        out_shape=out_shape,
        scratch_shapes=[pltpu.VMEM((tm, D), BF16)],
        compiler_params=_cparams(("arbitrary", "arbitrary")),
        name="in_proj",
    )(x, sc, sh, g, w_bf)


def _pc_body(pu_ref, cb_ref, cc_ref, ch_ref, pup_ref, ccp_ref, chp_ref, ph_ref, cvh_ref,
             pw_ref, ps_ref, cw_ref, ob_ref, oc_ref, pho_ref, cho_ref, u_scr, g_scr, *, bb, tt, P):
    t = pl.program_id(1)
    first = t == 0
    pu = pu_ref[...]
    g = cc_ref[...] * ch_ref[...]
    u_scr[:, 0:16, :] = jnp.where(first, ph_ref[...], pup_ref[...])
    u_scr[:, 16:16 + tt, :] = pu
    g_scr[:, 0:16, :] = jnp.where(first, cvh_ref[...], ccp_ref[...] * chp_ref[...])
    g_scr[:, 16:16 + tt, :] = g

    pos = P + t * tt + lax.broadcasted_iota(I32, (1, tt, 1), 1)
    for gi, win in enumerate(POOL_WINDOWS):
        lo = gi * POOL_GROUP
        acc = pu[:, :, lo:lo + POOL_GROUP]
        for j in range(1, win):
            acc = acc + u_scr[:, 16 - j:16 - j + tt, lo:lo + POOL_GROUP]
        cnt = jnp.minimum(pos + 1, win).astype(F32)
        d = acc / cnt - pu[:, :, lo:lo + POOL_GROUP]
        y = jnp.dot(d.reshape(bb * tt, POOL_GROUP).astype(BF16), pw_ref[gi], preferred_element_type=F32)
        ob_ref[:, :, lo:lo + POOL_GROUP] = (y * ps_ref[:, lo:lo + POOL_GROUP]).reshape(bb, tt, POOL_GROUP)

    cw = cw_ref[...]
    conv = (cw[0:1, :] * g_scr[:, 14:14 + tt, :] + cw[1:2, :] * g_scr[:, 15:15 + tt, :] + cw[2:3, :] * g)
    oc_ref[...] = cb_ref[...] * conv
    pho_ref[...] = u_scr[:, tt:tt + 16, :]
    cho_ref[...] = g_scr[:, tt:tt + 16, :]


def _k_pc(z3, ph16, cvh16, pw_bf, ps, cw, *, bb, tt, P, zoff):
    B, T, _ = z3.shape
    nt = T // tt
    r = tt // 16

    def cur(c):
        return pl.BlockSpec((bb, tt, 512), lambda b, t: (b, t, (c - zoff) // 512))

    def prev(c):
        return pl.BlockSpec((bb, 16, 512), lambda b, t: (b, jnp.maximum(t * r - 1, 0), (c - zoff) // 512))

    hist = pl.BlockSpec((bb, 16, 512), lambda b, t: (b, 0, 0))
    return pl.pallas_call(
        functools.partial(_pc_body, bb=bb, tt=tt, P=P),
        grid=(B // bb, nt),
        in_specs=[cur(C_PU), cur(C_CB), cur(C_CC), cur(C_CH), prev(C_PU), prev(C_CC), prev(C_CH),
                  hist, hist, _const_spec((4, POOL_GROUP, POOL_GROUP)), _const_spec((1, 512)),
                  _const_spec((8, 512))],
        out_specs=[pl.BlockSpec((bb, tt, 512), lambda b, t: (b, t, 0)),
                   pl.BlockSpec((bb, tt, 512), lambda b, t: (b, t, 0)), hist, hist],
        out_shape=[jax.ShapeDtypeStruct((B, T, 512), F32), jax.ShapeDtypeStruct((B, T, 512), F32),
                   jax.ShapeDtypeStruct((B, 16, 512), F32), jax.ShapeDtypeStruct((B, 16, 512), F32)],
        scratch_shapes=[pltpu.VMEM((bb, tt + 16, 512), F32), pltpu.VMEM((bb, tt + 16, 512), F32)],
        compiler_params=_cparams(("arbitrary", "arbitrary")),
        name="pool_conv",
    )(z3, z3, z3, z3, z3, z3, z3, ph16, cvh16, pw_bf, ps, cw)


def _mp_body(mq_ref, mkv_ref, mkr_ref, cos_ref, sin_ref, gq_ref, gkv_ref, wuq_ref, wukt_ref,
             qlt_ref, qrt_ref, ckv_ref, ckvb_ref, ckvt_ref, kr_ref, krb_ref):
    a = _rms(mq_ref[...], gq_ref[...]).astype(BF16)
    cq = jnp.dot(a, wuq_ref[...], preferred_element_type=F32)
    cqt = cq.T
    cos = cos_ref[...]
    sin = sin_ref[...]
    c = MLA_SCALE * LOG2E
    qrt_ref[...] = ((cqt[512:768] * cos.T + cqt[768:1024] * sin.T) * c).astype(BF16)
    for h in range(MLA_HEADS):
        qnt = cqt[h * MLA_NOPE:(h + 1) * MLA_NOPE].astype(BF16)
        qlt_ref[h * MLA_KV_LORA:(h + 1) * MLA_KV_LORA, :] = (jnp.dot(
            wukt_ref[h], qnt, preferred_element_type=F32) * c).astype(BF16)
    ckv = _rms(mkv_ref[...], gkv_ref[...])
    ckv_ref[...] = ckv
    ckvb_ref[...] = ckv.astype(BF16)
    ckvt_ref[0, 0:MLA_KV_LORA, :] = ckv.T.astype(BF16)
    ckvt_ref[0, MLA_KV_LORA:MLA_VR, :] = jnp.ones((MLA_VR - MLA_KV_LORA, ckv.shape[0]), BF16)
    mkr = mkr_ref[...]
    kr = mkr[:, 0:32] * cos[:, 0:32] + mkr[:, 32:64] * sin[:, 0:32]
    kr_ref[...] = kr
    krb_ref[...] = kr.astype(BF16)


def _k_mlaprep(z, cos_t, sin_t, gq, gkv, wuq_bf, wukt_bf, *, tm, zoff):
    N = z.shape[0]
    nr = cos_t.shape[0] // tm
    tab = pl.BlockSpec((tm, 256), lambda i: (i % nr, 0))
    row = lambda w: pl.BlockSpec((tm, w), lambda i: (i, 0))
    col = lambda h: pl.BlockSpec((h, tm), lambda i: (0, i))
    return pl.pallas_call(
        _mp_body,
        grid=(N // tm,),
        in_specs=[pl.BlockSpec((tm, 256), lambda i: (i, (C_MQ - zoff) // 256)),
                  pl.BlockSpec((tm, 256), lambda i: (i, (C_MKV - zoff) // 256)),
                  pl.BlockSpec((tm, 128), lambda i: (i, (C_MKR - zoff) // 128)),
                  tab, tab, _const_spec((1, 256)), _const_spec((1, 256)),
                  _const_spec((256, 1024)), _const_spec((MLA_HEADS, MLA_KV_LORA, MLA_NOPE))],
        out_specs=[col(2048), col(256), row(256), row(256),
                   pl.BlockSpec((1, MLA_VR, tm), lambda i: (i, 0, 0)), row(32), row(32)],
        out_shape=[jax.ShapeDtypeStruct((2048, N), BF16), jax.ShapeDtypeStruct((256, N), BF16),
                   jax.ShapeDtypeStruct((N, 256), F32), jax.ShapeDtypeStruct((N, 256), BF16),
                   jax.ShapeDtypeStruct((N // tm, MLA_VR, tm), BF16),
                   jax.ShapeDtypeStruct((N, 32), F32), jax.ShapeDtypeStruct((N, 32), BF16)],
        compiler_params=_cparams(("arbitrary",)),
        name="mla_prep",
    )(z, z, z, cos_t, sin_t, gq, gkv, wuq_bf, wukt_bf)


def _dot_nt(a, b):
    return lax.dot_general(a, b, (((1,), (1,)), ((), ())), preferred_element_type=F32)


MLA_SB = 128
MLA_CB = 256


def _mla_body(qlt_ref, qrt_ref, ckv_ref, kr_ref, ckvt_ref, wuvt_ref, o_ref,
              qt_scr, rt_scr, s_scr, pt_scr, m_scr, acc_scr,
              *, tq, P, L, KC, nkc_total, stacked):
    qi = pl.program_id(1)
    q0 = P + qi * tq
    cols_n = MLA_HEADS * tq
    if stacked:
        qt_scr[...] = qlt_ref[0]
        rt_scr[...] = qrt_ref[0]
    else:
        for h in range(MLA_HEADS):
            qt_scr[:, h * tq:(h + 1) * tq] = qlt_ref[h * 256:(h + 1) * 256, :]
            rt_scr[:, h * tq:(h + 1) * tq] = qrt_ref[h * 32:(h + 1) * 32, :]
    max_cend = (((q0 + tq - 1) >> 6) + 1) << 6
    nkc = jnp.minimum((max_cend + KC - 1) // KC, nkc_total)
    n_full = jnp.minimum(((q0 >> 6) + 1) << 6, L) // KC
    cb_w = min(MLA_CB, cols_n)

    m_scr[...] = jnp.full((1, cols_n), M_INIT, F32)
    acc_scr[...] = jnp.zeros((MLA_VR, cols_n), F32)

    def make_step(masked):
        def step(kc, carry):
            k0 = pl.multiple_of(kc * KC, KC)
            ck = ckv_ref[0, pl.ds(k0, KC), :]
            kr = kr_ref[0, pl.ds(k0, KC), :]
            s_scr[...] = (jnp.dot(ck, qt_scr[...], preferred_element_type=F32)
                          + jnp.dot(kr, rt_scr[...], preferred_element_type=F32))
            for cb in range(cols_n // cb_w):
                cols = slice(cb * cb_w, (cb + 1) * cb_w)
                if masked:
                    lane = cb * cb_w + lax.broadcasted_iota(I32, (1, cb_w), 1)
                    qpos = q0 + (lane & (tq - 1))
                    lim = jnp.minimum(((qpos >> 6) + 1) << 6, L)
                mx = jnp.full((8, cb_w), -jnp.inf, F32)
                for sb in range(KC // MLA_SB):
                    rs = slice(sb * MLA_SB, (sb + 1) * MLA_SB)
                    x = s_scr[rs, cols]
                    if masked:
                        kpos = k0 + sb * MLA_SB + lax.broadcasted_iota(I32, (MLA_SB, 1), 0)
                        x = jnp.where(kpos < lim, x, NEG)
                        s_scr[rs, cols] = x
                    mx = jnp.maximum(mx, _max8(x))
                m_old = m_scr[:, cols]
                m_new = jnp.maximum(m_old, jnp.max(mx, axis=0, keepdims=True))
                for sb in range(KC // MLA_SB):
                    rs = slice(sb * MLA_SB, (sb + 1) * MLA_SB)
                    pt_scr[rs, cols] = jnp.exp2(s_scr[rs, cols] - m_new).astype(BF16)
                m_scr[:, cols] = m_new
                acc_scr[:, cols] = (jnp.exp2(m_old - m_new) * acc_scr[:, cols]
                                    + jnp.dot(ckvt_ref[0, kc], pt_scr[:, cols], preferred_element_type=F32))
            return carry
        return step

    lax.fori_loop(0, n_full, make_step(False), 0)
    lax.fori_loop(n_full, nkc, make_step(True), 0)
    ot = (acc_scr[0:MLA_KV_LORA, :] / acc_scr[MLA_KV_LORA:MLA_KV_LORA + 1, :]).astype(BF16)
    if stacked:
        for h in range(MLA_HEADS):
            o_ref[0, h * MLA_V:(h + 1) * MLA_V, :] = jnp.dot(wuvt_ref[h], ot, preferred_element_type=F32)
    else:
        outs = [jnp.dot(wuvt_ref[h], ot[:, h * tq:(h + 1) * tq], preferred_element_type=F32)
                for h in range(MLA_HEADS)]
        o_ref[0] = jnp.concatenate(outs, axis=0).T


def _k_mla(qlt, qrt, ckvb, krb, ckvt, wuvt_bf, *, B, T, tq, P, L, KC, stacked):
    Lp = ckvb.shape[1]
    nkc_total = Lp // KC
    cols_n = MLA_HEADS * tq
    nq = T // tq
    if stacked:
        q_specs = [pl.BlockSpec((1, MLA_KV_LORA, cols_n), lambda b, q: (b, 0, 0)),
                   pl.BlockSpec((1, MLA_ROPE, cols_n), lambda b, q: (b, 0, 0))]
        out_spec = pl.BlockSpec((1, MLA_HEADS * MLA_V, cols_n), lambda b, q: (b, 0, 0))
        out_shape = jax.ShapeDtypeStruct((B, MLA_HEADS * MLA_V, cols_n), F32)
    else:
        q_specs = [pl.BlockSpec((MLA_HEADS * MLA_KV_LORA, tq), lambda b, q: (0, b * nq + q)),
                   pl.BlockSpec((MLA_HEADS * MLA_ROPE, tq), lambda b, q: (0, b * nq + q))]
        out_spec = pl.BlockSpec((1, tq, 512), lambda b, q: (b, q, 0))
        out_shape = jax.ShapeDtypeStruct((B, T, 512), F32)
    return pl.pallas_call(
        functools.partial(_mla_body, tq=tq, P=P, L=L, KC=KC, nkc_total=nkc_total, stacked=stacked),
        grid=(B, nq),
        in_specs=q_specs + [pl.BlockSpec((1, Lp, 256), lambda b, q: (b, 0, 0)),
                            pl.BlockSpec((1, Lp, 32), lambda b, q: (b, 0, 0)),
                            pl.BlockSpec((1, nkc_total, MLA_VR, KC), lambda b, q: (b, 0, 0, 0)),
                            _const_spec((MLA_HEADS, MLA_V, MLA_KV_LORA))],
        out_specs=out_spec,
        out_shape=out_shape,
        scratch_shapes=[pltpu.VMEM((MLA_KV_LORA, cols_n), BF16), pltpu.VMEM((MLA_ROPE, cols_n), BF16),
                        pltpu.VMEM((KC, cols_n), F32), pltpu.VMEM((KC, cols_n), BF16),
                        pltpu.VMEM((1, cols_n), F32), pltpu.VMEM((MLA_VR, cols_n), F32)],
        compiler_params=_cparams(("arbitrary", "arbitrary")),
        name="mla_attn",
    )(qlt, qrt, ckvb, krb, ckvt, wuvt_bf)


def _fold(x, r):
    n, w = x.shape
    return jnp.sum(x.reshape(n // r, r, w), axis=0)


def _fold8(x):
    return _fold(x, 8)


def _max8(x):
    n, w = x.shape
    return jnp.max(x.reshape(n // 8, 8, w), axis=0)


DSA_SB_ELEMS = 128 * 256


def _dsa_body(qt_ref, iqt_ref, iwt_ref, kb_ref, vt_ref, ikb_ref, o_ref,
              s_scr, w_scr, lg_scr, pt_scr, m_scr, acc_scr, *, tq, P, L, KC, nkc_total, topk):
    qi = pl.program_id(1)
    q0 = P + qi * tq
    sbk = min(KC, DSA_SB_ELEMS // tq)
    acc_r = sbk // 2
    qpos = q0 + lax.broadcasted_iota(I32, (1, tq), 1)
    lim = jnp.minimum(((qpos >> 6) + 1) << 6, L)
    keff = jnp.minimum(lim, topk).astype(F32)
    max_cend = (((q0 + tq - 1) >> 6) + 1) << 6
    nkc = jnp.minimum((max_cend + KC - 1) // KC, nkc_total)

    iqt = iqt_ref[0]
    iwt = iwt_ref[0]
    zpad = jnp.zeros((IDX_DIM, tq), BF16)
    iq_h = [jnp.concatenate([iqt[h * IDX_DIM:(h + 1) * IDX_DIM], zpad], axis=0) for h in range(IDX_HEADS)]

    def score_step(kc, carry):
        k0 = pl.multiple_of(kc * KC, KC)
        ik = ikb_ref[0, pl.ds(k0, KC), :]
        for h in range(IDX_HEADS):
            lg_scr[h, :, 0:tq] = jnp.dot(ik, iq_h[h], preferred_element_type=F32)
        for sb in range(KC // sbk):
            rs = slice(sb * sbk, (sb + 1) * sbk)
            s = jnp.zeros((sbk, tq), F32)
            for h in range(IDX_HEADS):
                s = s + iwt[h:h + 1, :] * jnp.maximum(lg_scr[h, rs, 0:tq], 0.0)
            kpos = k0 + sb * sbk + lax.broadcasted_iota(I32, (sbk, 1), 0)
            s = jnp.where(kpos < lim, s, -jnp.inf)
            b = lax.bitcast_convert_type(s, I32)
            b = jnp.where(b == INT_MIN, 0, b)
            s_scr[kc, rs, :] = b ^ ((b >> 31) & np.int32(0x7FFFFFFF))
        return carry

    lax.fori_loop(0, nkc, score_step, 0)

    def bit_step(i, thr):
        cand = thr + jnp.left_shift(jnp.int32(1), 31 - i)

        def cnt_step(kc, acc):
            return acc + _fold(jnp.where(s_scr[kc] >= cand, 1.0, 0.0), acc_r)

        cnt = jnp.sum(lax.fori_loop(0, nkc, cnt_step, jnp.zeros((acc_r, tq), F32)), axis=0, keepdims=True)
        return jnp.where(cnt >= keff, cand, thr)

    thr = lax.fori_loop(0, 32, bit_step, jnp.full((1, tq), INT_MIN, I32))

    def tie_cnt_step(kc, carry):
        key = s_scr[kc]
        return (carry[0] + _fold8(jnp.where(key > thr, 1.0, 0.0)),
                carry[1] + _fold8(jnp.where(key == thr, 1.0, 0.0)))

    z8 = jnp.zeros((8, tq), F32)
    n_gt, n_eq = lax.fori_loop(0, nkc, tie_cnt_step, (z8, z8))
    need = keff - jnp.sum(n_gt, axis=0, keepdims=True)
    surplus = jnp.max(jnp.sum(n_eq, axis=0, keepdims=True) - need)

    @pl.when(surplus < 0.5)
    def _():
        def fast_step(kc, carry):
            s_scr[kc] = lax.bitcast_convert_type(jnp.where(s_scr[kc] >= thr, 0.0, NEG), I32)
            return carry
        lax.fori_loop(0, nkc, fast_step, 0)

    @pl.when(surplus >= 0.5)
    def _():
        tri = jnp.where(lax.broadcasted_iota(I32, (KC, KC), 1) < lax.broadcasted_iota(I32, (KC, KC), 0),
                        1.0, 0.0).astype(BF16)

        def slow_step(kc, seen):
            key = s_scr[kc]
            eq = key == thr
            eqf = jnp.where(eq, 1.0, 0.0)
            rank = seen + jnp.dot(tri, eqf.astype(BF16), preferred_element_type=F32)
            bias = jnp.where(key > thr, 0.0, jnp.where(eq, jnp.where(rank < need, 0.0, NEG), NEG))
            s_scr[kc] = lax.bitcast_convert_type(bias, I32)
            return seen + jnp.sum(_fold8(eqf), axis=0, keepdims=True)
        lax.fori_loop(0, nkc, slow_step, jnp.zeros((1, tq), F32))

    hd = DSA_HEAD_DIM
    zq = jnp.zeros((hd, tq), BF16)
    for hp in range(DSA_HEADS // 2):
        r0 = hp * 2 * hd
        w_scr[hp, 0:hd, 0:tq] = qt_ref[0, r0:r0 + hd, :]
        w_scr[hp, 0:hd, tq:2 * tq] = zq
        w_scr[hp, hd:2 * hd, 0:tq] = zq
        w_scr[hp, hd:2 * hd, tq:2 * tq] = qt_ref[0, r0 + hd:r0 + 2 * hd, :]
    m_scr[...] = jnp.full((DSA_HEADS, tq), M_INIT, F32)
    acc_scr[...] = jnp.zeros((DSA_HEADS * DSA_VR, tq), F32)
    nsb = KC // sbk

    nhp = DSA_HEADS // 2

    def att_step(kc, carry):
        k0 = pl.multiple_of(kc * KC, KC)
        for hp in range(nhp):
            r0 = hp * 2 * hd
            lg_scr[hp] = jnp.dot(kb_ref[0, pl.ds(k0, KC), r0:r0 + 2 * hd], w_scr[hp],
                                 preferred_element_type=F32)
        for hp in range(nhp):
            li = hp
            for e in range(2):
                h = 2 * hp + e
                cols = slice(e * tq, (e + 1) * tq)
                mx = jnp.full((8, tq), -jnp.inf, F32)
                for sb in range(nsb):
                    rs = slice(sb * sbk, (sb + 1) * sbk)
                    x = lg_scr[li, rs, cols] + lax.bitcast_convert_type(s_scr[kc, rs, :], F32)
                    lg_scr[li, rs, cols] = x
                    mx = jnp.maximum(mx, _max8(x))
                m_old = m_scr[h:h + 1, :]
                m_new = jnp.maximum(m_old, jnp.max(mx, axis=0, keepdims=True))
                alpha = jnp.exp2(m_old - m_new)
                for sb in range(nsb):
                    rs = slice(sb * sbk, (sb + 1) * sbk)
                    pt_scr[h, rs, :] = jnp.exp2(lg_scr[li, rs, cols] - m_new).astype(BF16)
                m_scr[h:h + 1, :] = m_new
                hs = slice(h * DSA_VR, (h + 1) * DSA_VR)
                acc_scr[hs, :] = alpha * acc_scr[hs, :] + jnp.dot(vt_ref[0, kc, hs, :], pt_scr[h],
                                                                  preferred_element_type=F32)
        return carry

    lax.fori_loop(0, nkc, att_step, 0)
    outs = [acc_scr[h * DSA_VR:h * DSA_VR + hd, :] / acc_scr[h * DSA_VR + hd:h * DSA_VR + hd + 1, :]
            for h in range(DSA_HEADS)]
    o_ref[0] = jnp.concatenate(outs, axis=0).T


def _k_dsa(qt, iqt, iwt, kb3, vt4, ikb3, *, tq, P, L, KC, topk):
    B, _, T = qt.shape
    Lp = kb3.shape[1]
    nkc_total = Lp // KC
    return pl.pallas_call(
        functools.partial(_dsa_body, tq=tq, P=P, L=L, KC=KC, nkc_total=nkc_total, topk=topk),
        grid=(B, T // tq),
        in_specs=[pl.BlockSpec((1, 512, tq), lambda b, q: (b, 0, q)),
                  pl.BlockSpec((1, 256, tq), lambda b, q: (b, 0, q)),
                  pl.BlockSpec((1, 8, tq), lambda b, q: (b, 0, q)),
                  pl.BlockSpec((1, Lp, 512), lambda b, q: (b, 0, 0)),
                  pl.BlockSpec((1, nkc_total, DSA_HEADS * DSA_VR, KC), lambda b, q: (b, 0, 0, 0)),
                  pl.BlockSpec((1, Lp, 128), lambda b, q: (b, 0, 0))],
        out_specs=pl.BlockSpec((1, tq, 512), lambda b, q: (b, q, 0)),
        out_shape=jax.ShapeDtypeStruct((B, T, 512), F32),
        scratch_shapes=[pltpu.VMEM((nkc_total, KC, tq), I32),
                        pltpu.VMEM((DSA_HEADS // 2, 2 * DSA_HEAD_DIM, 2 * tq), BF16),
                        pltpu.VMEM((DSA_HEADS // 2, KC, 2 * tq), F32),
                        pltpu.VMEM((DSA_HEADS, KC, tq), BF16),
                        pltpu.VMEM((DSA_HEADS, tq), F32),
                        pltpu.VMEM((DSA_HEADS * DSA_VR, tq), F32)],
        compiler_params=_cparams(("arbitrary", "arbitrary")),
        name="dsa_attn",
    )(qt, iqt, iwt, kb3, vt4, ikb3)


def _merge_body(x_ref, sc_ref, sh_ref, g1_ref, ng_ref, oa_ref, ob_ref, oc_ref, od_ref,
                wg_ref, bg_ref, wb_ref, wo_ref, o_ref, *, bb, tt):
    x = x_ref[...]
    tm = bb * tt
    h = (_rms(x, ng_ref[...]) * (1.0 + sc_ref[...]) + sh_ref[...]).reshape(tm, D_MODEL).astype(BF16)
    merged = jnp.zeros((tm, D_MODEL), F32)
    for i, oref in enumerate((oa_ref, ob_ref, oc_ref, od_ref)):
        gate = _sigmoid(jnp.dot(h, wg_ref[:, i * D_MODEL:(i + 1) * D_MODEL], preferred_element_type=F32)
                        + bg_ref[:, i * D_MODEL:(i + 1) * D_MODEL])
        br = jnp.dot(oref[...].astype(BF16), wb_ref[i * BRANCH_W:(i + 1) * BRANCH_W, :],
                     preferred_element_type=F32)
        merged = merged + gate * br
    y = jnp.dot(merged.astype(BF16), wo_ref[...], preferred_element_type=F32)
    o_ref[...] = x + g1_ref[...] * y.reshape(bb, tt, D_MODEL)


def _k_merge(x, sc, sh, g1, ng, oa, ob, oc, od, wg_bf, bg, wb_bf, wo_bf, *, bb, tt):
    B, T, D = x.shape
    nt = T // tt
    tm = bb * tt
    xs = pl.BlockSpec((bb, tt, D), lambda i: (i // nt, i % nt, 0))
    ms = pl.BlockSpec((bb, 1, D), lambda i: (i // nt, 0, 0))
    os_ = pl.BlockSpec((tm, 512), lambda i: (i, 0))
    return pl.pallas_call(
        functools.partial(_merge_body, bb=bb, tt=tt),
        grid=((B // bb) * nt,),
        in_specs=[xs, ms, ms, ms, _const_spec((1, 1, D)), os_, os_, os_, os_,
                  _const_spec((D, 4 * D)), _const_spec((1, 4 * D)), _const_spec((4 * BRANCH_W, D)),
                  _const_spec((D, D))],
        out_specs=xs,
        out_shape=jax.ShapeDtypeStruct((B, T, D), F32),
        compiler_params=_cparams(("arbitrary",)),
        name="merge_out",
    )(x, sc, sh, g1, ng, oa, ob, oc, od, wg_bf, bg, wb_bf, wo_bf)


FC = FFN_DIM // 2
MOE_RB = 256


def _swiglu_chunk(h, w1, w3, w2):
    a = jnp.dot(h, w1, preferred_element_type=F32)
    b = jnp.dot(h, w3, preferred_element_type=F32)
    act = (a * _sigmoid(a) * b).astype(BF16)
    return jnp.dot(act, w2, preferred_element_type=F32)


def _finish(x, g2, f, fg_ref, o_ref, bb, tt):
    y = x + g2 * f.reshape(bb, tt, D_MODEL)
    if fg_ref is not None:
        y = _rms(y, fg_ref[...])
    o_ref[...] = y


def _ffn_body(x_ref, sc_ref, sh_ref, g2_ref, ng_ref, w1_ref, w3_ref, w2_ref, *rest, bb, tt, final):
    fg_ref, o_ref = rest if final else (None, rest[0])
    x = x_ref[...]
    tm = bb * tt
    h = (_rms(x, ng_ref[...]) * (1.0 + sc_ref[...]) + sh_ref[...]).reshape(tm, D_MODEL).astype(BF16)
    f = jnp.zeros((tm, D_MODEL), F32)
    for ci in range(FFN_DIM // FC):
        f = f + _swiglu_chunk(h, w1_ref[:, ci * FC:(ci + 1) * FC], w3_ref[:, ci * FC:(ci + 1) * FC],
                              w2_ref[ci * FC:(ci + 1) * FC, :])
    _finish(x, g2_ref[...], f, fg_ref, o_ref, bb, tt)


def _k_ffn(x, sc, sh, g2, ng, w1_bf, w3_bf, w2_bf, fg, *, bb, tt):
    B, T, D = x.shape
    nt = T // tt
    final = fg is not None
    xs = pl.BlockSpec((bb, tt, D), lambda i: (i // nt, i % nt, 0))
    ms = pl.BlockSpec((bb, 1, D), lambda i: (i // nt, 0, 0))
    in_specs = [xs, ms, ms, ms, _const_spec((1, 1, D)), _const_spec((D, FFN_DIM)),
                _const_spec((D, FFN_DIM)), _const_spec((FFN_DIM, D))]
    args = [x, sc, sh, g2, ng, w1_bf, w3_bf, w2_bf]
    if final:
        in_specs.append(_const_spec((1, 1, D)))
        args.append(fg)
    return pl.pallas_call(
        functools.partial(_ffn_body, bb=bb, tt=tt, final=final),
        grid=((B // bb) * nt,),
        in_specs=in_specs,
        out_specs=xs,
        out_shape=jax.ShapeDtypeStruct((B, T, D), F32),
        compiler_params=_cparams(("arbitrary",)),
        name="ffn_dense",
    )(*args)


def _moe_body(x_ref, sc_ref, sh_ref, g2_ref, ng_ref, rw_ref, rb_ref, w1_ref, w3_ref, w2_ref, *rest,
              bb, tt, final):
    if final:
        fg_ref, o_ref, h_scr, gate_scr, rank_scr, rankt_scr, xg_scr, y_scr, acc_scr = rest
    else:
        fg_ref = None
        o_ref, h_scr, gate_scr, rank_scr, rankt_scr, xg_scr, y_scr, acc_scr = rest
    e = pl.program_id(1)
    ci = pl.program_id(2)
    last_c = FFN_DIM // FC - 1
    tm = bb * tt
    lane = lax.broadcasted_iota(I32, (tm, 128), 1)

    @pl.when((e == 0) & (ci == 0))
    def _():
        h = (_rms(x_ref[...], ng_ref[...]) * (1.0 + sc_ref[...]) + sh_ref[...]).reshape(tm, D_MODEL)
        h_scr[...] = h.astype(BF16)
        logits = jnp.dot(h, rw_ref[...], preferred_element_type=F32,
                         precision=lax.Precision.HIGHEST) + rb_ref[...]
        m1 = jnp.max(logits, axis=-1, keepdims=True)
        i1 = jnp.min(jnp.where(logits == m1, lane, 128), axis=-1, keepdims=True)
        rest_l = jnp.where(lane == i1, NEG, logits)
        m2 = jnp.max(rest_l, axis=-1, keepdims=True)
        i2 = jnp.min(jnp.where(rest_l == m2, lane, 128), axis=-1, keepdims=True)
        e2 = jnp.exp(m2 - m1)
        den = 1.0 + e2
        gate_scr[...] = jnp.where(lane == i1, 1.0 / den, jnp.where(lane == i2, e2 / den, 0.0))
        sel = jnp.where(lane == i1, 1.0, jnp.where(lane == i2, 1.0, 0.0))
        tri = jnp.where(lax.broadcasted_iota(I32, (tm, tm), 1) < lax.broadcasted_iota(I32, (tm, tm), 0),
                        1.0, 0.0).astype(BF16)
        rank = jnp.where(sel > 0.5, jnp.dot(tri, sel.astype(BF16), preferred_element_type=F32), -1.0)
        rank_scr[...] = rank
        rankt_scr[...] = rank.T
        acc_scr[...] = jnp.zeros((tm, D_MODEL), F32)

    is_e = lane == e
    gate_col = jnp.sum(jnp.where(is_e, gate_scr[...], 0.0), axis=-1, keepdims=True)
    rank_col = jnp.sum(jnp.where(is_e, rank_scr[...], 0.0), axis=-1, keepdims=True)
    rank_row = rankt_scr[pl.ds(e, 1), :]
    n_e = jnp.max(rank_row).astype(I32) + 1
    nblk = (n_e + MOE_RB - 1) // MOE_RB

    def blk_step(b, carry):
        base = (b * MOE_RB).astype(F32)

        @pl.when(ci == 0)
        def _():
            slot = base + lax.broadcasted_iota(I32, (MOE_RB, 1), 0).astype(F32)
            pick = jnp.where(rank_row == slot, 1.0, 0.0).astype(BF16)
            xg_scr[b] = jnp.dot(pick, h_scr[...], preferred_element_type=F32).astype(BF16)

        part = _swiglu_chunk(xg_scr[b], w1_ref[0], w3_ref[0], w2_ref[0])

        @pl.when(ci == 0)
        def _():
            y_scr[b] = part

        @pl.when((ci > 0) & (ci < last_c))
        def _():
            y_scr[b] += part

        @pl.when(ci == last_c)
        def _():
            y = y_scr[b] + part
            y_hi = y.astype(BF16)
            y_lo = (y - y_hi.astype(F32)).astype(BF16)
            slot = base + lax.broadcasted_iota(I32, (1, MOE_RB), 1).astype(F32)
            put = jnp.where(rank_col == slot, 1.0, 0.0).astype(BF16)
            acc_scr[...] += gate_col * (jnp.dot(put, y_hi, preferred_element_type=F32)
                                        + jnp.dot(put, y_lo, preferred_element_type=F32))
        return carry

    lax.fori_loop(0, nblk, blk_step, 0)

    @pl.when((e == N_EXPERTS - 1) & (ci == last_c))
    def _():
        _finish(x_ref[...], g2_ref[...], acc_scr[...], fg_ref, o_ref, bb, tt)


def _k_moe(x, sc, sh, g2, ng, rw_pad, rb_pad, w1_bf, w3_bf, w2_bf, fg, *, bb, tt):
    B, T, D = x.shape
    nt = T // tt
    tm = bb * tt
    final = fg is not None
    xs = pl.BlockSpec((bb, tt, D), lambda i, e, c: (i // nt, i % nt, 0))
    xin = pl.BlockSpec((bb, tt, D), lambda i, e, c: (i // nt, i % nt, 0), pipeline_mode=pl.Buffered(1))
    ms = pl.BlockSpec((bb, 1, D), lambda i, e, c: (i // nt, 0, 0))
    in_specs = [xin, ms, ms, ms, _const_spec((1, 1, D)), _const_spec((D, 128)), _const_spec((1, 128)),
                pl.BlockSpec((1, D, FC), lambda i, e, c: (e, 0, c)),
                pl.BlockSpec((1, D, FC), lambda i, e, c: (e, 0, c)),
                pl.BlockSpec((1, FC, D), lambda i, e, c: (e, c, 0))]
    args = [x, sc, sh, g2, ng, rw_pad, rb_pad, w1_bf, w3_bf, w2_bf]
    if final:
        in_specs.append(_const_spec((1, 1, D)))
        args.append(fg)
    return pl.pallas_call(
        functools.partial(_moe_body, bb=bb, tt=tt, final=final),
        grid=((B // bb) * nt, N_EXPERTS, FFN_DIM // FC),
        in_specs=in_specs,
        out_specs=xs,
        out_shape=jax.ShapeDtypeStruct((B, T, D), F32),
        scratch_shapes=[pltpu.VMEM((tm, D), BF16), pltpu.VMEM((tm, 128), F32), pltpu.VMEM((tm, 128), F32),
                        pltpu.VMEM((128, tm), F32), pltpu.VMEM((pl.cdiv(tm, MOE_RB), MOE_RB, D), BF16),
                        pltpu.VMEM((pl.cdiv(tm, MOE_RB), MOE_RB, D), F32), pltpu.VMEM((tm, D), F32)],
        compiler_params=_cparams(("arbitrary", "arbitrary", "arbitrary")),
        name="ffn_moe",
    )(*args)


def _cast_body(x_ref, o_ref):
    o_ref[...] = x_ref[...].astype(BF16)


def _cast_bf16(w, start=0, count=None):
    _, R, C = w.shape
    E = w.shape[0] if count is None else count
    rb = 256
    return pl.pallas_call(
        _cast_body,
        grid=(E, R // rb),
        in_specs=[pl.BlockSpec((1, rb, C), lambda e, r: (start + e, r, 0))],
        out_specs=pl.BlockSpec((1, rb, C), lambda e, r: (e, r, 0)),
        out_shape=jax.ShapeDtypeStruct((E, R, C), BF16),
        compiler_params=_cparams(("arbitrary", "arbitrary")),
        name="cast_bf16",
    )(w)


def _dsa_keys_body(ck_ref, cv_ref, ci_ref, nk_ref, nv_ref, ni_ref, kb_ref, vt_ref, ikb_ref, *, nc, kc_rows):
    cached = pl.program_id(1) < nc
    k = jnp.where(cached, ck_ref[0], nk_ref[0].astype(BF16))
    v = jnp.where(cached, cv_ref[0], nv_ref[0].astype(BF16))
    ik = jnp.where(cached, ci_ref[0], ni_ref[0])
    kb_ref[0] = k
    vt = v.astype(F32).T.astype(BF16)
    ones = jnp.ones((DSA_VR - DSA_HEAD_DIM, kc_rows), BF16)
    for h in range(DSA_HEADS):
        vt_ref[0, 0, h * DSA_VR:h * DSA_VR + DSA_HEAD_DIM, :] = vt[h * DSA_HEAD_DIM:(h + 1) * DSA_HEAD_DIM]
        vt_ref[0, 0, h * DSA_VR + DSA_HEAD_DIM:(h + 1) * DSA_VR, :] = ones
    ikb_ref[0, :, 0:IDX_DIM] = ik.astype(BF16)
    ikb_ref[0, :, IDX_DIM:128] = jnp.zeros((kc_rows, 128 - IDX_DIM), BF16)


def _k_dsa_keys(ck, cv, ci, nk, nv, ni, *, KC, row0):
    B = nk.shape[0]
    P = ck.shape[1]
    assert P % KC == 0
    nc = P // KC
    cblk = lambda w: pl.BlockSpec((1, KC, w), lambda b, c: (row0 + b, jnp.minimum(c, nc - 1), 0))
    nblk = lambda w: pl.BlockSpec((1, KC, w), lambda b, c: (b, 0, 0))
    return pl.pallas_call(
        functools.partial(_dsa_keys_body, nc=nc, kc_rows=KC),
        grid=(B, nc + 1),
        in_specs=[cblk(512), cblk(512), cblk(IDX_DIM), nblk(512), nblk(512), nblk(IDX_DIM)],
        out_specs=[pl.BlockSpec((1, KC, 512), lambda b, c: (b, c, 0)),
                   pl.BlockSpec((1, 1, DSA_HEADS * DSA_VR, KC), lambda b, c: (b, c, 0, 0)),
                   pl.BlockSpec((1, KC, 128), lambda b, c: (b, c, 0))],
        out_shape=[jax.ShapeDtypeStruct((B, P + KC, 512), BF16),
                   jax.ShapeDtypeStruct((B, nc + 1, DSA_HEADS * DSA_VR, KC), BF16),
                   jax.ShapeDtypeStruct((B, P + KC, 128), BF16)],
        compiler_params=_cparams(("arbitrary", "arbitrary")),
        name="dsa_keys",
    )(ck, cv, ci, nk, nv, ni)


def _mla_keys_body(cc_ref, cr_ref, nc_ref, nr_ref, cb_ref, ct_ref, rb_ref, *, nc, kc_rows):
    cached = pl.program_id(1) < nc
    ckv = jnp.where(cached, cc_ref[0], nc_ref[0])
    kr = jnp.where(cached, cr_ref[0], nr_ref[0])
    cb_ref[0] = ckv.astype(BF16)
    ct_ref[0, 0, 0:MLA_KV_LORA, :] = ckv.T.astype(BF16)
    ct_ref[0, 0, MLA_KV_LORA:MLA_VR, :] = jnp.ones((MLA_VR - MLA_KV_LORA, kc_rows), BF16)
    rb_ref[0] = kr.astype(BF16)


def _k_mla_keys(cc, cr, ncv, nr, *, KC, row0):
    B = ncv.shape[0]
    P = cc.shape[1]
    assert P % KC == 0
    nc = P // KC
    cblk = lambda w: pl.BlockSpec((1, KC, w), lambda b, c: (row0 + b, jnp.minimum(c, nc - 1), 0))
    nblk = lambda w: pl.BlockSpec((1, KC, w), lambda b, c: (b, 0, 0))
    return pl.pallas_call(
        functools.partial(_mla_keys_body, nc=nc, kc_rows=KC),
        grid=(B, nc + 1),
        in_specs=[cblk(MLA_KV_LORA), cblk(MLA_ROPE), nblk(MLA_KV_LORA), nblk(MLA_ROPE)],
        out_specs=[pl.BlockSpec((1, KC, MLA_KV_LORA), lambda b, c: (b, c, 0)),
                   pl.BlockSpec((1, 1, MLA_VR, KC), lambda b, c: (b, c, 0, 0)),
                   pl.BlockSpec((1, KC, MLA_ROPE), lambda b, c: (b, c, 0))],
        out_shape=[jax.ShapeDtypeStruct((B, P + KC, MLA_KV_LORA), BF16),
                   jax.ShapeDtypeStruct((B, nc + 1, MLA_VR, KC), BF16),
                   jax.ShapeDtypeStruct((B, P + KC, MLA_ROPE), BF16)],
        compiler_params=_cparams(("arbitrary", "arbitrary")),
        name="mla_keys",
    )(cc, cr, ncv, nr)


def _prep_w_in(w):
    aq, ak, av, iq, ik, iw, pu, cb, cc, ch, mq, mkv, mkr = jnp.split(w, IN_SPLIT_POINTS, axis=1)
    d = w.shape[0]
    mkr_sw = jnp.concatenate([mkr[:, 16:], mkr[:, :16]], axis=1)
    return jnp.concatenate([aq, ak, av, pu, cb, cc, ch, iq, mq, mkv, ik, iw, jnp.zeros((d, 60), w.dtype),
                            mkr, mkr_sw, jnp.zeros((d, 64), w.dtype)], axis=1).astype(BF16)


def _prep_w_uq(w):
    w3 = w.reshape(MLA_Q_LORA, MLA_HEADS, MLA_NOPE + MLA_ROPE)
    nope = w3[:, :, :MLA_NOPE].reshape(MLA_Q_LORA, MLA_HEADS * MLA_NOPE)
    rope = w3[:, :, MLA_NOPE:]
    rope_sw = jnp.concatenate([rope[:, :, 16:], rope[:, :, :16]], axis=-1)
    return jnp.concatenate([nope, rope.reshape(MLA_Q_LORA, -1), rope_sw.reshape(MLA_Q_LORA, -1)],
                           axis=1).astype(BF16)


def _rope_tables(pos):
    half = MLA_ROPE // 2
    inv = 1.0 / (ROPE_BASE ** (jnp.arange(half, dtype=F32) / half))
    ang = pos.astype(F32)[:, None] * inv[None, :]
    cos, sin = jnp.cos(ang), jnp.sin(ang)
    cos2 = jnp.tile(jnp.concatenate([cos, cos], axis=-1), (1, MLA_HEADS))
    sin2 = jnp.tile(jnp.concatenate([-sin, sin], axis=-1), (1, MLA_HEADS))
    return cos2, sin2


def _layer(x, mod, past, lw, fw, use_moe, final_g, cfg):
    B, T, D = x.shape
    bb, tt, KC, tq_dsa, tq_mla = cfg["bb"], cfg["tt"], cfg["KC"], cfg["tq_dsa"], cfg["tq_mla"]
    KCD = cfg["KCD"]
    N = B * T
    P = 0 if past is None else past["dsa_k"].shape[1]
    L = P + T
    sh1, sc1, g1, sh2, sc2, g2 = [m.reshape(B, 1, D) for m in jnp.split(mod, 6, axis=-1)]
    ng1 = lw["norm_mix_g"].reshape(1, 1, D)
    ng2 = lw["norm_ffn_g"].reshape(1, 1, D)

    emit = past is None
    res = _k_in(x, sc1, sh1, ng1, lw["w_in"], bb=bb, tt=tt, emit=emit, kc=KCD)
    z = res[0]
    zoff = Z_OFF if emit else 0
    z3 = z.reshape(B, T, ZW - zoff)
    if emit:
        k_new = res[1].reshape(B, T, DSA_HEADS, DSA_HEAD_DIM)
        v_new = res[2].reshape(B, T, DSA_HEADS, DSA_HEAD_DIM)
    else:
        k_new = z[:, C_AK:C_AK + 512].reshape(B, T, DSA_HEADS, DSA_HEAD_DIM)
        v_new = z[:, C_AV:C_AV + 512].reshape(B, T, DSA_HEADS, DSA_HEAD_DIM)
    ik_new = z3[:, :, C_IKW - zoff:C_IKW - zoff + IDX_DIM]

    if past is None:
        ph16 = jnp.zeros((B, 16, 512), F32)
        cvh16 = ph16
    else:
        ph16 = jnp.pad(past["pool"], ((0, 0), (1, 0), (0, 0)))
        cvh16 = jnp.pad(past["conv"], ((0, 0), (16 - (CONV_WIDTH - 1), 0), (0, 0)))
    o_b, o_c, ph_o, cvh_o = _k_pc(z3, ph16, cvh16, lw["pool_w"], lw["pool_scale"], lw["conv_w"],
                                  bb=bb, tt=tt, P=P, zoff=zoff)
    pool_hist = ph_o[:, 1:]
    conv_hist = cvh_o[:, 16 - (CONV_WIDTH - 1):]

    cos2, sin2 = _rope_tables(P + jnp.arange(T))
    if bb > 1:
        cos2, sin2 = jnp.tile(cos2, (bb, 1)), jnp.tile(sin2, (bb, 1))
    qlt, qrt, ckv_new, ckvb, ckvt, kr_new, krb = _k_mlaprep(
        z, cos2, sin2, lw["mla_q_norm_g"], lw["mla_kv_norm_g"], lw["mla_w_uq"], lw["mla_w_uk"], tm=bb * tt,
        zoff=zoff)

    topk = min(DSA_TOPK_MAX, L // 4)
    if past is None:
        kb3 = res[3].reshape(B, T, 512)
        vt4, qt, iqt, iwt = res[4], res[5], res[6], res[7]
        ikb3 = res[8].reshape(B, T, 128)
        ckvb3 = ckvb.reshape(B, T, 256)
        krb3 = krb.reshape(B, T, 32)
        Tq = T
    else:
        assert T <= KCD and T <= KC
        padn = lambda a, kc: jnp.pad(a, ((0, 0), (0, kc - T), (0, 0)))
        kb3, vt4, ikb3 = _k_dsa_keys(past["dsa_k"], past["dsa_v"], past["idx_k"],
                                     padn(k_new.reshape(B, T, 512), KCD), padn(v_new.reshape(B, T, 512), KCD),
                                     padn(ik_new, KCD), KC=KCD, row0=past["row0"])
        Tq = tq_dsa
        padq = ((0, 0), (0, 0), (0, Tq - T))
        qt = jnp.pad((z3[:, :, C_AQ:C_AQ + 512] * (DSA_SCALE * LOG2E)).transpose(0, 2, 1).astype(BF16), padq)
        iqt = jnp.pad(z3[:, :, C_IQ:C_IQ + 256].transpose(0, 2, 1).astype(BF16), padq)
        iwt = jnp.pad(z3[:, :, C_IKW + 64:C_IKW + 72].transpose(0, 2, 1) * IDX_SCALE, padq)
        ckvb3, ckvt4, krb3 = _k_mla_keys(past["ckv"], past["krope"], padn(ckv_new.reshape(B, T, 256), KC),
                                         padn(kr_new.reshape(B, T, 32), KC), KC=KC, row0=past["row0"])

    o_a = _k_dsa(qt, iqt, iwt, kb3, vt4, ikb3, tq=tq_dsa, P=P, L=L, KC=KCD, topk=topk)
    o_a = o_a[:, :T].reshape(N, 512)
    if past is None:
        ckvt4 = ckvt.reshape(B, T // KC, MLA_VR, KC)
        o_d = _k_mla(qlt, qrt, ckvb3, krb3, ckvt4, lw["mla_w_uv"], B=B, T=T, tq=tq_mla, P=P, L=L, KC=KC,
                     stacked=False).reshape(N, 512)
    else:
        qs = qlt.reshape(MLA_HEADS, 256, B, T).transpose(2, 1, 0, 3).reshape(B, 256, MLA_HEADS * T)
        rs = qrt.reshape(MLA_HEADS, 32, B, T).transpose(2, 1, 0, 3).reshape(B, 32, MLA_HEADS * T)
        o5 = _k_mla(qs, rs, ckvb3, krb3, ckvt4, lw["mla_w_uv"], B=B, T=T, tq=tq_mla, P=P, L=L, KC=KC,
                    stacked=True).reshape(B, MLA_HEADS, MLA_V, MLA_HEADS, T)
        o_d = jnp.stack([o5[:, h, :, h, :] for h in range(MLA_HEADS)], axis=1)
        o_d = o_d.transpose(0, 3, 1, 2).reshape(N, 512)

    x1 = _k_merge(x, sc1, sh1, g1, ng1, o_a, o_b.reshape(N, 512), o_c.reshape(N, 512), o_d,
                  lw["w_gate"], lw["b_gate"], lw["w_branch"], lw["w_out"], bb=bb, tt=tt)
    fg = None if final_g is None else final_g.reshape(1, 1, D)
    if use_moe:
        x2 = _k_moe(x1, sc2, sh2, g2, ng2, fw["rw"], fw["rb"], fw["w1"], fw["w3"], fw["w2"], fg,
                    bb=cfg["bb_moe"], tt=cfg["tt_moe"])
    else:
        x2 = _k_ffn(x1, sc2, sh2, g2, ng2, fw["w1"], fw["w3"], fw["w2"], fg, bb=bb, tt=tt)
    rows = (k_new, v_new, ik_new, ckv_new.reshape(B, T, 256), kr_new.reshape(B, T, 32), pool_hist, conv_hist)
    return x2, rows


def _prep_layer(l, ada_w, ada_b, norm_mix_g, norm_ffn_g, w_in, mla_q_norm_g, mla_kv_norm_g,
                mla_w_uq, mla_w_uk, mla_w_uv, pool_w, pool_scale, conv_w, w_gate, b_gate, w_branch, w_out):
    return dict(
        ada_w=_cast_bf16(ada_w, l, 1)[0], ada_b=ada_b[l].reshape(1, -1),
        norm_mix_g=norm_mix_g[l], norm_ffn_g=norm_ffn_g[l],
        w_in=_prep_w_in(w_in[l]),
        mla_q_norm_g=mla_q_norm_g[l].reshape(1, -1), mla_kv_norm_g=mla_kv_norm_g[l].reshape(1, -1),
        mla_w_uq=_prep_w_uq(mla_w_uq[l]),
        mla_w_uk=mla_w_uk[l].transpose(1, 0, 2).astype(BF16),
        mla_w_uv=mla_w_uv[l].transpose(1, 2, 0).astype(BF16),
        pool_w=pool_w[l].astype(BF16), pool_scale=pool_scale[l].reshape(1, -1),
        conv_w=jnp.pad(conv_w[l].reshape(CONV_WIDTH, BRANCH_W), ((0, 8 - CONV_WIDTH), (0, 0))),
        w_gate=_cast_bf16(w_gate, l, 1)[0], b_gate=b_gate[l].reshape(1, -1),
        w_branch=_cast_bf16(w_branch, l, 1)[0], w_out=_cast_bf16(w_out, l, 1)[0])


def _cfg_for(B, T, has_past):
    if has_past:
        return dict(bb=B, tt=T, KC=512, KCD=512, tq_dsa=128, tq_mla=T, bb_moe=B, tt_moe=T)
    t = min(512, T)
    return dict(bb=1, tt=t, KC=t, KCD=t, tq_dsa=t, tq_mla=t, bb_moe=1, tt_moe=min(1024, T))


def _run(x, c, pasts, lws, fws, final_norm_g):
    B, T, _ = x.shape
    cfg = _cfg_for(B, T, pasts is not None)
    outs = [[] for _ in range(7)]
    depth = len(lws)
    for l in range(depth):
        lw = lws[l]
        mod = _ada(c, lw["ada_w"], lw["ada_b"])
        x, rows = _layer(x, mod, None if pasts is None else pasts[l], lw, fws[l], l % 2 == 1,
                         final_norm_g if l == depth - 1 else None, cfg)
        for o, r in zip(outs, rows):
            o.append(r)
    return x, [jnp.stack(o) for o in outs]


def kernel(x_prompt, x_sample, c_prompt, c_sample, cache_dsa_k, cache_dsa_v, cache_dsa_idx_k, cache_mla_ckv, cache_mla_krope, state_pool, state_conv, ada_w, ada_b, norm_mix_g, norm_ffn_g, w_in, mla_q_norm_g, mla_kv_norm_g, mla_w_uq, mla_w_uk, mla_w_uv, pool_w, pool_scale, conv_w, w_gate, b_gate, w_branch, w_out, ffn_w1, ffn_w3, ffn_w2, moe_router_w, moe_router_b, moe_w1, moe_w3, moe_w2, final_norm_g):
    depth = ada_w.shape[0]
    lws = [_prep_layer(l, ada_w, ada_b, norm_mix_g, norm_ffn_g, w_in, mla_q_norm_g, mla_kv_norm_g,
                       mla_w_uq, mla_w_uk, mla_w_uv, pool_w, pool_scale, conv_w, w_gate, b_gate,
                       w_branch, w_out) for l in range(depth)]
    fws = []
    for l in range(depth):
        j = l // 2
        if l % 2 == 0:
            fws.append(dict(w1=_cast_bf16(ffn_w1, j, 1)[0], w3=_cast_bf16(ffn_w3, j, 1)[0],
                            w2=_cast_bf16(ffn_w2, j, 1)[0]))
        else:
            fws.append(dict(
                rw=jnp.pad(moe_router_w[j], ((0, 0), (0, 128 - N_EXPERTS))),
                rb=jnp.pad(moe_router_b[j].reshape(1, -1), ((0, 0), (0, 128 - N_EXPERTS)), constant_values=NEG),
                w1=_cast_bf16(moe_w1.reshape((-1,) + moe_w1.shape[2:]), j * N_EXPERTS, N_EXPERTS),
                w3=_cast_bf16(moe_w3.reshape((-1,) + moe_w3.shape[2:]), j * N_EXPERTS, N_EXPERTS),
                w2=_cast_bf16(moe_w2.reshape((-1,) + moe_w2.shape[2:]), j * N_EXPERTS, N_EXPERTS)))
    nb, plen = cache_dsa_k.shape[1], cache_dsa_k.shape[2]
    flat = lambda a: a.reshape(depth * nb, plen, -1)
    kc_bf, vc_bf = flat(cache_dsa_k.astype(BF16)), flat(cache_dsa_v.astype(BF16))
    sample_pasts = [dict(dsa_k=kc_bf, dsa_v=vc_bf, idx_k=flat(cache_dsa_idx_k),
                         ckv=flat(cache_mla_ckv), krope=flat(cache_mla_krope), row0=l * nb,
                         pool=state_pool[l], conv=state_conv[l]) for l in range(depth)]
    y_prompt, pn = _run(x_prompt, c_prompt, None, lws, fws, final_norm_g)
    y_sample, sn = _run(x_sample, c_sample, sample_pasts, lws, fws, final_norm_g)
    return (y_prompt, y_sample, pn[0], pn[1], pn[2], pn[3], pn[4], pn[5], pn[6],
            sn[0], sn[1], sn[2], sn[3], sn[4], sn[5], sn[6])
```

```python
import functools

import numpy as np
import jax
import jax.numpy as jnp
from jax import lax
from jax.experimental import pallas as pl
from jax.experimental.pallas import tpu as pltpu

F32 = jnp.float32
BF16 = jnp.bfloat16
I32 = jnp.int32

D_MODEL = 1024
CHUNK = 64
N_BRANCH = 4
BRANCH_W = 512
DSA_HEADS = 8
DSA_HEAD_DIM = 64
IDX_HEADS = 4
IDX_DIM = 64
DSA_TOPK_MAX = 256
POOL_WINDOWS = (2, 4, 8, 16)
POOL_GROUP = BRANCH_W // 4
POOL_HIST = 15
CONV_WIDTH = 3
MLA_HEADS = 8
MLA_Q_LORA = 256
MLA_KV_LORA = 256
MLA_NOPE = 64
MLA_ROPE = 32
MLA_V = 64
ROPE_BASE = 10000.0
FFN_DIM = 2816
N_EXPERTS = 8
EPS = 1e-6

DSA_SCALE = DSA_HEAD_DIM ** -0.5
IDX_SCALE = (IDX_HEADS * IDX_DIM) ** -0.5
MLA_SCALE = (MLA_NOPE + MLA_ROPE) ** -0.5
LOG2E = 1.4426950408889634

IN_SPLIT_WIDTHS = (512, 512, 512, 256, 64, 4, 512, 512, 512, 512, 256, 256, 32)
IN_SPLIT_POINTS = tuple(int(sum(IN_SPLIT_WIDTHS[:i + 1])) for i in range(len(IN_SPLIT_WIDTHS) - 1))

C_AQ, C_AK, C_AV, C_PU, C_CB, C_CC, C_CH = 0, 512, 1024, 1536, 2048, 2560, 3072
C_IQ, C_MQ, C_MKV, C_IKW, C_MKR, ZW = 3584, 3840, 4096, 4352, 4480, 4608
IN_TN = 1536
Z_OFF = IN_TN
DSA_VR = DSA_HEAD_DIM + 16
MLA_VR = MLA_KV_LORA + 16

NEG = -1e30
M_INIT = -1e29
INT_MIN = np.int32(-2 ** 31)
V7X_VMEM_LIMIT = 56 * 1024 * 1024


def _cparams(sem):
    return pltpu.CompilerParams(dimension_semantics=sem, vmem_limit_bytes=V7X_VMEM_LIMIT)


def _rms(x, g):
    return x * lax.rsqrt(jnp.mean(x * x, axis=-1, keepdims=True) + EPS) * g


def _sigmoid(x):
    return 1.0 / (1.0 + jnp.exp(-x))


def _const_spec(shape):
    nd = len(shape)
    return pl.BlockSpec(shape, lambda *_: (0,) * nd, pipeline_mode=pl.Buffered(1))


def _ada_body(c_ref, w_ref, b_ref, o_ref):
    c = c_ref[...]
    s = (c * _sigmoid(c)).astype(BF16)
    o_ref[...] = jnp.dot(s, w_ref[...], preferred_element_type=F32) + b_ref[...]


def _ada(c, w_bf, b):
    B, D = c.shape
    n = w_bf.shape[1]
    tn = 1536
    return pl.pallas_call(
        _ada_body,
        grid=(n // tn,),
        in_specs=[pl.BlockSpec((B, D), lambda j: (0, 0)),
                  pl.BlockSpec((D, tn), lambda j: (0, j)),
                  pl.BlockSpec((1, tn), lambda j: (0, j))],
        out_specs=pl.BlockSpec((B, tn), lambda j: (0, j)),
        out_shape=jax.ShapeDtypeStruct((B, n), F32),
        compiler_params=_cparams(("arbitrary",)),
        name="ada_mod",
    )(c, w_bf, b)


def _in_body(x_ref, sc_ref, sh_ref, g_ref, w_ref, z_ref, *rest, bb, tt, emit, kc):
    if emit:
        kn_ref, vn_ref, kb_ref, vt_ref, qt_ref, iqt_ref, iwt_ref, ikb_ref, h_scr = rest
    else:
        (h_scr,) = rest
    j = pl.program_id(1)

    @pl.when(j == 0)
    def _():
        h = _rms(x_ref[...], g_ref[...]) * (1.0 + sc_ref[...]) + sh_ref[...]
        h_scr[...] = h.reshape(bb * tt, D_MODEL).astype(BF16)

    z = jnp.dot(h_scr[...], w_ref[...], preferred_element_type=F32)

    if not emit:
        z_ref[...] = z
    else:
        @pl.when(j > 0)
        def _():
            z_ref[...] = z

        @pl.when(j == 0)
        def _():
            k = z[:, C_AK:C_AK + 512]
            v = z[:, C_AV:C_AV + 512]
            kn_ref[...] = k
            vn_ref[...] = v
            kb_ref[...] = k.astype(BF16)
            vt = v.T.astype(BF16)
            ones = jnp.ones((DSA_VR - DSA_HEAD_DIM, kc), BF16)
            for c in range(tt // kc):
                for h in range(DSA_HEADS):
                    vt_ref[0, c, h * DSA_VR:h * DSA_VR + DSA_HEAD_DIM, :] = vt[
                        h * DSA_HEAD_DIM:(h + 1) * DSA_HEAD_DIM, c * kc:(c + 1) * kc]
                    vt_ref[0, c, h * DSA_VR + DSA_HEAD_DIM:(h + 1) * DSA_VR, :] = ones
            qt_ref[0] = (z[:, C_AQ:C_AQ + 512] * (DSA_SCALE * LOG2E)).T.astype(BF16)

        @pl.when(j == 2)
        def _():
            o = 2 * IN_TN
            iqt_ref[0] = z[:, C_IQ - o:C_IQ - o + 256].T.astype(BF16)
            ikw = z[:, C_IKW - o:C_IKW - o + 128]
            ikb_ref[...] = ikw.astype(BF16)
            iwt_ref[0] = ikw.T[64:72, :] * IDX_SCALE


def _k_in(x, sc, sh, g, w_bf, *, bb, tt, emit, kc):
    B, T, D = x.shape
    nt = T // tt
    tm = bb * tt
    n_rows = (B // bb) * nt
    N = B * T
    xmap = lambda i, j: (i // nt, i % nt, 0)
    mmap = lambda i, j: (i // nt, 0, 0)
    in_specs = [pl.BlockSpec((bb, tt, D), xmap),
                pl.BlockSpec((bb, 1, D), mmap),
                pl.BlockSpec((bb, 1, D), mmap),
                _const_spec((1, 1, D)),
                pl.BlockSpec((D, IN_TN), lambda i, j: (0, j))]
    if emit:
        assert bb == 1
        out_specs = [pl.BlockSpec((tm, IN_TN), lambda i, j: (i, jnp.maximum(j - 1, 0)))]
        out_shape = [jax.ShapeDtypeStruct((N, ZW - Z_OFF), F32)]
        out_specs += [
            pl.BlockSpec((tm, 512), lambda i, j: (i, 0)),
            pl.BlockSpec((tm, 512), lambda i, j: (i, 0)),
            pl.BlockSpec((tm, 512), lambda i, j: (i, 0)),
            pl.BlockSpec((1, tt // kc, DSA_HEADS * DSA_VR, kc), lambda i, j: (i // nt, i % nt, 0, 0)),
            pl.BlockSpec((1, 512, tt), lambda i, j: (i // nt, 0, i % nt)),
            pl.BlockSpec((1, 256, tt), lambda i, j: (i // nt, 0, i % nt)),
            pl.BlockSpec((1, 8, tt), lambda i, j: (i // nt, 0, i % nt)),
            pl.BlockSpec((tm, 128), lambda i, j: (i, 0)),
        ]
        out_shape += [
            jax.ShapeDtypeStruct((N, 512), F32),
            jax.ShapeDtypeStruct((N, 512), F32),
            jax.ShapeDtypeStruct((N, 512), BF16),
            jax.ShapeDtypeStruct((B, T // kc, DSA_HEADS * DSA_VR, kc), BF16),
            jax.ShapeDtypeStruct((B, 512, T), BF16),
            jax.ShapeDtypeStruct((B, 256, T), BF16),
            jax.ShapeDtypeStruct((B, 8, T), F32),
            jax.ShapeDtypeStruct((N, 128), BF16),
        ]
    else:
        out_specs = [pl.BlockSpec((tm, IN_TN), lambda i, j: (i, j))]
        out_shape = [jax.ShapeDtypeStruct((N, ZW), F32)]
    return pl.pallas_call(
        functools.partial(_in_body, bb=bb, tt=tt, emit=emit, kc=kc),
        grid=(n_rows, ZW // IN_TN),
        in_specs=in_specs,
        out_specs=out_specs,
        out_shape=out_shape,
        scratch_shapes=[pltpu.VMEM((tm, D), BF16)],
        compiler_params=_cparams(("arbitrary", "arbitrary")),
        name="in_proj",
    )(x, sc, sh, g, w_bf)


def _pc_body(pu_ref, cb_ref, cc_ref, ch_ref, pup_ref, ccp_ref, chp_ref, ph_ref, cvh_ref,
             pw_ref, ps_ref, cw_ref, ob_ref, oc_ref, pho_ref, cho_ref, u_scr, g_scr, *, bb, tt, P):
    t = pl.program_id(1)
    first = t == 0
    pu = pu_ref[...]
    g = cc_ref[...] * ch_ref[...]
    u_scr[:, 0:16, :] = jnp.where(first, ph_ref[...], pup_ref[...])
    u_scr[:, 16:16 + tt, :] = pu
    g_scr[:, 0:16, :] = jnp.where(first, cvh_ref[...], ccp_ref[...] * chp_ref[...])
    g_scr[:, 16:16 + tt, :] = g

    pos = P + t * tt + lax.broadcasted_iota(I32, (1, tt, 1), 1)
    for gi, win in enumerate(POOL_WINDOWS):
        lo = gi * POOL_GROUP
        acc = pu[:, :, lo:lo + POOL_GROUP]
        for j in range(1, win):
            acc = acc + u_scr[:, 16 - j:16 - j + tt, lo:lo + POOL_GROUP]
        cnt = jnp.minimum(pos + 1, win).astype(F32)
        d = acc / cnt - pu[:, :, lo:lo + POOL_GROUP]
        y = jnp.dot(d.reshape(bb * tt, POOL_GROUP).astype(BF16), pw_ref[gi], preferred_element_type=F32)
        ob_ref[:, :, lo:lo + POOL_GROUP] = (y * ps_ref[:, lo:lo + POOL_GROUP]).reshape(bb, tt, POOL_GROUP)

    cw = cw_ref[...]
    conv = (cw[0:1, :] * g_scr[:, 14:14 + tt, :] + cw[1:2, :] * g_scr[:, 15:15 + tt, :] + cw[2:3, :] * g)
    oc_ref[...] = cb_ref[...] * conv
    pho_ref[...] = u_scr[:, tt:tt + 16, :]
    cho_ref[...] = g_scr[:, tt:tt + 16, :]


def _k_pc(z3, ph16, cvh16, pw_bf, ps, cw, *, bb, tt, P, zoff):
    B, T, _ = z3.shape
    nt = T // tt
    r = tt // 16

    def cur(c):
        return pl.BlockSpec((bb, tt, 512), lambda b, t: (b, t, (c - zoff) // 512))

    def prev(c):
        return pl.BlockSpec((bb, 16, 512), lambda b, t: (b, jnp.maximum(t * r - 1, 0), (c - zoff) // 512))

    hist = pl.BlockSpec((bb, 16, 512), lambda b, t: (b, 0, 0))
    return pl.pallas_call(
        functools.partial(_pc_body, bb=bb, tt=tt, P=P),
        grid=(B // bb, nt),
        in_specs=[cur(C_PU), cur(C_CB), cur(C_CC), cur(C_CH), prev(C_PU), prev(C_CC), prev(C_CH),
                  hist, hist, _const_spec((4, POOL_GROUP, POOL_GROUP)), _const_spec((1, 512)),
                  _const_spec((8, 512))],
        out_specs=[pl.BlockSpec((bb, tt, 512), lambda b, t: (b, t, 0)),
                   pl.BlockSpec((bb, tt, 512), lambda b, t: (b, t, 0)), hist, hist],
        out_shape=[jax.ShapeDtypeStruct((B, T, 512), F32), jax.ShapeDtypeStruct((B, T, 512), F32),
                   jax.ShapeDtypeStruct((B, 16, 512), F32), jax.ShapeDtypeStruct((B, 16, 512), F32)],
        scratch_shapes=[pltpu.VMEM((bb, tt + 16, 512), F32), pltpu.VMEM((bb, tt + 16, 512), F32)],
        compiler_params=_cparams(("arbitrary", "arbitrary")),
        name="pool_conv",
    )(z3, z3, z3, z3, z3, z3, z3, ph16, cvh16, pw_bf, ps, cw)


def _mp_body(mq_ref, mkv_ref, mkr_ref, cos_ref, sin_ref, gq_ref, gkv_ref, wuq_ref, wukt_ref,
             qlt_ref, qrt_ref, ckv_ref, ckvb_ref, ckvt_ref, kr_ref, krb_ref):
    a = _rms(mq_ref[...], gq_ref[...]).astype(BF16)
    cq = jnp.dot(a, wuq_ref[...], preferred_element_type=F32)
    cqt = cq.T
    cos = cos_ref[...]
    sin = sin_ref[...]
    c = MLA_SCALE * LOG2E
    qrt_ref[...] = ((cqt[512:768] * cos.T + cqt[768:1024] * sin.T) * c).astype(BF16)
    for h in range(MLA_HEADS):
        qnt = cqt[h * MLA_NOPE:(h + 1) * MLA_NOPE].astype(BF16)
        qlt_ref[h * MLA_KV_LORA:(h + 1) * MLA_KV_LORA, :] = (jnp.dot(
            wukt_ref[h], qnt, preferred_element_type=F32) * c).astype(BF16)
    ckv = _rms(mkv_ref[...], gkv_ref[...])
    ckv_ref[...] = ckv
    ckvb_ref[...] = ckv.astype(BF16)
    ckvt_ref[0, 0:MLA_KV_LORA, :] = ckv.T.astype(BF16)
    ckvt_ref[0, MLA_KV_LORA:MLA_VR, :] = jnp.ones((MLA_VR - MLA_KV_LORA, ckv.shape[0]), BF16)
    mkr = mkr_ref[...]
    kr = mkr[:, 0:32] * cos[:, 0:32] + mkr[:, 32:64] * sin[:, 0:32]
    kr_ref[...] = kr
    krb_ref[...] = kr.astype(BF16)


def _k_mlaprep(z, cos_t, sin_t, gq, gkv, wuq_bf, wukt_bf, *, tm, zoff):
    N = z.shape[0]
    nr = cos_t.shape[0] // tm
    tab = pl.BlockSpec((tm, 256), lambda i: (i % nr, 0))
    row = lambda w: pl.BlockSpec((tm, w), lambda i: (i, 0))
    col = lambda h: pl.BlockSpec((h, tm), lambda i: (0, i))
    return pl.pallas_call(
        _mp_body,
        grid=(N // tm,),
        in_specs=[pl.BlockSpec((tm, 256), lambda i: (i, (C_MQ - zoff) // 256)),
                  pl.BlockSpec((tm, 256), lambda i: (i, (C_MKV - zoff) // 256)),
                  pl.BlockSpec((tm, 128), lambda i: (i, (C_MKR - zoff) // 128)),
                  tab, tab, _const_spec((1, 256)), _const_spec((1, 256)),
                  _const_spec((256, 1024)), _const_spec((MLA_HEADS, MLA_KV_LORA, MLA_NOPE))],
        out_specs=[col(2048), col(256), row(256), row(256),
                   pl.BlockSpec((1, MLA_VR, tm), lambda i: (i, 0, 0)), row(32), row(32)],
        out_shape=[jax.ShapeDtypeStruct((2048, N), BF16), jax.ShapeDtypeStruct((256, N), BF16),
                   jax.ShapeDtypeStruct((N, 256), F32), jax.ShapeDtypeStruct((N, 256), BF16),
                   jax.ShapeDtypeStruct((N // tm, MLA_VR, tm), BF16),
                   jax.ShapeDtypeStruct((N, 32), F32), jax.ShapeDtypeStruct((N, 32), BF16)],
        compiler_params=_cparams(("arbitrary",)),
        name="mla_prep",
    )(z, z, z, cos_t, sin_t, gq, gkv, wuq_bf, wukt_bf)


def _dot_nt(a, b):
    return lax.dot_general(a, b, (((1,), (1,)), ((), ())), preferred_element_type=F32)


MLA_SB = 128
MLA_CB = 256


def _mla_body(qlt_ref, qrt_ref, ckv_ref, kr_ref, ckvt_ref, wuvt_ref, o_ref,
              qt_scr, rt_scr, s_scr, pt_scr, m_scr, acc_scr,
              *, tq, P, L, KC, nkc_total, stacked):
    qi = pl.program_id(1)
    q0 = P + qi * tq
    cols_n = MLA_HEADS * tq
    if stacked:
        qt_scr[...] = qlt_ref[0]
        rt_scr[...] = qrt_ref[0]
    else:
        for h in range(MLA_HEADS):
            qt_scr[:, h * tq:(h + 1) * tq] = qlt_ref[h * 256:(h + 1) * 256, :]
            rt_scr[:, h * tq:(h + 1) * tq] = qrt_ref[h * 32:(h + 1) * 32, :]
    max_cend = (((q0 + tq - 1) >> 6) + 1) << 6
    nkc = jnp.minimum((max_cend + KC - 1) // KC, nkc_total)
    n_full = jnp.minimum(((q0 >> 6) + 1) << 6, L) // KC
    cb_w = min(MLA_CB, cols_n)

    m_scr[...] = jnp.full((1, cols_n), M_INIT, F32)
    acc_scr[...] = jnp.zeros((MLA_VR, cols_n), F32)

    def make_step(masked):
        def step(kc, carry):
            k0 = pl.multiple_of(kc * KC, KC)
            ck = ckv_ref[0, pl.ds(k0, KC), :]
            kr = kr_ref[0, pl.ds(k0, KC), :]
            s_scr[...] = (jnp.dot(ck, qt_scr[...], preferred_element_type=F32)
                          + jnp.dot(kr, rt_scr[...], preferred_element_type=F32))
            for cb in range(cols_n // cb_w):
                cols = slice(cb * cb_w, (cb + 1) * cb_w)
                if masked:
                    lane = cb * cb_w + lax.broadcasted_iota(I32, (1, cb_w), 1)
                    qpos = q0 + (lane & (tq - 1))
                    lim = jnp.minimum(((qpos >> 6) + 1) << 6, L)
                mx = jnp.full((8, cb_w), -jnp.inf, F32)
                for sb in range(KC // MLA_SB):
                    rs = slice(sb * MLA_SB, (sb + 1) * MLA_SB)
                    x = s_scr[rs, cols]
                    if masked:
                        kpos = k0 + sb * MLA_SB + lax.broadcasted_iota(I32, (MLA_SB, 1), 0)
                        x = jnp.where(kpos < lim, x, NEG)
                        s_scr[rs, cols] = x
                    mx = jnp.maximum(mx, _max8(x))
                m_old = m_scr[:, cols]
                m_new = jnp.maximum(m_old, jnp.max(mx, axis=0, keepdims=True))
                for sb in range(KC // MLA_SB):
                    rs = slice(sb * MLA_SB, (sb + 1) * MLA_SB)
                    pt_scr[rs, cols] = jnp.exp2(s_scr[rs, cols] - m_new).astype(BF16)
                m_scr[:, cols] = m_new
                acc_scr[:, cols] = (jnp.exp2(m_old - m_new) * acc_scr[:, cols]
                                    + jnp.dot(ckvt_ref[0, kc], pt_scr[:, cols], preferred_element_type=F32))
            return carry
        return step

    lax.fori_loop(0, n_full, make_step(False), 0)
    lax.fori_loop(n_full, nkc, make_step(True), 0)
    ot = (acc_scr[0:MLA_KV_LORA, :] / acc_scr[MLA_KV_LORA:MLA_KV_LORA + 1, :]).astype(BF16)
    if stacked:
        for h in range(MLA_HEADS):
            o_ref[0, h * MLA_V:(h + 1) * MLA_V, :] = jnp.dot(wuvt_ref[h], ot, preferred_element_type=F32)
    else:
        outs = [jnp.dot(wuvt_ref[h], ot[:, h * tq:(h + 1) * tq], preferred_element_type=F32)
                for h in range(MLA_HEADS)]
        o_ref[0] = jnp.concatenate(outs, axis=0).T


def _k_mla(qlt, qrt, ckvb, krb, ckvt, wuvt_bf, *, B, T, tq, P, L, KC, stacked):
    Lp = ckvb.shape[1]
    nkc_total = Lp // KC
    cols_n = MLA_HEADS * tq
    nq = T // tq
    if stacked:
        q_specs = [pl.BlockSpec((1, MLA_KV_LORA, cols_n), lambda b, q: (b, 0, 0)),
                   pl.BlockSpec((1, MLA_ROPE, cols_n), lambda b, q: (b, 0, 0))]
        out_spec = pl.BlockSpec((1, MLA_HEADS * MLA_V, cols_n), lambda b, q: (b, 0, 0))
        out_shape = jax.ShapeDtypeStruct((B, MLA_HEADS * MLA_V, cols_n), F32)
    else:
        q_specs = [pl.BlockSpec((MLA_HEADS * MLA_KV_LORA, tq), lambda b, q: (0, b * nq + q)),
                   pl.BlockSpec((MLA_HEADS * MLA_ROPE, tq), lambda b, q: (0, b * nq + q))]
        out_spec = pl.BlockSpec((1, tq, 512), lambda b, q: (b, q, 0))
        out_shape = jax.ShapeDtypeStruct((B, T, 512), F32)
    return pl.pallas_call(
        functools.partial(_mla_body, tq=tq, P=P, L=L, KC=KC, nkc_total=nkc_total, stacked=stacked),
        grid=(B, nq),
        in_specs=q_specs + [pl.BlockSpec((1, Lp, 256), lambda b, q: (b, 0, 0)),
                            pl.BlockSpec((1, Lp, 32), lambda b, q: (b, 0, 0)),
                            pl.BlockSpec((1, nkc_total, MLA_VR, KC), lambda b, q: (b, 0, 0, 0)),
                            _const_spec((MLA_HEADS, MLA_V, MLA_KV_LORA))],
        out_specs=out_spec,
        out_shape=out_shape,
        scratch_shapes=[pltpu.VMEM((MLA_KV_LORA, cols_n), BF16), pltpu.VMEM((MLA_ROPE, cols_n), BF16),
                        pltpu.VMEM((KC, cols_n), F32), pltpu.VMEM((KC, cols_n), BF16),
                        pltpu.VMEM((1, cols_n), F32), pltpu.VMEM((MLA_VR, cols_n), F32)],
        compiler_params=_cparams(("arbitrary", "arbitrary")),
        name="mla_attn",
    )(qlt, qrt, ckvb, krb, ckvt, wuvt_bf)


def _fold(x, r):
    n, w = x.shape
    return jnp.sum(x.reshape(n // r, r, w), axis=0)


def _fold8(x):
    return _fold(x, 8)


def _max8(x):
    n, w = x.shape
    return jnp.max(x.reshape(n // 8, 8, w), axis=0)


DSA_SB_ELEMS = 128 * 256


def _dsa_body(qt_ref, iqt_ref, iwt_ref, kb_ref, vt_ref, ikb_ref, o_ref,
              s_scr, w_scr, lg_scr, pt_scr, m_scr, acc_scr, *, tq, P, L, KC, nkc_total, topk):
    qi = pl.program_id(1)
    q0 = P + qi * tq
    sbk = min(KC, DSA_SB_ELEMS // tq)
    acc_r = sbk // 2
    qpos = q0 + lax.broadcasted_iota(I32, (1, tq), 1)
    lim = jnp.minimum(((qpos >> 6) + 1) << 6, L)
    keff = jnp.minimum(lim, topk).astype(F32)
    max_cend = (((q0 + tq - 1) >> 6) + 1) << 6
    nkc = jnp.minimum((max_cend + KC - 1) // KC, nkc_total)

    iqt = iqt_ref[0]
    iwt = iwt_ref[0]
    zpad = jnp.zeros((IDX_DIM, tq), BF16)
    iq_h = [jnp.concatenate([iqt[h * IDX_DIM:(h + 1) * IDX_DIM], zpad], axis=0) for h in range(IDX_HEADS)]

    def score_step(kc, carry):
        k0 = pl.multiple_of(kc * KC, KC)
        ik = ikb_ref[0, pl.ds(k0, KC), :]
        for h in range(IDX_HEADS):
            lg_scr[h, :, 0:tq] = jnp.dot(ik, iq_h[h], preferred_element_type=F32)
        for sb in range(KC // sbk):
            rs = slice(sb * sbk, (sb + 1) * sbk)
            s = jnp.zeros((sbk, tq), F32)
            for h in range(IDX_HEADS):
                s = s + iwt[h:h + 1, :] * jnp.maximum(lg_scr[h, rs, 0:tq], 0.0)
            kpos = k0 + sb * sbk + lax.broadcasted_iota(I32, (sbk, 1), 0)
            s = jnp.where(kpos < lim, s, -jnp.inf)
            b = lax.bitcast_convert_type(s, I32)
            b = jnp.where(b == INT_MIN, 0, b)
            s_scr[kc, rs, :] = b ^ ((b >> 31) & np.int32(0x7FFFFFFF))
        return carry

    lax.fori_loop(0, nkc, score_step, 0)

    def bit_step(i, thr):
        cand = thr + jnp.left_shift(jnp.int32(1), 31 - i)

        def cnt_step(kc, acc):
            return acc + _fold(jnp.where(s_scr[kc] >= cand, 1.0, 0.0), acc_r)

        cnt = jnp.sum(lax.fori_loop(0, nkc, cnt_step, jnp.zeros((acc_r, tq), F32)), axis=0, keepdims=True)
        return jnp.where(cnt >= keff, cand, thr)

    thr = lax.fori_loop(0, 32, bit_step, jnp.full((1, tq), INT_MIN, I32))

    def tie_cnt_step(kc, carry):
        key = s_scr[kc]
        return (carry[0] + _fold8(jnp.where(key > thr, 1.0, 0.0)),
                carry[1] + _fold8(jnp.where(key == thr, 1.0, 0.0)))

    z8 = jnp.zeros((8, tq), F32)
    n_gt, n_eq = lax.fori_loop(0, nkc, tie_cnt_step, (z8, z8))
    need = keff - jnp.sum(n_gt, axis=0, keepdims=True)
    surplus = jnp.max(jnp.sum(n_eq, axis=0, keepdims=True) - need)

    @pl.when(surplus < 0.5)
    def _():
        def fast_step(kc, carry):
            s_scr[kc] = lax.bitcast_convert_type(jnp.where(s_scr[kc] >= thr, 0.0, NEG), I32)
            return carry
        lax.fori_loop(0, nkc, fast_step, 0)

    @pl.when(surplus >= 0.5)
    def _():
        tri = jnp.where(lax.broadcasted_iota(I32, (KC, KC), 1) < lax.broadcasted_iota(I32, (KC, KC), 0),
                        1.0, 0.0).astype(BF16)

        def slow_step(kc, seen):
            key = s_scr[kc]
            eq = key == thr
            eqf = jnp.where(eq, 1.0, 0.0)
            rank = seen + jnp.dot(tri, eqf.astype(BF16), preferred_element_type=F32)
            bias = jnp.where(key > thr, 0.0, jnp.where(eq, jnp.where(rank < need, 0.0, NEG), NEG))
            s_scr[kc] = lax.bitcast_convert_type(bias, I32)
            return seen + jnp.sum(_fold8(eqf), axis=0, keepdims=True)
        lax.fori_loop(0, nkc, slow_step, jnp.zeros((1, tq), F32))

    hd = DSA_HEAD_DIM
    zq = jnp.zeros((hd, tq), BF16)
    for hp in range(DSA_HEADS // 2):
        r0 = hp * 2 * hd
        w_scr[hp, 0:hd, 0:tq] = qt_ref[0, r0:r0 + hd, :]
        w_scr[hp, 0:hd, tq:2 * tq] = zq
        w_scr[hp, hd:2 * hd, 0:tq] = zq
        w_scr[hp, hd:2 * hd, tq:2 * tq] = qt_ref[0, r0 + hd:r0 + 2 * hd, :]
    m_scr[...] = jnp.full((DSA_HEADS, tq), M_INIT, F32)
    acc_scr[...] = jnp.zeros((DSA_HEADS * DSA_VR, tq), F32)
    nsb = KC // sbk

    nhp = DSA_HEADS // 2

    def att_step(kc, carry):
        k0 = pl.multiple_of(kc * KC, KC)
        for hp in range(nhp):
            r0 = hp * 2 * hd
            lg_scr[hp] = jnp.dot(kb_ref[0, pl.ds(k0, KC), r0:r0 + 2 * hd], w_scr[hp],
                                 preferred_element_type=F32)
        for hp in range(nhp):
            li = hp
            for e in range(2):
                h = 2 * hp + e
                cols = slice(e * tq, (e + 1) * tq)
                mx = jnp.full((8, tq), -jnp.inf, F32)
                for sb in range(nsb):
                    rs = slice(sb * sbk, (sb + 1) * sbk)
                    x = lg_scr[li, rs, cols] + lax.bitcast_convert_type(s_scr[kc, rs, :], F32)
                    lg_scr[li, rs, cols] = x
                    mx = jnp.maximum(mx, _max8(x))
                m_old = m_scr[h:h + 1, :]
                m_new = jnp.maximum(m_old, jnp.max(mx, axis=0, keepdims=True))
                alpha = jnp.exp2(m_old - m_new)
                for sb in range(nsb):
                    rs = slice(sb * sbk, (sb + 1) * sbk)
                    pt_scr[h, rs, :] = jnp.exp2(lg_scr[li, rs, cols] - m_new).astype(BF16)
                m_scr[h:h + 1, :] = m_new
                hs = slice(h * DSA_VR, (h + 1) * DSA_VR)
                acc_scr[hs, :] = alpha * acc_scr[hs, :] + jnp.dot(vt_ref[0, kc, hs, :], pt_scr[h],
                                                                  preferred_element_type=F32)
        return carry

    lax.fori_loop(0, nkc, att_step, 0)
    outs = [acc_scr[h * DSA_VR:h * DSA_VR + hd, :] / acc_scr[h * DSA_VR + hd:h * DSA_VR + hd + 1, :]
            for h in range(DSA_HEADS)]
    o_ref[0] = jnp.concatenate(outs, axis=0).T


def _k_dsa(qt, iqt, iwt, kb3, vt4, ikb3, *, tq, P, L, KC, topk):
    B, _, T = qt.shape
    Lp = kb3.shape[1]
    nkc_total = Lp // KC
    return pl.pallas_call(
        functools.partial(_dsa_body, tq=tq, P=P, L=L, KC=KC, nkc_total=nkc_total, topk=topk),
        grid=(B, T // tq),
        in_specs=[pl.BlockSpec((1, 512, tq), lambda b, q: (b, 0, q)),
                  pl.BlockSpec((1, 256, tq), lambda b, q: (b, 0, q)),
                  pl.BlockSpec((1, 8, tq), lambda b, q: (b, 0, q)),
                  pl.BlockSpec((1, Lp, 512), lambda b, q: (b, 0, 0)),
                  pl.BlockSpec((1, nkc_total, DSA_HEADS * DSA_VR, KC), lambda b, q: (b, 0, 0, 0)),
                  pl.BlockSpec((1, Lp, 128), lambda b, q: (b, 0, 0))],
        out_specs=pl.BlockSpec((1, tq, 512), lambda b, q: (b, q, 0)),
        out_shape=jax.ShapeDtypeStruct((B, T, 512), F32),
        scratch_shapes=[pltpu.VMEM((nkc_total, KC, tq), I32),
                        pltpu.VMEM((DSA_HEADS // 2, 2 * DSA_HEAD_DIM, 2 * tq), BF16),
                        pltpu.VMEM((DSA_HEADS // 2, KC, 2 * tq), F32),
                        pltpu.VMEM((DSA_HEADS, KC, tq), BF16),
                        pltpu.VMEM((DSA_HEADS, tq), F32),
                        pltpu.VMEM((DSA_HEADS * DSA_VR, tq), F32)],
        compiler_params=_cparams(("arbitrary", "arbitrary")),
        name="dsa_attn",
    )(qt, iqt, iwt, kb3, vt4, ikb3)


def _merge_body(x_ref, sc_ref, sh_ref, g1_ref, ng_ref, oa_ref, ob_ref, oc_ref, od_ref,
                wg_ref, bg_ref, wb_ref, wo_ref, o_ref, *, bb, tt):
    x = x_ref[...]
    tm = bb * tt
    h = (_rms(x, ng_ref[...]) * (1.0 + sc_ref[...]) + sh_ref[...]).reshape(tm, D_MODEL).astype(BF16)
    merged = jnp.zeros((tm, D_MODEL), F32)
    for i, oref in enumerate((oa_ref, ob_ref, oc_ref, od_ref)):
        gate = _sigmoid(jnp.dot(h, wg_ref[:, i * D_MODEL:(i + 1) * D_MODEL], preferred_element_type=F32)
                        + bg_ref[:, i * D_MODEL:(i + 1) * D_MODEL])
        br = jnp.dot(oref[...].astype(BF16), wb_ref[i * BRANCH_W:(i + 1) * BRANCH_W, :],
                     preferred_element_type=F32)
        merged = merged + gate * br
    y = jnp.dot(merged.astype(BF16), wo_ref[...], preferred_element_type=F32)
    o_ref[...] = x + g1_ref[...] * y.reshape(bb, tt, D_MODEL)


def _k_merge(x, sc, sh, g1, ng, oa, ob, oc, od, wg_bf, bg, wb_bf, wo_bf, *, bb, tt):
    B, T, D = x.shape
    nt = T // tt
    tm = bb * tt
    xs = pl.BlockSpec((bb, tt, D), lambda i: (i // nt, i % nt, 0))
    ms = pl.BlockSpec((bb, 1, D), lambda i: (i // nt, 0, 0))
    os_ = pl.BlockSpec((tm, 512), lambda i: (i, 0))
    return pl.pallas_call(
        functools.partial(_merge_body, bb=bb, tt=tt),
        grid=((B // bb) * nt,),
        in_specs=[xs, ms, ms, ms, _const_spec((1, 1, D)), os_, os_, os_, os_,
                  _const_spec((D, 4 * D)), _const_spec((1, 4 * D)), _const_spec((4 * BRANCH_W, D)),
                  _const_spec((D, D))],
        out_specs=xs,
        out_shape=jax.ShapeDtypeStruct((B, T, D), F32),
        compiler_params=_cparams(("arbitrary",)),
        name="merge_out",
    )(x, sc, sh, g1, ng, oa, ob, oc, od, wg_bf, bg, wb_bf, wo_bf)


FC = FFN_DIM // 2
MOE_RB = 256
MOE_RS = 64


def _swiglu_chunk(h, w1, w3, w2):
    a = jnp.dot(h, w1, preferred_element_type=F32)
    b = jnp.dot(h, w3, preferred_element_type=F32)
    act = (a * _sigmoid(a) * b).astype(BF16)
    return jnp.dot(act, w2, preferred_element_type=F32)


def _finish(x, g2, f, fg_ref, o_ref, bb, tt):
    y = x + g2 * f.reshape(bb, tt, D_MODEL)
    if fg_ref is not None:
        y = _rms(y, fg_ref[...])
    o_ref[...] = y


def _ffn_body(x_ref, sc_ref, sh_ref, g2_ref, ng_ref, w1_ref, w3_ref, w2_ref, *rest, bb, tt, final):
    fg_ref, o_ref = rest if final else (None, rest[0])
    x = x_ref[...]
    tm = bb * tt
    h = (_rms(x, ng_ref[...]) * (1.0 + sc_ref[...]) + sh_ref[...]).reshape(tm, D_MODEL).astype(BF16)
    f = jnp.zeros((tm, D_MODEL), F32)
    for ci in range(FFN_DIM // FC):
        f = f + _swiglu_chunk(h, w1_ref[:, ci * FC:(ci + 1) * FC], w3_ref[:, ci * FC:(ci + 1) * FC],
                              w2_ref[ci * FC:(ci + 1) * FC, :])
    _finish(x, g2_ref[...], f, fg_ref, o_ref, bb, tt)


def _k_ffn(x, sc, sh, g2, ng, w1_bf, w3_bf, w2_bf, fg, *, bb, tt):
    B, T, D = x.shape
    nt = T // tt
    final = fg is not None
    xs = pl.BlockSpec((bb, tt, D), lambda i: (i // nt, i % nt, 0))
    ms = pl.BlockSpec((bb, 1, D), lambda i: (i // nt, 0, 0))
    in_specs = [xs, ms, ms, ms, _const_spec((1, 1, D)), _const_spec((D, FFN_DIM)),
                _const_spec((D, FFN_DIM)), _const_spec((FFN_DIM, D))]
    args = [x, sc, sh, g2, ng, w1_bf, w3_bf, w2_bf]
    if final:
        in_specs.append(_const_spec((1, 1, D)))
        args.append(fg)
    return pl.pallas_call(
        functools.partial(_ffn_body, bb=bb, tt=tt, final=final),
        grid=((B // bb) * nt,),
        in_specs=in_specs,
        out_specs=xs,
        out_shape=jax.ShapeDtypeStruct((B, T, D), F32),
        compiler_params=_cparams(("arbitrary",)),
        name="ffn_dense",
    )(*args)


def _moe_body(x_ref, sc_ref, sh_ref, g2_ref, ng_ref, rw_ref, rb_ref, w1_ref, w3_ref, w2_ref, *rest,
              bb, tt, final):
    if final:
        fg_ref, o_ref, h_scr, gate_scr, rank_scr, rankt_scr, xg_scr, y_scr, acc_scr = rest
    else:
        fg_ref = None
        o_ref, h_scr, gate_scr, rank_scr, rankt_scr, xg_scr, y_scr, acc_scr = rest
    e = pl.program_id(1)
    ci = pl.program_id(2)
    last_c = FFN_DIM // FC - 1
    tm = bb * tt
    lane = lax.broadcasted_iota(I32, (tm, 128), 1)

    @pl.when((e == 0) & (ci == 0))
    def _():
        h = (_rms(x_ref[...], ng_ref[...]) * (1.0 + sc_ref[...]) + sh_ref[...]).reshape(tm, D_MODEL)
        h_scr[...] = h.astype(BF16)
        logits = jnp.dot(h, rw_ref[...], preferred_element_type=F32,
                         precision=lax.Precision.HIGHEST) + rb_ref[...]
        m1 = jnp.max(logits, axis=-1, keepdims=True)
        i1 = jnp.min(jnp.where(logits == m1, lane, 128), axis=-1, keepdims=True)
        rest_l = jnp.where(lane == i1, NEG, logits)
        m2 = jnp.max(rest_l, axis=-1, keepdims=True)
        i2 = jnp.min(jnp.where(rest_l == m2, lane, 128), axis=-1, keepdims=True)
        e2 = jnp.exp(m2 - m1)
        den = 1.0 + e2
        gate_scr[...] = jnp.where(lane == i1, 1.0 / den, jnp.where(lane == i2, e2 / den, 0.0))
        sel = jnp.where(lane == i1, 1.0, jnp.where(lane == i2, 1.0, 0.0))
        tri = jnp.where(lax.broadcasted_iota(I32, (tm, tm), 1) < lax.broadcasted_iota(I32, (tm, tm), 0),
                        1.0, 0.0).astype(BF16)
        rank = jnp.where(sel > 0.5, jnp.dot(tri, sel.astype(BF16), preferred_element_type=F32), -1.0)
        rank_scr[...] = rank
        rankt_scr[...] = rank.T
        acc_scr[...] = jnp.zeros((tm, D_MODEL), F32)

    is_e = lane == e
    gate_col = jnp.sum(jnp.where(is_e, gate_scr[...], 0.0), axis=-1, keepdims=True)
    rank_col = jnp.sum(jnp.where(is_e, rank_scr[...], 0.0), axis=-1, keepdims=True)
    rank_row = rankt_scr[pl.ds(e, 1), :]
    n_e = jnp.max(rank_row).astype(I32) + 1

    def block(row0, rows):
        rsl = pl.ds(pl.multiple_of(row0, MOE_RS), rows)
        base = row0.astype(F32)

        @pl.when(ci == 0)
        def _():
            slot = base + lax.broadcasted_iota(I32, (rows, 1), 0).astype(F32)
            pick = jnp.where(rank_row == slot, 1.0, 0.0).astype(BF16)
            xg_scr[rsl, :] = jnp.dot(pick, h_scr[...], preferred_element_type=F32).astype(BF16)

        part = _swiglu_chunk(xg_scr[rsl, :], w1_ref[0], w3_ref[0], w2_ref[0])

        @pl.when(ci == 0)
        def _():
            y_scr[rsl, :] = part

        @pl.when((ci > 0) & (ci < last_c))
        def _():
            y_scr[rsl, :] += part

        @pl.when(ci == last_c)
        def _():
            y = y_scr[rsl, :] + part
            y_hi = y.astype(BF16)
            y_lo = (y - y_hi.astype(F32)).astype(BF16)
            slot = base + lax.broadcasted_iota(I32, (1, rows), 1).astype(F32)
            put = jnp.where(rank_col == slot, 1.0, 0.0).astype(BF16)
            acc_scr[...] += gate_col * (jnp.dot(put, y_hi, preferred_element_type=F32)
                                        + jnp.dot(put, y_lo, preferred_element_type=F32))

    rb_main = min(MOE_RB, tm)

    def main_step(b, carry):
        block(b * rb_main, rb_main)
        return carry

    def spill_step(b, carry):
        block(rb_main + b * MOE_RS, MOE_RS)
        return carry

    lax.fori_loop(0, jnp.minimum(n_e, 1), main_step, 0)
    lax.fori_loop(0, (jnp.maximum(n_e - rb_main, 0) + MOE_RS - 1) // MOE_RS, spill_step, 0)

    @pl.when((e == N_EXPERTS - 1) & (ci == last_c))
    def _():
        _finish(x_ref[...], g2_ref[...], acc_scr[...], fg_ref, o_ref, bb, tt)


def _k_moe(x, sc, sh, g2, ng, rw_pad, rb_pad, w1_bf, w3_bf, w2_bf, fg, *, bb, tt):
    B, T, D = x.shape
    nt = T // tt
    tm = bb * tt
    final = fg is not None
    xs = pl.BlockSpec((bb, tt, D), lambda i, e, c: (i // nt, i % nt, 0))
    xin = pl.BlockSpec((bb, tt, D), lambda i, e, c: (i // nt, i % nt, 0), pipeline_mode=pl.Buffered(1))
    ms = pl.BlockSpec((bb, 1, D), lambda i, e, c: (i // nt, 0, 0))
    in_specs = [xin, ms, ms, ms, _const_spec((1, 1, D)), _const_spec((D, 128)), _const_spec((1, 128)),
                pl.BlockSpec((1, D, FC), lambda i, e, c: (e, 0, c)),
                pl.BlockSpec((1, D, FC), lambda i, e, c: (e, 0, c)),
                pl.BlockSpec((1, FC, D), lambda i, e, c: (e, c, 0))]
    args = [x, sc, sh, g2, ng, rw_pad, rb_pad, w1_bf, w3_bf, w2_bf]
    if final:
        in_specs.append(_const_spec((1, 1, D)))
        args.append(fg)
    return pl.pallas_call(
        functools.partial(_moe_body, bb=bb, tt=tt, final=final),
        grid=((B // bb) * nt, N_EXPERTS, FFN_DIM // FC),
        in_specs=in_specs,
        out_specs=xs,
        out_shape=jax.ShapeDtypeStruct((B, T, D), F32),
        scratch_shapes=[pltpu.VMEM((tm, D), BF16), pltpu.VMEM((tm, 128), F32), pltpu.VMEM((tm, 128), F32),
                        pltpu.VMEM((128, tm), F32), pltpu.VMEM((tm, D), BF16),
                        pltpu.VMEM((tm, D), F32), pltpu.VMEM((tm, D), F32)],
        compiler_params=_cparams(("arbitrary", "arbitrary", "arbitrary")),
        name="ffn_moe",
    )(*args)


def _cast_body(x_ref, o_ref):
    o_ref[...] = x_ref[...].astype(BF16)


def _cast_bf16(w, start=0, count=None):
    _, R, C = w.shape
    E = w.shape[0] if count is None else count
    rb = 256
    return pl.pallas_call(
        _cast_body,
        grid=(E, R // rb),
        in_specs=[pl.BlockSpec((1, rb, C), lambda e, r: (start + e, r, 0))],
        out_specs=pl.BlockSpec((1, rb, C), lambda e, r: (e, r, 0)),
        out_shape=jax.ShapeDtypeStruct((E, R, C), BF16),
        compiler_params=_cparams(("arbitrary", "arbitrary")),
        name="cast_bf16",
    )(w)


def _dsa_keys_body(ck_ref, cv_ref, ci_ref, nk_ref, nv_ref, ni_ref, kb_ref, vt_ref, ikb_ref, *, nc, kc_rows):
    cached = pl.program_id(1) < nc
    k = jnp.where(cached, ck_ref[0], nk_ref[0].astype(BF16))
    v = jnp.where(cached, cv_ref[0], nv_ref[0].astype(BF16))
    ik = jnp.where(cached, ci_ref[0], ni_ref[0])
    kb_ref[0] = k
    vt = v.astype(F32).T.astype(BF16)
    ones = jnp.ones((DSA_VR - DSA_HEAD_DIM, kc_rows), BF16)
    for h in range(DSA_HEADS):
        vt_ref[0, 0, h * DSA_VR:h * DSA_VR + DSA_HEAD_DIM, :] = vt[h * DSA_HEAD_DIM:(h + 1) * DSA_HEAD_DIM]
        vt_ref[0, 0, h * DSA_VR + DSA_HEAD_DIM:(h + 1) * DSA_VR, :] = ones
    ikb_ref[0, :, 0:IDX_DIM] = ik.astype(BF16)
    ikb_ref[0, :, IDX_DIM:128] = jnp.zeros((kc_rows, 128 - IDX_DIM), BF16)


def _k_dsa_keys(ck, cv, ci, nk, nv, ni, *, KC, row0):
    B = nk.shape[0]
    P = ck.shape[1]
    assert P % KC == 0
    nc = P // KC
    cblk = lambda w: pl.BlockSpec((1, KC, w), lambda b, c: (row0 + b, jnp.minimum(c, nc - 1), 0))
    nblk = lambda w: pl.BlockSpec((1, KC, w), lambda b, c: (b, 0, 0))
    return pl.pallas_call(
        functools.partial(_dsa_keys_body, nc=nc, kc_rows=KC),
        grid=(B, nc + 1),
        in_specs=[cblk(512), cblk(512), cblk(IDX_DIM), nblk(512), nblk(512), nblk(IDX_DIM)],
        out_specs=[pl.BlockSpec((1, KC, 512), lambda b, c: (b, c, 0)),
                   pl.BlockSpec((1, 1, DSA_HEADS * DSA_VR, KC), lambda b, c: (b, c, 0, 0)),
                   pl.BlockSpec((1, KC, 128), lambda b, c: (b, c, 0))],
        out_shape=[jax.ShapeDtypeStruct((B, P + KC, 512), BF16),
                   jax.ShapeDtypeStruct((B, nc + 1, DSA_HEADS * DSA_VR, KC), BF16),
                   jax.ShapeDtypeStruct((B, P + KC, 128), BF16)],
        compiler_params=_cparams(("arbitrary", "arbitrary")),
        name="dsa_keys",
    )(ck, cv, ci, nk, nv, ni)


def _mla_keys_body(cc_ref, cr_ref, nc_ref, nr_ref, cb_ref, ct_ref, rb_ref, *, nc, kc_rows):
    cached = pl.program_id(1) < nc
    ckv = jnp.where(cached, cc_ref[0], nc_ref[0])
    kr = jnp.where(cached, cr_ref[0], nr_ref[0])
    cb_ref[0] = ckv.astype(BF16)
    ct_ref[0, 0, 0:MLA_KV_LORA, :] = ckv.T.astype(BF16)
    ct_ref[0, 0, MLA_KV_LORA:MLA_VR, :] = jnp.ones((MLA_VR - MLA_KV_LORA, kc_rows), BF16)
    rb_ref[0] = kr.astype(BF16)


def _k_mla_keys(cc, cr, ncv, nr, *, KC, row0):
    B = ncv.shape[0]
    P = cc.shape[1]
    assert P % KC == 0
    nc = P // KC
    cblk = lambda w: pl.BlockSpec((1, KC, w), lambda b, c: (row0 + b, jnp.minimum(c, nc - 1), 0))
    nblk = lambda w: pl.BlockSpec((1, KC, w), lambda b, c: (b, 0, 0))
    return pl.pallas_call(
        functools.partial(_mla_keys_body, nc=nc, kc_rows=KC),
        grid=(B, nc + 1),
        in_specs=[cblk(MLA_KV_LORA), cblk(MLA_ROPE), nblk(MLA_KV_LORA), nblk(MLA_ROPE)],
        out_specs=[pl.BlockSpec((1, KC, MLA_KV_LORA), lambda b, c: (b, c, 0)),
                   pl.BlockSpec((1, 1, MLA_VR, KC), lambda b, c: (b, c, 0, 0)),
                   pl.BlockSpec((1, KC, MLA_ROPE), lambda b, c: (b, c, 0))],
        out_shape=[jax.ShapeDtypeStruct((B, P + KC, MLA_KV_LORA), BF16),
                   jax.ShapeDtypeStruct((B, nc + 1, MLA_VR, KC), BF16),
                   jax.ShapeDtypeStruct((B, P + KC, MLA_ROPE), BF16)],
        compiler_params=_cparams(("arbitrary", "arbitrary")),
        name="mla_keys",
    )(cc, cr, ncv, nr)


def _prep_w_in(w):
    aq, ak, av, iq, ik, iw, pu, cb, cc, ch, mq, mkv, mkr = jnp.split(w, IN_SPLIT_POINTS, axis=1)
    d = w.shape[0]
    mkr_sw = jnp.concatenate([mkr[:, 16:], mkr[:, :16]], axis=1)
    return jnp.concatenate([aq, ak, av, pu, cb, cc, ch, iq, mq, mkv, ik, iw, jnp.zeros((d, 60), w.dtype),
                            mkr, mkr_sw, jnp.zeros((d, 64), w.dtype)], axis=1).astype(BF16)


def _prep_w_uq(w):
    w3 = w.reshape(MLA_Q_LORA, MLA_HEADS, MLA_NOPE + MLA_ROPE)
    nope = w3[:, :, :MLA_NOPE].reshape(MLA_Q_LORA, MLA_HEADS * MLA_NOPE)
    rope = w3[:, :, MLA_NOPE:]
    rope_sw = jnp.concatenate([rope[:, :, 16:], rope[:, :, :16]], axis=-1)
    return jnp.concatenate([nope, rope.reshape(MLA_Q_LORA, -1), rope_sw.reshape(MLA_Q_LORA, -1)],
                           axis=1).astype(BF16)


def _rope_tables(pos):
    half = MLA_ROPE // 2
    inv = 1.0 / (ROPE_BASE ** (jnp.arange(half, dtype=F32) / half))
    ang = pos.astype(F32)[:, None] * inv[None, :]
    cos, sin = jnp.cos(ang), jnp.sin(ang)
    cos2 = jnp.tile(jnp.concatenate([cos, cos], axis=-1), (1, MLA_HEADS))
    sin2 = jnp.tile(jnp.concatenate([-sin, sin], axis=-1), (1, MLA_HEADS))
    return cos2, sin2


def _layer(x, mod, past, lw, fw, use_moe, final_g, cfg):
    B, T, D = x.shape
    bb, tt, KC, tq_dsa, tq_mla = cfg["bb"], cfg["tt"], cfg["KC"], cfg["tq_dsa"], cfg["tq_mla"]
    KCD = cfg["KCD"]
    N = B * T
    P = 0 if past is None else past["dsa_k"].shape[1]
    L = P + T
    sh1, sc1, g1, sh2, sc2, g2 = [m.reshape(B, 1, D) for m in jnp.split(mod, 6, axis=-1)]
    ng1 = lw["norm_mix_g"].reshape(1, 1, D)
    ng2 = lw["norm_ffn_g"].reshape(1, 1, D)

    emit = past is None
    res = _k_in(x, sc1, sh1, ng1, lw["w_in"], bb=bb, tt=tt, emit=emit, kc=KCD)
    z = res[0]
    zoff = Z_OFF if emit else 0
    z3 = z.reshape(B, T, ZW - zoff)
    if emit:
        k_new = res[1].reshape(B, T, DSA_HEADS, DSA_HEAD_DIM)
        v_new = res[2].reshape(B, T, DSA_HEADS, DSA_HEAD_DIM)
    else:
        k_new = z[:, C_AK:C_AK + 512].reshape(B, T, DSA_HEADS, DSA_HEAD_DIM)
        v_new = z[:, C_AV:C_AV + 512].reshape(B, T, DSA_HEADS, DSA_HEAD_DIM)
    ik_new = z3[:, :, C_IKW - zoff:C_IKW - zoff + IDX_DIM]

    if past is None:
        ph16 = jnp.zeros((B, 16, 512), F32)
        cvh16 = ph16
    else:
        ph16 = jnp.pad(past["pool"], ((0, 0), (1, 0), (0, 0)))
        cvh16 = jnp.pad(past["conv"], ((0, 0), (16 - (CONV_WIDTH - 1), 0), (0, 0)))
    o_b, o_c, ph_o, cvh_o = _k_pc(z3, ph16, cvh16, lw["pool_w"], lw["pool_scale"], lw["conv_w"],
                                  bb=bb, tt=tt, P=P, zoff=zoff)
    pool_hist = ph_o[:, 1:]
    conv_hist = cvh_o[:, 16 - (CONV_WIDTH - 1):]

    cos2, sin2 = _rope_tables(P + jnp.arange(T))
    if bb > 1:
        cos2, sin2 = jnp.tile(cos2, (bb, 1)), jnp.tile(sin2, (bb, 1))
    qlt, qrt, ckv_new, ckvb, ckvt, kr_new, krb = _k_mlaprep(
        z, cos2, sin2, lw["mla_q_norm_g"], lw["mla_kv_norm_g"], lw["mla_w_uq"], lw["mla_w_uk"], tm=bb * tt,
        zoff=zoff)

    topk = min(DSA_TOPK_MAX, L // 4)
    if past is None:
        kb3 = res[3].reshape(B, T, 512)
        vt4, qt, iqt, iwt = res[4], res[5], res[6], res[7]
        ikb3 = res[8].reshape(B, T, 128)
        ckvb3 = ckvb.reshape(B, T, 256)
        krb3 = krb.reshape(B, T, 32)
        Tq = T
    else:
        assert T <= KCD and T <= KC
        padn = lambda a, kc: jnp.pad(a, ((0, 0), (0, kc - T), (0, 0)))
        kb3, vt4, ikb3 = _k_dsa_keys(past["dsa_k"], past["dsa_v"], past["idx_k"],
                                     padn(k_new.reshape(B, T, 512), KCD), padn(v_new.reshape(B, T, 512), KCD),
                                     padn(ik_new, KCD), KC=KCD, row0=past["row0"])
        Tq = tq_dsa
        padq = ((0, 0), (0, 0), (0, Tq - T))
        qt = jnp.pad((z3[:, :, C_AQ:C_AQ + 512] * (DSA_SCALE * LOG2E)).transpose(0, 2, 1).astype(BF16), padq)
        iqt = jnp.pad(z3[:, :, C_IQ:C_IQ + 256].transpose(0, 2, 1).astype(BF16), padq)
        iwt = jnp.pad(z3[:, :, C_IKW + 64:C_IKW + 72].transpose(0, 2, 1) * IDX_SCALE, padq)
        ckvb3, ckvt4, krb3 = _k_mla_keys(past["ckv"], past["krope"], padn(ckv_new.reshape(B, T, 256), KC),
                                         padn(kr_new.reshape(B, T, 32), KC), KC=KC, row0=past["row0"])

    o_a = _k_dsa(qt, iqt, iwt, kb3, vt4, ikb3, tq=tq_dsa, P=P, L=L, KC=KCD, topk=topk)
    o_a = o_a[:, :T].reshape(N, 512)
    if past is None:
        ckvt4 = ckvt.reshape(B, T // KC, MLA_VR, KC)
        o_d = _k_mla(qlt, qrt, ckvb3, krb3, ckvt4, lw["mla_w_uv"], B=B, T=T, tq=tq_mla, P=P, L=L, KC=KC,
                     stacked=False).reshape(N, 512)
    else:
        qs = qlt.reshape(MLA_HEADS, 256, B, T).transpose(2, 1, 0, 3).reshape(B, 256, MLA_HEADS * T)
        rs = qrt.reshape(MLA_HEADS, 32, B, T).transpose(2, 1, 0, 3).reshape(B, 32, MLA_HEADS * T)
        o5 = _k_mla(qs, rs, ckvb3, krb3, ckvt4, lw["mla_w_uv"], B=B, T=T, tq=tq_mla, P=P, L=L, KC=KC,
                    stacked=True).reshape(B, MLA_HEADS, MLA_V, MLA_HEADS, T)
        o_d = jnp.stack([o5[:, h, :, h, :] for h in range(MLA_HEADS)], axis=1)
        o_d = o_d.transpose(0, 3, 1, 2).reshape(N, 512)

    x1 = _k_merge(x, sc1, sh1, g1, ng1, o_a, o_b.reshape(N, 512), o_c.reshape(N, 512), o_d,
                  lw["w_gate"], lw["b_gate"], lw["w_branch"], lw["w_out"], bb=bb, tt=tt)
    fg = None if final_g is None else final_g.reshape(1, 1, D)
    if use_moe:
        x2 = _k_moe(x1, sc2, sh2, g2, ng2, fw["rw"], fw["rb"], fw["w1"], fw["w3"], fw["w2"], fg,
                    bb=cfg["bb_moe"], tt=cfg["tt_moe"])
    else:
        x2 = _k_ffn(x1, sc2, sh2, g2, ng2, fw["w1"], fw["w3"], fw["w2"], fg, bb=bb, tt=tt)
    rows = (k_new, v_new, ik_new, ckv_new.reshape(B, T, 256), kr_new.reshape(B, T, 32), pool_hist, conv_hist)
    return x2, rows


def _prep_layer(l, ada_w, ada_b, norm_mix_g, norm_ffn_g, w_in, mla_q_norm_g, mla_kv_norm_g,
                mla_w_uq, mla_w_uk, mla_w_uv, pool_w, pool_scale, conv_w, w_gate, b_gate, w_branch, w_out):
    return dict(
        ada_w=_cast_bf16(ada_w, l, 1)[0], ada_b=ada_b[l].reshape(1, -1),
        norm_mix_g=norm_mix_g[l], norm_ffn_g=norm_ffn_g[l],
        w_in=_prep_w_in(w_in[l]),
        mla_q_norm_g=mla_q_norm_g[l].reshape(1, -1), mla_kv_norm_g=mla_kv_norm_g[l].reshape(1, -1),
        mla_w_uq=_prep_w_uq(mla_w_uq[l]),
        mla_w_uk=mla_w_uk[l].transpose(1, 0, 2).astype(BF16),
        mla_w_uv=mla_w_uv[l].transpose(1, 2, 0).astype(BF16),
        pool_w=pool_w[l].astype(BF16), pool_scale=pool_scale[l].reshape(1, -1),
        conv_w=jnp.pad(conv_w[l].reshape(CONV_WIDTH, BRANCH_W), ((0, 8 - CONV_WIDTH), (0, 0))),
        w_gate=_cast_bf16(w_gate, l, 1)[0], b_gate=b_gate[l].reshape(1, -1),
        w_branch=_cast_bf16(w_branch, l, 1)[0], w_out=_cast_bf16(w_out, l, 1)[0])


def _cfg_for(B, T, has_past):
    if has_past:
        return dict(bb=B, tt=T, KC=512, KCD=512, tq_dsa=128, tq_mla=T, bb_moe=B, tt_moe=T)
    t = min(512, T)
    return dict(bb=1, tt=t, KC=t, KCD=t, tq_dsa=t, tq_mla=t, bb_moe=1, tt_moe=min(1024, T))


def _run(x, c, pasts, lws, fws, final_norm_g):
    B, T, _ = x.shape
    cfg = _cfg_for(B, T, pasts is not None)
    outs = [[] for _ in range(7)]
    depth = len(lws)
    for l in range(depth):
        lw = lws[l]
        mod = _ada(c, lw["ada_w"], lw["ada_b"])
        x, rows = _layer(x, mod, None if pasts is None else pasts[l], lw, fws[l], l % 2 == 1,
                         final_norm_g if l == depth - 1 else None, cfg)
        for o, r in zip(outs, rows):
            o.append(r)
    return x, [jnp.stack(o) for o in outs]


def kernel(x_prompt, x_sample, c_prompt, c_sample, cache_dsa_k, cache_dsa_v, cache_dsa_idx_k, cache_mla_ckv, cache_mla_krope, state_pool, state_conv, ada_w, ada_b, norm_mix_g, norm_ffn_g, w_in, mla_q_norm_g, mla_kv_norm_g, mla_w_uq, mla_w_uk, mla_w_uv, pool_w, pool_scale, conv_w, w_gate, b_gate, w_branch, w_out, ffn_w1, ffn_w3, ffn_w2, moe_router_w, moe_router_b, moe_w1, moe_w3, moe_w2, final_norm_g):
    depth = ada_w.shape[0]
    lws = [_prep_layer(l, ada_w, ada_b, norm_mix_g, norm_ffn_g, w_in, mla_q_norm_g, mla_kv_norm_g,
                       mla_w_uq, mla_w_uk, mla_w_uv, pool_w, pool_scale, conv_w, w_gate, b_gate,
                       w_branch, w_out) for l in range(depth)]
    fws = []
    for l in range(depth):
        j = l // 2
        if l % 2 == 0:
            fws.append(dict(w1=_cast_bf16(ffn_w1, j, 1)[0], w3=_cast_bf16(ffn_w3, j, 1)[0],
                            w2=_cast_bf16(ffn_w2, j, 1)[0]))
        else:
            fws.append(dict(
                rw=jnp.pad(moe_router_w[j], ((0, 0), (0, 128 - N_EXPERTS))),
                rb=jnp.pad(moe_router_b[j].reshape(1, -1), ((0, 0), (0, 128 - N_EXPERTS)), constant_values=NEG),
                w1=_cast_bf16(moe_w1.reshape((-1,) + moe_w1.shape[2:]), j * N_EXPERTS, N_EXPERTS),
                w3=_cast_bf16(moe_w3.reshape((-1,) + moe_w3.shape[2:]), j * N_EXPERTS, N_EXPERTS),
                w2=_cast_bf16(moe_w2.reshape((-1,) + moe_w2.shape[2:]), j * N_EXPERTS, N_EXPERTS)))
    nb, plen = cache_dsa_k.shape[1], cache_dsa_k.shape[2]
    flat = lambda a: a.reshape(depth * nb, plen, -1)
    kc_bf, vc_bf = flat(cache_dsa_k.astype(BF16)), flat(cache_dsa_v.astype(BF16))
    sample_pasts = [dict(dsa_k=kc_bf, dsa_v=vc_bf, idx_k=flat(cache_dsa_idx_k),
                         ckv=flat(cache_mla_ckv), krope=flat(cache_mla_krope), row0=l * nb,
                         pool=state_pool[l], conv=state_conv[l]) for l in range(depth)]
    y_prompt, pn = _run(x_prompt, c_prompt, None, lws, fws, final_norm_g)
    y_sample, sn = _run(x_sample, c_sample, sample_pasts, lws, fws, final_norm_g)
    return (y_prompt, y_sample, pn[0], pn[1], pn[2], pn[3], pn[4], pn[5], pn[6],
            sn[0], sn[1], sn[2], sn[3], sn[4], sn[5], sn[6])
```

```python
import functools

import numpy as np
import jax
import jax.numpy as jnp
from jax import lax
from jax.experimental import pallas as pl
from jax.experimental.pallas import tpu as pltpu

F32 = jnp.float32
BF16 = jnp.bfloat16
I32 = jnp.int32

D_MODEL = 1024
CHUNK = 64
N_BRANCH = 4
BRANCH_W = 512
DSA_HEADS = 8
DSA_HEAD_DIM = 64
IDX_HEADS = 4
IDX_DIM = 64
DSA_TOPK_MAX = 256
POOL_WINDOWS = (2, 4, 8, 16)
POOL_GROUP = BRANCH_W // 4
POOL_HIST = 15
CONV_WIDTH = 3
MLA_HEADS = 8
MLA_Q_LORA = 256
MLA_KV_LORA = 256
MLA_NOPE = 64
MLA_ROPE = 32
MLA_V = 64
ROPE_BASE = 10000.0
FFN_DIM = 2816
N_EXPERTS = 8
EPS = 1e-6

DSA_SCALE = DSA_HEAD_DIM ** -0.5
IDX_SCALE = (IDX_HEADS * IDX_DIM) ** -0.5
MLA_SCALE = (MLA_NOPE + MLA_ROPE) ** -0.5
LOG2E = 1.4426950408889634

IN_SPLIT_WIDTHS = (512, 512, 512, 256, 64, 4, 512, 512, 512, 512, 256, 256, 32)
IN_SPLIT_POINTS = tuple(int(sum(IN_SPLIT_WIDTHS[:i + 1])) for i in range(len(IN_SPLIT_WIDTHS) - 1))

C_AQ, C_AK, C_AV, C_PU, C_CB, C_CC, C_CH = 0, 512, 1024, 1536, 2048, 2560, 3072
C_IQ, C_MQ, C_MKV, C_IKW, C_MKR, ZW = 3584, 3840, 4096, 4352, 4480, 4608
IN_TN = 1536
Z_OFF = IN_TN
DSA_VR = DSA_HEAD_DIM + 16
MLA_VR = MLA_KV_LORA + 16

NEG = -1e30
M_INIT = -1e29
INT_MIN = np.int32(-2 ** 31)
V7X_VMEM_LIMIT = 56 * 1024 * 1024


def _cparams(sem):
    return pltpu.CompilerParams(dimension_semantics=sem, vmem_limit_bytes=V7X_VMEM_LIMIT)


def _rms(x, g):
    return x * lax.rsqrt(jnp.mean(x * x, axis=-1, keepdims=True) + EPS) * g


def _sigmoid(x):
    return 1.0 / (1.0 + jnp.exp(-x))


def _const_spec(shape):
    nd = len(shape)
    return pl.BlockSpec(shape, lambda *_: (0,) * nd, pipeline_mode=pl.Buffered(1))


def _ada_body(c_ref, w_ref, b_ref, o_ref):
    c = c_ref[...]
    s = (c * _sigmoid(c)).astype(BF16)
    o_ref[...] = jnp.dot(s, w_ref[...], preferred_element_type=F32) + b_ref[...]


def _ada(c, w_bf, b):
    B, D = c.shape
    n = w_bf.shape[1]
    tn = 1536
    return pl.pallas_call(
        _ada_body,
        grid=(n // tn,),
        in_specs=[pl.BlockSpec((B, D), lambda j: (0, 0)),
                  pl.BlockSpec((D, tn), lambda j: (0, j)),
                  pl.BlockSpec((1, tn), lambda j: (0, j))],
        out_specs=pl.BlockSpec((B, tn), lambda j: (0, j)),
        out_shape=jax.ShapeDtypeStruct((B, n), F32),
        compiler_params=_cparams(("arbitrary",)),
        name="ada_mod",
    )(c, w_bf, b)


def _in_body(x_ref, sc_ref, sh_ref, g_ref, w_ref, z_ref, *rest, bb, tt, emit, kc):
    if emit:
        kn_ref, vn_ref, kb_ref, vt_ref, qt_ref, iqt_ref, iwt_ref, ikb_ref, h_scr = rest
    else:
        (h_scr,) = rest
    j = pl.program_id(1)

    @pl.when(j == 0)
    def _():
        h = _rms(x_ref[...], g_ref[...]) * (1.0 + sc_ref[...]) + sh_ref[...]
        h_scr[...] = h.reshape(bb * tt, D_MODEL).astype(BF16)

    z = jnp.dot(h_scr[...], w_ref[...], preferred_element_type=F32)

    if not emit:
        z_ref[...] = z
    else:
        @pl.when(j > 0)
        def _():
            z_ref[...] = z

        @pl.when(j == 0)
        def _():
            k = z[:, C_AK:C_AK + 512]
            v = z[:, C_AV:C_AV + 512]
            kn_ref[...] = k
            vn_ref[...] = v
            kb_ref[...] = k.astype(BF16)
            vt = v.T.astype(BF16)
            ones = jnp.ones((DSA_VR - DSA_HEAD_DIM, kc), BF16)
            for c in range(tt // kc):
                for h in range(DSA_HEADS):
                    vt_ref[0, c, h * DSA_VR:h * DSA_VR + DSA_HEAD_DIM, :] = vt[
                        h * DSA_HEAD_DIM:(h + 1) * DSA_HEAD_DIM, c * kc:(c + 1) * kc]
                    vt_ref[0, c, h * DSA_VR + DSA_HEAD_DIM:(h + 1) * DSA_VR, :] = ones
            qt_ref[0] = (z[:, C_AQ:C_AQ + 512] * (DSA_SCALE * LOG2E)).T.astype(BF16)

        @pl.when(j == 2)
        def _():
            o = 2 * IN_TN
            iqt_ref[0] = z[:, C_IQ - o:C_IQ - o + 256].T.astype(BF16)
            ikw = z[:, C_IKW - o:C_IKW - o + 128]
            ikb_ref[...] = ikw.astype(BF16)
            iwt_ref[0] = ikw.T[64:72, :] * IDX_SCALE


def _k_in(x, sc, sh, g, w_bf, *, bb, tt, emit, kc):
    B, T, D = x.shape
    nt = T // tt
    tm = bb * tt
    n_rows = (B // bb) * nt
    N = B * T
    xmap = lambda i, j: (i // nt, i % nt, 0)
    mmap = lambda i, j: (i // nt, 0, 0)
    in_specs = [pl.BlockSpec((bb, tt, D), xmap),
                pl.BlockSpec((bb, 1, D), mmap),
                pl.BlockSpec((bb, 1, D), mmap),
                _const_spec((1, 1, D)),
                pl.BlockSpec((D, IN_TN), lambda i, j: (0, j))]
    if emit:
        assert bb == 1
        out_specs = [pl.BlockSpec((tm, IN_TN), lambda i, j: (i, jnp.maximum(j - 1, 0)))]
        out_shape = [jax.ShapeDtypeStruct((N, ZW - Z_OFF), F32)]
        out_specs += [
            pl.BlockSpec((tm, 512), lambda i, j: (i, 0)),
            pl.BlockSpec((tm, 512), lambda i, j: (i, 0)),
            pl.BlockSpec((tm, 512), lambda i, j: (i, 0)),
            pl.BlockSpec((1, tt // kc, DSA_HEADS * DSA_VR, kc), lambda i, j: (i // nt, i % nt, 0, 0)),
            pl.BlockSpec((1, 512, tt), lambda i, j: (i // nt, 0, i % nt)),
            pl.BlockSpec((1, 256, tt), lambda i, j: (i // nt, 0, i % nt)),
            pl.BlockSpec((1, 8, tt), lambda i, j: (i // nt, 0, i % nt)),
            pl.BlockSpec((tm, 128), lambda i, j: (i, 0)),
        ]
        out_shape += [
            jax.ShapeDtypeStruct((N, 512), F32),
            jax.ShapeDtypeStruct((N, 512), F32),
            jax.ShapeDtypeStruct((N, 512), BF16),
            jax.ShapeDtypeStruct((B, T // kc, DSA_HEADS * DSA_VR, kc), BF16),
            jax.ShapeDtypeStruct((B, 512, T), BF16),
            jax.ShapeDtypeStruct((B, 256, T), BF16),
            jax.ShapeDtypeStruct((B, 8, T), F32),
            jax.ShapeDtypeStruct((N, 128), BF16),
        ]
    else:
        out_specs = [pl.BlockSpec((tm, IN_TN), lambda i, j: (i, j))]
        out_shape = [jax.ShapeDtypeStruct((N, ZW), F32)]
    return pl.pallas_call(
        functools.partial(_in_body, bb=bb, tt=tt, emit=emit, kc=kc),
        grid=(n_rows, ZW // IN_TN),
        in_specs=in_specs,
        out_specs=out_specs,
        out_shape=out_shape,
        scratch_shapes=[pltpu.VMEM((tm, D), BF16)],
        compiler_params=_cparams(("arbitrary", "arbitrary")),
        name="in_proj",
    )(x, sc, sh, g, w_bf)


def _pc_body(pu_ref, cb_ref, cc_ref, ch_ref, pup_ref, ccp_ref, chp_ref, ph_ref, cvh_ref,
             pw_ref, ps_ref, cw_ref, ob_ref, oc_ref, pho_ref, cho_ref, u_scr, g_scr, *, bb, tt, P):
    t = pl.program_id(1)
    first = t == 0
    pu = pu_ref[...]
    g = cc_ref[...] * ch_ref[...]
    u_scr[:, 0:16, :] = jnp.where(first, ph_ref[...], pup_ref[...])
    u_scr[:, 16:16 + tt, :] = pu
    g_scr[:, 0:16, :] = jnp.where(first, cvh_ref[...], ccp_ref[...] * chp_ref[...])
    g_scr[:, 16:16 + tt, :] = g

    pos = P + t * tt + lax.broadcasted_iota(I32, (1, tt, 1), 1)
    for gi, win in enumerate(POOL_WINDOWS):
        lo = gi * POOL_GROUP
        acc = pu[:, :, lo:lo + POOL_GROUP]
        for j in range(1, win):
            acc = acc + u_scr[:, 16 - j:16 - j + tt, lo:lo + POOL_GROUP]
        cnt = jnp.minimum(pos + 1, win).astype(F32)
        d = acc / cnt - pu[:, :, lo:lo + POOL_GROUP]
        y = jnp.dot(d.reshape(bb * tt, POOL_GROUP).astype(BF16), pw_ref[gi], preferred_element_type=F32)
        ob_ref[:, :, lo:lo + POOL_GROUP] = (y * ps_ref[:, lo:lo + POOL_GROUP]).reshape(bb, tt, POOL_GROUP)

    cw = cw_ref[...]
    conv = (cw[0:1, :] * g_scr[:, 14:14 + tt, :] + cw[1:2, :] * g_scr[:, 15:15 + tt, :] + cw[2:3, :] * g)
    oc_ref[...] = cb_ref[...] * conv
    pho_ref[...] = u_scr[:, tt:tt + 16, :]
    cho_ref[...] = g_scr[:, tt:tt + 16, :]


def _k_pc(z3, ph16, cvh16, pw_bf, ps, cw, *, bb, tt, P, zoff):
    B, T, _ = z3.shape
    nt = T // tt
    r = tt // 16

    def cur(c):
        return pl.BlockSpec((bb, tt, 512), lambda b, t: (b, t, (c - zoff) // 512))

    def prev(c):
        return pl.BlockSpec((bb, 16, 512), lambda b, t: (b, jnp.maximum(t * r - 1, 0), (c - zoff) // 512))

    hist = pl.BlockSpec((bb, 16, 512), lambda b, t: (b, 0, 0))
    return pl.pallas_call(
        functools.partial(_pc_body, bb=bb, tt=tt, P=P),
        grid=(B // bb, nt),
        in_specs=[cur(C_PU), cur(C_CB), cur(C_CC), cur(C_CH), prev(C_PU), prev(C_CC), prev(C_CH),
                  hist, hist, _const_spec((4, POOL_GROUP, POOL_GROUP)), _const_spec((1, 512)),
                  _const_spec((8, 512))],
        out_specs=[pl.BlockSpec((bb, tt, 512), lambda b, t: (b, t, 0)),
                   pl.BlockSpec((bb, tt, 512), lambda b, t: (b, t, 0)), hist, hist],
        out_shape=[jax.ShapeDtypeStruct((B, T, 512), F32), jax.ShapeDtypeStruct((B, T, 512), F32),
                   jax.ShapeDtypeStruct((B, 16, 512), F32), jax.ShapeDtypeStruct((B, 16, 512), F32)],
        scratch_shapes=[pltpu.VMEM((bb, tt + 16, 512), F32), pltpu.VMEM((bb, tt + 16, 512), F32)],
        compiler_params=_cparams(("arbitrary", "arbitrary")),
        name="pool_conv",
    )(z3, z3, z3, z3, z3, z3, z3, ph16, cvh16, pw_bf, ps, cw)


def _mp_body(mq_ref, mkv_ref, mkr_ref, cos_ref, sin_ref, gq_ref, gkv_ref, wuq_ref, wukt_ref,
             qlt_ref, qrt_ref, ckv_ref, ckvb_ref, ckvt_ref, kr_ref, krb_ref):
    a = _rms(mq_ref[...], gq_ref[...]).astype(BF16)
    cq = jnp.dot(a, wuq_ref[...], preferred_element_type=F32)
    cqt = cq.T
    cos = cos_ref[...]
    sin = sin_ref[...]
    c = MLA_SCALE * LOG2E
    qrt_ref[...] = ((cqt[512:768] * cos.T + cqt[768:1024] * sin.T) * c).astype(BF16)
    for h in range(MLA_HEADS):
        qnt = cqt[h * MLA_NOPE:(h + 1) * MLA_NOPE].astype(BF16)
        qlt_ref[h * MLA_KV_LORA:(h + 1) * MLA_KV_LORA, :] = (jnp.dot(
            wukt_ref[h], qnt, preferred_element_type=F32) * c).astype(BF16)
    ckv = _rms(mkv_ref[...], gkv_ref[...])
    ckv_ref[...] = ckv
    ckvb_ref[...] = ckv.astype(BF16)
    ckvt_ref[0, 0:MLA_KV_LORA, :] = ckv.T.astype(BF16)
    ckvt_ref[0, MLA_KV_LORA:MLA_VR, :] = jnp.ones((MLA_VR - MLA_KV_LORA, ckv.shape[0]), BF16)
    mkr = mkr_ref[...]
    kr = mkr[:, 0:32] * cos[:, 0:32] + mkr[:, 32:64] * sin[:, 0:32]
    kr_ref[...] = kr
    krb_ref[...] = kr.astype(BF16)


def _k_mlaprep(z, cos_t, sin_t, gq, gkv, wuq_bf, wukt_bf, *, tm, zoff):
    N = z.shape[0]
    nr = cos_t.shape[0] // tm
    tab = pl.BlockSpec((tm, 256), lambda i: (i % nr, 0))
    row = lambda w: pl.BlockSpec((tm, w), lambda i: (i, 0))
    col = lambda h: pl.BlockSpec((h, tm), lambda i: (0, i))
    return pl.pallas_call(
        _mp_body,
        grid=(N // tm,),
        in_specs=[pl.BlockSpec((tm, 256), lambda i: (i, (C_MQ - zoff) // 256)),
                  pl.BlockSpec((tm, 256), lambda i: (i, (C_MKV - zoff) // 256)),
                  pl.BlockSpec((tm, 128), lambda i: (i, (C_MKR - zoff) // 128)),
                  tab, tab, _const_spec((1, 256)), _const_spec((1, 256)),
                  _const_spec((256, 1024)), _const_spec((MLA_HEADS, MLA_KV_LORA, MLA_NOPE))],
        out_specs=[col(2048), col(256), row(256), row(256),
                   pl.BlockSpec((1, MLA_VR, tm), lambda i: (i, 0, 0)), row(32), row(32)],
        out_shape=[jax.ShapeDtypeStruct((2048, N), BF16), jax.ShapeDtypeStruct((256, N), BF16),
                   jax.ShapeDtypeStruct((N, 256), F32), jax.ShapeDtypeStruct((N, 256), BF16),
                   jax.ShapeDtypeStruct((N // tm, MLA_VR, tm), BF16),
                   jax.ShapeDtypeStruct((N, 32), F32), jax.ShapeDtypeStruct((N, 32), BF16)],
        compiler_params=_cparams(("arbitrary",)),
        name="mla_prep",
    )(z, z, z, cos_t, sin_t, gq, gkv, wuq_bf, wukt_bf)


def _dot_nt(a, b):
    return lax.dot_general(a, b, (((1,), (1,)), ((), ())), preferred_element_type=F32)


MLA_SB = 128
MLA_CB = 256


def _mla_body(qlt_ref, qrt_ref, ckv_ref, kr_ref, ckvt_ref, wuvt_ref, o_ref,
              qt_scr, rt_scr, s_scr, pt_scr, m_scr, acc_scr,
              *, tq, P, L, KC, nkc_total, stacked):
    qi = pl.program_id(1)
    q0 = P + qi * tq
    cols_n = MLA_HEADS * tq
    if stacked:
        qt_scr[...] = qlt_ref[0]
        rt_scr[...] = qrt_ref[0]
    else:
        for h in range(MLA_HEADS):
            qt_scr[:, h * tq:(h + 1) * tq] = qlt_ref[h * 256:(h + 1) * 256, :]
            rt_scr[:, h * tq:(h + 1) * tq] = qrt_ref[h * 32:(h + 1) * 32, :]
    max_cend = (((q0 + tq - 1) >> 6) + 1) << 6
    nkc = jnp.minimum((max_cend + KC - 1) // KC, nkc_total)
    n_full = jnp.minimum(((q0 >> 6) + 1) << 6, L) // KC
    cb_w = min(MLA_CB, cols_n)

    m_scr[...] = jnp.full((1, cols_n), M_INIT, F32)
    acc_scr[...] = jnp.zeros((MLA_VR, cols_n), F32)

    def make_step(masked):
        def step(kc, carry):
            k0 = pl.multiple_of(kc * KC, KC)
            ck = ckv_ref[0, pl.ds(k0, KC), :]
            kr = kr_ref[0, pl.ds(k0, KC), :]
            s_scr[...] = (jnp.dot(ck, qt_scr[...], preferred_element_type=F32)
                          + jnp.dot(kr, rt_scr[...], preferred_element_type=F32))
            for cb in range(cols_n // cb_w):
                cols = slice(cb * cb_w, (cb + 1) * cb_w)
                if masked:
                    lane = cb * cb_w + lax.broadcasted_iota(I32, (1, cb_w), 1)
                    qpos = q0 + (lane & (tq - 1))
                    lim = jnp.minimum(((qpos >> 6) + 1) << 6, L)
                mx = jnp.full((8, cb_w), -jnp.inf, F32)
                for sb in range(KC // MLA_SB):
                    rs = slice(sb * MLA_SB, (sb + 1) * MLA_SB)
                    x = s_scr[rs, cols]
                    if masked:
                        kpos = k0 + sb * MLA_SB + lax.broadcasted_iota(I32, (MLA_SB, 1), 0)
                        x = jnp.where(kpos < lim, x, NEG)
                        s_scr[rs, cols] = x
                    mx = jnp.maximum(mx, _max8(x))
                m_old = m_scr[:, cols]
                m_new = jnp.maximum(m_old, jnp.max(mx, axis=0, keepdims=True))
                for sb in range(KC // MLA_SB):
                    rs = slice(sb * MLA_SB, (sb + 1) * MLA_SB)
                    pt_scr[rs, cols] = jnp.exp2(s_scr[rs, cols] - m_new).astype(BF16)
                m_scr[:, cols] = m_new
                acc_scr[:, cols] = (jnp.exp2(m_old - m_new) * acc_scr[:, cols]
                                    + jnp.dot(ckvt_ref[0, kc], pt_scr[:, cols], preferred_element_type=F32))
            return carry
        return step

    lax.fori_loop(0, n_full, make_step(False), 0)
    lax.fori_loop(n_full, nkc, make_step(True), 0)
    ot = (acc_scr[0:MLA_KV_LORA, :] / acc_scr[MLA_KV_LORA:MLA_KV_LORA + 1, :]).astype(BF16)
    if stacked:
        for h in range(MLA_HEADS):
            o_ref[0, h * MLA_V:(h + 1) * MLA_V, :] = jnp.dot(wuvt_ref[h], ot, preferred_element_type=F32)
    else:
        outs = [jnp.dot(wuvt_ref[h], ot[:, h * tq:(h + 1) * tq], preferred_element_type=F32)
                for h in range(MLA_HEADS)]
        o_ref[0] = jnp.concatenate(outs, axis=0).T


def _k_mla(qlt, qrt, ckvb, krb, ckvt, wuvt_bf, *, B, T, tq, P, L, KC, stacked):
    Lp = ckvb.shape[1]
    nkc_total = Lp // KC
    cols_n = MLA_HEADS * tq
    nq = T // tq
    if stacked:
        q_specs = [pl.BlockSpec((1, MLA_KV_LORA, cols_n), lambda b, q: (b, 0, 0)),
                   pl.BlockSpec((1, MLA_ROPE, cols_n), lambda b, q: (b, 0, 0))]
        out_spec = pl.BlockSpec((1, MLA_HEADS * MLA_V, cols_n), lambda b, q: (b, 0, 0))
        out_shape = jax.ShapeDtypeStruct((B, MLA_HEADS * MLA_V, cols_n), F32)
    else:
        q_specs = [pl.BlockSpec((MLA_HEADS * MLA_KV_LORA, tq), lambda b, q: (0, b * nq + q)),
                   pl.BlockSpec((MLA_HEADS * MLA_ROPE, tq), lambda b, q: (0, b * nq + q))]
        out_spec = pl.BlockSpec((1, tq, 512), lambda b, q: (b, q, 0))
        out_shape = jax.ShapeDtypeStruct((B, T, 512), F32)
    return pl.pallas_call(
        functools.partial(_mla_body, tq=tq, P=P, L=L, KC=KC, nkc_total=nkc_total, stacked=stacked),
        grid=(B, nq),
        in_specs=q_specs + [pl.BlockSpec((1, Lp, 256), lambda b, q: (b, 0, 0)),
                            pl.BlockSpec((1, Lp, 32), lambda b, q: (b, 0, 0)),
                            pl.BlockSpec((1, nkc_total, MLA_VR, KC), lambda b, q: (b, 0, 0, 0)),
                            _const_spec((MLA_HEADS, MLA_V, MLA_KV_LORA))],
        out_specs=out_spec,
        out_shape=out_shape,
        scratch_shapes=[pltpu.VMEM((MLA_KV_LORA, cols_n), BF16), pltpu.VMEM((MLA_ROPE, cols_n), BF16),
                        pltpu.VMEM((KC, cols_n), F32), pltpu.VMEM((KC, cols_n), BF16),
                        pltpu.VMEM((1, cols_n), F32), pltpu.VMEM((MLA_VR, cols_n), F32)],
        compiler_params=_cparams(("arbitrary", "arbitrary")),
        name="mla_attn",
    )(qlt, qrt, ckvb, krb, ckvt, wuvt_bf)


def _fold(x, r):
    n, w = x.shape
    return jnp.sum(x.reshape(n // r, r, w), axis=0)


def _fold8(x):
    return _fold(x, 8)


def _max8(x):
    n, w = x.shape
    return jnp.max(x.reshape(n // 8, 8, w), axis=0)


DSA_SB_ELEMS = 128 * 256


def _dsa_body(qt_ref, iqt_ref, iwt_ref, kb_ref, vt_ref, ikb_ref, o_ref,
              s_scr, w_scr, lg_scr, pt_scr, m_scr, acc_scr, *, tq, P, L, KC, nkc_total, topk):
    qi = pl.program_id(1)
    q0 = P + qi * tq
    sbk = min(KC, DSA_SB_ELEMS // tq)
    acc_r = sbk // 4
    qpos = q0 + lax.broadcasted_iota(I32, (1, tq), 1)
    lim = jnp.minimum(((qpos >> 6) + 1) << 6, L)
    keff = jnp.minimum(lim, topk).astype(F32)
    max_cend = (((q0 + tq - 1) >> 6) + 1) << 6
    nkc = jnp.minimum((max_cend + KC - 1) // KC, nkc_total)

    iqt = iqt_ref[0]
    iwt = iwt_ref[0]
    zpad = jnp.zeros((IDX_DIM, tq), BF16)
    iq_h = [jnp.concatenate([iqt[h * IDX_DIM:(h + 1) * IDX_DIM], zpad], axis=0) for h in range(IDX_HEADS)]

    def score_step(kc, carry):
        k0 = pl.multiple_of(kc * KC, KC)
        ik = ikb_ref[0, pl.ds(k0, KC), :]
        for h in range(IDX_HEADS):
            lg_scr[h, :, 0:tq] = jnp.dot(ik, iq_h[h], preferred_element_type=F32)
        for sb in range(KC // sbk):
            rs = slice(sb * sbk, (sb + 1) * sbk)
            s = jnp.zeros((sbk, tq), F32)
            for h in range(IDX_HEADS):
                s = s + iwt[h:h + 1, :] * jnp.maximum(lg_scr[h, rs, 0:tq], 0.0)
            kpos = k0 + sb * sbk + lax.broadcasted_iota(I32, (sbk, 1), 0)
            s = jnp.where(kpos < lim, s, -jnp.inf)
            b = lax.bitcast_convert_type(s, I32)
            b = jnp.where(b == INT_MIN, 0, b)
            s_scr[kc, rs, :] = b ^ ((b >> 31) & np.int32(0x7FFFFFFF))
        return carry

    lax.fori_loop(0, nkc, score_step, 0)

    def bit_step(i, thr):
        cand = thr + jnp.left_shift(jnp.int32(1), 31 - i)

        def cnt_step(kc, acc):
            for sb in range(KC // sbk):
                acc = acc + _fold(jnp.where(s_scr[kc, sb * sbk:(sb + 1) * sbk, :] >= cand, 1.0, 0.0), acc_r)
            return acc

        cnt = jnp.sum(lax.fori_loop(0, nkc, cnt_step, jnp.zeros((acc_r, tq), F32)), axis=0, keepdims=True)
        return jnp.where(cnt >= keff, cand, thr)

    thr = lax.fori_loop(0, 32, bit_step, jnp.full((1, tq), INT_MIN, I32))

    def tie_cnt_step(kc, carry):
        gt, eq = carry
        for sb in range(KC // sbk):
            key = s_scr[kc, sb * sbk:(sb + 1) * sbk, :]
            gt = gt + _fold(jnp.where(key > thr, 1.0, 0.0), acc_r)
            eq = eq + _fold(jnp.where(key == thr, 1.0, 0.0), acc_r)
        return gt, eq

    z8 = jnp.zeros((acc_r, tq), F32)
    n_gt, n_eq = lax.fori_loop(0, nkc, tie_cnt_step, (z8, z8))
    need = keff - jnp.sum(n_gt, axis=0, keepdims=True)
    surplus = jnp.max(jnp.sum(n_eq, axis=0, keepdims=True) - need)

    @pl.when(surplus < 0.5)
    def _():
        def fast_step(kc, carry):
            s_scr[kc] = lax.bitcast_convert_type(jnp.where(s_scr[kc] >= thr, 0.0, NEG), I32)
            return carry
        lax.fori_loop(0, nkc, fast_step, 0)

    @pl.when(surplus >= 0.5)
    def _():
        tri = jnp.where(lax.broadcasted_iota(I32, (KC, KC), 1) < lax.broadcasted_iota(I32, (KC, KC), 0),
                        1.0, 0.0).astype(BF16)

        def slow_step(kc, seen):
            key = s_scr[kc]
            eq = key == thr
            eqf = jnp.where(eq, 1.0, 0.0)
            rank = seen + jnp.dot(tri, eqf.astype(BF16), preferred_element_type=F32)
            bias = jnp.where(key > thr, 0.0, jnp.where(eq, jnp.where(rank < need, 0.0, NEG), NEG))
            s_scr[kc] = lax.bitcast_convert_type(bias, I32)
            return seen + jnp.sum(_fold8(eqf), axis=0, keepdims=True)
        lax.fori_loop(0, nkc, slow_step, jnp.zeros((1, tq), F32))

    hd = DSA_HEAD_DIM
    zq = jnp.zeros((hd, tq), BF16)
    for hp in range(DSA_HEADS // 2):
        r0 = hp * 2 * hd
        w_scr[hp, 0:hd, 0:tq] = qt_ref[0, r0:r0 + hd, :]
        w_scr[hp, 0:hd, tq:2 * tq] = zq
        w_scr[hp, hd:2 * hd, 0:tq] = zq
        w_scr[hp, hd:2 * hd, tq:2 * tq] = qt_ref[0, r0 + hd:r0 + 2 * hd, :]
    m_scr[...] = jnp.full((DSA_HEADS, tq), M_INIT, F32)
    acc_scr[...] = jnp.zeros((DSA_HEADS * DSA_VR, tq), F32)
    nsb = KC // sbk

    nhp = DSA_HEADS // 2

    def att_step(kc, carry):
        k0 = pl.multiple_of(kc * KC, KC)
        for hp in range(nhp):
            r0 = hp * 2 * hd
            lg_scr[hp] = jnp.dot(kb_ref[0, pl.ds(k0, KC), r0:r0 + 2 * hd], w_scr[hp],
                                 preferred_element_type=F32)
        for hp in range(nhp):
            li = hp
            for e in range(2):
                h = 2 * hp + e
                cols = slice(e * tq, (e + 1) * tq)
                mx = jnp.full((8, tq), -jnp.inf, F32)
                for sb in range(nsb):
                    rs = slice(sb * sbk, (sb + 1) * sbk)
                    x = lg_scr[li, rs, cols] + lax.bitcast_convert_type(s_scr[kc, rs, :], F32)
                    lg_scr[li, rs, cols] = x
                    mx = jnp.maximum(mx, _max8(x))
                m_old = m_scr[h:h + 1, :]
                m_new = jnp.maximum(m_old, jnp.max(mx, axis=0, keepdims=True))
                alpha = jnp.exp2(m_old - m_new)
                for sb in range(nsb):
                    rs = slice(sb * sbk, (sb + 1) * sbk)
                    pt_scr[h, rs, :] = jnp.exp2(lg_scr[li, rs, cols] - m_new).astype(BF16)
                m_scr[h:h + 1, :] = m_new
                hs = slice(h * DSA_VR, (h + 1) * DSA_VR)
                acc_scr[hs, :] = alpha * acc_scr[hs, :] + jnp.dot(vt_ref[0, kc, hs, :], pt_scr[h],
                                                                  preferred_element_type=F32)
        return carry

    lax.fori_loop(0, nkc, att_step, 0)
    outs = [acc_scr[h * DSA_VR:h * DSA_VR + hd, :] / acc_scr[h * DSA_VR + hd:h * DSA_VR + hd + 1, :]
            for h in range(DSA_HEADS)]
    o_ref[0] = jnp.concatenate(outs, axis=0).T


def _k_dsa(qt, iqt, iwt, kb3, vt4, ikb3, *, tq, P, L, KC, topk):
    B, _, T = qt.shape
    Lp = kb3.shape[1]
    nkc_total = Lp // KC
    return pl.pallas_call(
        functools.partial(_dsa_body, tq=tq, P=P, L=L, KC=KC, nkc_total=nkc_total, topk=topk),
        grid=(B, T // tq),
        in_specs=[pl.BlockSpec((1, 512, tq), lambda b, q: (b, 0, q)),
                  pl.BlockSpec((1, 256, tq), lambda b, q: (b, 0, q)),
                  pl.BlockSpec((1, 8, tq), lambda b, q: (b, 0, q)),
                  pl.BlockSpec((1, Lp, 512), lambda b, q: (b, 0, 0)),
                  pl.BlockSpec((1, nkc_total, DSA_HEADS * DSA_VR, KC), lambda b, q: (b, 0, 0, 0)),
                  pl.BlockSpec((1, Lp, 128), lambda b, q: (b, 0, 0))],
        out_specs=pl.BlockSpec((1, tq, 512), lambda b, q: (b, q, 0)),
        out_shape=jax.ShapeDtypeStruct((B, T, 512), F32),
        scratch_shapes=[pltpu.VMEM((nkc_total, KC, tq), I32),
                        pltpu.VMEM((DSA_HEADS // 2, 2 * DSA_HEAD_DIM, 2 * tq), BF16),
                        pltpu.VMEM((DSA_HEADS // 2, KC, 2 * tq), F32),
                        pltpu.VMEM((DSA_HEADS, KC, tq), BF16),
                        pltpu.VMEM((DSA_HEADS, tq), F32),
                        pltpu.VMEM((DSA_HEADS * DSA_VR, tq), F32)],
        compiler_params=_cparams(("arbitrary", "arbitrary")),
        name="dsa_attn",
    )(qt, iqt, iwt, kb3, vt4, ikb3)


def _merge_body(x_ref, sc_ref, sh_ref, g1_ref, ng_ref, oa_ref, ob_ref, oc_ref, od_ref,
                wg_ref, bg_ref, wb_ref, wo_ref, o_ref, *, bb, tt):
    x = x_ref[...]
    tm = bb * tt
    h = (_rms(x, ng_ref[...]) * (1.0 + sc_ref[...]) + sh_ref[...]).reshape(tm, D_MODEL).astype(BF16)
    merged = jnp.zeros((tm, D_MODEL), F32)
    for i, oref in enumerate((oa_ref, ob_ref, oc_ref, od_ref)):
        gate = _sigmoid(jnp.dot(h, wg_ref[:, i * D_MODEL:(i + 1) * D_MODEL], preferred_element_type=F32)
                        + bg_ref[:, i * D_MODEL:(i + 1) * D_MODEL])
        br = jnp.dot(oref[...].astype(BF16), wb_ref[i * BRANCH_W:(i + 1) * BRANCH_W, :],
                     preferred_element_type=F32)
        merged = merged + gate * br
    y = jnp.dot(merged.astype(BF16), wo_ref[...], preferred_element_type=F32)
    o_ref[...] = x + g1_ref[...] * y.reshape(bb, tt, D_MODEL)


def _k_merge(x, sc, sh, g1, ng, oa, ob, oc, od, wg_bf, bg, wb_bf, wo_bf, *, bb, tt):
    B, T, D = x.shape
    nt = T // tt
    tm = bb * tt
    xs = pl.BlockSpec((bb, tt, D), lambda i: (i // nt, i % nt, 0))
    ms = pl.BlockSpec((bb, 1, D), lambda i: (i // nt, 0, 0))
    os_ = pl.BlockSpec((tm, 512), lambda i: (i, 0))
    return pl.pallas_call(
        functools.partial(_merge_body, bb=bb, tt=tt),
        grid=((B // bb) * nt,),
        in_specs=[xs, ms, ms, ms, _const_spec((1, 1, D)), os_, os_, os_, os_,
                  _const_spec((D, 4 * D)), _const_spec((1, 4 * D)), _const_spec((4 * BRANCH_W, D)),
                  _const_spec((D, D))],
        out_specs=xs,
        out_shape=jax.ShapeDtypeStruct((B, T, D), F32),
        compiler_params=_cparams(("arbitrary",)),
        name="merge_out",
    )(x, sc, sh, g1, ng, oa, ob, oc, od, wg_bf, bg, wb_bf, wo_bf)


FC = FFN_DIM // 2
MOE_RB = 256


def _swiglu_chunk(h, w1, w3, w2):
    a = jnp.dot(h, w1, preferred_element_type=F32)
    b = jnp.dot(h, w3, preferred_element_type=F32)
    act = (a * _sigmoid(a) * b).astype(BF16)
    return jnp.dot(act, w2, preferred_element_type=F32)


def _finish(x, g2, f, fg_ref, o_ref, bb, tt):
    y = x + g2 * f.reshape(bb, tt, D_MODEL)
    if fg_ref is not None:
        y = _rms(y, fg_ref[...])
    o_ref[...] = y


def _ffn_body(x_ref, sc_ref, sh_ref, g2_ref, ng_ref, w1_ref, w3_ref, w2_ref, *rest, bb, tt, final):
    fg_ref, o_ref = rest if final else (None, rest[0])
    x = x_ref[...]
    tm = bb * tt
    h = (_rms(x, ng_ref[...]) * (1.0 + sc_ref[...]) + sh_ref[...]).reshape(tm, D_MODEL).astype(BF16)
    f = jnp.zeros((tm, D_MODEL), F32)
    for ci in range(FFN_DIM // FC):
        f = f + _swiglu_chunk(h, w1_ref[:, ci * FC:(ci + 1) * FC], w3_ref[:, ci * FC:(ci + 1) * FC],
                              w2_ref[ci * FC:(ci + 1) * FC, :])
    _finish(x, g2_ref[...], f, fg_ref, o_ref, bb, tt)


def _k_ffn(x, sc, sh, g2, ng, w1_bf, w3_bf, w2_bf, fg, *, bb, tt):
    B, T, D = x.shape
    nt = T // tt
    final = fg is not None
    xs = pl.BlockSpec((bb, tt, D), lambda i: (i // nt, i % nt, 0))
    ms = pl.BlockSpec((bb, 1, D), lambda i: (i // nt, 0, 0))
    in_specs = [xs, ms, ms, ms, _const_spec((1, 1, D)), _const_spec((D, FFN_DIM)),
                _const_spec((D, FFN_DIM)), _const_spec((FFN_DIM, D))]
    args = [x, sc, sh, g2, ng, w1_bf, w3_bf, w2_bf]
    if final:
        in_specs.append(_const_spec((1, 1, D)))
        args.append(fg)
    return pl.pallas_call(
        functools.partial(_ffn_body, bb=bb, tt=tt, final=final),
        grid=((B // bb) * nt,),
        in_specs=in_specs,
        out_specs=xs,
        out_shape=jax.ShapeDtypeStruct((B, T, D), F32),
        compiler_params=_cparams(("arbitrary",)),
        name="ffn_dense",
    )(*args)


def _moe_body(x_ref, sc_ref, sh_ref, g2_ref, ng_ref, rw_ref, rb_ref, w1_ref, w3_ref, w2_ref, *rest,
              bb, tt, final):
    if final:
        fg_ref, o_ref, h_scr, gate_scr, rank_scr, rankt_scr, xg_scr, y_scr, acc_scr = rest
    else:
        fg_ref = None
        o_ref, h_scr, gate_scr, rank_scr, rankt_scr, xg_scr, y_scr, acc_scr = rest
    e = pl.program_id(1)
    ci = pl.program_id(2)
    last_c = FFN_DIM // FC - 1
    tm = bb * tt
    lane = lax.broadcasted_iota(I32, (tm, 128), 1)

    @pl.when((e == 0) & (ci == 0))
    def _():
        h = (_rms(x_ref[...], ng_ref[...]) * (1.0 + sc_ref[...]) + sh_ref[...]).reshape(tm, D_MODEL)
        h_scr[...] = h.astype(BF16)
        logits = jnp.dot(h, rw_ref[...], preferred_element_type=F32,
                         precision=lax.Precision.HIGHEST) + rb_ref[...]
        m1 = jnp.max(logits, axis=-1, keepdims=True)
        i1 = jnp.min(jnp.where(logits == m1, lane, 128), axis=-1, keepdims=True)
        rest_l = jnp.where(lane == i1, NEG, logits)
        m2 = jnp.max(rest_l, axis=-1, keepdims=True)
        i2 = jnp.min(jnp.where(rest_l == m2, lane, 128), axis=-1, keepdims=True)
        e2 = jnp.exp(m2 - m1)
        den = 1.0 + e2
        gate_scr[...] = jnp.where(lane == i1, 1.0 / den, jnp.where(lane == i2, e2 / den, 0.0))
        sel = jnp.where(lane == i1, 1.0, jnp.where(lane == i2, 1.0, 0.0))
        tri = jnp.where(lax.broadcasted_iota(I32, (tm, tm), 1) < lax.broadcasted_iota(I32, (tm, tm), 0),
                        1.0, 0.0).astype(BF16)
        rank = jnp.where(sel > 0.5, jnp.dot(tri, sel.astype(BF16), preferred_element_type=F32), -1.0)
        rank_scr[...] = rank
        rankt_scr[...] = rank.T
        acc_scr[...] = jnp.zeros((tm, D_MODEL), F32)

    is_e = lane == e
    gate_col = jnp.sum(jnp.where(is_e, gate_scr[...], 0.0), axis=-1, keepdims=True)
    rank_col = jnp.sum(jnp.where(is_e, rank_scr[...], 0.0), axis=-1, keepdims=True)
    rank_row = rankt_scr[pl.ds(e, 1), :]
    n_e = jnp.max(rank_row).astype(I32) + 1
    nblk = (n_e + MOE_RB - 1) // MOE_RB

    def blk_step(b, carry):
        base = (b * MOE_RB).astype(F32)

        @pl.when(ci == 0)
        def _():
            slot = base + lax.broadcasted_iota(I32, (MOE_RB, 1), 0).astype(F32)
            pick = jnp.where(rank_row == slot, 1.0, 0.0).astype(BF16)
            xg_scr[b] = jnp.dot(pick, h_scr[...], preferred_element_type=F32).astype(BF16)

        part = _swiglu_chunk(xg_scr[b], w1_ref[0], w3_ref[0], w2_ref[0])

        @pl.when(ci == 0)
        def _():
            y_scr[b] = part

        @pl.when((ci > 0) & (ci < last_c))
        def _():
            y_scr[b] += part

        @pl.when(ci == last_c)
        def _():
            y = y_scr[b] + part
            y_hi = y.astype(BF16)
            y_lo = (y - y_hi.astype(F32)).astype(BF16)
            slot = base + lax.broadcasted_iota(I32, (1, MOE_RB), 1).astype(F32)
            put = jnp.where(rank_col == slot, 1.0, 0.0).astype(BF16)
            acc_scr[...] += gate_col * (jnp.dot(put, y_hi, preferred_element_type=F32)
                                        + jnp.dot(put, y_lo, preferred_element_type=F32))
        return carry

    lax.fori_loop(0, nblk, blk_step, 0)

    @pl.when((e == N_EXPERTS - 1) & (ci == last_c))
    def _():
        _finish(x_ref[...], g2_ref[...], acc_scr[...], fg_ref, o_ref, bb, tt)


def _k_moe(x, sc, sh, g2, ng, rw_pad, rb_pad, w1_bf, w3_bf, w2_bf, fg, *, bb, tt):
    B, T, D = x.shape
    nt = T // tt
    tm = bb * tt
    final = fg is not None
    xs = pl.BlockSpec((bb, tt, D), lambda i, e, c: (i // nt, i % nt, 0))
    xin = pl.BlockSpec((bb, tt, D), lambda i, e, c: (i // nt, i % nt, 0), pipeline_mode=pl.Buffered(1))
    ms = pl.BlockSpec((bb, 1, D), lambda i, e, c: (i // nt, 0, 0))
    in_specs = [xin, ms, ms, ms, _const_spec((1, 1, D)), _const_spec((D, 128)), _const_spec((1, 128)),
                pl.BlockSpec((1, D, FC), lambda i, e, c: (e, 0, c)),
                pl.BlockSpec((1, D, FC), lambda i, e, c: (e, 0, c)),
                pl.BlockSpec((1, FC, D), lambda i, e, c: (e, c, 0))]
    args = [x, sc, sh, g2, ng, rw_pad, rb_pad, w1_bf, w3_bf, w2_bf]
    if final:
        in_specs.append(_const_spec((1, 1, D)))
        args.append(fg)
    return pl.pallas_call(
        functools.partial(_moe_body, bb=bb, tt=tt, final=final),
        grid=((B // bb) * nt, N_EXPERTS, FFN_DIM // FC),
        in_specs=in_specs,
        out_specs=xs,
        out_shape=jax.ShapeDtypeStruct((B, T, D), F32),
        scratch_shapes=[pltpu.VMEM((tm, D), BF16), pltpu.VMEM((tm, 128), F32), pltpu.VMEM((tm, 128), F32),
                        pltpu.VMEM((128, tm), F32), pltpu.VMEM((pl.cdiv(tm, MOE_RB), MOE_RB, D), BF16),
                        pltpu.VMEM((pl.cdiv(tm, MOE_RB), MOE_RB, D), F32), pltpu.VMEM((tm, D), F32)],
        compiler_params=_cparams(("arbitrary", "arbitrary", "arbitrary")),
        name="ffn_moe",
    )(*args)


def _cast_body(x_ref, o_ref):
    o_ref[...] = x_ref[...].astype(BF16)


def _cast_bf16(w, start=0, count=None):
    _, R, C = w.shape
    E = w.shape[0] if count is None else count
    rb = 256
    return pl.pallas_call(
        _cast_body,
        grid=(E, R // rb),
        in_specs=[pl.BlockSpec((1, rb, C), lambda e, r: (start + e, r, 0))],
        out_specs=pl.BlockSpec((1, rb, C), lambda e, r: (e, r, 0)),
        out_shape=jax.ShapeDtypeStruct((E, R, C), BF16),
        compiler_params=_cparams(("arbitrary", "arbitrary")),
        name="cast_bf16",
    )(w)


def _dsa_keys_body(ck_ref, cv_ref, ci_ref, nk_ref, nv_ref, ni_ref, kb_ref, vt_ref, ikb_ref, *, nc, kc_rows):
    cached = pl.program_id(1) < nc
    k = jnp.where(cached, ck_ref[0], nk_ref[0].astype(BF16))
    v = jnp.where(cached, cv_ref[0], nv_ref[0].astype(BF16))
    ik = jnp.where(cached, ci_ref[0], ni_ref[0])
    kb_ref[0] = k
    vt = v.astype(F32).T.astype(BF16)
    ones = jnp.ones((DSA_VR - DSA_HEAD_DIM, kc_rows), BF16)
    for h in range(DSA_HEADS):
        vt_ref[0, 0, h * DSA_VR:h * DSA_VR + DSA_HEAD_DIM, :] = vt[h * DSA_HEAD_DIM:(h + 1) * DSA_HEAD_DIM]
        vt_ref[0, 0, h * DSA_VR + DSA_HEAD_DIM:(h + 1) * DSA_VR, :] = ones
    ikb_ref[0, :, 0:IDX_DIM] = ik.astype(BF16)
    ikb_ref[0, :, IDX_DIM:128] = jnp.zeros((kc_rows, 128 - IDX_DIM), BF16)


def _k_dsa_keys(ck, cv, ci, nk, nv, ni, *, KC, row0):
    B = nk.shape[0]
    P = ck.shape[1]
    assert P % KC == 0
    nc = P // KC
    cblk = lambda w: pl.BlockSpec((1, KC, w), lambda b, c: (row0 + b, jnp.minimum(c, nc - 1), 0))
    nblk = lambda w: pl.BlockSpec((1, KC, w), lambda b, c: (b, 0, 0))
    return pl.pallas_call(
        functools.partial(_dsa_keys_body, nc=nc, kc_rows=KC),
        grid=(B, nc + 1),
        in_specs=[cblk(512), cblk(512), cblk(IDX_DIM), nblk(512), nblk(512), nblk(IDX_DIM)],
        out_specs=[pl.BlockSpec((1, KC, 512), lambda b, c: (b, c, 0)),
                   pl.BlockSpec((1, 1, DSA_HEADS * DSA_VR, KC), lambda b, c: (b, c, 0, 0)),
                   pl.BlockSpec((1, KC, 128), lambda b, c: (b, c, 0))],
        out_shape=[jax.ShapeDtypeStruct((B, P + KC, 512), BF16),
                   jax.ShapeDtypeStruct((B, nc + 1, DSA_HEADS * DSA_VR, KC), BF16),
                   jax.ShapeDtypeStruct((B, P + KC, 128), BF16)],
        compiler_params=_cparams(("arbitrary", "arbitrary")),
        name="dsa_keys",
    )(ck, cv, ci, nk, nv, ni)


def _mla_keys_body(cc_ref, cr_ref, nc_ref, nr_ref, cb_ref, ct_ref, rb_ref, *, nc, kc_rows):
    cached = pl.program_id(1) < nc
    ckv = jnp.where(cached, cc_ref[0], nc_ref[0])
    kr = jnp.where(cached, cr_ref[0], nr_ref[0])
    cb_ref[0] = ckv.astype(BF16)
    ct_ref[0, 0, 0:MLA_KV_LORA, :] = ckv.T.astype(BF16)
    ct_ref[0, 0, MLA_KV_LORA:MLA_VR, :] = jnp.ones((MLA_VR - MLA_KV_LORA, kc_rows), BF16)
    rb_ref[0] = kr.astype(BF16)


def _k_mla_keys(cc, cr, ncv, nr, *, KC, row0):
    B = ncv.shape[0]
    P = cc.shape[1]
    assert P % KC == 0
    nc = P // KC
    cblk = lambda w: pl.BlockSpec((1, KC, w), lambda b, c: (row0 + b, jnp.minimum(c, nc - 1), 0))
    nblk = lambda w: pl.BlockSpec((1, KC, w), lambda b, c: (b, 0, 0))
    return pl.pallas_call(
        functools.partial(_mla_keys_body, nc=nc, kc_rows=KC),
        grid=(B, nc + 1),
        in_specs=[cblk(MLA_KV_LORA), cblk(MLA_ROPE), nblk(MLA_KV_LORA), nblk(MLA_ROPE)],
        out_specs=[pl.BlockSpec((1, KC, MLA_KV_LORA), lambda b, c: (b, c, 0)),
                   pl.BlockSpec((1, 1, MLA_VR, KC), lambda b, c: (b, c, 0, 0)),
                   pl.BlockSpec((1, KC, MLA_ROPE), lambda b, c: (b, c, 0))],
        out_shape=[jax.ShapeDtypeStruct((B, P + KC, MLA_KV_LORA), BF16),
                   jax.ShapeDtypeStruct((B, nc + 1, MLA_VR, KC), BF16),
                   jax.ShapeDtypeStruct((B, P + KC, MLA_ROPE), BF16)],
        compiler_params=_cparams(("arbitrary", "arbitrary")),
        name="mla_keys",
    )(cc, cr, ncv, nr)


def _prep_w_in(w):
    aq, ak, av, iq, ik, iw, pu, cb, cc, ch, mq, mkv, mkr = jnp.split(w, IN_SPLIT_POINTS, axis=1)
    d = w.shape[0]
    mkr_sw = jnp.concatenate([mkr[:, 16:], mkr[:, :16]], axis=1)
    return jnp.concatenate([aq, ak, av, pu, cb, cc, ch, iq, mq, mkv, ik, iw, jnp.zeros((d, 60), w.dtype),
                            mkr, mkr_sw, jnp.zeros((d, 64), w.dtype)], axis=1).astype(BF16)


def _prep_w_uq(w):
    w3 = w.reshape(MLA_Q_LORA, MLA_HEADS, MLA_NOPE + MLA_ROPE)
    nope = w3[:, :, :MLA_NOPE].reshape(MLA_Q_LORA, MLA_HEADS * MLA_NOPE)
    rope = w3[:, :, MLA_NOPE:]
    rope_sw = jnp.concatenate([rope[:, :, 16:], rope[:, :, :16]], axis=-1)
    return jnp.concatenate([nope, rope.reshape(MLA_Q_LORA, -1), rope_sw.reshape(MLA_Q_LORA, -1)],
                           axis=1).astype(BF16)


def _rope_tables(pos):
    half = MLA_ROPE // 2
    inv = 1.0 / (ROPE_BASE ** (jnp.arange(half, dtype=F32) / half))
    ang = pos.astype(F32)[:, None] * inv[None, :]
    cos, sin = jnp.cos(ang), jnp.sin(ang)
    cos2 = jnp.tile(jnp.concatenate([cos, cos], axis=-1), (1, MLA_HEADS))
    sin2 = jnp.tile(jnp.concatenate([-sin, sin], axis=-1), (1, MLA_HEADS))
    return cos2, sin2


def _layer(x, mod, past, lw, fw, use_moe, final_g, cfg):
    B, T, D = x.shape
    bb, tt, KC, tq_dsa, tq_mla = cfg["bb"], cfg["tt"], cfg["KC"], cfg["tq_dsa"], cfg["tq_mla"]
    KCD = cfg["KCD"]
    N = B * T
    P = 0 if past is None else past["dsa_k"].shape[1]
    L = P + T
    sh1, sc1, g1, sh2, sc2, g2 = [m.reshape(B, 1, D) for m in jnp.split(mod, 6, axis=-1)]
    ng1 = lw["norm_mix_g"].reshape(1, 1, D)
    ng2 = lw["norm_ffn_g"].reshape(1, 1, D)

    emit = past is None
    res = _k_in(x, sc1, sh1, ng1, lw["w_in"], bb=bb, tt=tt, emit=emit, kc=KCD)
    z = res[0]
    zoff = Z_OFF if emit else 0
    z3 = z.reshape(B, T, ZW - zoff)
    if emit:
        k_new = res[1].reshape(B, T, DSA_HEADS, DSA_HEAD_DIM)
        v_new = res[2].reshape(B, T, DSA_HEADS, DSA_HEAD_DIM)
    else:
        k_new = z[:, C_AK:C_AK + 512].reshape(B, T, DSA_HEADS, DSA_HEAD_DIM)
        v_new = z[:, C_AV:C_AV + 512].reshape(B, T, DSA_HEADS, DSA_HEAD_DIM)
    ik_new = z3[:, :, C_IKW - zoff:C_IKW - zoff + IDX_DIM]

    if past is None:
        ph16 = jnp.zeros((B, 16, 512), F32)
        cvh16 = ph16
    else:
        ph16 = jnp.pad(past["pool"], ((0, 0), (1, 0), (0, 0)))
        cvh16 = jnp.pad(past["conv"], ((0, 0), (16 - (CONV_WIDTH - 1), 0), (0, 0)))
    o_b, o_c, ph_o, cvh_o = _k_pc(z3, ph16, cvh16, lw["pool_w"], lw["pool_scale"], lw["conv_w"],
                                  bb=bb, tt=tt, P=P, zoff=zoff)
    pool_hist = ph_o[:, 1:]
    conv_hist = cvh_o[:, 16 - (CONV_WIDTH - 1):]

    cos2, sin2 = _rope_tables(P + jnp.arange(T))
    if bb > 1:
        cos2, sin2 = jnp.tile(cos2, (bb, 1)), jnp.tile(sin2, (bb, 1))
    qlt, qrt, ckv_new, ckvb, ckvt, kr_new, krb = _k_mlaprep(
        z, cos2, sin2, lw["mla_q_norm_g"], lw["mla_kv_norm_g"], lw["mla_w_uq"], lw["mla_w_uk"], tm=bb * tt,
        zoff=zoff)

    topk = min(DSA_TOPK_MAX, L // 4)
    if past is None:
        kb3 = res[3].reshape(B, T, 512)
        vt4, qt, iqt, iwt = res[4], res[5], res[6], res[7]
        ikb3 = res[8].reshape(B, T, 128)
        ckvb3 = ckvb.reshape(B, T, 256)
        krb3 = krb.reshape(B, T, 32)
        Tq = T
    else:
        assert T <= KCD and T <= KC
        padn = lambda a, kc: jnp.pad(a, ((0, 0), (0, kc - T), (0, 0)))
        kb3, vt4, ikb3 = _k_dsa_keys(past["dsa_k"], past["dsa_v"], past["idx_k"],
                                     padn(k_new.reshape(B, T, 512), KCD), padn(v_new.reshape(B, T, 512), KCD),
                                     padn(ik_new, KCD), KC=KCD, row0=past["row0"])
        Tq = tq_dsa
        padq = ((0, 0), (0, 0), (0, Tq - T))
        qt = jnp.pad((z3[:, :, C_AQ:C_AQ + 512] * (DSA_SCALE * LOG2E)).transpose(0, 2, 1).astype(BF16), padq)
        iqt = jnp.pad(z3[:, :, C_IQ:C_IQ + 256].transpose(0, 2, 1).astype(BF16), padq)
        iwt = jnp.pad(z3[:, :, C_IKW + 64:C_IKW + 72].transpose(0, 2, 1) * IDX_SCALE, padq)
        ckvb3, ckvt4, krb3 = _k_mla_keys(past["ckv"], past["krope"], padn(ckv_new.reshape(B, T, 256), KC),
                                         padn(kr_new.reshape(B, T, 32), KC), KC=KC, row0=past["row0"])

    o_a = _k_dsa(qt, iqt, iwt, kb3, vt4, ikb3, tq=tq_dsa, P=P, L=L, KC=KCD, topk=topk)
    o_a = o_a[:, :T].reshape(N, 512)
    if past is None:
        ckvt4 = ckvt.reshape(B, T // KC, MLA_VR, KC)
        o_d = _k_mla(qlt, qrt, ckvb3, krb3, ckvt4, lw["mla_w_uv"], B=B, T=T, tq=tq_mla, P=P, L=L, KC=KC,
                     stacked=False).reshape(N, 512)
    else:
        qs = qlt.reshape(MLA_HEADS, 256, B, T).transpose(2, 1, 0, 3).reshape(B, 256, MLA_HEADS * T)
        rs = qrt.reshape(MLA_HEADS, 32, B, T).transpose(2, 1, 0, 3).reshape(B, 32, MLA_HEADS * T)
        o5 = _k_mla(qs, rs, ckvb3, krb3, ckvt4, lw["mla_w_uv"], B=B, T=T, tq=tq_mla, P=P, L=L, KC=KC,
                    stacked=True).reshape(B, MLA_HEADS, MLA_V, MLA_HEADS, T)
        o_d = jnp.stack([o5[:, h, :, h, :] for h in range(MLA_HEADS)], axis=1)
        o_d = o_d.transpose(0, 3, 1, 2).reshape(N, 512)

    x1 = _k_merge(x, sc1, sh1, g1, ng1, o_a, o_b.reshape(N, 512), o_c.reshape(N, 512), o_d,
                  lw["w_gate"], lw["b_gate"], lw["w_branch"], lw["w_out"], bb=bb, tt=tt)
    fg = None if final_g is None else final_g.reshape(1, 1, D)
    if use_moe:
        x2 = _k_moe(x1, sc2, sh2, g2, ng2, fw["rw"], fw["rb"], fw["w1"], fw["w3"], fw["w2"], fg,
                    bb=cfg["bb_moe"], tt=cfg["tt_moe"])
    else:
        x2 = _k_ffn(x1, sc2, sh2, g2, ng2, fw["w1"], fw["w3"], fw["w2"], fg, bb=bb, tt=tt)
    rows = (k_new, v_new, ik_new, ckv_new.reshape(B, T, 256), kr_new.reshape(B, T, 32), pool_hist, conv_hist)
    return x2, rows


def _prep_layer(l, ada_w, ada_b, norm_mix_g, norm_ffn_g, w_in, mla_q_norm_g, mla_kv_norm_g,
                mla_w_uq, mla_w_uk, mla_w_uv, pool_w, pool_scale, conv_w, w_gate, b_gate, w_branch, w_out):
    return dict(
        ada_w=_cast_bf16(ada_w, l, 1)[0], ada_b=ada_b[l].reshape(1, -1),
        norm_mix_g=norm_mix_g[l], norm_ffn_g=norm_ffn_g[l],
        w_in=_prep_w_in(w_in[l]),
        mla_q_norm_g=mla_q_norm_g[l].reshape(1, -1), mla_kv_norm_g=mla_kv_norm_g[l].reshape(1, -1),
        mla_w_uq=_prep_w_uq(mla_w_uq[l]),
        mla_w_uk=mla_w_uk[l].transpose(1, 0, 2).astype(BF16),
        mla_w_uv=mla_w_uv[l].transpose(1, 2, 0).astype(BF16),
        pool_w=pool_w[l].astype(BF16), pool_scale=pool_scale[l].reshape(1, -1),
        conv_w=jnp.pad(conv_w[l].reshape(CONV_WIDTH, BRANCH_W), ((0, 8 - CONV_WIDTH), (0, 0))),
        w_gate=_cast_bf16(w_gate, l, 1)[0], b_gate=b_gate[l].reshape(1, -1),
        w_branch=_cast_bf16(w_branch, l, 1)[0], w_out=_cast_bf16(w_out, l, 1)[0])


def _cfg_for(B, T, has_past):
    if has_past:
        return dict(bb=B, tt=T, KC=512, KCD=512, tq_dsa=128, tq_mla=T, bb_moe=B, tt_moe=T)
    t = min(512, T)
    return dict(bb=1, tt=t, KC=t, KCD=t, tq_dsa=t, tq_mla=t, bb_moe=1, tt_moe=min(1024, T))


def _run(x, c, pasts, lws, fws, final_norm_g):
    B, T, _ = x.shape
    cfg = _cfg_for(B, T, pasts is not None)
    outs = [[] for _ in range(7)]
    depth = len(lws)
    for l in range(depth):
        lw = lws[l]
        mod = _ada(c, lw["ada_w"], lw["ada_b"])
        x, rows = _layer(x, mod, None if pasts is None else pasts[l], lw, fws[l], l % 2 == 1,
                         final_norm_g if l == depth - 1 else None, cfg)
        for o, r in zip(outs, rows):
            o.append(r)
    return x, [jnp.stack(o) for o in outs]


def kernel(x_prompt, x_sample, c_prompt, c_sample, cache_dsa_k, cache_dsa_v, cache_dsa_idx_k, cache_mla_ckv, cache_mla_krope, state_pool, state_conv, ada_w, ada_b, norm_mix_g, norm_ffn_g, w_in, mla_q_norm_g, mla_kv_norm_g, mla_w_uq, mla_w_uk, mla_w_uv, pool_w, pool_scale, conv_w, w_gate, b_gate, w_branch, w_out, ffn_w1, ffn_w3, ffn_w2, moe_router_w, moe_router_b, moe_w1, moe_w3, moe_w2, final_norm_g):
    depth = ada_w.shape[0]
    lws = [_prep_layer(l, ada_w, ada_b, norm_mix_g, norm_ffn_g, w_in, mla_q_norm_g, mla_kv_norm_g,
                       mla_w_uq, mla_w_uk, mla_w_uv, pool_w, pool_scale, conv_w, w_gate, b_gate,
                       w_branch, w_out) for l in range(depth)]
    fws = []
    for l in range(depth):
        j = l // 2
        if l % 2 == 0:
            fws.append(dict(w1=_cast_bf16(ffn_w1, j, 1)[0], w3=_cast_bf16(ffn_w3, j, 1)[0],
                            w2=_cast_bf16(ffn_w2, j, 1)[0]))
        else:
            fws.append(dict(
                rw=jnp.pad(moe_router_w[j], ((0, 0), (0, 128 - N_EXPERTS))),
                rb=jnp.pad(moe_router_b[j].reshape(1, -1), ((0, 0), (0, 128 - N_EXPERTS)), constant_values=NEG),
                w1=_cast_bf16(moe_w1.reshape((-1,) + moe_w1.shape[2:]), j * N_EXPERTS, N_EXPERTS),
                w3=_cast_bf16(moe_w3.reshape((-1,) + moe_w3.shape[2:]), j * N_EXPERTS, N_EXPERTS),
                w2=_cast_bf16(moe_w2.reshape((-1,) + moe_w2.shape[2:]), j * N_EXPERTS, N_EXPERTS)))
    nb, plen = cache_dsa_k.shape[1], cache_dsa_k.shape[2]
    flat = lambda a: a.reshape(depth * nb, plen, -1)
    kc_bf, vc_bf = flat(cache_dsa_k.astype(BF16)), flat(cache_dsa_v.astype(BF16))
    sample_pasts = [dict(dsa_k=kc_bf, dsa_v=vc_bf, idx_k=flat(cache_dsa_idx_k),
                         ckv=flat(cache_mla_ckv), krope=flat(cache_mla_krope), row0=l * nb,
                         pool=state_pool[l], conv=state_conv[l]) for l in range(depth)]
    y_prompt, pn = _run(x_prompt, c_prompt, None, lws, fws, final_norm_g)
    y_sample, sn = _run(x_sample, c_sample, sample_pasts, lws, fws, final_norm_g)
    return (y_prompt, y_sample, pn[0], pn[1], pn[2], pn[3], pn[4], pn[5], pn[6],
            sn[0], sn[1], sn[2], sn[3], sn[4], sn[5], sn[6])
```

```python
import functools

import numpy as np
import jax
import jax.numpy as jnp
from jax import lax
from jax.experimental import pallas as pl
from jax.experimental.pallas import tpu as pltpu

F32 = jnp.float32
BF16 = jnp.bfloat16
I32 = jnp.int32

D_MODEL = 1024
CHUNK = 64
N_BRANCH = 4
BRANCH_W = 512
DSA_HEADS = 8
DSA_HEAD_DIM = 64
IDX_HEADS = 4
IDX_DIM = 64
DSA_TOPK_MAX = 256
POOL_WINDOWS = (2, 4, 8, 16)
POOL_GROUP = BRANCH_W // 4
POOL_HIST = 15
CONV_WIDTH = 3
MLA_HEADS = 8
MLA_Q_LORA = 256
MLA_KV_LORA = 256
MLA_NOPE = 64
MLA_ROPE = 32
MLA_V = 64
ROPE_BASE = 10000.0
FFN_DIM = 2816
N_EXPERTS = 8
EPS = 1e-6

DSA_SCALE = DSA_HEAD_DIM ** -0.5
IDX_SCALE = (IDX_HEADS * IDX_DIM) ** -0.5
MLA_SCALE = (MLA_NOPE + MLA_ROPE) ** -0.5
LOG2E = 1.4426950408889634

IN_SPLIT_WIDTHS = (512, 512, 512, 256, 64, 4, 512, 512, 512, 512, 256, 256, 32)
IN_SPLIT_POINTS = tuple(int(sum(IN_SPLIT_WIDTHS[:i + 1])) for i in range(len(IN_SPLIT_WIDTHS) - 1))

C_AQ, C_AK, C_AV, C_PU, C_CB, C_CC, C_CH = 0, 512, 1024, 1536, 2048, 2560, 3072
C_IQ, C_MQ, C_MKV, C_IKW, C_MKR, ZW = 3584, 3840, 4096, 4352, 4480, 4608
IN_TN = 1536
Z_OFF = IN_TN
DSA_VR = DSA_HEAD_DIM + 16
MLA_VR = MLA_KV_LORA + 16

NEG = -1e30
M_INIT = -1e29
INT_MIN = np.int32(-2 ** 31)
V7X_VMEM_LIMIT = 56 * 1024 * 1024


def _cparams(sem):
    return pltpu.CompilerParams(dimension_semantics=sem, vmem_limit_bytes=V7X_VMEM_LIMIT)


def _rms(x, g):
    return x * lax.rsqrt(jnp.mean(x * x, axis=-1, keepdims=True) + EPS) * g


def _sigmoid(x):
    return 1.0 / (1.0 + jnp.exp(-x))


def _const_spec(shape):
    nd = len(shape)
    return pl.BlockSpec(shape, lambda *_: (0,) * nd, pipeline_mode=pl.Buffered(1))


def _ada_body(c_ref, w_ref, b_ref, o_ref):
    c = c_ref[...]
    s = (c * _sigmoid(c)).astype(BF16)
    o_ref[...] = jnp.dot(s, w_ref[...], preferred_element_type=F32) + b_ref[...]


def _ada(c, w_bf, b):
    B, D = c.shape
    n = w_bf.shape[1]
    tn = 1536
    return pl.pallas_call(
        _ada_body,
        grid=(n // tn,),
        in_specs=[pl.BlockSpec((B, D), lambda j: (0, 0)),
                  pl.BlockSpec((D, tn), lambda j: (0, j)),
                  pl.BlockSpec((1, tn), lambda j: (0, j))],
        out_specs=pl.BlockSpec((B, tn), lambda j: (0, j)),
        out_shape=jax.ShapeDtypeStruct((B, n), F32),
        compiler_params=_cparams(("arbitrary",)),
        name="ada_mod",
    )(c, w_bf, b)


def _in_body(x_ref, sc_ref, sh_ref, g_ref, w_ref, z_ref, *rest, bb, tt, emit, kc):
    if emit:
        kn_ref, vn_ref, kb_ref, vt_ref, qt_ref, iqt_ref, iwt_ref, ikb_ref, h_scr = rest
    else:
        (h_scr,) = rest
    j = pl.program_id(1)

    @pl.when(j == 0)
    def _():
        h = _rms(x_ref[...], g_ref[...]) * (1.0 + sc_ref[...]) + sh_ref[...]
        h_scr[...] = h.reshape(bb * tt, D_MODEL).astype(BF16)

    z = jnp.dot(h_scr[...], w_ref[...], preferred_element_type=F32)

    if not emit:
        z_ref[...] = z
    else:
        @pl.when(j > 0)
        def _():
            z_ref[...] = z

        @pl.when(j == 0)
        def _():
            k = z[:, C_AK:C_AK + 512]
            v = z[:, C_AV:C_AV + 512]
            kn_ref[...] = k
            vn_ref[...] = v
            kb_ref[...] = k.astype(BF16)
            vt = v.T.astype(BF16)
            ones = jnp.ones((DSA_VR - DSA_HEAD_DIM, kc), BF16)
            for c in range(tt // kc):
                for h in range(DSA_HEADS):
                    vt_ref[0, c, h * DSA_VR:h * DSA_VR + DSA_HEAD_DIM, :] = vt[
                        h * DSA_HEAD_DIM:(h + 1) * DSA_HEAD_DIM, c * kc:(c + 1) * kc]
                    vt_ref[0, c, h * DSA_VR + DSA_HEAD_DIM:(h + 1) * DSA_VR, :] = ones
            qt_ref[0] = (z[:, C_AQ:C_AQ + 512] * (DSA_SCALE * LOG2E)).T.astype(BF16)

        @pl.when(j == 2)
        def _():
            o = 2 * IN_TN
            iqt_ref[0] = z[:, C_IQ - o:C_IQ - o + 256].T.astype(BF16)
            ikw = z[:, C_IKW - o:C_IKW - o + 128]
            ikb_ref[...] = ikw.astype(BF16)
            iwt_ref[0] = ikw.T[64:72, :] * IDX_SCALE


def _k_in(x, sc, sh, g, w_bf, *, bb, tt, emit, kc):
    B, T, D = x.shape
    nt = T // tt
    tm = bb * tt
    n_rows = (B // bb) * nt
    N = B * T
    xmap = lambda i, j: (i // nt, i % nt, 0)
    mmap = lambda i, j: (i // nt, 0, 0)
    in_specs = [pl.BlockSpec((bb, tt, D), xmap),
                pl.BlockSpec((bb, 1, D), mmap),
                pl.BlockSpec((bb, 1, D), mmap),
                _const_spec((1, 1, D)),
                pl.BlockSpec((D, IN_TN), lambda i, j: (0, j))]
    if emit:
        assert bb == 1
        out_specs = [pl.BlockSpec((tm, IN_TN), lambda i, j: (i, jnp.maximum(j - 1, 0)))]
        out_shape = [jax.ShapeDtypeStruct((N, ZW - Z_OFF), F32)]
        out_specs += [
            pl.BlockSpec((tm, 512), lambda i, j: (i, 0)),
            pl.BlockSpec((tm, 512), lambda i, j: (i, 0)),
            pl.BlockSpec((tm, 512), lambda i, j: (i, 0)),
            pl.BlockSpec((1, tt // kc, DSA_HEADS * DSA_VR, kc), lambda i, j: (i // nt, i % nt, 0, 0)),
            pl.BlockSpec((1, 512, tt), lambda i, j: (i // nt, 0, i % nt)),
            pl.BlockSpec((1, 256, tt), lambda i, j: (i // nt, 0, i % nt)),
            pl.BlockSpec((1, 8, tt), lambda i, j: (i // nt, 0, i % nt)),
            pl.BlockSpec((tm, 128), lambda i, j: (i, 0)),
        ]
        out_shape += [
            jax.ShapeDtypeStruct((N, 512), F32),
            jax.ShapeDtypeStruct((N, 512), F32),
            jax.ShapeDtypeStruct((N, 512), BF16),
            jax.ShapeDtypeStruct((B, T // kc, DSA_HEADS * DSA_VR, kc), BF16),
            jax.ShapeDtypeStruct((B, 512, T), BF16),
            jax.ShapeDtypeStruct((B, 256, T), BF16),
            jax.ShapeDtypeStruct((B, 8, T), F32),
            jax.ShapeDtypeStruct((N, 128), BF16),
        ]
    else:
        out_specs = [pl.BlockSpec((tm, IN_TN), lambda i, j: (i, j))]
        out_shape = [jax.ShapeDtypeStruct((N, ZW), F32)]
    return pl.pallas_call(
        functools.partial(_in_body, bb=bb, tt=tt, emit=emit, kc=kc),
        grid=(n_rows, ZW // IN_TN),
        in_specs=in_specs,
        out_specs=out_specs,
        out_shape=out_shape,
        scratch_shapes=[pltpu.VMEM((tm, D), BF16)],
        compiler_params=_cparams(("arbitrary", "arbitrary")),
        name="in_proj",
    )(x, sc, sh, g, w_bf)


def _pc_body(pu_ref, cb_ref, cc_ref, ch_ref, pup_ref, ccp_ref, chp_ref, ph_ref, cvh_ref,
             pw_ref, ps_ref, cw_ref, ob_ref, oc_ref, pho_ref, cho_ref, u_scr, g_scr, *, bb, tt, P):
    t = pl.program_id(1)
    first = t == 0
    pu = pu_ref[...]
    g = cc_ref[...] * ch_ref[...]
    u_scr[:, 0:16, :] = jnp.where(first, ph_ref[...], pup_ref[...])
    u_scr[:, 16:16 + tt, :] = pu
    g_scr[:, 0:16, :] = jnp.where(first, cvh_ref[...], ccp_ref[...] * chp_ref[...])
    g_scr[:, 16:16 + tt, :] = g

    pos = P + t * tt + lax.broadcasted_iota(I32, (1, tt, 1), 1)
    for gi, win in enumerate(POOL_WINDOWS):
        lo = gi * POOL_GROUP
        acc = pu[:, :, lo:lo + POOL_GROUP]
        for j in range(1, win):
            acc = acc + u_scr[:, 16 - j:16 - j + tt, lo:lo + POOL_GROUP]
        cnt = jnp.minimum(pos + 1, win).astype(F32)
        d = acc / cnt - pu[:, :, lo:lo + POOL_GROUP]
        y = jnp.dot(d.reshape(bb * tt, POOL_GROUP).astype(BF16), pw_ref[gi], preferred_element_type=F32)
        ob_ref[:, :, lo:lo + POOL_GROUP] = (y * ps_ref[:, lo:lo + POOL_GROUP]).reshape(bb, tt, POOL_GROUP)

    cw = cw_ref[...]
    conv = (cw[0:1, :] * g_scr[:, 14:14 + tt, :] + cw[1:2, :] * g_scr[:, 15:15 + tt, :] + cw[2:3, :] * g)
    oc_ref[...] = cb_ref[...] * conv
    pho_ref[...] = u_scr[:, tt:tt + 16, :]
    cho_ref[...] = g_scr[:, tt:tt + 16, :]


def _k_pc(z3, ph16, cvh16, pw_bf, ps, cw, *, bb, tt, P, zoff):
    B, T, _ = z3.shape
    nt = T // tt
    r = tt // 16

    def cur(c):
        return pl.BlockSpec((bb, tt, 512), lambda b, t: (b, t, (c - zoff) // 512))

    def prev(c):
        return pl.BlockSpec((bb, 16, 512), lambda b, t: (b, jnp.maximum(t * r - 1, 0), (c - zoff) // 512))

    hist = pl.BlockSpec((bb, 16, 512), lambda b, t: (b, 0, 0))
    return pl.pallas_call(
        functools.partial(_pc_body, bb=bb, tt=tt, P=P),
        grid=(B // bb, nt),
        in_specs=[cur(C_PU), cur(C_CB), cur(C_CC), cur(C_CH), prev(C_PU), prev(C_CC), prev(C_CH),
                  hist, hist, _const_spec((4, POOL_GROUP, POOL_GROUP)), _const_spec((1, 512)),
                  _const_spec((8, 512))],
        out_specs=[pl.BlockSpec((bb, tt, 512), lambda b, t: (b, t, 0)),
                   pl.BlockSpec((bb, tt, 512), lambda b, t: (b, t, 0)), hist, hist],
        out_shape=[jax.ShapeDtypeStruct((B, T, 512), F32), jax.ShapeDtypeStruct((B, T, 512), F32),
                   jax.ShapeDtypeStruct((B, 16, 512), F32), jax.ShapeDtypeStruct((B, 16, 512), F32)],
        scratch_shapes=[pltpu.VMEM((bb, tt + 16, 512), F32), pltpu.VMEM((bb, tt + 16, 512), F32)],
        compiler_params=_cparams(("arbitrary", "arbitrary")),
        name="pool_conv",
    )(z3, z3, z3, z3, z3, z3, z3, ph16, cvh16, pw_bf, ps, cw)


def _mp_body(mq_ref, mkv_ref, mkr_ref, cos_ref, sin_ref, gq_ref, gkv_ref, wuq_ref, wukt_ref,
             qlt_ref, qrt_ref, ckv_ref, ckvb_ref, ckvt_ref, kr_ref, krb_ref):
    a = _rms(mq_ref[...], gq_ref[...]).astype(BF16)
    cq = jnp.dot(a, wuq_ref[...], preferred_element_type=F32)
    cqt = cq.T
    cos = cos_ref[...]
    sin = sin_ref[...]
    c = MLA_SCALE * LOG2E
    qrt_ref[...] = ((cqt[512:768] * cos.T + cqt[768:1024] * sin.T) * c).astype(BF16)
    for h in range(MLA_HEADS):
        qnt = cqt[h * MLA_NOPE:(h + 1) * MLA_NOPE].astype(BF16)
        qlt_ref[h * MLA_KV_LORA:(h + 1) * MLA_KV_LORA, :] = (jnp.dot(
            wukt_ref[h], qnt, preferred_element_type=F32) * c).astype(BF16)
    ckv = _rms(mkv_ref[...], gkv_ref[...])
    ckv_ref[...] = ckv
    ckvb_ref[...] = ckv.astype(BF16)
    ckvt_ref[0, 0:MLA_KV_LORA, :] = ckv.T.astype(BF16)
    ckvt_ref[0, MLA_KV_LORA:MLA_VR, :] = jnp.ones((MLA_VR - MLA_KV_LORA, ckv.shape[0]), BF16)
    mkr = mkr_ref[...]
    kr = mkr[:, 0:32] * cos[:, 0:32] + mkr[:, 32:64] * sin[:, 0:32]
    kr_ref[...] = kr
    krb_ref[...] = kr.astype(BF16)


def _k_mlaprep(z, cos_t, sin_t, gq, gkv, wuq_bf, wukt_bf, *, tm, zoff):
    N = z.shape[0]
    nr = cos_t.shape[0] // tm
    tab = pl.BlockSpec((tm, 256), lambda i: (i % nr, 0))
    row = lambda w: pl.BlockSpec((tm, w), lambda i: (i, 0))
    col = lambda h: pl.BlockSpec((h, tm), lambda i: (0, i))
    return pl.pallas_call(
        _mp_body,
        grid=(N // tm,),
        in_specs=[pl.BlockSpec((tm, 256), lambda i: (i, (C_MQ - zoff) // 256)),
                  pl.BlockSpec((tm, 256), lambda i: (i, (C_MKV - zoff) // 256)),
                  pl.BlockSpec((tm, 128), lambda i: (i, (C_MKR - zoff) // 128)),
                  tab, tab, _const_spec((1, 256)), _const_spec((1, 256)),
                  _const_spec((256, 1024)), _const_spec((MLA_HEADS, MLA_KV_LORA, MLA_NOPE))],
        out_specs=[col(2048), col(256), row(256), row(256),
                   pl.BlockSpec((1, MLA_VR, tm), lambda i: (i, 0, 0)), row(32), row(32)],
        out_shape=[jax.ShapeDtypeStruct((2048, N), BF16), jax.ShapeDtypeStruct((256, N), BF16),
                   jax.ShapeDtypeStruct((N, 256), F32), jax.ShapeDtypeStruct((N, 256), BF16),
                   jax.ShapeDtypeStruct((N // tm, MLA_VR, tm), BF16),
                   jax.ShapeDtypeStruct((N, 32), F32), jax.ShapeDtypeStruct((N, 32), BF16)],
        compiler_params=_cparams(("arbitrary",)),
        name="mla_prep",
    )(z, z, z, cos_t, sin_t, gq, gkv, wuq_bf, wukt_bf)


def _dot_nt(a, b):
    return lax.dot_general(a, b, (((1,), (1,)), ((), ())), preferred_element_type=F32)


MLA_SB = 128
MLA_CB = 256


def _mla_body(qlt_ref, qrt_ref, ckv_ref, kr_ref, ckvt_ref, wuvt_ref, o_ref,
              qt_scr, rt_scr, s_scr, pt_scr, m_scr, acc_scr,
              *, tq, P, L, KC, nkc_total, stacked):
    qi = pl.program_id(1)
    q0 = P + qi * tq
    cols_n = MLA_HEADS * tq
    if stacked:
        qt_scr[...] = qlt_ref[0]
        rt_scr[...] = qrt_ref[0]
    else:
        for h in range(MLA_HEADS):
            qt_scr[:, h * tq:(h + 1) * tq] = qlt_ref[h * 256:(h + 1) * 256, :]
            rt_scr[:, h * tq:(h + 1) * tq] = qrt_ref[h * 32:(h + 1) * 32, :]
    max_cend = (((q0 + tq - 1) >> 6) + 1) << 6
    nkc = jnp.minimum((max_cend + KC - 1) // KC, nkc_total)
    n_full = jnp.minimum(((q0 >> 6) + 1) << 6, L) // KC
    cb_w = min(MLA_CB, cols_n)

    m_scr[...] = jnp.full((1, cols_n), M_INIT, F32)
    acc_scr[...] = jnp.zeros((MLA_VR, cols_n), F32)

    def make_step(masked):
        def step(kc, carry):
            k0 = pl.multiple_of(kc * KC, KC)
            ck = ckv_ref[0, pl.ds(k0, KC), :]
            kr = kr_ref[0, pl.ds(k0, KC), :]
            s_scr[...] = (jnp.dot(ck, qt_scr[...], preferred_element_type=F32)
                          + jnp.dot(kr, rt_scr[...], preferred_element_type=F32))
            for cb in range(cols_n // cb_w):
                cols = slice(cb * cb_w, (cb + 1) * cb_w)
                if masked:
                    lane = cb * cb_w + lax.broadcasted_iota(I32, (1, cb_w), 1)
                    qpos = q0 + (lane & (tq - 1))
                    lim = jnp.minimum(((qpos >> 6) + 1) << 6, L)
                mx = jnp.full((8, cb_w), -jnp.inf, F32)
                for sb in range(KC // MLA_SB):
                    rs = slice(sb * MLA_SB, (sb + 1) * MLA_SB)
                    x = s_scr[rs, cols]
                    if masked:
                        kpos = k0 + sb * MLA_SB + lax.broadcasted_iota(I32, (MLA_SB, 1), 0)
                        x = jnp.where(kpos < lim, x, NEG)
                        s_scr[rs, cols] = x
                    mx = jnp.maximum(mx, _max8(x))
                m_old = m_scr[:, cols]
                m_new = jnp.maximum(m_old, jnp.max(mx, axis=0, keepdims=True))
                for sb in range(KC // MLA_SB):
                    rs = slice(sb * MLA_SB, (sb + 1) * MLA_SB)
                    pt_scr[rs, cols] = jnp.exp2(s_scr[rs, cols] - m_new).astype(BF16)
                m_scr[:, cols] = m_new
                acc_scr[:, cols] = (jnp.exp2(m_old - m_new) * acc_scr[:, cols]
                                    + jnp.dot(ckvt_ref[0, kc], pt_scr[:, cols], preferred_element_type=F32))
            return carry
        return step

    lax.fori_loop(0, n_full, make_step(False), 0)
    lax.fori_loop(n_full, nkc, make_step(True), 0)
    ot = (acc_scr[0:MLA_KV_LORA, :] / acc_scr[MLA_KV_LORA:MLA_KV_LORA + 1, :]).astype(BF16)
    if stacked:
        for h in range(MLA_HEADS):
            o_ref[0, h * MLA_V:(h + 1) * MLA_V, :] = jnp.dot(wuvt_ref[h], ot, preferred_element_type=F32)
    else:
        outs = [jnp.dot(wuvt_ref[h], ot[:, h * tq:(h + 1) * tq], preferred_element_type=F32)
                for h in range(MLA_HEADS)]
        o_ref[0] = jnp.concatenate(outs, axis=0).T


def _k_mla(qlt, qrt, ckvb, krb, ckvt, wuvt_bf, *, B, T, tq, P, L, KC, stacked):
    Lp = ckvb.shape[1]
    nkc_total = Lp // KC
    cols_n = MLA_HEADS * tq
    nq = T // tq
    if stacked:
        q_specs = [pl.BlockSpec((1, MLA_KV_LORA, cols_n), lambda b, q: (b, 0, 0)),
                   pl.BlockSpec((1, MLA_ROPE, cols_n), lambda b, q: (b, 0, 0))]
        out_spec = pl.BlockSpec((1, MLA_HEADS * MLA_V, cols_n), lambda b, q: (b, 0, 0))
        out_shape = jax.ShapeDtypeStruct((B, MLA_HEADS * MLA_V, cols_n), F32)
    else:
        q_specs = [pl.BlockSpec((MLA_HEADS * MLA_KV_LORA, tq), lambda b, q: (0, b * nq + q)),
                   pl.BlockSpec((MLA_HEADS * MLA_ROPE, tq), lambda b, q: (0, b * nq + q))]
        out_spec = pl.BlockSpec((1, tq, 512), lambda b, q: (b, q, 0))
        out_shape = jax.ShapeDtypeStruct((B, T, 512), F32)
    return pl.pallas_call(
        functools.partial(_mla_body, tq=tq, P=P, L=L, KC=KC, nkc_total=nkc_total, stacked=stacked),
        grid=(B, nq),
        in_specs=q_specs + [pl.BlockSpec((1, Lp, 256), lambda b, q: (b, 0, 0)),
                            pl.BlockSpec((1, Lp, 32), lambda b, q: (b, 0, 0)),
                            pl.BlockSpec((1, nkc_total, MLA_VR, KC), lambda b, q: (b, 0, 0, 0)),
                            _const_spec((MLA_HEADS, MLA_V, MLA_KV_LORA))],
        out_specs=out_spec,
        out_shape=out_shape,
        scratch_shapes=[pltpu.VMEM((MLA_KV_LORA, cols_n), BF16), pltpu.VMEM((MLA_ROPE, cols_n), BF16),
                        pltpu.VMEM((KC, cols_n), F32), pltpu.VMEM((KC, cols_n), BF16),
                        pltpu.VMEM((1, cols_n), F32), pltpu.VMEM((MLA_VR, cols_n), F32)],
        compiler_params=_cparams(("arbitrary", "arbitrary")),
        name="mla_attn",
    )(qlt, qrt, ckvb, krb, ckvt, wuvt_bf)


def _fold(x, r):
    n, w = x.shape
    return jnp.sum(x.reshape(n // r, r, w), axis=0)


def _fold8(x):
    return _fold(x, 8)


def _max8(x):
    n, w = x.shape
    return jnp.max(x.reshape(n // 8, 8, w), axis=0)


DSA_SB_ELEMS = 128 * 256


def _dsa_body(qt_ref, iqt_ref, iwt_ref, kb_ref, vt_ref, ikb_ref, o_ref,
              s_scr, w_scr, lg_scr, pt_scr, m_scr, acc_scr, *, tq, P, L, KC, nkc_total, topk):
    qi = pl.program_id(1)
    q0 = P + qi * tq
    sbk = min(KC, DSA_SB_ELEMS // tq)
    acc_r = sbk // 4
    qpos = q0 + lax.broadcasted_iota(I32, (1, tq), 1)
    lim = jnp.minimum(((qpos >> 6) + 1) << 6, L)
    keff = jnp.minimum(lim, topk).astype(F32)
    max_cend = (((q0 + tq - 1) >> 6) + 1) << 6
    nkc = jnp.minimum((max_cend + KC - 1) // KC, nkc_total)

    iqt = iqt_ref[0]
    iwt = iwt_ref[0]
    zpad = jnp.zeros((IDX_DIM, tq), BF16)
    iq_h = [jnp.concatenate([iqt[h * IDX_DIM:(h + 1) * IDX_DIM], zpad], axis=0) for h in range(IDX_HEADS)]

    def score_step(kc, carry):
        k0 = pl.multiple_of(kc * KC, KC)
        ik = ikb_ref[0, pl.ds(k0, KC), :]
        for h in range(IDX_HEADS):
            lg_scr[h, :, 0:tq] = jnp.dot(ik, iq_h[h], preferred_element_type=F32)
        for sb in range(KC // sbk):
            rs = slice(sb * sbk, (sb + 1) * sbk)
            s = jnp.zeros((sbk, tq), F32)
            for h in range(IDX_HEADS):
                s = s + iwt[h:h + 1, :] * jnp.maximum(lg_scr[h, rs, 0:tq], 0.0)
            kpos = k0 + sb * sbk + lax.broadcasted_iota(I32, (sbk, 1), 0)
            s = jnp.where(kpos < lim, s, -jnp.inf)
            b = lax.bitcast_convert_type(s, I32)
            b = jnp.where(b == INT_MIN, 0, b)
            s_scr[kc, rs, :] = b ^ ((b >> 31) & np.int32(0x7FFFFFFF))
        return carry

    lax.fori_loop(0, nkc, score_step, 0)

    def bit_step(i, carry):
        thr, n_ge = carry
        cand = thr + jnp.left_shift(jnp.int32(1), 31 - i)

        def cnt_step(kc, acc):
            for sb in range(KC // sbk):
                acc = acc + _fold(jnp.where(s_scr[kc, sb * sbk:(sb + 1) * sbk, :] >= cand, 1.0, 0.0), acc_r)
            return acc

        cnt = jnp.sum(lax.fori_loop(0, nkc, cnt_step, jnp.zeros((acc_r, tq), F32)), axis=0, keepdims=True)
        take = cnt >= keff
        return jnp.where(take, cand, thr), jnp.where(take, cnt, n_ge)

    n_all = jnp.zeros((1, tq), F32) + (nkc * KC).astype(F32)
    thr, n_ge = lax.fori_loop(0, 32, bit_step, (jnp.full((1, tq), INT_MIN, I32), n_all))

    surplus = jnp.max(n_ge - keff)

    @pl.when(surplus < 0.5)
    def _():
        def fast_step(kc, carry):
            s_scr[kc] = lax.bitcast_convert_type(jnp.where(s_scr[kc] >= thr, 0.0, NEG), I32)
            return carry
        lax.fori_loop(0, nkc, fast_step, 0)

    @pl.when(surplus >= 0.5)
    def _():
        tri = jnp.where(lax.broadcasted_iota(I32, (KC, KC), 1) < lax.broadcasted_iota(I32, (KC, KC), 0),
                        1.0, 0.0).astype(BF16)

        def gt_step(kc, acc):
            return acc + _fold8(jnp.where(s_scr[kc] > thr, 1.0, 0.0))

        n_gt = lax.fori_loop(0, nkc, gt_step, jnp.zeros((8, tq), F32))
        need = keff - jnp.sum(n_gt, axis=0, keepdims=True)

        def slow_step(kc, seen):
            key = s_scr[kc]
            eq = key == thr
            eqf = jnp.where(eq, 1.0, 0.0)
            rank = seen + jnp.dot(tri, eqf.astype(BF16), preferred_element_type=F32)
            bias = jnp.where(key > thr, 0.0, jnp.where(eq, jnp.where(rank < need, 0.0, NEG), NEG))
            s_scr[kc] = lax.bitcast_convert_type(bias, I32)
            return seen + jnp.sum(_fold8(eqf), axis=0, keepdims=True)
        lax.fori_loop(0, nkc, slow_step, jnp.zeros((1, tq), F32))

    hd = DSA_HEAD_DIM
    zq = jnp.zeros((hd, tq), BF16)
    for hp in range(DSA_HEADS // 2):
        r0 = hp * 2 * hd
        w_scr[hp, 0:hd, 0:tq] = qt_ref[0, r0:r0 + hd, :]
        w_scr[hp, 0:hd, tq:2 * tq] = zq
        w_scr[hp, hd:2 * hd, 0:tq] = zq
        w_scr[hp, hd:2 * hd, tq:2 * tq] = qt_ref[0, r0 + hd:r0 + 2 * hd, :]
    m_scr[...] = jnp.full((DSA_HEADS, tq), M_INIT, F32)
    acc_scr[...] = jnp.zeros((DSA_HEADS * DSA_VR, tq), F32)
    nsb = KC // sbk

    nhp = DSA_HEADS // 2

    def att_step(kc, carry):
        k0 = pl.multiple_of(kc * KC, KC)
        for hp in range(nhp):
            r0 = hp * 2 * hd
            lg_scr[hp] = jnp.dot(kb_ref[0, pl.ds(k0, KC), r0:r0 + 2 * hd], w_scr[hp],
                                 preferred_element_type=F32)
        for hp in range(nhp):
            li = hp
            for e in range(2):
                h = 2 * hp + e
                cols = slice(e * tq, (e + 1) * tq)
                mx = jnp.full((8, tq), -jnp.inf, F32)
                for sb in range(nsb):
                    rs = slice(sb * sbk, (sb + 1) * sbk)
                    x = lg_scr[li, rs, cols] + lax.bitcast_convert_type(s_scr[kc, rs, :], F32)
                    lg_scr[li, rs, cols] = x
                    mx = jnp.maximum(mx, _max8(x))
                m_old = m_scr[h:h + 1, :]
                m_new = jnp.maximum(m_old, jnp.max(mx, axis=0, keepdims=True))
                alpha = jnp.exp2(m_old - m_new)
                for sb in range(nsb):
                    rs = slice(sb * sbk, (sb + 1) * sbk)
                    pt_scr[h, rs, :] = jnp.exp2(lg_scr[li, rs, cols] - m_new).astype(BF16)
                m_scr[h:h + 1, :] = m_new
                hs = slice(h * DSA_VR, (h + 1) * DSA_VR)
                acc_scr[hs, :] = alpha * acc_scr[hs, :] + jnp.dot(vt_ref[0, kc, hs, :], pt_scr[h],
                                                                  preferred_element_type=F32)
        return carry

    lax.fori_loop(0, nkc, att_step, 0)
    outs = [acc_scr[h * DSA_VR:h * DSA_VR + hd, :] / acc_scr[h * DSA_VR + hd:h * DSA_VR + hd + 1, :]
            for h in range(DSA_HEADS)]
    o_ref[0] = jnp.concatenate(outs, axis=0).T


def _k_dsa(qt, iqt, iwt, kb3, vt4, ikb3, *, tq, P, L, KC, topk):
    B, _, T = qt.shape
    Lp = kb3.shape[1]
    nkc_total = Lp // KC
    return pl.pallas_call(
        functools.partial(_dsa_body, tq=tq, P=P, L=L, KC=KC, nkc_total=nkc_total, topk=topk),
        grid=(B, T // tq),
        in_specs=[pl.BlockSpec((1, 512, tq), lambda b, q: (b, 0, q)),
                  pl.BlockSpec((1, 256, tq), lambda b, q: (b, 0, q)),
                  pl.BlockSpec((1, 8, tq), lambda b, q: (b, 0, q)),
                  pl.BlockSpec((1, Lp, 512), lambda b, q: (b, 0, 0)),
                  pl.BlockSpec((1, nkc_total, DSA_HEADS * DSA_VR, KC), lambda b, q: (b, 0, 0, 0)),
                  pl.BlockSpec((1, Lp, 128), lambda b, q: (b, 0, 0))],
        out_specs=pl.BlockSpec((1, tq, 512), lambda b, q: (b, q, 0)),
        out_shape=jax.ShapeDtypeStruct((B, T, 512), F32),
        scratch_shapes=[pltpu.VMEM((nkc_total, KC, tq), I32),
                        pltpu.VMEM((DSA_HEADS // 2, 2 * DSA_HEAD_DIM, 2 * tq), BF16),
                        pltpu.VMEM((DSA_HEADS // 2, KC, 2 * tq), F32),
                        pltpu.VMEM((DSA_HEADS, KC, tq), BF16),
                        pltpu.VMEM((DSA_HEADS, tq), F32),
                        pltpu.VMEM((DSA_HEADS * DSA_VR, tq), F32)],
        compiler_params=_cparams(("arbitrary", "arbitrary")),
        name="dsa_attn",
    )(qt, iqt, iwt, kb3, vt4, ikb3)


def _merge_body(x_ref, sc_ref, sh_ref, g1_ref, ng_ref, oa_ref, ob_ref, oc_ref, od_ref,
                wg_ref, bg_ref, wb_ref, wo_ref, o_ref, *, bb, tt):
    x = x_ref[...]
    tm = bb * tt
    h = (_rms(x, ng_ref[...]) * (1.0 + sc_ref[...]) + sh_ref[...]).reshape(tm, D_MODEL).astype(BF16)
    merged = jnp.zeros((tm, D_MODEL), F32)
    for i, oref in enumerate((oa_ref, ob_ref, oc_ref, od_ref)):
        gate = _sigmoid(jnp.dot(h, wg_ref[:, i * D_MODEL:(i + 1) * D_MODEL], preferred_element_type=F32)
                        + bg_ref[:, i * D_MODEL:(i + 1) * D_MODEL])
        br = jnp.dot(oref[...].astype(BF16), wb_ref[i * BRANCH_W:(i + 1) * BRANCH_W, :],
                     preferred_element_type=F32)
        merged = merged + gate * br
    y = jnp.dot(merged.astype(BF16), wo_ref[...], preferred_element_type=F32)
    o_ref[...] = x + g1_ref[...] * y.reshape(bb, tt, D_MODEL)


def _k_merge(x, sc, sh, g1, ng, oa, ob, oc, od, wg_bf, bg, wb_bf, wo_bf, *, bb, tt):
    B, T, D = x.shape
    nt = T // tt
    tm = bb * tt
    xs = pl.BlockSpec((bb, tt, D), lambda i: (i // nt, i % nt, 0))
    ms = pl.BlockSpec((bb, 1, D), lambda i: (i // nt, 0, 0))
    os_ = pl.BlockSpec((tm, 512), lambda i: (i, 0))
    return pl.pallas_call(
        functools.partial(_merge_body, bb=bb, tt=tt),
        grid=((B // bb) * nt,),
        in_specs=[xs, ms, ms, ms, _const_spec((1, 1, D)), os_, os_, os_, os_,
                  _const_spec((D, 4 * D)), _const_spec((1, 4 * D)), _const_spec((4 * BRANCH_W, D)),
                  _const_spec((D, D))],
        out_specs=xs,
        out_shape=jax.ShapeDtypeStruct((B, T, D), F32),
        compiler_params=_cparams(("arbitrary",)),
        name="merge_out",
    )(x, sc, sh, g1, ng, oa, ob, oc, od, wg_bf, bg, wb_bf, wo_bf)


FC = FFN_DIM // 2
MOE_RB = 256


def _swiglu_chunk(h, w1, w3, w2):
    a = jnp.dot(h, w1, preferred_element_type=F32)
    b = jnp.dot(h, w3, preferred_element_type=F32)
    act = (a * _sigmoid(a) * b).astype(BF16)
    return jnp.dot(act, w2, preferred_element_type=F32)


def _finish(x, g2, f, fg_ref, o_ref, bb, tt):
    y = x + g2 * f.reshape(bb, tt, D_MODEL)
    if fg_ref is not None:
        y = _rms(y, fg_ref[...])
    o_ref[...] = y


def _ffn_body(x_ref, sc_ref, sh_ref, g2_ref, ng_ref, w1_ref, w3_ref, w2_ref, *rest, bb, tt, final):
    fg_ref, o_ref = rest if final else (None, rest[0])
    x = x_ref[...]
    tm = bb * tt
    h = (_rms(x, ng_ref[...]) * (1.0 + sc_ref[...]) + sh_ref[...]).reshape(tm, D_MODEL).astype(BF16)
    f = jnp.zeros((tm, D_MODEL), F32)
    for ci in range(FFN_DIM // FC):
        f = f + _swiglu_chunk(h, w1_ref[:, ci * FC:(ci + 1) * FC], w3_ref[:, ci * FC:(ci + 1) * FC],
                              w2_ref[ci * FC:(ci + 1) * FC, :])
    _finish(x, g2_ref[...], f, fg_ref, o_ref, bb, tt)


def _k_ffn(x, sc, sh, g2, ng, w1_bf, w3_bf, w2_bf, fg, *, bb, tt):
    B, T, D = x.shape
    nt = T // tt
    final = fg is not None
    xs = pl.BlockSpec((bb, tt, D), lambda i: (i // nt, i % nt, 0))
    ms = pl.BlockSpec((bb, 1, D), lambda i: (i // nt, 0, 0))
    in_specs = [xs, ms, ms, ms, _const_spec((1, 1, D)), _const_spec((D, FFN_DIM)),
                _const_spec((D, FFN_DIM)), _const_spec((FFN_DIM, D))]
    args = [x, sc, sh, g2, ng, w1_bf, w3_bf, w2_bf]
    if final:
        in_specs.append(_const_spec((1, 1, D)))
        args.append(fg)
    return pl.pallas_call(
        functools.partial(_ffn_body, bb=bb, tt=tt, final=final),
        grid=((B // bb) * nt,),
        in_specs=in_specs,
        out_specs=xs,
        out_shape=jax.ShapeDtypeStruct((B, T, D), F32),
        compiler_params=_cparams(("arbitrary",)),
        name="ffn_dense",
    )(*args)


def _moe_body(x_ref, sc_ref, sh_ref, g2_ref, ng_ref, rw_ref, rb_ref, w1_ref, w3_ref, w2_ref, *rest,
              bb, tt, final):
    if final:
        fg_ref, o_ref, h_scr, gate_scr, rank_scr, rankt_scr, xg_scr, y_scr, acc_scr = rest
    else:
        fg_ref = None
        o_ref, h_scr, gate_scr, rank_scr, rankt_scr, xg_scr, y_scr, acc_scr = rest
    e = pl.program_id(1)
    ci = pl.program_id(2)
    last_c = FFN_DIM // FC - 1
    tm = bb * tt
    lane = lax.broadcasted_iota(I32, (tm, 128), 1)

    @pl.when((e == 0) & (ci == 0))
    def _():
        h = (_rms(x_ref[...], ng_ref[...]) * (1.0 + sc_ref[...]) + sh_ref[...]).reshape(tm, D_MODEL)
        h_scr[...] = h.astype(BF16)
        logits = jnp.dot(h, rw_ref[...], preferred_element_type=F32,
                         precision=lax.Precision.HIGHEST) + rb_ref[...]
        m1 = jnp.max(logits, axis=-1, keepdims=True)
        i1 = jnp.min(jnp.where(logits == m1, lane, 128), axis=-1, keepdims=True)
        rest_l = jnp.where(lane == i1, NEG, logits)
        m2 = jnp.max(rest_l, axis=-1, keepdims=True)
        i2 = jnp.min(jnp.where(rest_l == m2, lane, 128), axis=-1, keepdims=True)
        e2 = jnp.exp(m2 - m1)
        den = 1.0 + e2
        gate_scr[...] = jnp.where(lane == i1, 1.0 / den, jnp.where(lane == i2, e2 / den, 0.0))
        sel = jnp.where(lane == i1, 1.0, jnp.where(lane == i2, 1.0, 0.0))
        tri = jnp.where(lax.broadcasted_iota(I32, (tm, tm), 1) < lax.broadcasted_iota(I32, (tm, tm), 0),
                        1.0, 0.0).astype(BF16)
        rank = jnp.where(sel > 0.5, jnp.dot(tri, sel.astype(BF16), preferred_element_type=F32), -1.0)
        rank_scr[...] = rank
        rankt_scr[...] = rank.T
        acc_scr[...] = jnp.zeros((tm, D_MODEL), F32)

    is_e = lane == e
    gate_col = jnp.sum(jnp.where(is_e, gate_scr[...], 0.0), axis=-1, keepdims=True)
    rank_col = jnp.sum(jnp.where(is_e, rank_scr[...], 0.0), axis=-1, keepdims=True)
    rank_row = rankt_scr[pl.ds(e, 1), :]
    n_e = jnp.max(rank_row).astype(I32) + 1
    nblk = (n_e + MOE_RB - 1) // MOE_RB

    def blk_step(b, carry):
        base = (b * MOE_RB).astype(F32)

        @pl.when(ci == 0)
        def _():
            slot = base + lax.broadcasted_iota(I32, (MOE_RB, 1), 0).astype(F32)
            pick = jnp.where(rank_row == slot, 1.0, 0.0).astype(BF16)
            xg_scr[b] = jnp.dot(pick, h_scr[...], preferred_element_type=F32).astype(BF16)

        part = _swiglu_chunk(xg_scr[b], w1_ref[0], w3_ref[0], w2_ref[0])

        @pl.when(ci == 0)
        def _():
            y_scr[b] = part

        @pl.when((ci > 0) & (ci < last_c))
        def _():
            y_scr[b] += part

        @pl.when(ci == last_c)
        def _():
            y = y_scr[b] + part
            y_hi = y.astype(BF16)
            y_lo = (y - y_hi.astype(F32)).astype(BF16)
            slot = base + lax.broadcasted_iota(I32, (1, MOE_RB), 1).astype(F32)
            put = jnp.where(rank_col == slot, 1.0, 0.0).astype(BF16)
            acc_scr[...] += gate_col * (jnp.dot(put, y_hi, preferred_element_type=F32)
                                        + jnp.dot(put, y_lo, preferred_element_type=F32))
        return carry

    lax.fori_loop(0, nblk, blk_step, 0)

    @pl.when((e == N_EXPERTS - 1) & (ci == last_c))
    def _():
        _finish(x_ref[...], g2_ref[...], acc_scr[...], fg_ref, o_ref, bb, tt)


def _k_moe(x, sc, sh, g2, ng, rw_pad, rb_pad, w1_bf, w3_bf, w2_bf, fg, *, bb, tt):
    B, T, D = x.shape
    nt = T // tt
    tm = bb * tt
    final = fg is not None
    xs = pl.BlockSpec((bb, tt, D), lambda i, e, c: (i // nt, i % nt, 0))
    xin = pl.BlockSpec((bb, tt, D), lambda i, e, c: (i // nt, i % nt, 0), pipeline_mode=pl.Buffered(1))
    ms = pl.BlockSpec((bb, 1, D), lambda i, e, c: (i // nt, 0, 0))
    in_specs = [xin, ms, ms, ms, _const_spec((1, 1, D)), _const_spec((D, 128)), _const_spec((1, 128)),
                pl.BlockSpec((1, D, FC), lambda i, e, c: (e, 0, c)),
                pl.BlockSpec((1, D, FC), lambda i, e, c: (e, 0, c)),
                pl.BlockSpec((1, FC, D), lambda i, e, c: (e, c, 0))]
    args = [x, sc, sh, g2, ng, rw_pad, rb_pad, w1_bf, w3_bf, w2_bf]
    if final:
        in_specs.append(_const_spec((1, 1, D)))
        args.append(fg)
    return pl.pallas_call(
        functools.partial(_moe_body, bb=bb, tt=tt, final=final),
        grid=((B // bb) * nt, N_EXPERTS, FFN_DIM // FC),
        in_specs=in_specs,
        out_specs=xs,
        out_shape=jax.ShapeDtypeStruct((B, T, D), F32),
        scratch_shapes=[pltpu.VMEM((tm, D), BF16), pltpu.VMEM((tm, 128), F32), pltpu.VMEM((tm, 128), F32),
                        pltpu.VMEM((128, tm), F32), pltpu.VMEM((pl.cdiv(tm, MOE_RB), MOE_RB, D), BF16),
                        pltpu.VMEM((pl.cdiv(tm, MOE_RB), MOE_RB, D), F32), pltpu.VMEM((tm, D), F32)],
        compiler_params=_cparams(("arbitrary", "arbitrary", "arbitrary")),
        name="ffn_moe",
    )(*args)


def _cast_body(x_ref, o_ref):
    o_ref[...] = x_ref[...].astype(BF16)


def _cast_bf16(w, start=0, count=None):
    _, R, C = w.shape
    E = w.shape[0] if count is None else count
    rb = 256
    return pl.pallas_call(
        _cast_body,
        grid=(E, R // rb),
        in_specs=[pl.BlockSpec((1, rb, C), lambda e, r: (start + e, r, 0))],
        out_specs=pl.BlockSpec((1, rb, C), lambda e, r: (e, r, 0)),
        out_shape=jax.ShapeDtypeStruct((E, R, C), BF16),
        compiler_params=_cparams(("arbitrary", "arbitrary")),
        name="cast_bf16",
    )(w)


def _dsa_keys_body(ck_ref, cv_ref, ci_ref, nk_ref, nv_ref, ni_ref, kb_ref, vt_ref, ikb_ref, *, nc, kc_rows):
    cached = pl.program_id(1) < nc
    k = jnp.where(cached, ck_ref[0], nk_ref[0])
    v = jnp.where(cached, cv_ref[0], nv_ref[0])
    ik = jnp.where(cached, ci_ref[0], ni_ref[0])
    kb_ref[0] = k.astype(BF16)
    vt = v.T.astype(BF16)
    ones = jnp.ones((DSA_VR - DSA_HEAD_DIM, kc_rows), BF16)
    for h in range(DSA_HEADS):
        vt_ref[0, 0, h * DSA_VR:h * DSA_VR + DSA_HEAD_DIM, :] = vt[h * DSA_HEAD_DIM:(h + 1) * DSA_HEAD_DIM]
        vt_ref[0, 0, h * DSA_VR + DSA_HEAD_DIM:(h + 1) * DSA_VR, :] = ones
    ikb_ref[0, :, 0:IDX_DIM] = ik.astype(BF16)
    ikb_ref[0, :, IDX_DIM:128] = jnp.zeros((kc_rows, 128 - IDX_DIM), BF16)


def _k_dsa_keys(ck, cv, ci, nk, nv, ni, *, KC, row0):
    B = nk.shape[0]
    P = ck.shape[1]
    assert P % KC == 0
    nc = P // KC
    cblk = lambda w: pl.BlockSpec((1, KC, w), lambda b, c: (row0 + b, jnp.minimum(c, nc - 1), 0))
    nblk = lambda w: pl.BlockSpec((1, KC, w), lambda b, c: (b, 0, 0))
    return pl.pallas_call(
        functools.partial(_dsa_keys_body, nc=nc, kc_rows=KC),
        grid=(B, nc + 1),
        in_specs=[cblk(512), cblk(512), cblk(IDX_DIM), nblk(512), nblk(512), nblk(IDX_DIM)],
        out_specs=[pl.BlockSpec((1, KC, 512), lambda b, c: (b, c, 0)),
                   pl.BlockSpec((1, 1, DSA_HEADS * DSA_VR, KC), lambda b, c: (b, c, 0, 0)),
                   pl.BlockSpec((1, KC, 128), lambda b, c: (b, c, 0))],
        out_shape=[jax.ShapeDtypeStruct((B, P + KC, 512), BF16),
                   jax.ShapeDtypeStruct((B, nc + 1, DSA_HEADS * DSA_VR, KC), BF16),
                   jax.ShapeDtypeStruct((B, P + KC, 128), BF16)],
        compiler_params=_cparams(("arbitrary", "arbitrary")),
        name="dsa_keys",
    )(ck, cv, ci, nk, nv, ni)


def _mla_keys_body(cc_ref, cr_ref, nc_ref, nr_ref, cb_ref, ct_ref, rb_ref, *, nc, kc_rows):
    cached = pl.program_id(1) < nc
    ckv = jnp.where(cached, cc_ref[0], nc_ref[0])
    kr = jnp.where(cached, cr_ref[0], nr_ref[0])
    cb_ref[0] = ckv.astype(BF16)
    ct_ref[0, 0, 0:MLA_KV_LORA, :] = ckv.T.astype(BF16)
    ct_ref[0, 0, MLA_KV_LORA:MLA_VR, :] = jnp.ones((MLA_VR - MLA_KV_LORA, kc_rows), BF16)
    rb_ref[0] = kr.astype(BF16)


def _k_mla_keys(cc, cr, ncv, nr, *, KC, row0):
    B = ncv.shape[0]
    P = cc.shape[1]
    assert P % KC == 0
    nc = P // KC
    cblk = lambda w: pl.BlockSpec((1, KC, w), lambda b, c: (row0 + b, jnp.minimum(c, nc - 1), 0))
    nblk = lambda w: pl.BlockSpec((1, KC, w), lambda b, c: (b, 0, 0))
    return pl.pallas_call(
        functools.partial(_mla_keys_body, nc=nc, kc_rows=KC),
        grid=(B, nc + 1),
        in_specs=[cblk(MLA_KV_LORA), cblk(MLA_ROPE), nblk(MLA_KV_LORA), nblk(MLA_ROPE)],
        out_specs=[pl.BlockSpec((1, KC, MLA_KV_LORA), lambda b, c: (b, c, 0)),
                   pl.BlockSpec((1, 1, MLA_VR, KC), lambda b, c: (b, c, 0, 0)),
                   pl.BlockSpec((1, KC, MLA_ROPE), lambda b, c: (b, c, 0))],
        out_shape=[jax.ShapeDtypeStruct((B, P + KC, MLA_KV_LORA), BF16),
                   jax.ShapeDtypeStruct((B, nc + 1, MLA_VR, KC), BF16),
                   jax.ShapeDtypeStruct((B, P + KC, MLA_ROPE), BF16)],
        compiler_params=_cparams(("arbitrary", "arbitrary")),
        name="mla_keys",
    )(cc, cr, ncv, nr)


def _prep_w_in(w):
    aq, ak, av, iq, ik, iw, pu, cb, cc, ch, mq, mkv, mkr = jnp.split(w, IN_SPLIT_POINTS, axis=1)
    d = w.shape[0]
    mkr_sw = jnp.concatenate([mkr[:, 16:], mkr[:, :16]], axis=1)
    return jnp.concatenate([aq, ak, av, pu, cb, cc, ch, iq, mq, mkv, ik, iw, jnp.zeros((d, 60), w.dtype),
                            mkr, mkr_sw, jnp.zeros((d, 64), w.dtype)], axis=1).astype(BF16)


def _prep_w_uq(w):
    w3 = w.reshape(MLA_Q_LORA, MLA_HEADS, MLA_NOPE + MLA_ROPE)
    nope = w3[:, :, :MLA_NOPE].reshape(MLA_Q_LORA, MLA_HEADS * MLA_NOPE)
    rope = w3[:, :, MLA_NOPE:]
    rope_sw = jnp.concatenate([rope[:, :, 16:], rope[:, :, :16]], axis=-1)
    return jnp.concatenate([nope, rope.reshape(MLA_Q_LORA, -1), rope_sw.reshape(MLA_Q_LORA, -1)],
                           axis=1).astype(BF16)


def _rope_tables(pos):
    half = MLA_ROPE // 2
    inv = 1.0 / (ROPE_BASE ** (jnp.arange(half, dtype=F32) / half))
    ang = pos.astype(F32)[:, None] * inv[None, :]
    cos, sin = jnp.cos(ang), jnp.sin(ang)
    cos2 = jnp.tile(jnp.concatenate([cos, cos], axis=-1), (1, MLA_HEADS))
    sin2 = jnp.tile(jnp.concatenate([-sin, sin], axis=-1), (1, MLA_HEADS))
    return cos2, sin2


def _layer(x, mod, past, lw, fw, use_moe, final_g, cfg):
    B, T, D = x.shape
    bb, tt, KC, tq_dsa, tq_mla = cfg["bb"], cfg["tt"], cfg["KC"], cfg["tq_dsa"], cfg["tq_mla"]
    KCD = cfg["KCD"]
    N = B * T
    P = 0 if past is None else past["dsa_k"].shape[1]
    L = P + T
    sh1, sc1, g1, sh2, sc2, g2 = [m.reshape(B, 1, D) for m in jnp.split(mod, 6, axis=-1)]
    ng1 = lw["norm_mix_g"].reshape(1, 1, D)
    ng2 = lw["norm_ffn_g"].reshape(1, 1, D)

    emit = past is None
    res = _k_in(x, sc1, sh1, ng1, lw["w_in"], bb=bb, tt=tt, emit=emit, kc=KCD)
    z = res[0]
    zoff = Z_OFF if emit else 0
    z3 = z.reshape(B, T, ZW - zoff)
    if emit:
        k_new = res[1].reshape(B, T, DSA_HEADS, DSA_HEAD_DIM)
        v_new = res[2].reshape(B, T, DSA_HEADS, DSA_HEAD_DIM)
    else:
        k_new = z[:, C_AK:C_AK + 512].reshape(B, T, DSA_HEADS, DSA_HEAD_DIM)
        v_new = z[:, C_AV:C_AV + 512].reshape(B, T, DSA_HEADS, DSA_HEAD_DIM)
    ik_new = z3[:, :, C_IKW - zoff:C_IKW - zoff + IDX_DIM]

    if past is None:
        ph16 = jnp.zeros((B, 16, 512), F32)
        cvh16 = ph16
    else:
        ph16 = jnp.pad(past["pool"], ((0, 0), (1, 0), (0, 0)))
        cvh16 = jnp.pad(past["conv"], ((0, 0), (16 - (CONV_WIDTH - 1), 0), (0, 0)))
    o_b, o_c, ph_o, cvh_o = _k_pc(z3, ph16, cvh16, lw["pool_w"], lw["pool_scale"], lw["conv_w"],
                                  bb=bb, tt=tt, P=P, zoff=zoff)
    pool_hist = ph_o[:, 1:]
    conv_hist = cvh_o[:, 16 - (CONV_WIDTH - 1):]

    cos2, sin2 = _rope_tables(P + jnp.arange(T))
    if bb > 1:
        cos2, sin2 = jnp.tile(cos2, (bb, 1)), jnp.tile(sin2, (bb, 1))
    qlt, qrt, ckv_new, ckvb, ckvt, kr_new, krb = _k_mlaprep(
        z, cos2, sin2, lw["mla_q_norm_g"], lw["mla_kv_norm_g"], lw["mla_w_uq"], lw["mla_w_uk"], tm=bb * tt,
        zoff=zoff)

    topk = min(DSA_TOPK_MAX, L // 4)
    if past is None:
        kb3 = res[3].reshape(B, T, 512)
        vt4, qt, iqt, iwt = res[4], res[5], res[6], res[7]
        ikb3 = res[8].reshape(B, T, 128)
        ckvb3 = ckvb.reshape(B, T, 256)
        krb3 = krb.reshape(B, T, 32)
        Tq = T
    else:
        assert T <= KCD and T <= KC
        padn = lambda a, kc: jnp.pad(a, ((0, 0), (0, kc - T), (0, 0)))
        kb3, vt4, ikb3 = _k_dsa_keys(past["dsa_k"], past["dsa_v"], past["idx_k"],
                                     padn(k_new.reshape(B, T, 512), KCD), padn(v_new.reshape(B, T, 512), KCD),
                                     padn(ik_new, KCD), KC=KCD, row0=past["row0"])
        Tq = tq_dsa
        padq = ((0, 0), (0, 0), (0, Tq - T))
        qt = jnp.pad((z3[:, :, C_AQ:C_AQ + 512] * (DSA_SCALE * LOG2E)).transpose(0, 2, 1).astype(BF16), padq)
        iqt = jnp.pad(z3[:, :, C_IQ:C_IQ + 256].transpose(0, 2, 1).astype(BF16), padq)
        iwt = jnp.pad(z3[:, :, C_IKW + 64:C_IKW + 72].transpose(0, 2, 1) * IDX_SCALE, padq)
        ckvb3, ckvt4, krb3 = _k_mla_keys(past["ckv"], past["krope"], padn(ckv_new.reshape(B, T, 256), KC),
                                         padn(kr_new.reshape(B, T, 32), KC), KC=KC, row0=past["row0"])

    o_a = _k_dsa(qt, iqt, iwt, kb3, vt4, ikb3, tq=tq_dsa, P=P, L=L, KC=KCD, topk=topk)
    o_a = o_a[:, :T].reshape(N, 512)
    if past is None:
        ckvt4 = ckvt.reshape(B, T // KC, MLA_VR, KC)
        o_d = _k_mla(qlt, qrt, ckvb3, krb3, ckvt4, lw["mla_w_uv"], B=B, T=T, tq=tq_mla, P=P, L=L, KC=KC,
                     stacked=False).reshape(N, 512)
    else:
        qs = qlt.reshape(MLA_HEADS, 256, B, T).transpose(2, 1, 0, 3).reshape(B, 256, MLA_HEADS * T)
        rs = qrt.reshape(MLA_HEADS, 32, B, T).transpose(2, 1, 0, 3).reshape(B, 32, MLA_HEADS * T)
        o5 = _k_mla(qs, rs, ckvb3, krb3, ckvt4, lw["mla_w_uv"], B=B, T=T, tq=tq_mla, P=P, L=L, KC=KC,
                    stacked=True).reshape(B, MLA_HEADS, MLA_V, MLA_HEADS, T)
        o_d = jnp.stack([o5[:, h, :, h, :] for h in range(MLA_HEADS)], axis=1)
        o_d = o_d.transpose(0, 3, 1, 2).reshape(N, 512)

    x1 = _k_merge(x, sc1, sh1, g1, ng1, o_a, o_b.reshape(N, 512), o_c.reshape(N, 512), o_d,
                  lw["w_gate"], lw["b_gate"], lw["w_branch"], lw["w_out"], bb=bb, tt=tt)
    fg = None if final_g is None else final_g.reshape(1, 1, D)
    if use_moe:
        x2 = _k_moe(x1, sc2, sh2, g2, ng2, fw["rw"], fw["rb"], fw["w1"], fw["w3"], fw["w2"], fg,
                    bb=cfg["bb_moe"], tt=cfg["tt_moe"])
    else:
        x2 = _k_ffn(x1, sc2, sh2, g2, ng2, fw["w1"], fw["w3"], fw["w2"], fg, bb=bb, tt=tt)
    rows = (k_new, v_new, ik_new, ckv_new.reshape(B, T, 256), kr_new.reshape(B, T, 32), pool_hist, conv_hist)
    return x2, rows


def _prep_layer(l, ada_w, ada_b, norm_mix_g, norm_ffn_g, w_in, mla_q_norm_g, mla_kv_norm_g,
                mla_w_uq, mla_w_uk, mla_w_uv, pool_w, pool_scale, conv_w, w_gate, b_gate, w_branch, w_out):
    return dict(
        ada_w=_cast_bf16(ada_w, l, 1)[0], ada_b=ada_b[l].reshape(1, -1),
        norm_mix_g=norm_mix_g[l], norm_ffn_g=norm_ffn_g[l],
        w_in=_prep_w_in(w_in[l]),
        mla_q_norm_g=mla_q_norm_g[l].reshape(1, -1), mla_kv_norm_g=mla_kv_norm_g[l].reshape(1, -1),
        mla_w_uq=_prep_w_uq(mla_w_uq[l]),
        mla_w_uk=mla_w_uk[l].transpose(1, 0, 2).astype(BF16),
        mla_w_uv=mla_w_uv[l].transpose(1, 2, 0).astype(BF16),
        pool_w=pool_w[l].astype(BF16), pool_scale=pool_scale[l].reshape(1, -1),
        conv_w=jnp.pad(conv_w[l].reshape(CONV_WIDTH, BRANCH_W), ((0, 8 - CONV_WIDTH), (0, 0))),
        w_gate=_cast_bf16(w_gate, l, 1)[0], b_gate=b_gate[l].reshape(1, -1),
        w_branch=_cast_bf16(w_branch, l, 1)[0], w_out=_cast_bf16(w_out, l, 1)[0])


def _cfg_for(B, T, has_past):
    if has_past:
        return dict(bb=B, tt=T, KC=512, KCD=512, tq_dsa=128, tq_mla=T, bb_moe=B, tt_moe=T)
    t = min(512, T)
    return dict(bb=1, tt=t, KC=t, KCD=t, tq_dsa=t, tq_mla=t, bb_moe=1, tt_moe=min(1024, T))


def _run(x, c, pasts, lws, fws, final_norm_g):
    B, T, _ = x.shape
    cfg = _cfg_for(B, T, pasts is not None)
    outs = [[] for _ in range(7)]
    depth = len(lws)
    for l in range(depth):
        lw = lws[l]
        mod = _ada(c, lw["ada_w"], lw["ada_b"])
        x, rows = _layer(x, mod, None if pasts is None else pasts[l], lw, fws[l], l % 2 == 1,
                         final_norm_g if l == depth - 1 else None, cfg)
        for o, r in zip(outs, rows):
            o.append(r)
    return x, [jnp.stack(o) for o in outs]


def kernel(x_prompt, x_sample, c_prompt, c_sample, cache_dsa_k, cache_dsa_v, cache_dsa_idx_k, cache_mla_ckv, cache_mla_krope, state_pool, state_conv, ada_w, ada_b, norm_mix_g, norm_ffn_g, w_in, mla_q_norm_g, mla_kv_norm_g, mla_w_uq, mla_w_uk, mla_w_uv, pool_w, pool_scale, conv_w, w_gate, b_gate, w_branch, w_out, ffn_w1, ffn_w3, ffn_w2, moe_router_w, moe_router_b, moe_w1, moe_w3, moe_w2, final_norm_g):
    depth = ada_w.shape[0]
    lws = [_prep_layer(l, ada_w, ada_b, norm_mix_g, norm_ffn_g, w_in, mla_q_norm_g, mla_kv_norm_g,
                       mla_w_uq, mla_w_uk, mla_w_uv, pool_w, pool_scale, conv_w, w_gate, b_gate,
                       w_branch, w_out) for l in range(depth)]
    fws = []
    for l in range(depth):
        j = l // 2
        if l % 2 == 0:
            fws.append(dict(w1=_cast_bf16(ffn_w1, j, 1)[0], w3=_cast_bf16(ffn_w3, j, 1)[0],
                            w2=_cast_bf16(ffn_w2, j, 1)[0]))
        else:
            fws.append(dict(
                rw=jnp.pad(moe_router_w[j], ((0, 0), (0, 128 - N_EXPERTS))),
                rb=jnp.pad(moe_router_b[j].reshape(1, -1), ((0, 0), (0, 128 - N_EXPERTS)), constant_values=NEG),
                w1=_cast_bf16(moe_w1.reshape((-1,) + moe_w1.shape[2:]), j * N_EXPERTS, N_EXPERTS),
                w3=_cast_bf16(moe_w3.reshape((-1,) + moe_w3.shape[2:]), j * N_EXPERTS, N_EXPERTS),
                w2=_cast_bf16(moe_w2.reshape((-1,) + moe_w2.shape[2:]), j * N_EXPERTS, N_EXPERTS)))
    nb, plen = cache_dsa_k.shape[1], cache_dsa_k.shape[2]
    flat = lambda a: a.reshape(depth * nb, plen, -1)
    sample_pasts = [dict(dsa_k=flat(cache_dsa_k), dsa_v=flat(cache_dsa_v), idx_k=flat(cache_dsa_idx_k),
                         ckv=flat(cache_mla_ckv), krope=flat(cache_mla_krope), row0=l * nb,
                         pool=state_pool[l], conv=state_conv[l]) for l in range(depth)]
    y_prompt, pn = _run(x_prompt, c_prompt, None, lws, fws, final_norm_g)
    y_sample, sn = _run(x_sample, c_sample, sample_pasts, lws, fws, final_norm_g)
    return (y_prompt, y_sample, pn[0], pn[1], pn[2], pn[3], pn[4], pn[5], pn[6],
            sn[0], sn[1], sn[2], sn[3], sn[4], sn[5], sn[6])
```

```python
import functools

import numpy as np
import jax
import jax.numpy as jnp
from jax import lax
from jax.experimental import pallas as pl
from jax.experimental.pallas import tpu as pltpu

F32 = jnp.float32
BF16 = jnp.bfloat16
I32 = jnp.int32

D_MODEL = 1024
CHUNK = 64
N_BRANCH = 4
BRANCH_W = 512
DSA_HEADS = 8
DSA_HEAD_DIM = 64
IDX_HEADS = 4
IDX_DIM = 64
DSA_TOPK_MAX = 256
POOL_WINDOWS = (2, 4, 8, 16)
POOL_GROUP = BRANCH_W // 4
POOL_HIST = 15
CONV_WIDTH = 3
MLA_HEADS = 8
MLA_Q_LORA = 256
MLA_KV_LORA = 256
MLA_NOPE = 64
MLA_ROPE = 32
MLA_V = 64
ROPE_BASE = 10000.0
FFN_DIM = 2816
N_EXPERTS = 8
EPS = 1e-6

DSA_SCALE = DSA_HEAD_DIM ** -0.5
IDX_SCALE = (IDX_HEADS * IDX_DIM) ** -0.5
MLA_SCALE = (MLA_NOPE + MLA_ROPE) ** -0.5
LOG2E = 1.4426950408889634

IN_SPLIT_WIDTHS = (512, 512, 512, 256, 64, 4, 512, 512, 512, 512, 256, 256, 32)
IN_SPLIT_POINTS = tuple(int(sum(IN_SPLIT_WIDTHS[:i + 1])) for i in range(len(IN_SPLIT_WIDTHS) - 1))

C_AQ, C_AK, C_AV, C_PU, C_CB, C_CC, C_CH = 0, 512, 1024, 1536, 2048, 2560, 3072
C_IQ, C_MQ, C_MKV, C_IKW, C_MKR, ZW = 3584, 3840, 4096, 4352, 4480, 4608
IN_TN = 1536
Z_OFF = IN_TN
DSA_VR = DSA_HEAD_DIM + 16
MLA_VR = MLA_KV_LORA + 16

NEG = -1e30
M_INIT = -1e29
INT_MIN = np.int32(-2 ** 31)
V7X_VMEM_LIMIT = 56 * 1024 * 1024


def _cparams(sem):
    return pltpu.CompilerParams(dimension_semantics=sem, vmem_limit_bytes=V7X_VMEM_LIMIT)


def _rms(x, g):
    return x * lax.rsqrt(jnp.mean(x * x, axis=-1, keepdims=True) + EPS) * g


def _sigmoid(x):
    return 1.0 / (1.0 + jnp.exp(-x))


def _const_spec(shape):
    nd = len(shape)
    return pl.BlockSpec(shape, lambda *_: (0,) * nd, pipeline_mode=pl.Buffered(1))


def _ada_body(c_ref, w_ref, b_ref, o_ref):
    c = c_ref[...]
    s = (c * _sigmoid(c)).astype(BF16)
    o_ref[...] = jnp.dot(s, w_ref[...], preferred_element_type=F32) + b_ref[...]


def _ada(c, w_bf, b):
    B, D = c.shape
    n = w_bf.shape[1]
    tn = 1536
    return pl.pallas_call(
        _ada_body,
        grid=(n // tn,),
        in_specs=[pl.BlockSpec((B, D), lambda j: (0, 0)),
                  pl.BlockSpec((D, tn), lambda j: (0, j)),
                  pl.BlockSpec((1, tn), lambda j: (0, j))],
        out_specs=pl.BlockSpec((B, tn), lambda j: (0, j)),
        out_shape=jax.ShapeDtypeStruct((B, n), F32),
        compiler_params=_cparams(("arbitrary",)),
        name="ada_mod",
    )(c, w_bf, b)


def _in_body(x_ref, sc_ref, sh_ref, g_ref, w_ref, z_ref, *rest, bb, tt, emit, kc):
    if emit:
        kn_ref, vn_ref, kb_ref, vt_ref, qt_ref, iqt_ref, iwt_ref, ikb_ref, h_scr = rest
    else:
        (h_scr,) = rest
    j = pl.program_id(1)

    @pl.when(j == 0)
    def _():
        h = _rms(x_ref[...], g_ref[...]) * (1.0 + sc_ref[...]) + sh_ref[...]
        h_scr[...] = h.reshape(bb * tt, D_MODEL).astype(BF16)

    z = jnp.dot(h_scr[...], w_ref[...], preferred_element_type=F32)

    if not emit:
        z_ref[...] = z
    else:
        @pl.when(j > 0)
        def _():
            z_ref[...] = z

        @pl.when(j == 0)
        def _():
            k = z[:, C_AK:C_AK + 512]
            v = z[:, C_AV:C_AV + 512]
            kn_ref[...] = k
            vn_ref[...] = v
            kb_ref[...] = k.astype(BF16)
            vt = v.T.astype(BF16)
            ones = jnp.ones((DSA_VR - DSA_HEAD_DIM, kc), BF16)
            for c in range(tt // kc):
                for h in range(DSA_HEADS):
                    vt_ref[0, c, h * DSA_VR:h * DSA_VR + DSA_HEAD_DIM, :] = vt[
                        h * DSA_HEAD_DIM:(h + 1) * DSA_HEAD_DIM, c * kc:(c + 1) * kc]
                    vt_ref[0, c, h * DSA_VR + DSA_HEAD_DIM:(h + 1) * DSA_VR, :] = ones
            qt_ref[0] = (z[:, C_AQ:C_AQ + 512] * (DSA_SCALE * LOG2E)).T.astype(BF16)

        @pl.when(j == 2)
        def _():
            o = 2 * IN_TN
            iqt_ref[0] = z[:, C_IQ - o:C_IQ - o + 256].T.astype(BF16)
            ikw = z[:, C_IKW - o:C_IKW - o + 128]
            ikb_ref[...] = ikw.astype(BF16)
            iwt_ref[0] = ikw.T[64:72, :] * IDX_SCALE


def _k_in(x, sc, sh, g, w_bf, *, bb, tt, emit, kc):
    B, T, D = x.shape
    nt = T // tt
    tm = bb * tt
    n_rows = (B // bb) * nt
    N = B * T
    xmap = lambda i, j: (i // nt, i % nt, 0)
    mmap = lambda i, j: (i // nt, 0, 0)
    in_specs = [pl.BlockSpec((bb, tt, D), xmap),
                pl.BlockSpec((bb, 1, D), mmap),
                pl.BlockSpec((bb, 1, D), mmap),
                _const_spec((1, 1, D)),
                pl.BlockSpec((D, IN_TN), lambda i, j: (0, j))]
    if emit:
        assert bb == 1
        out_specs = [pl.BlockSpec((tm, IN_TN), lambda i, j: (i, jnp.maximum(j - 1, 0)))]
        out_shape = [jax.ShapeDtypeStruct((N, ZW - Z_OFF), F32)]
        out_specs += [
            pl.BlockSpec((tm, 512), lambda i, j: (i, 0)),
            pl.BlockSpec((tm, 512), lambda i, j: (i, 0)),
            pl.BlockSpec((tm, 512), lambda i, j: (i, 0)),
            pl.BlockSpec((1, tt // kc, DSA_HEADS * DSA_VR, kc), lambda i, j: (i // nt, i % nt, 0, 0)),
            pl.BlockSpec((1, 512, tt), lambda i, j: (i // nt, 0, i % nt)),
            pl.BlockSpec((1, 256, tt), lambda i, j: (i // nt, 0, i % nt)),
            pl.BlockSpec((1, 8, tt), lambda i, j: (i // nt, 0, i % nt)),
            pl.BlockSpec((tm, 128), lambda i, j: (i, 0)),
        ]
        out_shape += [
            jax.ShapeDtypeStruct((N, 512), F32),
            jax.ShapeDtypeStruct((N, 512), F32),
            jax.ShapeDtypeStruct((N, 512), BF16),
            jax.ShapeDtypeStruct((B, T // kc, DSA_HEADS * DSA_VR, kc), BF16),
            jax.ShapeDtypeStruct((B, 512, T), BF16),
            jax.ShapeDtypeStruct((B, 256, T), BF16),
            jax.ShapeDtypeStruct((B, 8, T), F32),
            jax.ShapeDtypeStruct((N, 128), BF16),
        ]
    else:
        out_specs = [pl.BlockSpec((tm, IN_TN), lambda i, j: (i, j))]
        out_shape = [jax.ShapeDtypeStruct((N, ZW), F32)]
    return pl.pallas_call(
        functools.partial(_in_body, bb=bb, tt=tt, emit=emit, kc=kc),
        grid=(n_rows, ZW // IN_TN),
        in_specs=in_specs,
        out_specs=out_specs,
        out_shape=out_shape,
        scratch_shapes=[pltpu.VMEM((tm, D), BF16)],
        compiler_params=_cparams(("arbitrary", "arbitrary")),
        name="in_proj",
    )(x, sc, sh, g, w_bf)


def _pc_body(pu_ref, cb_ref, cc_ref, ch_ref, pup_ref, ccp_ref, chp_ref, ph_ref, cvh_ref,
             pw_ref, ps_ref, cw_ref, ob_ref, oc_ref, pho_ref, cho_ref, u_scr, g_scr, *, bb, tt, P):
    t = pl.program_id(1)
    first = t == 0
    pu = pu_ref[...]
    g = cc_ref[...] * ch_ref[...]
    u_scr[:, 0:16, :] = jnp.where(first, ph_ref[...], pup_ref[...])
    u_scr[:, 16:16 + tt, :] = pu
    g_scr[:, 0:16, :] = jnp.where(first, cvh_ref[...], ccp_ref[...] * chp_ref[...])
    g_scr[:, 16:16 + tt, :] = g

    pos = P + t * tt + lax.broadcasted_iota(I32, (1, tt, 1), 1)
    for gi, win in enumerate(POOL_WINDOWS):
        lo = gi * POOL_GROUP
        acc = pu[:, :, lo:lo + POOL_GROUP]
        for j in range(1, win):
            acc = acc + u_scr[:, 16 - j:16 - j + tt, lo:lo + POOL_GROUP]
        cnt = jnp.minimum(pos + 1, win).astype(F32)
        d = acc / cnt - pu[:, :, lo:lo + POOL_GROUP]
        y = jnp.dot(d.reshape(bb * tt, POOL_GROUP).astype(BF16), pw_ref[gi], preferred_element_type=F32)
        ob_ref[:, :, lo:lo + POOL_GROUP] = (y * ps_ref[:, lo:lo + POOL_GROUP]).reshape(bb, tt, POOL_GROUP)

    cw = cw_ref[...]
    conv = (cw[0:1, :] * g_scr[:, 14:14 + tt, :] + cw[1:2, :] * g_scr[:, 15:15 + tt, :] + cw[2:3, :] * g)
    oc_ref[...] = cb_ref[...] * conv
    pho_ref[...] = u_scr[:, tt:tt + 16, :]
    cho_ref[...] = g_scr[:, tt:tt + 16, :]


def _k_pc(z3, ph16, cvh16, pw_bf, ps, cw, *, bb, tt, P, zoff):
    B, T, _ = z3.shape
    nt = T // tt
    r = tt // 16

    def cur(c):
        return pl.BlockSpec((bb, tt, 512), lambda b, t: (b, t, (c - zoff) // 512))

    def prev(c):
        return pl.BlockSpec((bb, 16, 512), lambda b, t: (b, jnp.maximum(t * r - 1, 0), (c - zoff) // 512))

    hist = pl.BlockSpec((bb, 16, 512), lambda b, t: (b, 0, 0))
    return pl.pallas_call(
        functools.partial(_pc_body, bb=bb, tt=tt, P=P),
        grid=(B // bb, nt),
        in_specs=[cur(C_PU), cur(C_CB), cur(C_CC), cur(C_CH), prev(C_PU), prev(C_CC), prev(C_CH),
                  hist, hist, _const_spec((4, POOL_GROUP, POOL_GROUP)), _const_spec((1, 512)),
                  _const_spec((8, 512))],
        out_specs=[pl.BlockSpec((bb, tt, 512), lambda b, t: (b, t, 0)),
                   pl.BlockSpec((bb, tt, 512), lambda b, t: (b, t, 0)), hist, hist],
        out_shape=[jax.ShapeDtypeStruct((B, T, 512), F32), jax.ShapeDtypeStruct((B, T, 512), F32),
                   jax.ShapeDtypeStruct((B, 16, 512), F32), jax.ShapeDtypeStruct((B, 16, 512), F32)],
        scratch_shapes=[pltpu.VMEM((bb, tt + 16, 512), F32), pltpu.VMEM((bb, tt + 16, 512), F32)],
        compiler_params=_cparams(("arbitrary", "arbitrary")),
        name="pool_conv",
    )(z3, z3, z3, z3, z3, z3, z3, ph16, cvh16, pw_bf, ps, cw)


def _mp_body(mq_ref, mkv_ref, mkr_ref, cos_ref, sin_ref, gq_ref, gkv_ref, wuq_ref, wukt_ref,
             qlt_ref, qrt_ref, ckv_ref, ckvb_ref, ckvt_ref, kr_ref, krb_ref):
    a = _rms(mq_ref[...], gq_ref[...]).astype(BF16)
    cq = jnp.dot(a, wuq_ref[...], preferred_element_type=F32)
    cqt = cq.T
    cos = cos_ref[...]
    sin = sin_ref[...]
    c = MLA_SCALE * LOG2E
    qrt_ref[...] = ((cqt[512:768] * cos.T + cqt[768:1024] * sin.T) * c).astype(BF16)
    for h in range(MLA_HEADS):
        qnt = cqt[h * MLA_NOPE:(h + 1) * MLA_NOPE].astype(BF16)
        qlt_ref[h * MLA_KV_LORA:(h + 1) * MLA_KV_LORA, :] = (jnp.dot(
            wukt_ref[h], qnt, preferred_element_type=F32) * c).astype(BF16)
    ckv = _rms(mkv_ref[...], gkv_ref[...])
    ckv_ref[...] = ckv
    ckvb_ref[...] = ckv.astype(BF16)
    ckvt_ref[0, 0:MLA_KV_LORA, :] = ckv.T.astype(BF16)
    ckvt_ref[0, MLA_KV_LORA:MLA_VR, :] = jnp.ones((MLA_VR - MLA_KV_LORA, ckv.shape[0]), BF16)
    mkr = mkr_ref[...]
    kr = mkr[:, 0:32] * cos[:, 0:32] + mkr[:, 32:64] * sin[:, 0:32]
    kr_ref[...] = kr
    krb_ref[...] = kr.astype(BF16)


def _k_mlaprep(z, cos_t, sin_t, gq, gkv, wuq_bf, wukt_bf, *, tm, zoff):
    N = z.shape[0]
    nr = cos_t.shape[0] // tm
    tab = pl.BlockSpec((tm, 256), lambda i: (i % nr, 0))
    row = lambda w: pl.BlockSpec((tm, w), lambda i: (i, 0))
    col = lambda h: pl.BlockSpec((h, tm), lambda i: (0, i))
    return pl.pallas_call(
        _mp_body,
        grid=(N // tm,),
        in_specs=[pl.BlockSpec((tm, 256), lambda i: (i, (C_MQ - zoff) // 256)),
                  pl.BlockSpec((tm, 256), lambda i: (i, (C_MKV - zoff) // 256)),
                  pl.BlockSpec((tm, 128), lambda i: (i, (C_MKR - zoff) // 128)),
                  tab, tab, _const_spec((1, 256)), _const_spec((1, 256)),
                  _const_spec((256, 1024)), _const_spec((MLA_HEADS, MLA_KV_LORA, MLA_NOPE))],
        out_specs=[col(2048), col(256), row(256), row(256),
                   pl.BlockSpec((1, MLA_VR, tm), lambda i: (i, 0, 0)), row(32), row(32)],
        out_shape=[jax.ShapeDtypeStruct((2048, N), BF16), jax.ShapeDtypeStruct((256, N), BF16),
                   jax.ShapeDtypeStruct((N, 256), F32), jax.ShapeDtypeStruct((N, 256), BF16),
                   jax.ShapeDtypeStruct((N // tm, MLA_VR, tm), BF16),
                   jax.ShapeDtypeStruct((N, 32), F32), jax.ShapeDtypeStruct((N, 32), BF16)],
        compiler_params=_cparams(("arbitrary",)),
        name="mla_prep",
    )(z, z, z, cos_t, sin_t, gq, gkv, wuq_bf, wukt_bf)


def _dot_nt(a, b):
    return lax.dot_general(a, b, (((1,), (1,)), ((), ())), preferred_element_type=F32)


MLA_SB = 128
MLA_CB = 256


def _mla_body(qlt_ref, qrt_ref, ckv_ref, kr_ref, ckvt_ref, wuvt_ref, o_ref,
              qt_scr, rt_scr, s_scr, pt_scr, m_scr, acc_scr,
              *, tq, P, L, KC, nkc_total, stacked):
    qi = pl.program_id(1)
    q0 = P + qi * tq
    cols_n = MLA_HEADS * tq
    if stacked:
        qt_scr[...] = qlt_ref[0]
        rt_scr[...] = qrt_ref[0]
    else:
        for h in range(MLA_HEADS):
            qt_scr[:, h * tq:(h + 1) * tq] = qlt_ref[h * 256:(h + 1) * 256, :]
            rt_scr[:, h * tq:(h + 1) * tq] = qrt_ref[h * 32:(h + 1) * 32, :]
    max_cend = (((q0 + tq - 1) >> 6) + 1) << 6
    nkc = jnp.minimum((max_cend + KC - 1) // KC, nkc_total)
    n_full = jnp.minimum(((q0 >> 6) + 1) << 6, L) // KC
    cb_w = min(MLA_CB, cols_n)

    m_scr[...] = jnp.full((1, cols_n), M_INIT, F32)
    acc_scr[...] = jnp.zeros((MLA_VR, cols_n), F32)

    def make_step(masked):
        def step(kc, carry):
            k0 = pl.multiple_of(kc * KC, KC)
            ck = ckv_ref[0, pl.ds(k0, KC), :]
            kr = kr_ref[0, pl.ds(k0, KC), :]
            s_scr[...] = (jnp.dot(ck, qt_scr[...], preferred_element_type=F32)
                          + jnp.dot(kr, rt_scr[...], preferred_element_type=F32))
            for cb in range(cols_n // cb_w):
                cols = slice(cb * cb_w, (cb + 1) * cb_w)
                if masked:
                    lane = cb * cb_w + lax.broadcasted_iota(I32, (1, cb_w), 1)
                    qpos = q0 + (lane & (tq - 1))
                    lim = jnp.minimum(((qpos >> 6) + 1) << 6, L)
                mx = jnp.full((8, cb_w), -jnp.inf, F32)
                for sb in range(KC // MLA_SB):
                    rs = slice(sb * MLA_SB, (sb + 1) * MLA_SB)
                    x = s_scr[rs, cols]
                    if masked:
                        kpos = k0 + sb * MLA_SB + lax.broadcasted_iota(I32, (MLA_SB, 1), 0)
                        x = jnp.where(kpos < lim, x, NEG)
                        s_scr[rs, cols] = x
                    mx = jnp.maximum(mx, _max8(x))
                m_old = m_scr[:, cols]
                m_new = jnp.maximum(m_old, jnp.max(mx, axis=0, keepdims=True))
                for sb in range(KC // MLA_SB):
                    rs = slice(sb * MLA_SB, (sb + 1) * MLA_SB)
                    pt_scr[rs, cols] = jnp.exp2(s_scr[rs, cols] - m_new).astype(BF16)
                m_scr[:, cols] = m_new
                acc_scr[:, cols] = (jnp.exp2(m_old - m_new) * acc_scr[:, cols]
                                    + jnp.dot(ckvt_ref[0, kc], pt_scr[:, cols], preferred_element_type=F32))
            return carry
        return step

    lax.fori_loop(0, n_full, make_step(False), 0)
    lax.fori_loop(n_full, nkc, make_step(True), 0)
    ot = (acc_scr[0:MLA_KV_LORA, :] / acc_scr[MLA_KV_LORA:MLA_KV_LORA + 1, :]).astype(BF16)
    if stacked:
        for h in range(MLA_HEADS):
            o_ref[0, h * MLA_V:(h + 1) * MLA_V, :] = jnp.dot(wuvt_ref[h], ot, preferred_element_type=F32)
    else:
        outs = [jnp.dot(wuvt_ref[h], ot[:, h * tq:(h + 1) * tq], preferred_element_type=F32)
                for h in range(MLA_HEADS)]
        o_ref[0] = jnp.concatenate(outs, axis=0).T


def _k_mla(qlt, qrt, ckvb, krb, ckvt, wuvt_bf, *, B, T, tq, P, L, KC, stacked):
    Lp = ckvb.shape[1]
    nkc_total = Lp // KC
    cols_n = MLA_HEADS * tq
    nq = T // tq
    if stacked:
        q_specs = [pl.BlockSpec((1, MLA_KV_LORA, cols_n), lambda b, q: (b, 0, 0)),
                   pl.BlockSpec((1, MLA_ROPE, cols_n), lambda b, q: (b, 0, 0))]
        out_spec = pl.BlockSpec((1, MLA_HEADS * MLA_V, cols_n), lambda b, q: (b, 0, 0))
        out_shape = jax.ShapeDtypeStruct((B, MLA_HEADS * MLA_V, cols_n), F32)
    else:
        q_specs = [pl.BlockSpec((MLA_HEADS * MLA_KV_LORA, tq), lambda b, q: (0, b * nq + q)),
                   pl.BlockSpec((MLA_HEADS * MLA_ROPE, tq), lambda b, q: (0, b * nq + q))]
        out_spec = pl.BlockSpec((1, tq, 512), lambda b, q: (b, q, 0))
        out_shape = jax.ShapeDtypeStruct((B, T, 512), F32)
    return pl.pallas_call(
        functools.partial(_mla_body, tq=tq, P=P, L=L, KC=KC, nkc_total=nkc_total, stacked=stacked),
        grid=(B, nq),
        in_specs=q_specs + [pl.BlockSpec((1, Lp, 256), lambda b, q: (b, 0, 0)),
                            pl.BlockSpec((1, Lp, 32), lambda b, q: (b, 0, 0)),
                            pl.BlockSpec((1, nkc_total, MLA_VR, KC), lambda b, q: (b, 0, 0, 0)),
                            _const_spec((MLA_HEADS, MLA_V, MLA_KV_LORA))],
        out_specs=out_spec,
        out_shape=out_shape,
        scratch_shapes=[pltpu.VMEM((MLA_KV_LORA, cols_n), BF16), pltpu.VMEM((MLA_ROPE, cols_n), BF16),
                        pltpu.VMEM((KC, cols_n), F32), pltpu.VMEM((KC, cols_n), BF16),
                        pltpu.VMEM((1, cols_n), F32), pltpu.VMEM((MLA_VR, cols_n), F32)],
        compiler_params=_cparams(("arbitrary", "arbitrary")),
        name="mla_attn",
    )(qlt, qrt, ckvb, krb, ckvt, wuvt_bf)


def _fold(x, r):
    n, w = x.shape
    return jnp.sum(x.reshape(n // r, r, w), axis=0)


def _fold8(x):
    return _fold(x, 8)


def _max8(x):
    n, w = x.shape
    return jnp.max(x.reshape(n // 8, 8, w), axis=0)


DSA_SB_ELEMS = 128 * 256


def _dsa_body(qt_ref, iqt_ref, iwt_ref, kb_ref, vt_ref, ikb_ref, o_ref,
              s_scr, w_scr, lg_scr, pt_scr, m_scr, acc_scr, *, tq, P, L, KC, nkc_total, topk):
    qi = pl.program_id(1)
    q0 = P + qi * tq
    sbk = min(KC, DSA_SB_ELEMS // tq)
    acc_r = sbk // 4
    qpos = q0 + lax.broadcasted_iota(I32, (1, tq), 1)
    lim = jnp.minimum(((qpos >> 6) + 1) << 6, L)
    keff = jnp.minimum(lim, topk).astype(F32)
    max_cend = (((q0 + tq - 1) >> 6) + 1) << 6
    nkc = jnp.minimum((max_cend + KC - 1) // KC, nkc_total)

    iqt = iqt_ref[0]
    iwt = iwt_ref[0]
    zpad = jnp.zeros((IDX_DIM, tq), BF16)
    iq_h = [jnp.concatenate([iqt[h * IDX_DIM:(h + 1) * IDX_DIM], zpad], axis=0) for h in range(IDX_HEADS)]

    def score_step(kc, carry):
        k0 = pl.multiple_of(kc * KC, KC)
        ik = ikb_ref[0, pl.ds(k0, KC), :]
        for h in range(IDX_HEADS):
            lg_scr[h, :, 0:tq] = jnp.dot(ik, iq_h[h], preferred_element_type=F32)
        for sb in range(KC // sbk):
            rs = slice(sb * sbk, (sb + 1) * sbk)
            s = jnp.zeros((sbk, tq), F32)
            for h in range(IDX_HEADS):
                s = s + iwt[h:h + 1, :] * jnp.maximum(lg_scr[h, rs, 0:tq], 0.0)
            kpos = k0 + sb * sbk + lax.broadcasted_iota(I32, (sbk, 1), 0)
            s = jnp.where(kpos < lim, s, -jnp.inf)
            b = lax.bitcast_convert_type(s, I32)
            b = jnp.where(b == INT_MIN, 0, b)
            s_scr[kc, rs, :] = b ^ ((b >> 31) & np.int32(0x7FFFFFFF))
        return carry

    lax.fori_loop(0, nkc, score_step, 0)

    def bit_step(i, carry):
        thr, n_ge = carry
        cand = thr + jnp.left_shift(jnp.int32(1), 31 - i)

        def cnt_step(kc, acc):
            for sb in range(KC // sbk):
                acc = acc + _fold(jnp.where(s_scr[kc, sb * sbk:(sb + 1) * sbk, :] >= cand, 1.0, 0.0), acc_r)
            return acc

        cnt = jnp.sum(lax.fori_loop(0, nkc, cnt_step, jnp.zeros((acc_r, tq), F32)), axis=0, keepdims=True)
        take = cnt >= keff
        return jnp.where(take, cand, thr), jnp.where(take, cnt, n_ge)

    n_all = jnp.zeros((1, tq), F32) + (nkc * KC).astype(F32)
    thr, n_ge = lax.fori_loop(0, 32, bit_step, (jnp.full((1, tq), INT_MIN, I32), n_all))

    surplus = jnp.max(n_ge - keff)

    @pl.when(surplus < 0.5)
    def _():
        def fast_step(kc, carry):
            s_scr[kc] = lax.bitcast_convert_type(jnp.where(s_scr[kc] >= thr, 0.0, NEG), I32)
            return carry
        lax.fori_loop(0, nkc, fast_step, 0)

    @pl.when(surplus >= 0.5)
    def _():
        tri = jnp.where(lax.broadcasted_iota(I32, (KC, KC), 1) < lax.broadcasted_iota(I32, (KC, KC), 0),
                        1.0, 0.0).astype(BF16)

        def gt_step(kc, acc):
            return acc + _fold8(jnp.where(s_scr[kc] > thr, 1.0, 0.0))

        n_gt = lax.fori_loop(0, nkc, gt_step, jnp.zeros((8, tq), F32))
        need = keff - jnp.sum(n_gt, axis=0, keepdims=True)

        def slow_step(kc, seen):
            key = s_scr[kc]
            eq = key == thr
            eqf = jnp.where(eq, 1.0, 0.0)
            rank = seen + jnp.dot(tri, eqf.astype(BF16), preferred_element_type=F32)
            bias = jnp.where(key > thr, 0.0, jnp.where(eq, jnp.where(rank < need, 0.0, NEG), NEG))
            s_scr[kc] = lax.bitcast_convert_type(bias, I32)
            return seen + jnp.sum(_fold8(eqf), axis=0, keepdims=True)
        lax.fori_loop(0, nkc, slow_step, jnp.zeros((1, tq), F32))

    hd = DSA_HEAD_DIM
    zq = jnp.zeros((hd, tq), BF16)
    for hp in range(DSA_HEADS // 2):
        r0 = hp * 2 * hd
        w_scr[hp, 0:hd, 0:tq] = qt_ref[0, r0:r0 + hd, :]
        w_scr[hp, 0:hd, tq:2 * tq] = zq
        w_scr[hp, hd:2 * hd, 0:tq] = zq
        w_scr[hp, hd:2 * hd, tq:2 * tq] = qt_ref[0, r0 + hd:r0 + 2 * hd, :]
    m_scr[...] = jnp.full((DSA_HEADS, tq), M_INIT, F32)
    acc_scr[...] = jnp.zeros((DSA_HEADS * DSA_VR, tq), F32)
    nsb = KC // sbk

    nhp = DSA_HEADS // 2

    def att_step(kc, carry):
        k0 = pl.multiple_of(kc * KC, KC)
        for hp in range(nhp):
            r0 = hp * 2 * hd
            lg_scr[hp] = jnp.dot(kb_ref[0, pl.ds(k0, KC), r0:r0 + 2 * hd], w_scr[hp],
                                 preferred_element_type=F32)
        for hp in range(nhp):
            li = hp
            for e in range(2):
                h = 2 * hp + e
                cols = slice(e * tq, (e + 1) * tq)
                mx = jnp.full((8, tq), -jnp.inf, F32)
                for sb in range(nsb):
                    rs = slice(sb * sbk, (sb + 1) * sbk)
                    x = lg_scr[li, rs, cols] + lax.bitcast_convert_type(s_scr[kc, rs, :], F32)
                    lg_scr[li, rs, cols] = x
                    mx = jnp.maximum(mx, _max8(x))
                m_old = m_scr[h:h + 1, :]
                m_new = jnp.maximum(m_old, jnp.max(mx, axis=0, keepdims=True))
                alpha = jnp.exp2(m_old - m_new)
                for sb in range(nsb):
                    rs = slice(sb * sbk, (sb + 1) * sbk)
                    pt_scr[h, rs, :] = jnp.exp2(lg_scr[li, rs, cols] - m_new).astype(BF16)
                m_scr[h:h + 1, :] = m_new
                hs = slice(h * DSA_VR, (h + 1) * DSA_VR)
                acc_scr[hs, :] = alpha * acc_scr[hs, :] + jnp.dot(vt_ref[0, kc, hs, :], pt_scr[h],
                                                                  preferred_element_type=F32)
        return carry

    lax.fori_loop(0, nkc, att_step, 0)
    outs = [acc_scr[h * DSA_VR:h * DSA_VR + hd, :] / acc_scr[h * DSA_VR + hd:h * DSA_VR + hd + 1, :]
            for h in range(DSA_HEADS)]
    o_ref[0] = jnp.concatenate(outs, axis=0).T


def _k_dsa(qt, iqt, iwt, kb3, vt4, ikb3, *, tq, P, L, KC, topk):
    B, _, T = qt.shape
    Lp = kb3.shape[1]
    nkc_total = Lp // KC
    return pl.pallas_call(
        functools.partial(_dsa_body, tq=tq, P=P, L=L, KC=KC, nkc_total=nkc_total, topk=topk),
        grid=(B, T // tq),
        in_specs=[pl.BlockSpec((1, 512, tq), lambda b, q: (b, 0, q)),
                  pl.BlockSpec((1, 256, tq), lambda b, q: (b, 0, q)),
                  pl.BlockSpec((1, 8, tq), lambda b, q: (b, 0, q)),
                  pl.BlockSpec((1, Lp, 512), lambda b, q: (b, 0, 0)),
                  pl.BlockSpec((1, nkc_total, DSA_HEADS * DSA_VR, KC), lambda b, q: (b, 0, 0, 0)),
                  pl.BlockSpec((1, Lp, 128), lambda b, q: (b, 0, 0))],
        out_specs=pl.BlockSpec((1, tq, 512), lambda b, q: (b, q, 0)),
        out_shape=jax.ShapeDtypeStruct((B, T, 512), F32),
        scratch_shapes=[pltpu.VMEM((nkc_total, KC, tq), I32),
                        pltpu.VMEM((DSA_HEADS // 2, 2 * DSA_HEAD_DIM, 2 * tq), BF16),
                        pltpu.VMEM((DSA_HEADS // 2, KC, 2 * tq), F32),
                        pltpu.VMEM((DSA_HEADS, KC, tq), BF16),
                        pltpu.VMEM((DSA_HEADS, tq), F32),
                        pltpu.VMEM((DSA_HEADS * DSA_VR, tq), F32)],
        compiler_params=_cparams(("arbitrary", "arbitrary")),
        name="dsa_attn",
    )(qt, iqt, iwt, kb3, vt4, ikb3)


def _merge_body(x_ref, sc_ref, sh_ref, g1_ref, ng_ref, oa_ref, ob_ref, oc_ref, od_ref,
                wg_ref, bg_ref, wb_ref, wo_ref, o_ref, *, bb, tt):
    x = x_ref[...]
    tm = bb * tt
    h = (_rms(x, ng_ref[...]) * (1.0 + sc_ref[...]) + sh_ref[...]).reshape(tm, D_MODEL).astype(BF16)
    merged = jnp.zeros((tm, D_MODEL), F32)
    for i, oref in enumerate((oa_ref, ob_ref, oc_ref, od_ref)):
        gate = _sigmoid(jnp.dot(h, wg_ref[:, i * D_MODEL:(i + 1) * D_MODEL], preferred_element_type=F32)
                        + bg_ref[:, i * D_MODEL:(i + 1) * D_MODEL])
        br = jnp.dot(oref[...].astype(BF16), wb_ref[i * BRANCH_W:(i + 1) * BRANCH_W, :],
                     preferred_element_type=F32)
        merged = merged + gate * br
    y = jnp.dot(merged.astype(BF16), wo_ref[...], preferred_element_type=F32)
    o_ref[...] = x + g1_ref[...] * y.reshape(bb, tt, D_MODEL)


def _k_merge(x, sc, sh, g1, ng, oa, ob, oc, od, wg_bf, bg, wb_bf, wo_bf, *, bb, tt):
    B, T, D = x.shape
    nt = T // tt
    tm = bb * tt
    xs = pl.BlockSpec((bb, tt, D), lambda i: (i // nt, i % nt, 0))
    ms = pl.BlockSpec((bb, 1, D), lambda i: (i // nt, 0, 0))
    os_ = pl.BlockSpec((tm, 512), lambda i: (i, 0))
    return pl.pallas_call(
        functools.partial(_merge_body, bb=bb, tt=tt),
        grid=((B // bb) * nt,),
        in_specs=[xs, ms, ms, ms, _const_spec((1, 1, D)), os_, os_, os_, os_,
                  _const_spec((D, 4 * D)), _const_spec((1, 4 * D)), _const_spec((4 * BRANCH_W, D)),
                  _const_spec((D, D))],
        out_specs=xs,
        out_shape=jax.ShapeDtypeStruct((B, T, D), F32),
        compiler_params=_cparams(("arbitrary",)),
        name="merge_out",
    )(x, sc, sh, g1, ng, oa, ob, oc, od, wg_bf, bg, wb_bf, wo_bf)


FC = FFN_DIM // 2
MOE_RB = 256


def _swiglu_chunk(h, w1, w3, w2):
    a = jnp.dot(h, w1, preferred_element_type=F32)
    b = jnp.dot(h, w3, preferred_element_type=F32)
    act = (a * _sigmoid(a) * b).astype(BF16)
    return jnp.dot(act, w2, preferred_element_type=F32)


def _finish(x, g2, f, fg_ref, o_ref, bb, tt):
    y = x + g2 * f.reshape(bb, tt, D_MODEL)
    if fg_ref is not None:
        y = _rms(y, fg_ref[...])
    o_ref[...] = y


def _ffn_body(x_ref, sc_ref, sh_ref, g2_ref, ng_ref, w1_ref, w3_ref, w2_ref, *rest, bb, tt, final):
    fg_ref, o_ref = rest if final else (None, rest[0])
    x = x_ref[...]
    tm = bb * tt
    h = (_rms(x, ng_ref[...]) * (1.0 + sc_ref[...]) + sh_ref[...]).reshape(tm, D_MODEL).astype(BF16)
    f = jnp.zeros((tm, D_MODEL), F32)
    for ci in range(FFN_DIM // FC):
        f = f + _swiglu_chunk(h, w1_ref[:, ci * FC:(ci + 1) * FC], w3_ref[:, ci * FC:(ci + 1) * FC],
                              w2_ref[ci * FC:(ci + 1) * FC, :])
    _finish(x, g2_ref[...], f, fg_ref, o_ref, bb, tt)


def _k_ffn(x, sc, sh, g2, ng, w1_bf, w3_bf, w2_bf, fg, *, bb, tt):
    B, T, D = x.shape
    nt = T // tt
    final = fg is not None
    xs = pl.BlockSpec((bb, tt, D), lambda i: (i // nt, i % nt, 0))
    ms = pl.BlockSpec((bb, 1, D), lambda i: (i // nt, 0, 0))
    in_specs = [xs, ms, ms, ms, _const_spec((1, 1, D)), _const_spec((D, FFN_DIM)),
                _const_spec((D, FFN_DIM)), _const_spec((FFN_DIM, D))]
    args = [x, sc, sh, g2, ng, w1_bf, w3_bf, w2_bf]
    if final:
        in_specs.append(_const_spec((1, 1, D)))
        args.append(fg)
    return pl.pallas_call(
        functools.partial(_ffn_body, bb=bb, tt=tt, final=final),
        grid=((B // bb) * nt,),
        in_specs=in_specs,
        out_specs=xs,
        out_shape=jax.ShapeDtypeStruct((B, T, D), F32),
        compiler_params=_cparams(("arbitrary",)),
        name="ffn_dense",
    )(*args)


def _moe_body(x_ref, sc_ref, sh_ref, g2_ref, ng_ref, rw_ref, rb_ref, w1_ref, w3_ref, w2_ref, *rest,
              bb, tt, final):
    if final:
        fg_ref, o_ref, h_scr, gate_scr, rank_scr, rankt_scr, xg_scr, y_scr, acc_scr = rest
    else:
        fg_ref = None
        o_ref, h_scr, gate_scr, rank_scr, rankt_scr, xg_scr, y_scr, acc_scr = rest
    e = pl.program_id(1)
    ci = pl.program_id(2)
    last_c = FFN_DIM // FC - 1
    tm = bb * tt
    lane = lax.broadcasted_iota(I32, (tm, 128), 1)

    @pl.when((e == 0) & (ci == 0))
    def _():
        h = (_rms(x_ref[...], ng_ref[...]) * (1.0 + sc_ref[...]) + sh_ref[...]).reshape(tm, D_MODEL)
        h_scr[...] = h.astype(BF16)
        logits = jnp.dot(h, rw_ref[...], preferred_element_type=F32,
                         precision=lax.Precision.HIGHEST) + rb_ref[...]
        m1 = jnp.max(logits, axis=-1, keepdims=True)
        i1 = jnp.min(jnp.where(logits == m1, lane, 128), axis=-1, keepdims=True)
        rest_l = jnp.where(lane == i1, NEG, logits)
        m2 = jnp.max(rest_l, axis=-1, keepdims=True)
        i2 = jnp.min(jnp.where(rest_l == m2, lane, 128), axis=-1, keepdims=True)
        e2 = jnp.exp(m2 - m1)
        den = 1.0 + e2
        gate_scr[...] = jnp.where(lane == i1, 1.0 / den, jnp.where(lane == i2, e2 / den, 0.0))
        sel = jnp.where(lane == i1, 1.0, jnp.where(lane == i2, 1.0, 0.0))
        tri = jnp.where(lax.broadcasted_iota(I32, (tm, tm), 1) < lax.broadcasted_iota(I32, (tm, tm), 0),
                        1.0, 0.0).astype(BF16)
        rank = jnp.where(sel > 0.5, jnp.dot(tri, sel.astype(BF16), preferred_element_type=F32), -1.0)
        rank_scr[...] = rank
        rankt_scr[...] = rank.T
        acc_scr[...] = jnp.zeros((tm, D_MODEL), F32)

    is_e = lane == e
    gate_col = jnp.sum(jnp.where(is_e, gate_scr[...], 0.0), axis=-1, keepdims=True)
    rank_col = jnp.sum(jnp.where(is_e, rank_scr[...], 0.0), axis=-1, keepdims=True)
    rank_row = rankt_scr[pl.ds(e, 1), :]
    n_e = jnp.max(rank_row).astype(I32) + 1
    nblk = (n_e + MOE_RB - 1) // MOE_RB

    def blk_step(b, carry):
        base = (b * MOE_RB).astype(F32)

        @pl.when(ci == 0)
        def _():
            slot = base + lax.broadcasted_iota(I32, (MOE_RB, 1), 0).astype(F32)
            pick = jnp.where(rank_row == slot, 1.0, 0.0).astype(BF16)
            xg_scr[b] = jnp.dot(pick, h_scr[...], preferred_element_type=F32).astype(BF16)

        part = _swiglu_chunk(xg_scr[b], w1_ref[0], w3_ref[0], w2_ref[0])

        @pl.when(ci == 0)
        def _():
            y_scr[b] = part

        @pl.when((ci > 0) & (ci < last_c))
        def _():
            y_scr[b] += part

        @pl.when(ci == last_c)
        def _():
            y = y_scr[b] + part
            y_hi = y.astype(BF16)
            y_lo = (y - y_hi.astype(F32)).astype(BF16)
            slot = base + lax.broadcasted_iota(I32, (1, MOE_RB), 1).astype(F32)
            put = jnp.where(rank_col == slot, 1.0, 0.0).astype(BF16)
            acc_scr[...] += gate_col * (jnp.dot(put, y_hi, preferred_element_type=F32)
                                        + jnp.dot(put, y_lo, preferred_element_type=F32))
        return carry

    lax.fori_loop(0, nblk, blk_step, 0)

    @pl.when((e == N_EXPERTS - 1) & (ci == last_c))
    def _():
        _finish(x_ref[...], g2_ref[...], acc_scr[...], fg_ref, o_ref, bb, tt)


def _k_moe(x, sc, sh, g2, ng, rw_pad, rb_pad, w1_bf, w3_bf, w2_bf, fg, *, bb, tt):
    B, T, D = x.shape
    nt = T // tt
    tm = bb * tt
    final = fg is not None
    xs = pl.BlockSpec((bb, tt, D), lambda i, e, c: (i // nt, i % nt, 0))
    xin = pl.BlockSpec((bb, tt, D), lambda i, e, c: (i // nt, i % nt, 0), pipeline_mode=pl.Buffered(1))
    ms = pl.BlockSpec((bb, 1, D), lambda i, e, c: (i // nt, 0, 0))
    in_specs = [xin, ms, ms, ms, _const_spec((1, 1, D)), _const_spec((D, 128)), _const_spec((1, 128)),
                pl.BlockSpec((1, D, FC), lambda i, e, c: (e, 0, c)),
                pl.BlockSpec((1, D, FC), lambda i, e, c: (e, 0, c)),
                pl.BlockSpec((1, FC, D), lambda i, e, c: (e, c, 0))]
    args = [x, sc, sh, g2, ng, rw_pad, rb_pad, w1_bf, w3_bf, w2_bf]
    if final:
        in_specs.append(_const_spec((1, 1, D)))
        args.append(fg)
    return pl.pallas_call(
        functools.partial(_moe_body, bb=bb, tt=tt, final=final),
        grid=((B // bb) * nt, N_EXPERTS, FFN_DIM // FC),
        in_specs=in_specs,
        out_specs=xs,
        out_shape=jax.ShapeDtypeStruct((B, T, D), F32),
        scratch_shapes=[pltpu.VMEM((tm, D), BF16), pltpu.VMEM((tm, 128), F32), pltpu.VMEM((tm, 128), F32),
                        pltpu.VMEM((128, tm), F32), pltpu.VMEM((pl.cdiv(tm, MOE_RB), MOE_RB, D), BF16),
                        pltpu.VMEM((pl.cdiv(tm, MOE_RB), MOE_RB, D), F32), pltpu.VMEM((tm, D), F32)],
        compiler_params=_cparams(("arbitrary", "arbitrary", "arbitrary")),
        name="ffn_moe",
    )(*args)


def _cast_body(x_ref, o_ref):
    o_ref[...] = x_ref[...].astype(BF16)


def _cast_bf16(w, start=0, count=None):
    _, R, C = w.shape
    E = w.shape[0] if count is None else count
    rb = 256
    return pl.pallas_call(
        _cast_body,
        grid=(E, R // rb),
        in_specs=[pl.BlockSpec((1, rb, C), lambda e, r: (start + e, r, 0))],
        out_specs=pl.BlockSpec((1, rb, C), lambda e, r: (e, r, 0)),
        out_shape=jax.ShapeDtypeStruct((E, R, C), BF16),
        compiler_params=_cparams(("arbitrary", "arbitrary")),
        name="cast_bf16",
    )(w)


def _dsa_keys_body(ck_ref, cv_ref, ci_ref, nk_ref, nv_ref, ni_ref, kb_ref, vt_ref, ikb_ref, *, nc, kc_rows):
    cached = pl.program_id(1) < nc
    k = jnp.where(cached, ck_ref[0], nk_ref[0].astype(BF16))
    v = jnp.where(cached, cv_ref[0], nv_ref[0].astype(BF16))
    ik = jnp.where(cached, ci_ref[0], ni_ref[0])
    kb_ref[0] = k
    vt = v.astype(F32).T.astype(BF16)
    ones = jnp.ones((DSA_VR - DSA_HEAD_DIM, kc_rows), BF16)
    for h in range(DSA_HEADS):
        vt_ref[0, 0, h * DSA_VR:h * DSA_VR + DSA_HEAD_DIM, :] = vt[h * DSA_HEAD_DIM:(h + 1) * DSA_HEAD_DIM]
        vt_ref[0, 0, h * DSA_VR + DSA_HEAD_DIM:(h + 1) * DSA_VR, :] = ones
    ikb_ref[0, :, 0:IDX_DIM] = ik.astype(BF16)
    ikb_ref[0, :, IDX_DIM:128] = jnp.zeros((kc_rows, 128 - IDX_DIM), BF16)


def _k_dsa_keys(ck, cv, ci, nk, nv, ni, *, KC, row0):
    B = nk.shape[0]
    P = ck.shape[1]
    assert P % KC == 0
    nc = P // KC
    cblk = lambda w: pl.BlockSpec((1, KC, w), lambda b, c: (row0 + b, jnp.minimum(c, nc - 1), 0))
    nblk = lambda w: pl.BlockSpec((1, KC, w), lambda b, c: (b, 0, 0))
    return pl.pallas_call(
        functools.partial(_dsa_keys_body, nc=nc, kc_rows=KC),
        grid=(B, nc + 1),
        in_specs=[cblk(512), cblk(512), cblk(IDX_DIM), nblk(512), nblk(512), nblk(IDX_DIM)],
        out_specs=[pl.BlockSpec((1, KC, 512), lambda b, c: (b, c, 0)),
                   pl.BlockSpec((1, 1, DSA_HEADS * DSA_VR, KC), lambda b, c: (b, c, 0, 0)),
                   pl.BlockSpec((1, KC, 128), lambda b, c: (b, c, 0))],
        out_shape=[jax.ShapeDtypeStruct((B, P + KC, 512), BF16),
                   jax.ShapeDtypeStruct((B, nc + 1, DSA_HEADS * DSA_VR, KC), BF16),
                   jax.ShapeDtypeStruct((B, P + KC, 128), BF16)],
        compiler_params=_cparams(("arbitrary", "arbitrary")),
        name="dsa_keys",
    )(ck, cv, ci, nk, nv, ni)


def _mla_keys_body(cc_ref, cr_ref, nc_ref, nr_ref, cb_ref, ct_ref, rb_ref, *, nc, kc_rows):
    cached = pl.program_id(1) < nc
    ckv = jnp.where(cached, cc_ref[0], nc_ref[0])
    kr = jnp.where(cached, cr_ref[0], nr_ref[0])
    cb_ref[0] = ckv.astype(BF16)
    ct_ref[0, 0, 0:MLA_KV_LORA, :] = ckv.T.astype(BF16)
    ct_ref[0, 0, MLA_KV_LORA:MLA_VR, :] = jnp.ones((MLA_VR - MLA_KV_LORA, kc_rows), BF16)
    rb_ref[0] = kr.astype(BF16)


def _k_mla_keys(cc, cr, ncv, nr, *, KC, row0):
    B = ncv.shape[0]
    P = cc.shape[1]
    assert P % KC == 0
    nc = P // KC
    cblk = lambda w: pl.BlockSpec((1, KC, w), lambda b, c: (row0 + b, jnp.minimum(c, nc - 1), 0))
    nblk = lambda w: pl.BlockSpec((1, KC, w), lambda b, c: (b, 0, 0))
    return pl.pallas_call(
        functools.partial(_mla_keys_body, nc=nc, kc_rows=KC),
        grid=(B, nc + 1),
        in_specs=[cblk(MLA_KV_LORA), cblk(MLA_ROPE), nblk(MLA_KV_LORA), nblk(MLA_ROPE)],
        out_specs=[pl.BlockSpec((1, KC, MLA_KV_LORA), lambda b, c: (b, c, 0)),
                   pl.BlockSpec((1, 1, MLA_VR, KC), lambda b, c: (b, c, 0, 0)),
                   pl.BlockSpec((1, KC, MLA_ROPE), lambda b, c: (b, c, 0))],
        out_shape=[jax.ShapeDtypeStruct((B, P + KC, MLA_KV_LORA), BF16),
                   jax.ShapeDtypeStruct((B, nc + 1, MLA_VR, KC), BF16),
                   jax.ShapeDtypeStruct((B, P + KC, MLA_ROPE), BF16)],
        compiler_params=_cparams(("arbitrary", "arbitrary")),
        name="mla_keys",
    )(cc, cr, ncv, nr)


def _prep_w_in(w):
    aq, ak, av, iq, ik, iw, pu, cb, cc, ch, mq, mkv, mkr = jnp.split(w, IN_SPLIT_POINTS, axis=1)
    d = w.shape[0]
    mkr_sw = jnp.concatenate([mkr[:, 16:], mkr[:, :16]], axis=1)
    return jnp.concatenate([aq, ak, av, pu, cb, cc, ch, iq, mq, mkv, ik, iw, jnp.zeros((d, 60), w.dtype),
                            mkr, mkr_sw, jnp.zeros((d, 64), w.dtype)], axis=1).astype(BF16)


def _prep_w_uq(w):
    w3 = w.reshape(MLA_Q_LORA, MLA_HEADS, MLA_NOPE + MLA_ROPE)
    nope = w3[:, :, :MLA_NOPE].reshape(MLA_Q_LORA, MLA_HEADS * MLA_NOPE)
    rope = w3[:, :, MLA_NOPE:]
    rope_sw = jnp.concatenate([rope[:, :, 16:], rope[:, :, :16]], axis=-1)
    return jnp.concatenate([nope, rope.reshape(MLA_Q_LORA, -1), rope_sw.reshape(MLA_Q_LORA, -1)],
                           axis=1).astype(BF16)


def _rope_tables(pos):
    half = MLA_ROPE // 2
    inv = 1.0 / (ROPE_BASE ** (jnp.arange(half, dtype=F32) / half))
    ang = pos.astype(F32)[:, None] * inv[None, :]
    cos, sin = jnp.cos(ang), jnp.sin(ang)
    cos2 = jnp.tile(jnp.concatenate([cos, cos], axis=-1), (1, MLA_HEADS))
    sin2 = jnp.tile(jnp.concatenate([-sin, sin], axis=-1), (1, MLA_HEADS))
    return cos2, sin2


def _layer(x, mod, past, lw, fw, use_moe, final_g, cfg):
    B, T, D = x.shape
    bb, tt, KC, tq_dsa, tq_mla = cfg["bb"], cfg["tt"], cfg["KC"], cfg["tq_dsa"], cfg["tq_mla"]
    KCD = cfg["KCD"]
    N = B * T
    P = 0 if past is None else past["dsa_k"].shape[1]
    L = P + T
    sh1, sc1, g1, sh2, sc2, g2 = [m.reshape(B, 1, D) for m in jnp.split(mod, 6, axis=-1)]
    ng1 = lw["norm_mix_g"].reshape(1, 1, D)
    ng2 = lw["norm_ffn_g"].reshape(1, 1, D)

    emit = past is None
    res = _k_in(x, sc1, sh1, ng1, lw["w_in"], bb=bb, tt=tt, emit=emit, kc=KCD)
    z = res[0]
    zoff = Z_OFF if emit else 0
    z3 = z.reshape(B, T, ZW - zoff)
    if emit:
        k_new = res[1].reshape(B, T, DSA_HEADS, DSA_HEAD_DIM)
        v_new = res[2].reshape(B, T, DSA_HEADS, DSA_HEAD_DIM)
    else:
        k_new = z[:, C_AK:C_AK + 512].reshape(B, T, DSA_HEADS, DSA_HEAD_DIM)
        v_new = z[:, C_AV:C_AV + 512].reshape(B, T, DSA_HEADS, DSA_HEAD_DIM)
    ik_new = z3[:, :, C_IKW - zoff:C_IKW - zoff + IDX_DIM]

    if past is None:
        ph16 = jnp.zeros((B, 16, 512), F32)
        cvh16 = ph16
    else:
        ph16 = jnp.pad(past["pool"], ((0, 0), (1, 0), (0, 0)))
        cvh16 = jnp.pad(past["conv"], ((0, 0), (16 - (CONV_WIDTH - 1), 0), (0, 0)))
    o_b, o_c, ph_o, cvh_o = _k_pc(z3, ph16, cvh16, lw["pool_w"], lw["pool_scale"], lw["conv_w"],
                                  bb=bb, tt=tt, P=P, zoff=zoff)
    pool_hist = ph_o[:, 1:]
    conv_hist = cvh_o[:, 16 - (CONV_WIDTH - 1):]

    cos2, sin2 = _rope_tables(P + jnp.arange(T))
    if bb > 1:
        cos2, sin2 = jnp.tile(cos2, (bb, 1)), jnp.tile(sin2, (bb, 1))
    qlt, qrt, ckv_new, ckvb, ckvt, kr_new, krb = _k_mlaprep(
        z, cos2, sin2, lw["mla_q_norm_g"], lw["mla_kv_norm_g"], lw["mla_w_uq"], lw["mla_w_uk"], tm=bb * tt,
        zoff=zoff)

    topk = min(DSA_TOPK_MAX, L // 4)
    if past is None:
        kb3 = res[3].reshape(B, T, 512)
        vt4, qt, iqt, iwt = res[4], res[5], res[6], res[7]
        ikb3 = res[8].reshape(B, T, 128)
        ckvb3 = ckvb.reshape(B, T, 256)
        krb3 = krb.reshape(B, T, 32)
        Tq = T
    else:
        assert T <= KCD and T <= KC
        padn = lambda a, kc: jnp.pad(a, ((0, 0), (0, kc - T), (0, 0)))
        kb3, vt4, ikb3 = _k_dsa_keys(past["dsa_k"], past["dsa_v"], past["idx_k"],
                                     padn(k_new.reshape(B, T, 512), KCD), padn(v_new.reshape(B, T, 512), KCD),
                                     padn(ik_new, KCD), KC=KCD, row0=past["row0"])
        Tq = tq_dsa
        padq = ((0, 0), (0, 0), (0, Tq - T))
        qt = jnp.pad((z3[:, :, C_AQ:C_AQ + 512] * (DSA_SCALE * LOG2E)).transpose(0, 2, 1).astype(BF16), padq)
        iqt = jnp.pad(z3[:, :, C_IQ:C_IQ + 256].transpose(0, 2, 1).astype(BF16), padq)
        iwt = jnp.pad(z3[:, :, C_IKW + 64:C_IKW + 72].transpose(0, 2, 1) * IDX_SCALE, padq)
        ckvb3, ckvt4, krb3 = _k_mla_keys(past["ckv"], past["krope"], padn(ckv_new.reshape(B, T, 256), KC),
                                         padn(kr_new.reshape(B, T, 32), KC), KC=KC, row0=past["row0"])

    o_a = _k_dsa(qt, iqt, iwt, kb3, vt4, ikb3, tq=tq_dsa, P=P, L=L, KC=KCD, topk=topk)
    o_a = o_a[:, :T].reshape(N, 512)
    if past is None:
        ckvt4 = ckvt.reshape(B, T // KC, MLA_VR, KC)
        o_d = _k_mla(qlt, qrt, ckvb3, krb3, ckvt4, lw["mla_w_uv"], B=B, T=T, tq=tq_mla, P=P, L=L, KC=KC,
                     stacked=False).reshape(N, 512)
    else:
        qs = qlt.reshape(MLA_HEADS, 256, B, T).transpose(2, 1, 0, 3).reshape(B, 256, MLA_HEADS * T)
        rs = qrt.reshape(MLA_HEADS, 32, B, T).transpose(2, 1, 0, 3).reshape(B, 32, MLA_HEADS * T)
        o5 = _k_mla(qs, rs, ckvb3, krb3, ckvt4, lw["mla_w_uv"], B=B, T=T, tq=tq_mla, P=P, L=L, KC=KC,
                    stacked=True).reshape(B, MLA_HEADS, MLA_V, MLA_HEADS, T)
        o_d = jnp.stack([o5[:, h, :, h, :] for h in range(MLA_HEADS)], axis=1)
        o_d = o_d.transpose(0, 3, 1, 2).reshape(N, 512)

    x1 = _k_merge(x, sc1, sh1, g1, ng1, o_a, o_b.reshape(N, 512), o_c.reshape(N, 512), o_d,
                  lw["w_gate"], lw["b_gate"], lw["w_branch"], lw["w_out"], bb=bb, tt=tt)
    fg = None if final_g is None else final_g.reshape(1, 1, D)
    if use_moe:
        x2 = _k_moe(x1, sc2, sh2, g2, ng2, fw["rw"], fw["rb"], fw["w1"], fw["w3"], fw["w2"], fg,
                    bb=cfg["bb_moe"], tt=cfg["tt_moe"])
    else:
        x2 = _k_ffn(x1, sc2, sh2, g2, ng2, fw["w1"], fw["w3"], fw["w2"], fg, bb=bb, tt=tt)
    rows = (k_new, v_new, ik_new, ckv_new.reshape(B, T, 256), kr_new.reshape(B, T, 32), pool_hist, conv_hist)
    return x2, rows


def _prep_layer(l, ada_w, ada_b, norm_mix_g, norm_ffn_g, w_in, mla_q_norm_g, mla_kv_norm_g,
                mla_w_uq, mla_w_uk, mla_w_uv, pool_w, pool_scale, conv_w, w_gate, b_gate, w_branch, w_out):
    return dict(
        ada_w=_cast_bf16(ada_w, l, 1)[0], ada_b=ada_b[l].reshape(1, -1),
        norm_mix_g=norm_mix_g[l], norm_ffn_g=norm_ffn_g[l],
        w_in=_prep_w_in(w_in[l]),
        mla_q_norm_g=mla_q_norm_g[l].reshape(1, -1), mla_kv_norm_g=mla_kv_norm_g[l].reshape(1, -1),
        mla_w_uq=_prep_w_uq(mla_w_uq[l]),
        mla_w_uk=mla_w_uk[l].transpose(1, 0, 2).astype(BF16),
        mla_w_uv=mla_w_uv[l].transpose(1, 2, 0).astype(BF16),
        pool_w=pool_w[l].astype(BF16), pool_scale=pool_scale[l].reshape(1, -1),
        conv_w=jnp.pad(conv_w[l].reshape(CONV_WIDTH, BRANCH_W), ((0, 8 - CONV_WIDTH), (0, 0))),
        w_gate=_cast_bf16(w_gate, l, 1)[0], b_gate=b_gate[l].reshape(1, -1),
        w_branch=_cast_bf16(w_branch, l, 1)[0], w_out=_cast_bf16(w_out, l, 1)[0])


def _cfg_for(B, T, has_past):
    if has_past:
        return dict(bb=B, tt=T, KC=512, KCD=512, tq_dsa=128, tq_mla=T, bb_moe=B, tt_moe=T)
    t = min(512, T)
    return dict(bb=1, tt=t, KC=t, KCD=t, tq_dsa=t, tq_mla=t, bb_moe=1, tt_moe=min(1024, T))


def _run(x, c, pasts, lws, fws, final_norm_g):
    B, T, _ = x.shape
    cfg = _cfg_for(B, T, pasts is not None)
    outs = [[] for _ in range(7)]
    depth = len(lws)
    for l in range(depth):
        lw = lws[l]
        mod = _ada(c, lw["ada_w"], lw["ada_b"])
        x, rows = _layer(x, mod, None if pasts is None else pasts[l], lw, fws[l], l % 2 == 1,
                         final_norm_g if l == depth - 1 else None, cfg)
        for o, r in zip(outs, rows):
            o.append(r)
    return x, [jnp.stack(o) for o in outs]


def kernel(x_prompt, x_sample, c_prompt, c_sample, cache_dsa_k, cache_dsa_v, cache_dsa_idx_k, cache_mla_ckv, cache_mla_krope, state_pool, state_conv, ada_w, ada_b, norm_mix_g, norm_ffn_g, w_in, mla_q_norm_g, mla_kv_norm_g, mla_w_uq, mla_w_uk, mla_w_uv, pool_w, pool_scale, conv_w, w_gate, b_gate, w_branch, w_out, ffn_w1, ffn_w3, ffn_w2, moe_router_w, moe_router_b, moe_w1, moe_w3, moe_w2, final_norm_g):
    depth = ada_w.shape[0]
    lws = [_prep_layer(l, ada_w, ada_b, norm_mix_g, norm_ffn_g, w_in, mla_q_norm_g, mla_kv_norm_g,
                       mla_w_uq, mla_w_uk, mla_w_uv, pool_w, pool_scale, conv_w, w_gate, b_gate,
                       w_branch, w_out) for l in range(depth)]
    fws = []
    for l in range(depth):
        j = l // 2
        if l % 2 == 0:
            fws.append(dict(w1=_cast_bf16(ffn_w1, j, 1)[0], w3=_cast_bf16(ffn_w3, j, 1)[0],
                            w2=_cast_bf16(ffn_w2, j, 1)[0]))
        else:
            fws.append(dict(
                rw=jnp.pad(moe_router_w[j], ((0, 0), (0, 128 - N_EXPERTS))),
                rb=jnp.pad(moe_router_b[j].reshape(1, -1), ((0, 0), (0, 128 - N_EXPERTS)), constant_values=NEG),
                w1=_cast_bf16(moe_w1.reshape((-1,) + moe_w1.shape[2:]), j * N_EXPERTS, N_EXPERTS),
                w3=_cast_bf16(moe_w3.reshape((-1,) + moe_w3.shape[2:]), j * N_EXPERTS, N_EXPERTS),
                w2=_cast_bf16(moe_w2.reshape((-1,) + moe_w2.shape[2:]), j * N_EXPERTS, N_EXPERTS)))
    nb, plen = cache_dsa_k.shape[1], cache_dsa_k.shape[2]
    flat = lambda a: a.reshape(depth * nb, plen, -1)
    kc_bf, vc_bf = flat(cache_dsa_k.astype(BF16)), flat(cache_dsa_v.astype(BF16))
    sample_pasts = [dict(dsa_k=kc_bf, dsa_v=vc_bf, idx_k=flat(cache_dsa_idx_k),
                         ckv=flat(cache_mla_ckv), krope=flat(cache_mla_krope), row0=l * nb,
                         pool=state_pool[l], conv=state_conv[l]) for l in range(depth)]
    y_prompt, pn = _run(x_prompt, c_prompt, None, lws, fws, final_norm_g)
    y_sample, sn = _run(x_sample, c_sample, sample_pasts, lws, fws, final_norm_g)
    return (y_prompt, y_sample, pn[0], pn[1], pn[2], pn[3], pn[4], pn[5], pn[6],
            sn[0], sn[1], sn[2], sn[3], sn[4], sn[5], sn[6])
```
